```python
import jax
import jax.numpy as jnp
from jax import lax
import numpy as np

D_MODEL = 1024
BATCH = 1
SEQ = 16384
DEPTH = 2

GRID_W = 64
CTX_LEN = 256
HEAD_DIM = 64
EPS = 1e-6
ROPE_THETA = 10000.0

D_LRU = 256
LRU_BLOCKS = 4
LRU_BLOCK = D_LRU // LRU_BLOCKS
CONV_W = 4
CONV_PAD = (2, 1)
LRU_C = 8.0
WIN_Q_HEADS = 6
WIN_KV_HEADS = 2
WINDOW = 128
WIN_BLOCK = 128
NA_HEADS = 6
NA_KH = 8
NA_KW = 16
NA_QW = 16
NA_KCOLS = NA_QW + NA_KW
PEER_HEADS = 8
PEER_NKEYS = 128
PEER_N = PEER_NKEYS * PEER_NKEYS
PEER_DKEY = 256
PEER_TOPK = 16
PEER_CHUNK = 128

D_WIN_Q = WIN_Q_HEADS * HEAD_DIM
D_WIN_KV = WIN_KV_HEADS * HEAD_DIM
D_NA = NA_HEADS * HEAD_DIM
D_MIX = D_LRU + D_WIN_Q + D_NA
IN_WIDTHS = (D_LRU, D_LRU, D_WIN_Q, D_WIN_KV, D_WIN_KV, D_NA, D_NA, D_NA)
D_IN = sum(IN_WIDTHS)

kernel_name = 'hybrid_lru_window_natten_peer_prefix'


def rms_norm(x, g):
    xf = x.astype(jnp.float32)
    y = xf * lax.rsqrt(jnp.mean(xf * xf, axis=-1, keepdims=True) + EPS)
    return (y * g.astype(jnp.float32)).astype(x.dtype)


def heads(t, n):
    return t.reshape(*t.shape[:-1], n, HEAD_DIM)


def split_in(p):
    bounds = np.cumsum(IN_WIDTHS)[:-1].tolist()
    return jnp.split(p, bounds, axis=-1)


def rope_angles(L):
    t = jnp.arange(L)
    row = (t // GRID_W).astype(jnp.float32)
    col = (t % GRID_W).astype(jnp.float32)
    n = HEAD_DIM // 4
    inv = ROPE_THETA ** (-jnp.arange(n, dtype=jnp.float32) / n)
    ang = jnp.concatenate([row[:, None] * inv, col[:, None] * inv], axis=-1)
    return jnp.cos(ang), jnp.sin(ang)


def apply_rope_2d(x, cos, sin):
    n = HEAD_DIM // 4
    xf = x.astype(jnp.float32)
    parts = []
    for axis in range(2):
        seg = xf[..., axis * 2 * n:(axis + 1) * 2 * n]
        x1, x2 = seg[..., :n], seg[..., n:]
        cs = cos[None, :, None, axis * n:(axis + 1) * n]
        sn = sin[None, :, None, axis * n:(axis + 1) * n]
        parts += [x1 * cs - x2 * sn, x1 * sn + x2 * cs]
    return jnp.concatenate(parts, axis=-1).astype(x.dtype)


def dwconv(x, w, b):
    y = lax.conv_general_dilated(x, w[:, None, :].astype(x.dtype), window_strides=(1,),
                                 padding=[CONV_PAD], dimension_numbers=('NWC', 'WIO', 'NWC'),
                                 feature_group_count=x.shape[-1])
    return y + b


def block_linear(x, w, b):
    B, L, C = x.shape
    y = jnp.einsum('blnd,nde->blne', x.reshape(B, L, LRU_BLOCKS, LRU_BLOCK), w)
    return y.reshape(B, L, C) + b


def rglru_coeffs(u, wa, ba, wx, bx, lam):
    r = jax.nn.sigmoid(block_linear(u, wa, ba).astype(jnp.float32))
    i = jax.nn.sigmoid(block_linear(u, wx, bx).astype(jnp.float32))
    log_a = -LRU_C * r * jax.nn.softplus(-lam.astype(jnp.float32))
    a = jnp.exp(log_a)
    b = jnp.sqrt(-jnp.expm1(2.0 * log_a)) * i * u.astype(jnp.float32)
    return a, b


def _combine(left, right):
    a_l, b_l = left
    a_r, b_r = right
    return a_l * a_r, a_r * b_l + b_r


def linear_scan(a, b, h0, reverse):
    if h0 is not None:
        edge = -1 if reverse else 0
        b = b.at[:, edge].add(a[:, edge] * h0)
    _, h = lax.associative_scan(_combine, (a, b), reverse=reverse, axis=1)
    return h


def lru_mixer(xb_c, gb_c, xb_l, gb_l, conv_w, conv_b, wa, ba, wx, bx, lam, ctx_out):
    u_c = dwconv(xb_c, conv_w, conv_b)
    u_l = dwconv(xb_l, conv_w, conv_b)
    hs_c, hs_l = [], []
    for d in range(2):
        rev = d == 1
        a_c, b_c = rglru_coeffs(u_c, wa[d], ba[d], wx[d], bx[d], lam[d])
        h_c = linear_scan(a_c, b_c, None, rev)
        h0 = h_c[:, 0] if rev else h_c[:, -1]
        a_l, b_l = rglru_coeffs(u_l, wa[d], ba[d], wx[d], bx[d], lam[d])
        hs_l.append(linear_scan(a_l, b_l, h0, rev))
        hs_c.append(h_c)
    y_l = (hs_l[0] + hs_l[1]).astype(xb_l.dtype) * jax.nn.gelu(gb_l)
    y_c = (hs_c[0] + hs_c[1]).astype(xb_c.dtype) * jax.nn.gelu(gb_c) if ctx_out else None
    return y_c, y_l


def ctx_attention(q, k, v, sink):
    B, Lc, H, hd = q.shape
    G = k.shape[2]
    R = H // G
    qg = q.reshape(B, Lc, G, R, hd)
    s = jnp.einsum('bqgrd,bkgd->bgrqk', qg, k).astype(jnp.float32) * hd ** -0.5
    if sink is not None:
        s_sink = jnp.broadcast_to(sink.astype(jnp.float32).reshape(1, G, R, 1, 1), (B, G, R, Lc, 1))
        s = jnp.concatenate([s, s_sink], axis=-1)
    p = jax.nn.softmax(s, axis=-1)[..., :Lc].astype(v.dtype)
    o = jnp.einsum('bgrqk,bkgd->bqgrd', p, v)
    return o.reshape(B, Lc, H * hd)


def window_attention(q, k, v, kc, vc, sink):
    B, L, H, hd = q.shape
    G = k.shape[2]
    R = H // G
    W = WIN_BLOCK
    nb = L // W
    Lc = kc.shape[1]
    scale = hd ** -0.5
    qb = q.reshape(B, nb, W, G, R, hd)

    def bands(t):
        tp = jnp.pad(t, ((0, 0), (W, W), (0, 0), (0, 0))).reshape(B, nb + 2, W, G, hd)
        return jnp.concatenate([tp[:, :-2], tp[:, 1:-1], tp[:, 2:]], axis=2)

    kb, vb = bands(k), bands(v)
    s_loc = jnp.einsum('bnqgrd,bnkgd->bngrqk', qb, kb).astype(jnp.float32) * scale
    s_ctx = jnp.einsum('bnqgrd,bcgd->bngrqc', qb, kc).astype(jnp.float32) * scale
    qi = jnp.arange(W)[:, None]
    ki = jnp.arange(3 * W)[None, :]
    kpos = jnp.arange(nb)[:, None, None] * W + ki[None] - W
    valid = (jnp.abs(ki - W - qi) <= WINDOW)[None] & (kpos >= 0) & (kpos < L)
    s_loc = jnp.where(valid[None, :, None, None], s_loc, -jnp.inf)
    s_sink = jnp.broadcast_to(sink.astype(jnp.float32).reshape(1, 1, G, R, 1, 1), (B, nb, G, R, W, 1))
    p = jax.nn.softmax(jnp.concatenate([s_loc, s_ctx, s_sink], axis=-1), axis=-1).astype(v.dtype)
    o = (jnp.einsum('bngrqk,bnkgd->bnqgrd', p[..., :3 * W], vb)
         + jnp.einsum('bngrqc,bcgd->bnqgrd', p[..., 3 * W:3 * W + Lc], vc))
    return o.reshape(B, L, H * hd)


def neighborhood_attention(q, k, v, kc, vc, rpb):
    B, L, H, hd = q.shape
    rows = L // GRID_W
    kh = min(NA_KH, rows)
    ncb = GRID_W // NA_QW
    K = kh * NA_KCOLS
    scale = hd ** -0.5
    r = jnp.arange(rows)
    kr = jnp.clip(r - NA_KH // 2, 0, rows - kh)[:, None] + jnp.arange(kh)[None]
    kcol = (jnp.clip(jnp.arange(ncb) * NA_QW - NA_KW // 2, 0, GRID_W - NA_KCOLS)[:, None]
            + jnp.arange(NA_KCOLS)[None])
    qcol = jnp.arange(ncb)[:, None] * NA_QW + jnp.arange(NA_QW)[None]
    qstart = jnp.clip(qcol - NA_KW // 2, 0, GRID_W - NA_KW)
    rel = kcol[:, None, :] - qstart[:, :, None]
    colmask = (rel >= 0) & (rel < NA_KW)
    mask = jnp.broadcast_to(colmask[:, :, None, :], (ncb, NA_QW, kh, NA_KCOLS)).reshape(ncb, NA_QW, K)
    dc = jnp.clip(kcol[:, None, :] - qcol[:, :, None] + NA_KW - 1, 0, 2 * NA_KW - 2)
    dr = kr - r[:, None] + NA_KH - 1
    idx = (kr[:, None, :, None] * GRID_W + kcol[None, :, None, :]).reshape(rows, ncb, K)
    qrows = jnp.moveaxis(q.reshape(B, rows, ncb, NA_QW, H, hd), 1, 0)

    def row_block(args):
        q_r, idx_r, dr_r = args
        k_r = k[:, idx_r]
        v_r = v[:, idx_r]
        bias = rpb[:, dr_r[None, None, :, None], dc[:, :, None, :]]
        bias = jnp.moveaxis(bias, 0, 1).reshape(ncb, H, NA_QW, K).astype(jnp.float32)
        s = jnp.einsum('bjqhd,bjkhd->bjhqk', q_r, k_r).astype(jnp.float32) * scale + bias[None]
        s = jnp.where(mask[None, :, None], s, -jnp.inf)
        s_c = jnp.einsum('bjqhd,bchd->bjhqc', q_r, kc).astype(jnp.float32) * scale
        p = jax.nn.softmax(jnp.concatenate([s, s_c], axis=-1), axis=-1).astype(v.dtype)
        return (jnp.einsum('bjhqk,bjkhd->bjqhd', p[..., :K], v_r)
                + jnp.einsum('bjhqc,bchd->bjqhd', p[..., K:], vc))

    o = lax.map(row_block, (qrows, idx, dr))
    return jnp.moveaxis(o, 0, 1).reshape(B, L, H * hd)


def peer_ffn(h, w_q, k1, k2, u, v):
    B, L, D = h.shape
    half = PEER_DKEY // 2
    q = (h @ w_q).reshape(B, L, PEER_HEADS, 2, half)
    s1 = jnp.einsum('blhd,nd->blhn', q[..., 0, :], k1).astype(jnp.float32)
    s2 = jnp.einsum('blhd,nd->blhn', q[..., 1, :], k2).astype(jnp.float32)
    v1, i1 = lax.top_k(s1, PEER_TOPK)
    v2, i2 = lax.top_k(s2, PEER_TOPK)
    cand = (v1[..., :, None] + v2[..., None, :]).reshape(B, L, PEER_HEADS, PEER_TOPK * PEER_TOPK)
    cidx = (i1[..., :, None] * PEER_NKEYS + i2[..., None, :]).reshape(B, L, PEER_HEADS, PEER_TOPK * PEER_TOPK)
    best, pos = lax.top_k(cand, PEER_TOPK)
    eidx = jnp.take_along_axis(cidx, pos, axis=-1)
    gate = jax.nn.softmax(best, axis=-1).astype(h.dtype)
    n = B * L
    E = PEER_HEADS * PEER_TOPK
    nch = n // PEER_CHUNK
    xs = (h.reshape(nch, PEER_CHUNK, D), eidx.reshape(nch, PEER_CHUNK, E), gate.reshape(nch, PEER_CHUNK, E))

    def chunk(args):
        h_c, e_c, g_c = args
        act = jax.nn.gelu(jnp.einsum('ted,td->te', u[e_c], h_c))
        return jnp.einsum('te,ted->td', g_c * act, v[e_c])

    return lax.map(chunk, xs).reshape(B, L, D)


def setup_inputs(seed: int = 0) -> dict:
    key = jax.random.key(seed)
    ks = iter(jax.random.split(key, 40))
    f32 = jnp.float32
    D = D_MODEL

    def nrm(shape, s):
        return jax.random.normal(next(ks), shape, f32) * s

    def gain(shape):
        return 1.0 + nrm(shape, 0.02)

    x = nrm((BATCH, SEQ, D), 1.0)
    c = nrm((BATCH, D), 1.0)
    ctx = nrm((BATCH, CTX_LEN, D), 1.0)
    c_ctx = nrm((D,), 1.0)
    w_mod = nrm((DEPTH, D, 6 * D), 0.5 * D ** -0.5)
    b_mod = nrm((DEPTH, 6 * D), 0.01)
    norm1_g = gain((DEPTH, D))
    norm2_g = gain((DEPTH, D))
    w_in = nrm((DEPTH, D, D_IN), D ** -0.5)
    w_out = nrm((DEPTH, D_MIX, D), D_MIX ** -0.5)
    lru_conv_w = nrm((DEPTH, CONV_W, D_LRU), CONV_W ** -0.5)
    lru_conv_b = nrm((DEPTH, D_LRU), 0.01)
    lru_wa = nrm((DEPTH, 2, LRU_BLOCKS, LRU_BLOCK, LRU_BLOCK), LRU_BLOCK ** -0.5)
    lru_ba = nrm((DEPTH, 2, D_LRU), 0.01)
    lru_wx = nrm((DEPTH, 2, LRU_BLOCKS, LRU_BLOCK, LRU_BLOCK), LRU_BLOCK ** -0.5)
    lru_bx = nrm((DEPTH, 2, D_LRU), 0.01)
    a8 = jax.random.uniform(next(ks), (DEPTH, 2, D_LRU), f32, 0.9, 0.999)
    a_base = a8 ** (1.0 / LRU_C)
    lru_lam = jnp.log(a_base) - jnp.log1p(-a_base)
    win_qn_g = gain((DEPTH, HEAD_DIM))
    win_kn_g = gain((DEPTH, HEAD_DIM))
    win_sink = nrm((DEPTH, WIN_Q_HEADS), 0.5)
    na_qn_g = gain((DEPTH, HEAD_DIM))
    na_kn_g = gain((DEPTH, HEAD_DIM))
    na_rpb = nrm((DEPTH, NA_HEADS, 2 * NA_KH - 1, 2 * NA_KW - 1), 0.1)
    peer_wq = nrm((DEPTH, D, PEER_HEADS * PEER_DKEY), D ** -0.5)
    peer_k1 = nrm((DEPTH, PEER_NKEYS, PEER_DKEY // 2), (PEER_DKEY // 2) ** -0.5)
    peer_k2 = nrm((DEPTH, PEER_NKEYS, PEER_DKEY // 2), (PEER_DKEY // 2) ** -0.5)
    peer_u = nrm((DEPTH, PEER_N, D), D ** -0.5)
    peer_v = nrm((DEPTH, PEER_N, D), 1.0)
    return {'x': x, 'c': c, 'ctx': ctx, 'c_ctx': c_ctx, 'w_mod': w_mod, 'b_mod': b_mod,
            'norm1_g': norm1_g, 'norm2_g': norm2_g, 'w_in': w_in, 'w_out': w_out,
            'lru_conv_w': lru_conv_w, 'lru_conv_b': lru_conv_b, 'lru_wa': lru_wa, 'lru_ba': lru_ba,
            'lru_wx': lru_wx, 'lru_bx': lru_bx, 'lru_lam': lru_lam,
            'win_qn_g': win_qn_g, 'win_kn_g': win_kn_g, 'win_sink': win_sink,
            'na_qn_g': na_qn_g, 'na_kn_g': na_kn_g, 'na_rpb': na_rpb,
            'peer_wq': peer_wq, 'peer_k1': peer_k1, 'peer_k2': peer_k2, 'peer_u': peer_u, 'peer_v': peer_v}


def reference(x, c, ctx, c_ctx, w_mod, b_mod, norm1_g, norm2_g, w_in, w_out,
              lru_conv_w, lru_conv_b, lru_wa, lru_ba, lru_wx, lru_bx, lru_lam,
              win_qn_g, win_kn_g, win_sink, na_qn_g, na_kn_g, na_rpb,
              peer_wq, peer_k1, peer_k2, peer_u, peer_v):
    B, L, D = x.shape
    Lc = ctx.shape[1]
    cos, sin = rope_angles(L)
    xc = ctx
    c_act = jax.nn.silu(c)
    cc_act = jax.nn.silu(c_ctx)
    for l in range(DEPTH):
        ctx_out = l < DEPTH - 1
        mod = (c_act @ w_mod[l] + b_mod[l])[:, None, :]
        mod_c = (cc_act @ w_mod[l] + b_mod[l])[None, None, :]
        sh1, sc1, g1, sh2, sc2, g2 = jnp.split(mod, 6, axis=-1)
        csh1, csc1, cg1, csh2, csc2, cg2 = jnp.split(mod_c, 6, axis=-1)

        h_l = rms_norm(x, norm1_g[l]) * (1 + sc1) + sh1
        h_c = rms_norm(xc, norm1_g[l]) * (1 + csc1) + csh1
        xa_l, ga_l, qw_l, kw_l, vw_l, qn_l, kn_l, vn_l = split_in(h_l @ w_in[l])
        xa_c, ga_c, qw_c, kw_c, vw_c, qn_c, kn_c, vn_c = split_in(h_c @ w_in[l])

        ya_c, ya_l = lru_mixer(xa_c, ga_c, xa_l, ga_l, lru_conv_w[l], lru_conv_b[l], lru_wa[l], lru_ba[l],
                               lru_wx[l], lru_bx[l], lru_lam[l], ctx_out)

        kwc = rms_norm(heads(kw_c, WIN_KV_HEADS), win_kn_g[l])
        vwc = heads(vw_c, WIN_KV_HEADS)
        qwl = apply_rope_2d(rms_norm(heads(qw_l, WIN_Q_HEADS), win_qn_g[l]), cos, sin)
        kwl = apply_rope_2d(rms_norm(heads(kw_l, WIN_KV_HEADS), win_kn_g[l]), cos, sin)
        yb_l = window_attention(qwl, kwl, heads(vw_l, WIN_KV_HEADS), kwc, vwc, win_sink[l])

        knc = rms_norm(heads(kn_c, NA_HEADS), na_kn_g[l])
        vnc = heads(vn_c, NA_HEADS)
        qnl = rms_norm(heads(qn_l, NA_HEADS), na_qn_g[l])
        knl = rms_norm(heads(kn_l, NA_HEADS), na_kn_g[l])
        yc_l = neighborhood_attention(qnl, knl, heads(vn_l, NA_HEADS), knc, vnc, na_rpb[l])

        x = x + g1 * (jnp.concatenate([ya_l, yb_l, yc_l], axis=-1) @ w_out[l])
        if ctx_out:
            yb_c = ctx_attention(rms_norm(heads(qw_c, WIN_Q_HEADS), win_qn_g[l]), kwc, vwc, win_sink[l])
            yc_c = ctx_attention(rms_norm(heads(qn_c, NA_HEADS), na_qn_g[l]), knc, vnc, None)
            xc = xc + cg1 * (jnp.concatenate([ya_c, yb_c, yc_c], axis=-1) @ w_out[l])

        h2_l = rms_norm(x, norm2_g[l]) * (1 + sc2) + sh2
        if ctx_out:
            h2_c = rms_norm(xc, norm2_g[l]) * (1 + csc2) + csh2
            f = peer_ffn(jnp.concatenate([h2_c, h2_l], axis=1), peer_wq[l], peer_k1[l], peer_k2[l],
                         peer_u[l], peer_v[l])
            xc = xc + cg2 * f[:, :Lc]
            x = x + g2 * f[:, Lc:]
        else:
            x = x + g2 * peer_ffn(h2_l, peer_wq[l], peer_k1[l], peer_k2[l], peer_u[l], peer_v[l])
    return x
```

```python
import functools

import numpy as np
import jax
import jax.numpy as jnp
from jax import lax
from jax.experimental import pallas as pl
from jax.experimental.pallas import tpu as pltpu

F32 = jnp.float32
BF16 = jnp.bfloat16

HEAD_DIM = 64
GRID_W = 64
EPS = 1e-6
ROPE_THETA = 10000.0
D_LRU = 256
LRU_BLOCKS = 4
LRU_C = 8.0
WIN_Q_HEADS = 6
WIN_KV_HEADS = 2
WIN_BLOCK = 128
NA_HEADS = 6
NA_KH = 8
NA_KW = 16
PEER_HEADS = 8
PEER_NKEYS = 128
PEER_TOPK = 16
D_WIN_Q = WIN_Q_HEADS * HEAD_DIM
D_WIN_KV = WIN_KV_HEADS * HEAD_DIM
D_NA = NA_HEADS * HEAD_DIM

LANES = 128
TOK_TILE = 256
PEER_TOK_TILE = 512
PEER_EXPERT_CHUNK = 1024
PREP_TOK_TILE = 256
VMEM_LIMIT = 56 * 1024 * 1024
NEG = -1e30


def _cparams(*sem):
    return pltpu.CompilerParams(dimension_semantics=sem, vmem_limit_bytes=VMEM_LIMIT)


def _dot(a, b):
    return jnp.dot(a, b, preferred_element_type=F32)


def _dot_nt(a, b):
    return lax.dot_general(a, b, (((1,), (1,)), ((), ())), preferred_element_type=F32)


def _mod_kernel(c_ref, w_ref, b_ref, o_ref):
    c = c_ref[...]
    a = (c * jax.nn.sigmoid(c)).astype(BF16)
    o_ref[0] = _dot(a, w_ref[0].astype(BF16)) + b_ref[0]


def _modulation(cvec, w_mod, b_mod):
    depth, d, d6 = w_mod.shape
    return pl.pallas_call(
        _mod_kernel,
        out_shape=jax.ShapeDtypeStruct((depth, 8, d6), F32),
        grid=(depth, d6 // d),
        in_specs=[pl.BlockSpec((8, d), lambda l, j: (0, 0)),
                  pl.BlockSpec((1, d, d), lambda l, j: (l, 0, j)),
                  pl.BlockSpec((1, 1, d), lambda l, j: (l, 0, j))],
        out_specs=pl.BlockSpec((1, 8, d), lambda l, j: (l, 0, j)),
        compiler_params=_cparams("arbitrary", "arbitrary"),
        name="modulation",
    )(cvec, w_mod, b_mod.reshape(depth, 1, d6))


def _pair_mean_matrix():
    r = lax.broadcasted_iota(jnp.int32, (LANES, LANES), 0) < HEAD_DIM
    c = lax.broadcasted_iota(jnp.int32, (LANES, LANES), 1) < HEAD_DIM
    return jnp.where(r == c, 1.0 / HEAD_DIM, 0.0).astype(BF16)


def _head_norm(z, gain, bd):
    z2 = z * z
    hi = z2.astype(BF16)
    lo = (z2 - hi.astype(F32)).astype(BF16)
    ms = _dot(hi, bd) + _dot(lo, bd)
    return z * lax.rsqrt(ms + EPS) * gain


def _rope(z, cos, sin):
    lane = lax.broadcasted_iota(jnp.int32, z.shape, 1)
    first = (lane & 16) == 0
    partner = jnp.where(first, pltpu.roll(z, LANES - 16, 1), pltpu.roll(z, 16, 1))
    return z * cos + partner * sin


def _in_kernel(xl_ref, xc_ref, mod_ref, g_ref, w_ref, cos_ref, sin_ref, hg_ref,
               xa_ref, ga_ref, qw_ref, kw_ref, vw_ref, qn_ref, kn_ref, vn_ref, *, n_lat):
    d = xl_ref.shape[1]
    i = pl.program_id(0)
    row = (i == n_lat).astype(jnp.int32)
    is_ctx = jnp.full((TOK_TILE, 1), row) == 1
    x = jnp.where(is_ctx, xc_ref[...], xl_ref[...])
    ms = jnp.mean(x * x, axis=-1, keepdims=True)
    y = x * lax.rsqrt(ms + EPS) * g_ref[...]
    sh = mod_ref[pl.ds(row, 1), 0:d]
    sc = mod_ref[pl.ds(row, 1), d:2 * d]
    h = (y * (1.0 + sc) + sh).astype(BF16)
    z = _dot(h, w_ref[...])

    bd = _pair_mean_matrix()
    cos = cos_ref[...]
    sin = sin_ref[...]
    scale = HEAD_DIM ** -0.5
    o = 0
    xa_ref[...] = z[:, o:o + D_LRU]
    o += D_LRU
    ga_ref[...] = z[:, o:o + D_LRU]
    o += D_LRU
    for g in range(D_WIN_Q // LANES):
        zz = _rope(_head_norm(z[:, o:o + LANES], hg_ref[0:1], bd), cos, sin)
        qw_ref[:, g * LANES:(g + 1) * LANES] = (zz * scale).astype(BF16)
        o += LANES
    kw_ref[...] = _rope(_head_norm(z[:, o:o + LANES], hg_ref[1:2], bd), cos, sin).astype(BF16)
    o += LANES
    vw_ref[...] = z[:, o:o + LANES].astype(BF16)
    o += LANES
    for g in range(D_NA // LANES):
        zz = _head_norm(z[:, o:o + LANES], hg_ref[2:3], bd)
        qn_ref[:, g * LANES:(g + 1) * LANES] = (zz * scale).astype(BF16)
        o += LANES
    for g in range(D_NA // LANES):
        zz = _head_norm(z[:, o:o + LANES], hg_ref[3:4], bd)
        kn_ref[:, g * LANES:(g + 1) * LANES] = zz.astype(BF16)
        o += LANES
    vn_ref[...] = z[:, o:o + D_NA].astype(BF16)


def _in_proj(xl, xc, mod, g, w_bf, cos_t, sin_t, hg):
    L, d = xl.shape
    n_lat = L // TOK_TILE
    n = L + xc.shape[0]
    d_in = w_bf.shape[1]
    T = TOK_TILE
    tok = lambda w: pl.BlockSpec((T, w), lambda i: (i, 0))
    full = lambda a: pl.BlockSpec(a.shape, lambda i: (0,) * a.ndim)
    widths = (D_LRU, D_LRU, D_WIN_Q, D_WIN_KV, D_WIN_KV, D_NA, D_NA, D_NA)
    dtypes = (F32, F32, BF16, BF16, BF16, BF16, BF16, BF16)
    return pl.pallas_call(
        functools.partial(_in_kernel, n_lat=n_lat),
        out_shape=[jax.ShapeDtypeStruct((n, w), t) for w, t in zip(widths, dtypes)],
        grid=(n_lat + 1,),
        in_specs=[pl.BlockSpec((T, d), lambda i: (jnp.minimum(i, n_lat - 1), 0)),
                  pl.BlockSpec((T, d), lambda i: (0, 0)),
                  full(mod), full(g), full(w_bf), tok(LANES), tok(LANES), full(hg)],
        out_specs=[tok(w) for w in widths],
        compiler_params=_cparams("arbitrary"),
        name="in_proj",
    )(xl, xc, mod, g, w_bf, cos_t, sin_t, hg)


def _chunk_scan(a, b, reverse):
    T = a.shape[0]
    rows = lax.broadcasted_iota(jnp.int32, a.shape, 0)
    s = 1
    while s < T:
        if reverse:
            edge = rows >= T - s
            shift = T - s
        else:
            edge = rows < s
            shift = s
        a_s = jnp.where(edge, 1.0, pltpu.roll(a, shift, 0))
        b_s = jnp.where(edge, 0.0, pltpu.roll(b, shift, 0))
        b = a * b_s + b
        a = a * a_s
        s *= 2
    return a, b


def _lru_direction(xm, ph, nh, pv, nv, cw, cb, w, bias, spl, carry_ref, first, reverse):
    T, C = xm.shape
    rows = lax.broadcasted_iota(jnp.int32, (T, C), 0)
    p6 = ph[6:7] * pv
    p7 = ph[7:8] * pv
    n0 = nh[0:1] * nv
    x_m1 = jnp.where(rows == 0, p7, pltpu.roll(xm, 1, 0))
    x_m2 = jnp.where(rows == 0, p6, jnp.where(rows == 1, p7, pltpu.roll(xm, 2, 0)))
    x_p1 = jnp.where(rows == T - 1, n0, pltpu.roll(xm, T - 1, 0))
    u = cw[0:1] * x_m2 + cw[1:2] * x_m1 + cw[2:3] * xm + cw[3:4] * x_p1 + cb
    zz = _dot(u.astype(BF16), w) + bias
    r = jax.nn.sigmoid(zz[:, :C])
    ig = jax.nn.sigmoid(zz[:, C:])
    log_a = -LRU_C * r * spl
    a = jnp.exp(log_a)
    b = jnp.sqrt(-jnp.tanh(log_a) * (a * a + 1.0)) * ig * u
    a, b = _chunk_scan(a, b, reverse)

    @pl.when(first)
    def _():
        carry_ref[...] = jnp.zeros(carry_ref.shape, F32)

    h = a * carry_ref[0:1] + b
    edge = h[0:1] if reverse else h[T - 1:T]
    carry_ref[...] = jnp.broadcast_to(edge, carry_ref.shape)
    return h


def _lru_kernel(xf_ref, pf_ref, nf_ref, xb_ref, pb_ref, nb_ref, cw_ref, cb_ref, w_ref, b_ref,
                lam_ref, hf_ref, hb_ref, cf_ref, cbk_ref, *, n_lat):
    j = pl.program_id(0)
    C = D_LRU
    fblk = jnp.where(j == 0, n_lat, j - 1)
    bblk = jnp.where(j == 0, n_lat, n_lat - j)
    lam = lam_ref[...]
    spl = jnp.maximum(-lam, 0.0) + jnp.log1p(jnp.exp(-jnp.abs(lam)))
    cw = cw_ref[...]
    cb = cb_ref[...]

    def halo_valid(blk):
        pv = jnp.logical_and(blk != 0, blk != n_lat).astype(F32)
        nv = jnp.logical_and(blk != n_lat - 1, blk != n_lat).astype(F32)
        return pv, nv

    pv, nv = halo_valid(fblk)
    hf_ref[...] = _lru_direction(xf_ref[...], pf_ref[...], nf_ref[...], pv, nv, cw, cb,
                                 w_ref[:, 0:2 * C], b_ref[:, 0:2 * C], spl[0:1], cf_ref, j == 0, False)
    pv, nv = halo_valid(bblk)
    hb_ref[...] = _lru_direction(xb_ref[...], pb_ref[...], nb_ref[...], pv, nv, cw, cb,
                                 w_ref[:, 2 * C:4 * C], b_ref[:, 2 * C:4 * C], spl[1:2], cbk_ref, j == 0, True)


def _lru(xa, conv_w, conv_b, w_gates, b_gates, lam, n_lat):
    n, C = xa.shape
    T = TOK_TILE
    sub = T // 8
    nblk8 = n // 8
    fblk = lambda j: jnp.where(j == 0, n_lat, j - 1)
    bblk = lambda j: jnp.where(j == 0, n_lat, n_lat - j)
    prev8 = lambda blk: jnp.maximum(blk * sub - 1, 0)
    next8 = lambda blk: jnp.minimum((blk + 1) * sub, nblk8 - 1)
    full = lambda a: pl.BlockSpec(a.shape, lambda j: (0,) * a.ndim)
    return pl.pallas_call(
        functools.partial(_lru_kernel, n_lat=n_lat),
        out_shape=[jax.ShapeDtypeStruct((n, C), F32)] * 2,
        grid=(n_lat + 1,),
        in_specs=[pl.BlockSpec((T, C), lambda j: (fblk(j), 0)),
                  pl.BlockSpec((8, C), lambda j: (prev8(fblk(j)), 0)),
                  pl.BlockSpec((8, C), lambda j: (next8(fblk(j)), 0)),
                  pl.BlockSpec((T, C), lambda j: (bblk(j), 0)),
                  pl.BlockSpec((8, C), lambda j: (prev8(bblk(j)), 0)),
                  pl.BlockSpec((8, C), lambda j: (next8(bblk(j)), 0)),
                  full(conv_w), full(conv_b), full(w_gates), full(b_gates), full(lam)],
        out_specs=[pl.BlockSpec((T, C), lambda j: (fblk(j), 0)),
                   pl.BlockSpec((T, C), lambda j: (bblk(j), 0))],
        scratch_shapes=[pltpu.VMEM((8, C), F32), pltpu.VMEM((8, C), F32)],
        compiler_params=_cparams("arbitrary"),
        name="rglru",
    )(xa, xa, xa, xa, xa, xa, conv_w, conv_b, w_gates, b_gates, lam)


def _win_kernel(sink_ref, q_ref, kp_ref, kc_ref, kn_ref, vp_ref, vc_ref, vn_ref, kx_ref, vx_ref,
                o_ref, *, nb):
    W = WIN_BLOCK
    R = WIN_Q_HEADS // WIN_KV_HEADS
    b = pl.program_id(0)
    blk = jnp.where(b < nb, b, -4)
    rows = lax.broadcasted_iota(jnp.int32, (R * W, 3 * W), 0)
    cols = lax.broadcasted_iota(jnp.int32, (R * W, 3 * W), 1)
    kpos = (blk - 1) * W + cols
    qpos = blk * W + (rows & (W - 1))
    valid = jnp.logical_and(jnp.abs(kpos - qpos) <= W, jnp.logical_and(kpos >= 0, kpos < nb * W))
    row1 = lax.broadcasted_iota(jnp.int32, (R * W, 1), 0)
    q = q_ref[...]
    for g in range(WIN_KV_HEADS):
        ls = slice(g * HEAD_DIM, (g + 1) * HEAD_DIM)
        qs = jnp.concatenate([q[:, (g * R + r) * HEAD_DIM:(g * R + r + 1) * HEAD_DIM] for r in range(R)], axis=0)
        kl = jnp.concatenate([kp_ref[:, ls], kc_ref[:, ls], kn_ref[:, ls]], axis=0)
        vl = jnp.concatenate([vp_ref[:, ls], vc_ref[:, ls], vn_ref[:, ls]], axis=0)
        s_loc = jnp.where(valid, _dot_nt(qs, kl), NEG)
        s_ctx = _dot_nt(qs, kx_ref[:, ls])
        sink = jnp.full((R * W, 1), sink_ref[g * R], F32)
        for r in range(1, R):
            sink = jnp.where(row1 >= r * W, sink_ref[g * R + r], sink)
        m = jnp.maximum(jnp.maximum(jnp.max(s_loc, axis=-1, keepdims=True),
                                    jnp.max(s_ctx, axis=-1, keepdims=True)), sink)
        p_loc = jnp.exp(s_loc - m)
        p_ctx = jnp.exp(s_ctx - m)
        den = (jnp.sum(p_loc, axis=-1, keepdims=True) + jnp.sum(p_ctx, axis=-1, keepdims=True)
               + jnp.exp(sink - m))
        o = (_dot(p_loc.astype(BF16), vl) + _dot(p_ctx.astype(BF16), vx_ref[:, ls])) / den
        for r in range(R):
            hh = g * R + r
            o_ref[:, hh * HEAD_DIM:(hh + 1) * HEAD_DIM] = o[r * W:(r + 1) * W].astype(o_ref.dtype)


def _window_attention(sink, qw, kw, vw, L, with_ctx):
    n = qw.shape[0]
    W = WIN_BLOCK
    nb = L // W
    nq = n // W if with_ctx else nb
    ctx_blk = L // TOK_TILE
    lat = lambda b: jnp.minimum(b, nb - 1)
    kv = lambda f: pl.BlockSpec((W, D_WIN_KV), lambda b: (f(b), 0))
    prv = lambda b: jnp.maximum(lat(b) - 1, 0)
    nxt = lambda b: jnp.minimum(lat(b) + 1, nb - 1)
    ctx = pl.BlockSpec((n - L, D_WIN_KV), lambda b: (ctx_blk, 0))
    return pl.pallas_call(
        functools.partial(_win_kernel, nb=nb),
        out_shape=jax.ShapeDtypeStruct((nq * W, D_WIN_Q), BF16),
        grid=(nq,),
        in_specs=[pl.BlockSpec(memory_space=pltpu.SMEM),
                  pl.BlockSpec((W, D_WIN_Q), lambda b: (b, 0)),
                  kv(prv), kv(lat), kv(nxt), kv(prv), kv(lat), kv(nxt), ctx, ctx],
        out_specs=pl.BlockSpec((W, D_WIN_Q), lambda b: (b, 0)),
        compiler_params=_cparams("arbitrary"),
        name="window_attn",
    )(sink, qw, kw, kw, kw, vw, vw, vw, kw, vw)


def _na_kernel(q_ref, kl_ref, vl_ref, kx_ref, vx_ref, bias_ref, o_ref, *, rows):
    b = pl.program_id(0)
    latent = jnp.full((GRID_W, 1), (b < rows).astype(jnp.int32)) == 1
    q = q_ref[...]
    for h in range(NA_HEADS):
        ls = slice(h * HEAD_DIM, (h + 1) * HEAD_DIM)
        qh = q[:, ls]
        s_loc = jnp.where(latent, _dot_nt(qh, kl_ref[:, ls]) + bias_ref[0, h], NEG)
        s_ctx = _dot_nt(qh, kx_ref[:, ls])
        m = jnp.maximum(jnp.max(s_loc, axis=-1, keepdims=True), jnp.max(s_ctx, axis=-1, keepdims=True))
        p_loc = jnp.exp(s_loc - m)
        p_ctx = jnp.exp(s_ctx - m)
        den = jnp.sum(p_loc, axis=-1, keepdims=True) + jnp.sum(p_ctx, axis=-1, keepdims=True)
        o = (_dot(p_loc.astype(BF16), vl_ref[:, ls]) + _dot(p_ctx.astype(BF16), vx_ref[:, ls])) / den
        o_ref[:, ls] = o.astype(o_ref.dtype)


def _na_bias_table(rpb, rows):
    qc = np.arange(GRID_W)[:, None]
    kc = np.arange(GRID_W)[None, :]
    qstart = np.clip(qc - NA_KW // 2, 0, GRID_W - NA_KW)
    inside = (kc - qstart >= 0) & (kc - qstart < NA_KW)
    dc = np.clip(kc - qc + NA_KW - 1, 0, 2 * NA_KW - 2)
    per_dr = jnp.where(inside[None, None], rpb[:, :, dc].astype(F32), NEG)
    cases = []
    for c in range(NA_KH):
        if c < NA_KH // 2:
            dr = np.arange(NA_KH) - c + NA_KH - 1
        elif c == NA_KH // 2:
            dr = np.arange(NA_KH) + NA_KH // 2 - 1
        else:
            dr = np.arange(NA_KH) + NA_KH - 1 - c
        blk = per_dr[:, dr]
        cases.append(jnp.transpose(blk, (0, 2, 1, 3)).reshape(NA_HEADS, GRID_W, NA_KH * GRID_W))
    return jnp.stack(cases)


def _neighborhood_attention(qn, kn, vn, bias, L, with_ctx):
    n = qn.shape[0]
    rows = L // GRID_W
    nr = n // GRID_W if with_ctx else rows
    ctx_blk = L // TOK_TILE
    half = NA_KH // 2

    def row_of(b):
        return jnp.minimum(b, rows - 1)

    def key_start(b):
        return jnp.clip(row_of(b) - half, 0, rows - NA_KH) * GRID_W

    def case_of(b):
        r = row_of(b)
        return jnp.where(r < half, r, jnp.where(r <= rows - half, half, r - (rows - NA_KH)))

    loc = pl.BlockSpec((pl.Element(NA_KH * GRID_W), pl.Element(D_NA)), lambda b: (key_start(b), 0))
    ctx = pl.BlockSpec((n - L, D_NA), lambda b: (ctx_blk, 0))
    return pl.pallas_call(
        functools.partial(_na_kernel, rows=rows),
        out_shape=jax.ShapeDtypeStruct((nr * GRID_W, D_NA), BF16),
        grid=(nr,),
        in_specs=[pl.BlockSpec((GRID_W, D_NA), lambda b: (b, 0)), loc, loc, ctx, ctx,
                  pl.BlockSpec((1, NA_HEADS, GRID_W, NA_KH * GRID_W), lambda b: (case_of(b), 0, 0, 0))],
        out_specs=pl.BlockSpec((GRID_W, D_NA), lambda b: (b, 0)),
        compiler_params=_cparams("arbitrary"),
        name="neighborhood_attn",
    )(qn, kn, vn, kn, vn, bias)


def _out_kernel(xl_ref, xc_ref, hf_ref, hb_ref, ga_ref, yb_ref, yc_ref, w_ref, mod_ref, g_ref,
                xo_ref, ht_ref, *, n_lat):
    d = xl_ref.shape[1]
    i = pl.program_id(0)
    row = (i == n_lat).astype(jnp.int32)
    is_ctx = jnp.full((TOK_TILE, 1), row) == 1
    x = jnp.where(is_ctx, xc_ref[...], xl_ref[...])
    ya = ((hf_ref[...] + hb_ref[...]) * jax.nn.gelu(ga_ref[...])).astype(BF16)
    o1 = D_LRU
    o2 = D_LRU + D_WIN_Q
    mix = (_dot(ya, w_ref[0:o1]) + _dot(yb_ref[...], w_ref[o1:o2]) + _dot(yc_ref[...], w_ref[o2:o2 + D_NA]))
    xn = x + mod_ref[pl.ds(row, 1), 2 * d:3 * d] * mix
    xo_ref[...] = xn
    ms = jnp.mean(xn * xn, axis=-1, keepdims=True)
    y = xn * lax.rsqrt(ms + EPS) * g_ref[...]
    h2 = y * (1.0 + mod_ref[pl.ds(row, 1), 4 * d:5 * d]) + mod_ref[pl.ds(row, 1), 3 * d:4 * d]
    ht_ref[...] = h2.T.astype(BF16)


def _out_proj(xl, xc, hf, hb, ga, yb, yc, w_bf, mod, g, with_ctx):
    L, d = xl.shape
    T = TOK_TILE
    n_lat = L // T
    nt = n_lat + 1 if with_ctx else n_lat
    tok = lambda w: pl.BlockSpec((T, w), lambda i: (i, 0))
    full = lambda a: pl.BlockSpec(a.shape, lambda i: (0,) * a.ndim)
    return pl.pallas_call(
        functools.partial(_out_kernel, n_lat=n_lat),
        out_shape=[jax.ShapeDtypeStruct((nt * T, d), F32), jax.ShapeDtypeStruct((d, nt * T), BF16)],
        grid=(nt,),
        in_specs=[pl.BlockSpec((T, d), lambda i: (jnp.minimum(i, n_lat - 1), 0)),
                  pl.BlockSpec((T, d), lambda i: (0, 0)),
                  tok(D_LRU), tok(D_LRU), tok(D_LRU), tok(D_WIN_Q), tok(D_NA),
                  full(w_bf), full(mod), full(g)],
        out_specs=[tok(d), pl.BlockSpec((d, T), lambda i: (0, i))],
        compiler_params=_cparams("arbitrary"),
        name="out_proj",
    )(xl, xc, hf, hb, ga, yb, yc, w_bf, mod, g)


def _top_rows(s, k):
    out = []
    cur = s
    for _ in range(k):
        mk = jnp.max(cur, axis=0, keepdims=True)
        out.append(mk)
        cur = jnp.where(cur == mk, -jnp.inf, cur)
    return out


def _prep_kernel(ht_ref, wq_ref, k1_ref, k2_ref, s2_ref, e2_ref, e1_ref, tau_ref, q_scr):
    K = PEER_TOPK
    nk = PEER_NKEYS
    q_scr[...] = _dot(wq_ref[...], ht_ref[...])

    def head(h, carry):
        base = pl.multiple_of(h * 2 * nk, 2 * nk)
        s1 = _dot(k1_ref[...], q_scr[pl.ds(base, nk), :].astype(BF16))
        s2 = _dot(k2_ref[...], q_scr[pl.ds(base + nk, nk), :].astype(BF16))
        v1 = _top_rows(s1, K)
        v2 = _top_rows(s2, K)
        v2a = jnp.concatenate(v2, axis=0)
        cand = [v1[a] + v2a for a in range(K)]
        thr = _top_rows(jnp.concatenate(cand, axis=0), K)[K - 1]
        e2a = jnp.exp(v2a - v2[0])
        z = jnp.zeros_like(thr)
        tau = jnp.full(s1.shape, jnp.inf, F32)
        for a in reversed(range(K)):
            sel = cand[a] >= thr
            z = z + jnp.sum(jnp.where(sel, jnp.exp(v1[a] - v1[0]) * e2a, 0.0), axis=0, keepdims=True)
            tau_a = jnp.min(jnp.where(sel, v2a, jnp.inf), axis=0, keepdims=True)
            tau = jnp.where(s1 >= v1[a], tau_a, tau)
        s2_ref[h] = s2
        e2_ref[h] = jnp.exp(s2 - v2[0]) / z
        e1_ref[h] = jnp.exp(s1 - v1[0])
        tau_ref[h] = tau
        return carry

    lax.fori_loop(0, PEER_HEADS, head, 0)


def _peer_prep(ht, wq_t, k1, k2):
    d, n = ht.shape
    T = PREP_TOK_TILE
    nk = PEER_NKEYS
    full = lambda a: pl.BlockSpec(a.shape, lambda i: (0,) * a.ndim)
    spec = pl.BlockSpec((PEER_HEADS, nk, T), lambda i: (0, 0, i))
    return pl.pallas_call(
        _prep_kernel,
        out_shape=[jax.ShapeDtypeStruct((PEER_HEADS, nk, n), F32)] * 4,
        grid=(n // T,),
        in_specs=[pl.BlockSpec((d, T), lambda i: (0, i)), full(wq_t), full(k1), full(k2)],
        out_specs=[spec] * 4,
        scratch_shapes=[pltpu.VMEM((wq_t.shape[0], T), F32)],
        compiler_params=_cparams("arbitrary"),
        name="peer_prep",
    )(ht, wq_t, k1, k2)


def _peer_kernel(ht_ref, u_ref, vt_ref, s2_ref, e2_ref, e1_ref, tau_ref, x_ref, g_ref, o_ref,
                 acc_ref, a_ref, p_ref):
    nk = PEER_NKEYS
    c = pl.program_id(1)
    nc, T = a_ref.shape

    @pl.when(c == 0)
    def _():
        acc_ref[...] = jnp.zeros(acc_ref.shape, F32)

    a_ref[...] = _dot(u_ref[...], ht_ref[...])
    n_sub = nc // nk
    base = pl.multiple_of(c * n_sub, n_sub)
    for j in range(n_sub):
        for t in range(T // LANES):
            ts = slice(t * LANES, (t + 1) * LANES)
            gate = jnp.zeros((nk, LANES), F32)
            for h in range(PEER_HEADS):
                tau = tau_ref[h, pl.ds(base, n_sub), ts][j:j + 1]
                e1 = e1_ref[h, pl.ds(base, n_sub), ts][j:j + 1]
                gate = gate + jnp.where(s2_ref[h, :, ts] >= tau, e2_ref[h, :, ts] * e1, 0.0)
            act = jax.nn.gelu(a_ref[j * nk:(j + 1) * nk, ts])
            p_ref[j * nk:(j + 1) * nk, ts] = (gate * act).astype(BF16)
    acc_ref[...] += _dot(vt_ref[...], p_ref[...])

    @pl.when(c == pl.num_programs(1) - 1)
    def _():
        o_ref[...] = x_ref[...] + g_ref[...] * acc_ref[...].T


def _peer_dense(ht, u_bf, vt_bf, s2, e2, e1, tau, x, g2, tok_tile, tok_off, n_tok):
    d = ht.shape[0]
    n_exp = u_bf.shape[0]
    T = tok_tile
    NC = PEER_EXPERT_CHUNK
    nk = PEER_NKEYS
    rt = pl.BlockSpec((PEER_HEADS, nk, T), lambda i, c: (0, 0, i + tok_off))
    return pl.pallas_call(
        _peer_kernel,
        out_shape=jax.ShapeDtypeStruct((n_tok, d), F32),
        grid=(n_tok // T, n_exp // NC),
        in_specs=[pl.BlockSpec((d, T), lambda i, c: (0, i + tok_off)),
                  pl.BlockSpec((NC, d), lambda i, c: (c, 0)),
                  pl.BlockSpec((d, NC), lambda i, c: (0, c)),
                  rt, rt, rt, rt,
                  pl.BlockSpec((T, d), lambda i, c: (i + tok_off, 0)),
                  pl.BlockSpec((1, d), lambda i, c: (0, 0))],
        out_specs=pl.BlockSpec((T, d), lambda i, c: (i, 0)),
        scratch_shapes=[pltpu.VMEM((d, T), F32), pltpu.VMEM((NC, T), F32), pltpu.VMEM((NC, T), BF16)],
        compiler_params=_cparams("arbitrary", "arbitrary"),
        name="peer_dense",
    )(ht, u_bf, vt_bf, s2, e2, e1, tau, x, g2)


def _rope_tables(L, n):
    t = jnp.arange(L)
    row = (t // GRID_W).astype(F32)
    col = (t % GRID_W).astype(F32)
    q = HEAD_DIM // 4
    inv = ROPE_THETA ** (-jnp.arange(q, dtype=F32) / q)
    ar = row[:, None] * inv
    ac = col[:, None] * inv
    cos = jnp.concatenate([jnp.cos(ar), jnp.cos(ar), jnp.cos(ac), jnp.cos(ac)], axis=-1)
    sin = jnp.concatenate([-jnp.sin(ar), jnp.sin(ar), -jnp.sin(ac), jnp.sin(ac)], axis=-1)
    cos = jnp.concatenate([cos, jnp.ones((n - L, HEAD_DIM), F32)], axis=0)
    sin = jnp.concatenate([sin, jnp.zeros((n - L, HEAD_DIM), F32)], axis=0)
    return jnp.tile(cos, (1, LANES // HEAD_DIM)), jnp.tile(sin, (1, LANES // HEAD_DIM))


def _block_diag(w):
    nb, di, do = w.shape
    eye = jnp.eye(nb, dtype=w.dtype)
    return (eye[:, None, :, None] * w[:, :, None, :]).reshape(nb * di, nb * do)


def kernel(x, c, ctx, c_ctx, w_mod, b_mod, norm1_g, norm2_g, w_in, w_out, lru_conv_w, lru_conv_b, lru_wa, lru_ba, lru_wx, lru_bx, lru_lam, win_qn_g, win_kn_g, win_sink, na_qn_g, na_kn_g, na_rpb, peer_wq, peer_k1, peer_k2, peer_u, peer_v):
    B, L, D = x.shape
    Lc = ctx.shape[1]
    depth = w_mod.shape[0]
    assert B == 1 and Lc == TOK_TILE and L % PEER_TOK_TILE == 0 and L // GRID_W >= NA_KH
    n = L + Lc
    n_lat = L // TOK_TILE
    rows = L // GRID_W

    cvec = jnp.zeros((8, D), F32).at[0].set(c[0]).at[1].set(c_ctx)
    mods = _modulation(cvec, w_mod, b_mod)
    cos_t, sin_t = _rope_tables(L, n)
    tile2 = lambda g: jnp.tile(g, LANES // HEAD_DIM)

    xl, xc = x[0], ctx[0]
    for l in range(depth):
        with_ctx = l < depth - 1
        mod = mods[l]
        hg = jnp.zeros((8, LANES), F32)
        hg = hg.at[0].set(tile2(win_qn_g[l])).at[1].set(tile2(win_kn_g[l]))
        hg = hg.at[2].set(tile2(na_qn_g[l])).at[3].set(tile2(na_kn_g[l]))
        xa, ga, qw, kw, vw, qn, kn, vn = _in_proj(xl, xc, mod, norm1_g[l][None], w_in[l].astype(BF16),
                                                  cos_t, sin_t, hg)

        w_gates = jnp.concatenate([_block_diag(lru_wa[l, 0]), _block_diag(lru_wx[l, 0]),
                                   _block_diag(lru_wa[l, 1]), _block_diag(lru_wx[l, 1])], axis=1).astype(BF16)
        b_gates = jnp.concatenate([lru_ba[l, 0], lru_bx[l, 0], lru_ba[l, 1], lru_bx[l, 1]])[None]
        conv_w = jnp.zeros((8, D_LRU), F32).at[0:lru_conv_w.shape[1]].set(lru_conv_w[l])
        lam = jnp.zeros((8, D_LRU), F32).at[0:2].set(lru_lam[l])
        hf, hb = _lru(xa, conv_w, lru_conv_b[l][None], w_gates, b_gates, lam, n_lat)

        yb = _window_attention(win_sink[l], qw, kw, vw, L, with_ctx)
        yc = _neighborhood_attention(qn, kn, vn, _na_bias_table(na_rpb[l], rows), L, with_ctx)

        xn, ht = _out_proj(xl, xc, hf, hb, ga, yb, yc, w_out[l].astype(BF16), mod, norm2_g[l][None], with_ctx)

        s2, e2, e1, tau = _peer_prep(ht, peer_wq[l].T.astype(BF16), peer_k1[l].astype(BF16),
                                     peer_k2[l].astype(BF16))
        u_bf = peer_u[l].astype(BF16)
        vt_bf = peer_v[l].T.astype(BF16)
        g2 = mod[:, 5 * D:6 * D]
        xl_new = _peer_dense(ht, u_bf, vt_bf, s2, e2, e1, tau, xn, g2[0:1], PEER_TOK_TILE, 0, L)
        if with_ctx:
            xc = _peer_dense(ht, u_bf, vt_bf, s2, e2, e1, tau, xn, g2[1:2], Lc, L // Lc, Lc)
        xl = xl_new
    return xl[None]
```

```python
import functools

import numpy as np
import jax
import jax.numpy as jnp
from jax import lax
from jax.experimental import pallas as pl
from jax.experimental.pallas import tpu as pltpu

F32 = jnp.float32
BF16 = jnp.bfloat16
U32 = jnp.uint32

HEAD_DIM = 64
GRID_W = 64
EPS = 1e-6
ROPE_THETA = 10000.0
D_LRU = 256
LRU_BLOCKS = 4
LRU_C = 8.0
WIN_Q_HEADS = 6
WIN_KV_HEADS = 2
WIN_BLOCK = 128
NA_HEADS = 6
NA_KH = 8
NA_KW = 16
PEER_HEADS = 8
PEER_NKEYS = 128
PEER_TOPK = 16
D_WIN_Q = WIN_Q_HEADS * HEAD_DIM
D_WIN_KV = WIN_KV_HEADS * HEAD_DIM
D_NA = NA_HEADS * HEAD_DIM

LANES = 128
TOK_TILE = 256
PEER_TOK_TILE = 512
PEER_EXPERT_CHUNK = 2048
PEER_SUB_CHUNK = 512
PREP_TOK_TILE = 256
VMEM_LIMIT = 56 * 1024 * 1024
NEG = -1e30
GELU_K0 = float(np.sqrt(2.0 / np.pi))
GELU_K1 = 0.044715 * GELU_K0


def _cparams(*sem):
    return pltpu.CompilerParams(dimension_semantics=sem, vmem_limit_bytes=VMEM_LIMIT)


def _dot(a, b):
    return jnp.dot(a, b, preferred_element_type=F32)


def _pack_rows(x):
    return pltpu.bitcast(x.astype(BF16), U32)


def _unpack_rows(w):
    return pltpu.bitcast(w, BF16)


def _dot_nt(a, b):
    return lax.dot_general(a, b, (((1,), (1,)), ((), ())), preferred_element_type=F32)


def _mod_kernel(c_ref, w_ref, b_ref, o_ref):
    c = c_ref[...]
    a = (c * jax.nn.sigmoid(c)).astype(BF16)
    o_ref[0] = _dot(a, w_ref[0].astype(BF16)) + b_ref[0]


def _modulation(cvec, w_mod, b_mod):
    depth, d, d6 = w_mod.shape
    return pl.pallas_call(
        _mod_kernel,
        out_shape=jax.ShapeDtypeStruct((depth, 8, d6), F32),
        grid=(depth, d6 // d),
        in_specs=[pl.BlockSpec((8, d), lambda l, j: (0, 0)),
                  pl.BlockSpec((1, d, d), lambda l, j: (l, 0, j)),
                  pl.BlockSpec((1, 1, d), lambda l, j: (l, 0, j))],
        out_specs=pl.BlockSpec((1, 8, d), lambda l, j: (l, 0, j)),
        compiler_params=_cparams("arbitrary", "arbitrary"),
        name="modulation",
    )(cvec, w_mod, b_mod.reshape(depth, 1, d6))


def _pair_mean_matrix():
    r = lax.broadcasted_iota(jnp.int32, (LANES, LANES), 0) < HEAD_DIM
    c = lax.broadcasted_iota(jnp.int32, (LANES, LANES), 1) < HEAD_DIM
    return jnp.where(r == c, 1.0 / HEAD_DIM, 0.0).astype(BF16)


def _head_norm(z, gain, bd):
    z2 = z * z
    hi = z2.astype(BF16)
    lo = (z2 - hi.astype(F32)).astype(BF16)
    ms = _dot(hi, bd) + _dot(lo, bd)
    return z * lax.rsqrt(ms + EPS) * gain


def _rope(z, cos, sin):
    lane = lax.broadcasted_iota(jnp.int32, z.shape, 1)
    first = (lane & 16) == 0
    partner = jnp.where(first, pltpu.roll(z, LANES - 16, 1), pltpu.roll(z, 16, 1))
    return z * cos + partner * sin


def _in_kernel(xl_ref, xc_ref, mod_ref, g_ref, w_ref, cos_ref, sin_ref, hg_ref,
               xa_ref, ga_ref, qw_ref, kw_ref, vw_ref, qn_ref, kn_ref, vn_ref, *, n_lat):
    d = xl_ref.shape[1]
    i = pl.program_id(0)
    row = (i == n_lat).astype(jnp.int32)
    is_ctx = jnp.full((TOK_TILE, 1), row) == 1
    x = jnp.where(is_ctx, xc_ref[...], xl_ref[...])
    ms = jnp.mean(x * x, axis=-1, keepdims=True)
    y = x * lax.rsqrt(ms + EPS) * g_ref[...]
    sh = mod_ref[pl.ds(row, 1), 0:d]
    sc = mod_ref[pl.ds(row, 1), d:2 * d]
    h = (y * (1.0 + sc) + sh).astype(BF16)
    z = _dot(h, w_ref[...])

    bd = _pair_mean_matrix()
    cos = cos_ref[...]
    sin = sin_ref[...]
    scale = HEAD_DIM ** -0.5
    o = 0
    xa_ref[...] = z[:, o:o + D_LRU]
    o += D_LRU
    ga_ref[...] = z[:, o:o + D_LRU]
    o += D_LRU
    for g in range(D_WIN_Q // LANES):
        zz = _rope(_head_norm(z[:, o:o + LANES], hg_ref[0:1], bd), cos, sin)
        qw_ref[:, g * LANES:(g + 1) * LANES] = (zz * scale).astype(BF16)
        o += LANES
    kw_ref[...] = _rope(_head_norm(z[:, o:o + LANES], hg_ref[1:2], bd), cos, sin).astype(BF16)
    o += LANES
    vw_ref[...] = z[:, o:o + LANES].astype(BF16)
    o += LANES
    for g in range(D_NA // LANES):
        zz = _head_norm(z[:, o:o + LANES], hg_ref[2:3], bd)
        qn_ref[:, g * LANES:(g + 1) * LANES] = (zz * scale).astype(BF16)
        o += LANES
    for g in range(D_NA // LANES):
        zz = _head_norm(z[:, o:o + LANES], hg_ref[3:4], bd)
        kn_ref[:, g * LANES:(g + 1) * LANES] = zz.astype(BF16)
        o += LANES
    vn_ref[...] = z[:, o:o + D_NA].astype(BF16)


def _in_proj(xl, xc, mod, g, w_bf, cos_t, sin_t, hg):
    L, d = xl.shape
    n_lat = L // TOK_TILE
    n = L + xc.shape[0]
    d_in = w_bf.shape[1]
    T = TOK_TILE
    tok = lambda w: pl.BlockSpec((T, w), lambda i: (i, 0))
    full = lambda a: pl.BlockSpec(a.shape, lambda i: (0,) * a.ndim)
    widths = (D_LRU, D_LRU, D_WIN_Q, D_WIN_KV, D_WIN_KV, D_NA, D_NA, D_NA)
    dtypes = (F32, F32, BF16, BF16, BF16, BF16, BF16, BF16)
    return pl.pallas_call(
        functools.partial(_in_kernel, n_lat=n_lat),
        out_shape=[jax.ShapeDtypeStruct((n, w), t) for w, t in zip(widths, dtypes)],
        grid=(n_lat + 1,),
        in_specs=[pl.BlockSpec((T, d), lambda i: (jnp.minimum(i, n_lat - 1), 0)),
                  pl.BlockSpec((T, d), lambda i: (0, 0)),
                  full(mod), full(g), full(w_bf), tok(LANES), tok(LANES), full(hg)],
        out_specs=[tok(w) for w in widths],
        compiler_params=_cparams("arbitrary"),
        name="in_proj",
    )(xl, xc, mod, g, w_bf, cos_t, sin_t, hg)


def _chunk_scan(a, b, reverse):
    T = a.shape[0]
    rows = lax.broadcasted_iota(jnp.int32, a.shape, 0)
    s = 1
    while s < T:
        if reverse:
            edge = rows >= T - s
            shift = T - s
        else:
            edge = rows < s
            shift = s
        a_s = jnp.where(edge, 1.0, pltpu.roll(a, shift, 0))
        b_s = jnp.where(edge, 0.0, pltpu.roll(b, shift, 0))
        b = a * b_s + b
        a = a * a_s
        s *= 2
    return a, b


def _lru_direction(xm, ph, nh, pv, nv, cw, cb, w, bias, spl, carry_ref, first, reverse):
    T, C = xm.shape
    rows = lax.broadcasted_iota(jnp.int32, (T, C), 0)
    p6 = ph[6:7] * pv
    p7 = ph[7:8] * pv
    n0 = nh[0:1] * nv
    x_m1 = jnp.where(rows == 0, p7, pltpu.roll(xm, 1, 0))
    x_m2 = jnp.where(rows == 0, p6, jnp.where(rows == 1, p7, pltpu.roll(xm, 2, 0)))
    x_p1 = jnp.where(rows == T - 1, n0, pltpu.roll(xm, T - 1, 0))
    u = cw[0:1] * x_m2 + cw[1:2] * x_m1 + cw[2:3] * xm + cw[3:4] * x_p1 + cb
    zz = _dot(u.astype(BF16), w) + bias
    r = jax.nn.sigmoid(zz[:, :C])
    ig = jax.nn.sigmoid(zz[:, C:])
    log_a = -LRU_C * r * spl
    a = jnp.exp(log_a)
    b = jnp.sqrt(-jnp.tanh(log_a) * (a * a + 1.0)) * ig * u
    a, b = _chunk_scan(a, b, reverse)

    @pl.when(first)
    def _():
        carry_ref[...] = jnp.zeros(carry_ref.shape, F32)

    h = a * carry_ref[0:1] + b
    edge = h[0:1] if reverse else h[T - 1:T]
    carry_ref[...] = jnp.broadcast_to(edge, carry_ref.shape)
    return h


def _lru_kernel(xf_ref, pf_ref, nf_ref, xb_ref, pb_ref, nb_ref, cw_ref, cb_ref, w_ref, b_ref,
                lam_ref, hf_ref, hb_ref, cf_ref, cbk_ref, *, n_lat):
    j = pl.program_id(0)
    C = D_LRU
    fblk = jnp.where(j == 0, n_lat, j - 1)
    bblk = jnp.where(j == 0, n_lat, n_lat - j)
    lam = lam_ref[...]
    spl = jnp.maximum(-lam, 0.0) + jnp.log1p(jnp.exp(-jnp.abs(lam)))
    cw = cw_ref[...]
    cb = cb_ref[...]

    def halo_valid(blk):
        pv = jnp.logical_and(blk != 0, blk != n_lat).astype(F32)
        nv = jnp.logical_and(blk != n_lat - 1, blk != n_lat).astype(F32)
        return pv, nv

    pv, nv = halo_valid(fblk)
    hf_ref[...] = _lru_direction(xf_ref[...], pf_ref[...], nf_ref[...], pv, nv, cw, cb,
                                 w_ref[:, 0:2 * C], b_ref[:, 0:2 * C], spl[0:1], cf_ref, j == 0, False)
    pv, nv = halo_valid(bblk)
    hb_ref[...] = _lru_direction(xb_ref[...], pb_ref[...], nb_ref[...], pv, nv, cw, cb,
                                 w_ref[:, 2 * C:4 * C], b_ref[:, 2 * C:4 * C], spl[1:2], cbk_ref, j == 0, True)


def _lru(xa, conv_w, conv_b, w_gates, b_gates, lam, n_lat):
    n, C = xa.shape
    T = TOK_TILE
    sub = T // 8
    nblk8 = n // 8
    fblk = lambda j: jnp.where(j == 0, n_lat, j - 1)
    bblk = lambda j: jnp.where(j == 0, n_lat, n_lat - j)
    prev8 = lambda blk: jnp.maximum(blk * sub - 1, 0)
    next8 = lambda blk: jnp.minimum((blk + 1) * sub, nblk8 - 1)
    full = lambda a: pl.BlockSpec(a.shape, lambda j: (0,) * a.ndim)
    return pl.pallas_call(
        functools.partial(_lru_kernel, n_lat=n_lat),
        out_shape=[jax.ShapeDtypeStruct((n, C), F32)] * 2,
        grid=(n_lat + 1,),
        in_specs=[pl.BlockSpec((T, C), lambda j: (fblk(j), 0)),
                  pl.BlockSpec((8, C), lambda j: (prev8(fblk(j)), 0)),
                  pl.BlockSpec((8, C), lambda j: (next8(fblk(j)), 0)),
                  pl.BlockSpec((T, C), lambda j: (bblk(j), 0)),
                  pl.BlockSpec((8, C), lambda j: (prev8(bblk(j)), 0)),
                  pl.BlockSpec((8, C), lambda j: (next8(bblk(j)), 0)),
                  full(conv_w), full(conv_b), full(w_gates), full(b_gates), full(lam)],
        out_specs=[pl.BlockSpec((T, C), lambda j: (fblk(j), 0)),
                   pl.BlockSpec((T, C), lambda j: (bblk(j), 0))],
        scratch_shapes=[pltpu.VMEM((8, C), F32), pltpu.VMEM((8, C), F32)],
        compiler_params=_cparams("arbitrary"),
        name="rglru",
    )(xa, xa, xa, xa, xa, xa, conv_w, conv_b, w_gates, b_gates, lam)


def _win_kernel(sink_ref, q_ref, kp_ref, kc_ref, kn_ref, vp_ref, vc_ref, vn_ref, kx_ref, vx_ref,
                o_ref, *, nb):
    W = WIN_BLOCK
    R = WIN_Q_HEADS // WIN_KV_HEADS
    b = pl.program_id(0)
    blk = jnp.where(b < nb, b, -4)
    rows = lax.broadcasted_iota(jnp.int32, (R * W, 3 * W), 0)
    cols = lax.broadcasted_iota(jnp.int32, (R * W, 3 * W), 1)
    kpos = (blk - 1) * W + cols
    qpos = blk * W + (rows & (W - 1))
    valid = jnp.logical_and(jnp.abs(kpos - qpos) <= W, jnp.logical_and(kpos >= 0, kpos < nb * W))
    row1 = lax.broadcasted_iota(jnp.int32, (R * W, 1), 0)
    q = q_ref[...]
    for g in range(WIN_KV_HEADS):
        ls = slice(g * HEAD_DIM, (g + 1) * HEAD_DIM)
        qs = jnp.concatenate([q[:, (g * R + r) * HEAD_DIM:(g * R + r + 1) * HEAD_DIM] for r in range(R)], axis=0)
        kl = jnp.concatenate([kp_ref[:, ls], kc_ref[:, ls], kn_ref[:, ls]], axis=0)
        vl = jnp.concatenate([vp_ref[:, ls], vc_ref[:, ls], vn_ref[:, ls]], axis=0)
        s_loc = jnp.where(valid, _dot_nt(qs, kl), NEG)
        s_ctx = _dot_nt(qs, kx_ref[:, ls])
        sink = jnp.full((R * W, 1), sink_ref[g * R], F32)
        for r in range(1, R):
            sink = jnp.where(row1 >= r * W, sink_ref[g * R + r], sink)
        m = jnp.maximum(jnp.maximum(jnp.max(s_loc, axis=-1, keepdims=True),
                                    jnp.max(s_ctx, axis=-1, keepdims=True)), sink)
        p_loc = jnp.exp(s_loc - m)
        p_ctx = jnp.exp(s_ctx - m)
        den = (jnp.sum(p_loc, axis=-1, keepdims=True) + jnp.sum(p_ctx, axis=-1, keepdims=True)
               + jnp.exp(sink - m))
        o = (_dot(p_loc.astype(BF16), vl) + _dot(p_ctx.astype(BF16), vx_ref[:, ls])) / den
        for r in range(R):
            hh = g * R + r
            o_ref[:, hh * HEAD_DIM:(hh + 1) * HEAD_DIM] = o[r * W:(r + 1) * W].astype(o_ref.dtype)


def _window_attention(sink, qw, kw, vw, L, with_ctx):
    n = qw.shape[0]
    W = WIN_BLOCK
    nb = L // W
    nq = n // W if with_ctx else nb
    ctx_blk = L // TOK_TILE
    lat = lambda b: jnp.minimum(b, nb - 1)
    kv = lambda f: pl.BlockSpec((W, D_WIN_KV), lambda b: (f(b), 0))
    prv = lambda b: jnp.maximum(lat(b) - 1, 0)
    nxt = lambda b: jnp.minimum(lat(b) + 1, nb - 1)
    ctx = pl.BlockSpec((n - L, D_WIN_KV), lambda b: (ctx_blk, 0))
    return pl.pallas_call(
        functools.partial(_win_kernel, nb=nb),
        out_shape=jax.ShapeDtypeStruct((nq * W, D_WIN_Q), BF16),
        grid=(nq,),
        in_specs=[pl.BlockSpec(memory_space=pltpu.SMEM),
                  pl.BlockSpec((W, D_WIN_Q), lambda b: (b, 0)),
                  kv(prv), kv(lat), kv(nxt), kv(prv), kv(lat), kv(nxt), ctx, ctx],
        out_specs=pl.BlockSpec((W, D_WIN_Q), lambda b: (b, 0)),
        compiler_params=_cparams("arbitrary"),
        name="window_attn",
    )(sink, qw, kw, kw, kw, vw, vw, vw, kw, vw)


def _na_kernel(q_ref, kl_ref, vl_ref, kx_ref, vx_ref, bias_ref, o_ref, *, rows):
    b = pl.program_id(0)
    latent = jnp.full((GRID_W, 1), (b < rows).astype(jnp.int32)) == 1
    q = q_ref[...]
    for h in range(NA_HEADS):
        ls = slice(h * HEAD_DIM, (h + 1) * HEAD_DIM)
        qh = q[:, ls]
        s_loc = jnp.where(latent, _dot_nt(qh, kl_ref[:, ls]) + bias_ref[0, h], NEG)
        s_ctx = _dot_nt(qh, kx_ref[:, ls])
        m = jnp.maximum(jnp.max(s_loc, axis=-1, keepdims=True), jnp.max(s_ctx, axis=-1, keepdims=True))
        p_loc = jnp.exp(s_loc - m)
        p_ctx = jnp.exp(s_ctx - m)
        den = jnp.sum(p_loc, axis=-1, keepdims=True) + jnp.sum(p_ctx, axis=-1, keepdims=True)
        o = (_dot(p_loc.astype(BF16), vl_ref[:, ls]) + _dot(p_ctx.astype(BF16), vx_ref[:, ls])) / den
        o_ref[:, ls] = o.astype(o_ref.dtype)


def _na_bias_table(rpb, rows):
    qc = np.arange(GRID_W)[:, None]
    kc = np.arange(GRID_W)[None, :]
    qstart = np.clip(qc - NA_KW // 2, 0, GRID_W - NA_KW)
    inside = (kc - qstart >= 0) & (kc - qstart < NA_KW)
    dc = np.clip(kc - qc + NA_KW - 1, 0, 2 * NA_KW - 2)
    per_dr = jnp.where(inside[None, None], rpb[:, :, dc].astype(F32), NEG)
    cases = []
    for c in range(NA_KH):
        if c < NA_KH // 2:
            dr = np.arange(NA_KH) - c + NA_KH - 1
        elif c == NA_KH // 2:
            dr = np.arange(NA_KH) + NA_KH // 2 - 1
        else:
            dr = np.arange(NA_KH) + NA_KH - 1 - c
        blk = per_dr[:, dr]
        cases.append(jnp.transpose(blk, (0, 2, 1, 3)).reshape(NA_HEADS, GRID_W, NA_KH * GRID_W))
    return jnp.stack(cases)


def _neighborhood_attention(qn, kn, vn, bias, L, with_ctx):
    n = qn.shape[0]
    rows = L // GRID_W
    nr = n // GRID_W if with_ctx else rows
    ctx_blk = L // TOK_TILE
    half = NA_KH // 2

    def row_of(b):
        return jnp.minimum(b, rows - 1)

    def key_start(b):
        return jnp.clip(row_of(b) - half, 0, rows - NA_KH) * GRID_W

    def case_of(b):
        r = row_of(b)
        return jnp.where(r < half, r, jnp.where(r <= rows - half, half, r - (rows - NA_KH)))

    loc = pl.BlockSpec((pl.Element(NA_KH * GRID_W), pl.Element(D_NA)), lambda b: (key_start(b), 0))
    ctx = pl.BlockSpec((n - L, D_NA), lambda b: (ctx_blk, 0))
    return pl.pallas_call(
        functools.partial(_na_kernel, rows=rows),
        out_shape=jax.ShapeDtypeStruct((nr * GRID_W, D_NA), BF16),
        grid=(nr,),
        in_specs=[pl.BlockSpec((GRID_W, D_NA), lambda b: (b, 0)), loc, loc, ctx, ctx,
                  pl.BlockSpec((1, NA_HEADS, GRID_W, NA_KH * GRID_W), lambda b: (case_of(b), 0, 0, 0))],
        out_specs=pl.BlockSpec((GRID_W, D_NA), lambda b: (b, 0)),
        compiler_params=_cparams("arbitrary"),
        name="neighborhood_attn",
    )(qn, kn, vn, kn, vn, bias)


def _out_kernel(xl_ref, xc_ref, hf_ref, hb_ref, ga_ref, yb_ref, yc_ref, w_ref, mod_ref, g_ref,
                xo_ref, ht_ref, *, n_lat):
    d = xl_ref.shape[1]
    i = pl.program_id(0)
    row = (i == n_lat).astype(jnp.int32)
    is_ctx = jnp.full((TOK_TILE, 1), row) == 1
    x = jnp.where(is_ctx, xc_ref[...], xl_ref[...])
    ya = ((hf_ref[...] + hb_ref[...]) * jax.nn.gelu(ga_ref[...])).astype(BF16)
    o1 = D_LRU
    o2 = D_LRU + D_WIN_Q
    mix = (_dot(ya, w_ref[0:o1]) + _dot(yb_ref[...], w_ref[o1:o2]) + _dot(yc_ref[...], w_ref[o2:o2 + D_NA]))
    xn = x + mod_ref[pl.ds(row, 1), 2 * d:3 * d] * mix
    xo_ref[...] = xn
    ms = jnp.mean(xn * xn, axis=-1, keepdims=True)
    y = xn * lax.rsqrt(ms + EPS) * g_ref[...]
    h2 = y * (1.0 + mod_ref[pl.ds(row, 1), 4 * d:5 * d]) + mod_ref[pl.ds(row, 1), 3 * d:4 * d]
    ht_ref[...] = h2.T.astype(BF16)


def _out_proj(xl, xc, hf, hb, ga, yb, yc, w_bf, mod, g, with_ctx):
    L, d = xl.shape
    T = TOK_TILE
    n_lat = L // T
    nt = n_lat + 1 if with_ctx else n_lat
    tok = lambda w: pl.BlockSpec((T, w), lambda i: (i, 0))
    full = lambda a: pl.BlockSpec(a.shape, lambda i: (0,) * a.ndim)
    return pl.pallas_call(
        functools.partial(_out_kernel, n_lat=n_lat),
        out_shape=[jax.ShapeDtypeStruct((nt * T, d), F32), jax.ShapeDtypeStruct((d, nt * T), BF16)],
        grid=(nt,),
        in_specs=[pl.BlockSpec((T, d), lambda i: (jnp.minimum(i, n_lat - 1), 0)),
                  pl.BlockSpec((T, d), lambda i: (0, 0)),
                  tok(D_LRU), tok(D_LRU), tok(D_LRU), tok(D_WIN_Q), tok(D_NA),
                  full(w_bf), full(mod), full(g)],
        out_specs=[tok(d), pl.BlockSpec((d, T), lambda i: (0, i))],
        compiler_params=_cparams("arbitrary"),
        name="out_proj",
    )(xl, xc, hf, hb, ga, yb, yc, w_bf, mod, g)


def _top_rows(s, k):
    out = []
    cur = s
    for _ in range(k):
        mk = jnp.max(cur, axis=0, keepdims=True)
        out.append(mk)
        cur = jnp.where(cur == mk, -jnp.inf, cur)
    return out


def _prep_kernel(ht_ref, wq_ref, k1_ref, k2_ref, r2_ref, e2_ref, c1_ref, e1_ref, q_scr):
    K = PEER_TOPK
    nk = PEER_NKEYS
    q_scr[...] = _dot(wq_ref[...], ht_ref[...])

    def head(h, carry):
        base = pl.multiple_of(h * 2 * nk, 2 * nk)
        s1 = _dot(k1_ref[...], q_scr[pl.ds(base, nk), :].astype(BF16))
        s2 = _dot(k2_ref[...], q_scr[pl.ds(base + nk, nk), :].astype(BF16))
        v1 = _top_rows(s1, K)
        v2 = _top_rows(s2, K)
        v2a = jnp.concatenate(v2, axis=0)
        cand = [v1[a] + v2a for a in range(K)]
        thr = _top_rows(jnp.concatenate(cand, axis=0), K)[K - 1]
        e2a = jnp.exp(v2a - v2[0])
        z = jnp.zeros_like(thr)
        c1 = jnp.zeros(s1.shape, F32)
        r2 = jnp.zeros(s2.shape, F32)
        for a in reversed(range(K)):
            sel = cand[a] >= thr
            z = z + jnp.sum(jnp.where(sel, jnp.exp(v1[a] - v1[0]) * e2a, 0.0), axis=0, keepdims=True)
            cnt = jnp.sum(jnp.where(sel, 1.0, 0.0), axis=0, keepdims=True)
            c1 = jnp.where(s1 >= v1[a], cnt, c1)
            r2 = r2 + jnp.where(s2 < v2[a], 1.0, 0.0)
        r2_ref[h] = _pack_rows(r2)
        e2_ref[h] = _pack_rows(jnp.exp(s2 - v2[0]) / z)
        c1_ref[h] = c1
        e1_ref[h] = jnp.exp(s1 - v1[0])
        return carry

    lax.fori_loop(0, PEER_HEADS, head, 0)


def _peer_prep(ht, wq_t, k1, k2):
    d, n = ht.shape
    T = PREP_TOK_TILE
    nk = PEER_NKEYS
    full = lambda a: pl.BlockSpec(a.shape, lambda i: (0,) * a.ndim)
    spec = lambda rows: pl.BlockSpec((PEER_HEADS, rows, T), lambda i: (0, 0, i))
    return pl.pallas_call(
        _prep_kernel,
        out_shape=[jax.ShapeDtypeStruct((PEER_HEADS, nk // 2, n), U32)] * 2
        + [jax.ShapeDtypeStruct((PEER_HEADS, nk, n), F32)] * 2,
        grid=(n // T,),
        in_specs=[pl.BlockSpec((d, T), lambda i: (0, i)), full(wq_t), full(k1), full(k2)],
        out_specs=[spec(nk // 2), spec(nk // 2), spec(nk), spec(nk)],
        scratch_shapes=[pltpu.VMEM((wq_t.shape[0], T), F32)],
        compiler_params=_cparams("arbitrary"),
        name="peer_prep",
    )(ht, wq_t, k1, k2)


def _peer_kernel(ht_ref, u_ref, un_ref, vt_ref, vp_ref, r2_ref, e2_ref, c1_ref, e1_ref, x_ref, g_ref,
                 o_ref, acc_ref, *ap_refs):
    nk = PEER_NKEYS
    c = pl.program_id(1)
    n_sub = len(ap_refs) // 2
    a_refs, p_refs = ap_refs[:n_sub], ap_refs[n_sub:]
    sub = 2 * a_refs[0].shape[0]
    T = a_refs[0].shape[1]
    rows_per_sub = sub // nk
    n_rows = n_sub * rows_per_sub
    PK = 16
    WK = PK // 2
    base = pl.multiple_of(c * n_rows, n_rows)

    def activations(s):
        a_refs[s][...] = _pack_rows(_dot(u_ref[s * sub:(s + 1) * sub, :], ht_ref[...]))

    def gates(s):
        for j in range(rows_per_sub):
            row = s * rows_per_sub + j
            for t in range(T // LANES):
                ts = slice(t * LANES, (t + 1) * LANES)
                gate = [None] * (nk // PK)
                for h in range(PEER_HEADS):
                    c1 = c1_ref[h, pl.ds(base, n_rows), ts][row:row + 1]
                    e1 = e1_ref[h, pl.ds(base, n_rows), ts][row:row + 1]
                    c1 = jnp.broadcast_to(c1, (PK, LANES)).astype(BF16)
                    e1 = jnp.broadcast_to(e1, (PK, LANES)).astype(BF16)
                    for v in range(nk // PK):
                        ws = slice(v * WK, (v + 1) * WK)
                        r2 = _unpack_rows(r2_ref[h, ws, ts])
                        e2 = _unpack_rows(e2_ref[h, ws, ts])
                        g = jnp.where(r2 < c1, e2 * e1, 0.0)
                        gate[v] = g if h == 0 else gate[v] + g
                for v in range(nk // PK):
                    ws = slice((j * nk + v * PK) // 2, (j * nk + (v + 1) * PK) // 2)
                    a = _unpack_rows(a_refs[s][ws, ts])
                    th = jnp.tanh(a * (GELU_K0 + GELU_K1 * (a * a)))
                    p_refs[s][ws, ts] = pltpu.bitcast(gate[v] * (a * (0.5 + 0.5 * th)), U32)

    def project(s):
        acc_ref[...] += _dot(vt_ref[:, s * sub:(s + 1) * sub], _unpack_rows(p_refs[s][...]))

    last = n_sub - 1

    def project_deferred():
        acc_ref[...] += _dot(vp_ref[...], _unpack_rows(p_refs[last][...]))

    @pl.when(c == 0)
    def _():
        acc_ref[...] = jnp.zeros(acc_ref.shape, F32)
        p_refs[last][...] = jnp.zeros(p_refs[last].shape, U32)
        activations(0)

    for s in range(n_sub):
        gates(s)
        if s == 0:
            project_deferred()
        else:
            project(s - 1)
        if s < last:
            activations(s + 1)
    a_refs[0][...] = _pack_rows(_dot(un_ref[...], ht_ref[...]))

    @pl.when(c == pl.num_programs(1) - 1)
    def _():
        project(last)
        o_ref[...] = x_ref[...] + g_ref[...] * acc_ref[...].T


def _peer_dense(ht, u_bf, vt_bf, r2, e2, c1, e1, x, g2, tok_tile, tok_off, n_tok):
    d = ht.shape[0]
    n_exp = u_bf.shape[0]
    T = tok_tile
    NC = PEER_EXPERT_CHUNK
    nk = PEER_NKEYS
    SUB = PEER_SUB_CHUNK
    n_sub = NC // SUB
    rt = lambda rows: pl.BlockSpec((PEER_HEADS, rows, T), lambda i, c: (0, 0, i + tok_off))
    return pl.pallas_call(
        _peer_kernel,
        out_shape=jax.ShapeDtypeStruct((n_tok, d), F32),
        grid=(n_tok // T, n_exp // NC),
        in_specs=[pl.BlockSpec((d, T), lambda i, c: (0, i + tok_off)),
                  pl.BlockSpec((NC, d), lambda i, c: (c, 0)),
                  pl.BlockSpec((SUB, d), lambda i, c: (jnp.minimum((c + 1) * n_sub, n_exp // SUB - 1), 0)),
                  pl.BlockSpec((d, NC), lambda i, c: (0, c)),
                  pl.BlockSpec((d, SUB), lambda i, c: (0, jnp.maximum(c * n_sub - 1, 0))),
                  rt(nk // 2), rt(nk // 2), rt(nk), rt(nk),
                  pl.BlockSpec((T, d), lambda i, c: (i + tok_off, 0)),
                  pl.BlockSpec((1, d), lambda i, c: (0, 0))],
        out_specs=pl.BlockSpec((T, d), lambda i, c: (i, 0)),
        scratch_shapes=[pltpu.VMEM((d, T), F32)] + [pltpu.VMEM((PEER_SUB_CHUNK // 2, T), U32)] * (2 * n_sub),
        compiler_params=_cparams("arbitrary", "arbitrary"),
        name="peer_dense",
    )(ht, u_bf, u_bf, vt_bf, vt_bf, r2, e2, c1, e1, x, g2)


def _rope_tables(L, n):
    t = jnp.arange(L)
    row = (t // GRID_W).astype(F32)
    col = (t % GRID_W).astype(F32)
    q = HEAD_DIM // 4
    inv = ROPE_THETA ** (-jnp.arange(q, dtype=F32) / q)
    ar = row[:, None] * inv
    ac = col[:, None] * inv
    cos = jnp.concatenate([jnp.cos(ar), jnp.cos(ar), jnp.cos(ac), jnp.cos(ac)], axis=-1)
    sin = jnp.concatenate([-jnp.sin(ar), jnp.sin(ar), -jnp.sin(ac), jnp.sin(ac)], axis=-1)
    cos = jnp.concatenate([cos, jnp.ones((n - L, HEAD_DIM), F32)], axis=0)
    sin = jnp.concatenate([sin, jnp.zeros((n - L, HEAD_DIM), F32)], axis=0)
    return jnp.tile(cos, (1, LANES // HEAD_DIM)), jnp.tile(sin, (1, LANES // HEAD_DIM))


def _block_diag(w):
    nb, di, do = w.shape
    eye = jnp.eye(nb, dtype=w.dtype)
    return (eye[:, None, :, None] * w[:, :, None, :]).reshape(nb * di, nb * do)


def kernel(x, c, ctx, c_ctx, w_mod, b_mod, norm1_g, norm2_g, w_in, w_out, lru_conv_w, lru_conv_b, lru_wa, lru_ba, lru_wx, lru_bx, lru_lam, win_qn_g, win_kn_g, win_sink, na_qn_g, na_kn_g, na_rpb, peer_wq, peer_k1, peer_k2, peer_u, peer_v):
    B, L, D = x.shape
    Lc = ctx.shape[1]
    depth = w_mod.shape[0]
    assert B == 1 and Lc == TOK_TILE and L % PEER_TOK_TILE == 0 and L // GRID_W >= NA_KH
    n = L + Lc
    n_lat = L // TOK_TILE
    rows = L // GRID_W

    cvec = jnp.zeros((8, D), F32).at[0].set(c[0]).at[1].set(c_ctx)
    mods = _modulation(cvec, w_mod, b_mod)
    cos_t, sin_t = _rope_tables(L, n)
    tile2 = lambda g: jnp.tile(g, LANES // HEAD_DIM)

    xl, xc = x[0], ctx[0]
    for l in range(depth):
        with_ctx = l < depth - 1
        mod = mods[l]
        hg = jnp.zeros((8, LANES), F32)
        hg = hg.at[0].set(tile2(win_qn_g[l])).at[1].set(tile2(win_kn_g[l]))
        hg = hg.at[2].set(tile2(na_qn_g[l])).at[3].set(tile2(na_kn_g[l]))
        xa, ga, qw, kw, vw, qn, kn, vn = _in_proj(xl, xc, mod, norm1_g[l][None], w_in[l].astype(BF16),
                                                  cos_t, sin_t, hg)

        w_gates = jnp.concatenate([_block_diag(lru_wa[l, 0]), _block_diag(lru_wx[l, 0]),
                                   _block_diag(lru_wa[l, 1]), _block_diag(lru_wx[l, 1])], axis=1).astype(BF16)
        b_gates = jnp.concatenate([lru_ba[l, 0], lru_bx[l, 0], lru_ba[l, 1], lru_bx[l, 1]])[None]
        conv_w = jnp.zeros((8, D_LRU), F32).at[0:lru_conv_w.shape[1]].set(lru_conv_w[l])
        lam = jnp.zeros((8, D_LRU), F32).at[0:2].set(lru_lam[l])
        hf, hb = _lru(xa, conv_w, lru_conv_b[l][None], w_gates, b_gates, lam, n_lat)

        yb = _window_attention(win_sink[l], qw, kw, vw, L, with_ctx)
        yc = _neighborhood_attention(qn, kn, vn, _na_bias_table(na_rpb[l], rows), L, with_ctx)

        xn, ht = _out_proj(xl, xc, hf, hb, ga, yb, yc, w_out[l].astype(BF16), mod, norm2_g[l][None], with_ctx)

        r2, e2, c1, e1 = _peer_prep(ht, peer_wq[l].T.astype(BF16), peer_k1[l].astype(BF16),
                                    peer_k2[l].astype(BF16))
        u_bf = peer_u[l].astype(BF16)
        vt_bf = peer_v[l].T.astype(BF16)
        g2 = mod[:, 5 * D:6 * D]
        xl_new = _peer_dense(ht, u_bf, vt_bf, r2, e2, c1, e1, xn, g2[0:1], PEER_TOK_TILE, 0, L)
        if with_ctx:
            xc = _peer_dense(ht, u_bf, vt_bf, r2, e2, c1, e1, xn, g2[1:2], Lc, L // Lc, Lc)
        xl = xl_new
    return xl[None]
```

```python
import functools

import numpy as np
import jax
import jax.numpy as jnp
from jax import lax
from jax.experimental import pallas as pl
from jax.experimental.pallas import tpu as pltpu

F32 = jnp.float32
BF16 = jnp.bfloat16
U32 = jnp.uint32

HEAD_DIM = 64
GRID_W = 64
EPS = 1e-6
ROPE_THETA = 10000.0
D_LRU = 256
LRU_BLOCKS = 4
LRU_C = 8.0
WIN_Q_HEADS = 6
WIN_KV_HEADS = 2
WIN_BLOCK = 128
NA_HEADS = 6
NA_KH = 8
NA_KW = 16
PEER_HEADS = 8
PEER_NKEYS = 128
PEER_TOPK = 16
D_WIN_Q = WIN_Q_HEADS * HEAD_DIM
D_WIN_KV = WIN_KV_HEADS * HEAD_DIM
D_NA = NA_HEADS * HEAD_DIM

LANES = 128
TOK_TILE = 256
PEER_TOK_TILE = 512
PEER_EXPERT_CHUNK = 2048
PEER_SUB_CHUNK = 512
PREP_TOK_TILE = 256
VMEM_LIMIT = 56 * 1024 * 1024
NEG = -1e30
GELU_K0 = float(np.sqrt(2.0 / np.pi))
GELU_K1 = 0.044715 * GELU_K0


def _cparams(*sem):
    return pltpu.CompilerParams(dimension_semantics=sem, vmem_limit_bytes=VMEM_LIMIT)


def _dot(a, b):
    return jnp.dot(a, b, preferred_element_type=F32)


def _pack_rows(x):
    return pltpu.bitcast(x.astype(BF16), U32)


def _unpack_rows(w):
    return pltpu.bitcast(w, BF16)


def _dot_nt(a, b):
    return lax.dot_general(a, b, (((1,), (1,)), ((), ())), preferred_element_type=F32)


def _mod_kernel(c_ref, w_ref, b_ref, o_ref):
    c = c_ref[...]
    a = (c * jax.nn.sigmoid(c)).astype(BF16)
    o_ref[0] = _dot(a, w_ref[0].astype(BF16)) + b_ref[0]


def _modulation(cvec, w_mod, b_mod):
    depth, d, d6 = w_mod.shape
    return pl.pallas_call(
        _mod_kernel,
        out_shape=jax.ShapeDtypeStruct((depth, 8, d6), F32),
        grid=(depth, d6 // d),
        in_specs=[pl.BlockSpec((8, d), lambda l, j: (0, 0)),
                  pl.BlockSpec((1, d, d), lambda l, j: (l, 0, j)),
                  pl.BlockSpec((1, 1, d), lambda l, j: (l, 0, j))],
        out_specs=pl.BlockSpec((1, 8, d), lambda l, j: (l, 0, j)),
        compiler_params=_cparams("arbitrary", "arbitrary"),
        name="modulation",
    )(cvec, w_mod, b_mod.reshape(depth, 1, d6))


def _pair_mean_matrix():
    r = lax.broadcasted_iota(jnp.int32, (LANES, LANES), 0) < HEAD_DIM
    c = lax.broadcasted_iota(jnp.int32, (LANES, LANES), 1) < HEAD_DIM
    return jnp.where(r == c, 1.0 / HEAD_DIM, 0.0).astype(BF16)


def _head_norm(z, gain, bd):
    z2 = z * z
    hi = z2.astype(BF16)
    lo = (z2 - hi.astype(F32)).astype(BF16)
    ms = _dot(hi, bd) + _dot(lo, bd)
    return z * lax.rsqrt(ms + EPS) * gain


def _rope(z, cos, sin):
    lane = lax.broadcasted_iota(jnp.int32, z.shape, 1)
    first = (lane & 16) == 0
    partner = jnp.where(first, pltpu.roll(z, LANES - 16, 1), pltpu.roll(z, 16, 1))
    return z * cos + partner * sin


def _in_kernel(xl_ref, xc_ref, mod_ref, g_ref, w_ref, cos_ref, sin_ref, hg_ref,
               xa_ref, ga_ref, qw_ref, kw_ref, vw_ref, qn_ref, kn_ref, vn_ref, *, n_lat):
    d = xl_ref.shape[1]
    i = pl.program_id(0)
    row = (i == n_lat).astype(jnp.int32)
    is_ctx = jnp.full((TOK_TILE, 1), row) == 1
    x = jnp.where(is_ctx, xc_ref[...], xl_ref[...])
    ms = jnp.mean(x * x, axis=-1, keepdims=True)
    y = x * lax.rsqrt(ms + EPS) * g_ref[...]
    sh = mod_ref[pl.ds(row, 1), 0:d]
    sc = mod_ref[pl.ds(row, 1), d:2 * d]
    h = (y * (1.0 + sc) + sh).astype(BF16)
    z = _dot(h, w_ref[...])

    bd = _pair_mean_matrix()
    cos = cos_ref[...]
    sin = sin_ref[...]
    scale = HEAD_DIM ** -0.5
    o = 0
    xa_ref[...] = z[:, o:o + D_LRU]
    o += D_LRU
    ga_ref[...] = z[:, o:o + D_LRU]
    o += D_LRU
    for g in range(D_WIN_Q // LANES):
        zz = _rope(_head_norm(z[:, o:o + LANES], hg_ref[0:1], bd), cos, sin)
        qw_ref[:, g * LANES:(g + 1) * LANES] = (zz * scale).astype(BF16)
        o += LANES
    kw_ref[...] = _rope(_head_norm(z[:, o:o + LANES], hg_ref[1:2], bd), cos, sin).astype(BF16)
    o += LANES
    vw_ref[...] = z[:, o:o + LANES].astype(BF16)
    o += LANES
    for g in range(D_NA // LANES):
        zz = _head_norm(z[:, o:o + LANES], hg_ref[2:3], bd)
        qn_ref[:, g * LANES:(g + 1) * LANES] = (zz * scale).astype(BF16)
        o += LANES
    for g in range(D_NA // LANES):
        zz = _head_norm(z[:, o:o + LANES], hg_ref[3:4], bd)
        kn_ref[:, g * LANES:(g + 1) * LANES] = zz.astype(BF16)
        o += LANES
    vn_ref[...] = z[:, o:o + D_NA].astype(BF16)


def _in_proj(xl, xc, mod, g, w_bf, cos_t, sin_t, hg):
    L, d = xl.shape
    n_lat = L // TOK_TILE
    n = L + xc.shape[0]
    d_in = w_bf.shape[1]
    T = TOK_TILE
    tok = lambda w: pl.BlockSpec((T, w), lambda i: (i, 0))
    full = lambda a: pl.BlockSpec(a.shape, lambda i: (0,) * a.ndim)
    widths = (D_LRU, D_LRU, D_WIN_Q, D_WIN_KV, D_WIN_KV, D_NA, D_NA, D_NA)
    dtypes = (F32, F32, BF16, BF16, BF16, BF16, BF16, BF16)
    return pl.pallas_call(
        functools.partial(_in_kernel, n_lat=n_lat),
        out_shape=[jax.ShapeDtypeStruct((n, w), t) for w, t in zip(widths, dtypes)],
        grid=(n_lat + 1,),
        in_specs=[pl.BlockSpec((T, d), lambda i: (jnp.minimum(i, n_lat - 1), 0)),
                  pl.BlockSpec((T, d), lambda i: (0, 0)),
                  full(mod), full(g), full(w_bf), tok(LANES), tok(LANES), full(hg)],
        out_specs=[tok(w) for w in widths],
        compiler_params=_cparams("arbitrary"),
        name="in_proj",
    )(xl, xc, mod, g, w_bf, cos_t, sin_t, hg)


def _chunk_scan(a, b, reverse):
    T = a.shape[0]
    rows = lax.broadcasted_iota(jnp.int32, a.shape, 0)
    s = 1
    while s < T:
        if reverse:
            edge = rows >= T - s
            shift = T - s
        else:
            edge = rows < s
            shift = s
        a_s = jnp.where(edge, 1.0, pltpu.roll(a, shift, 0))
        b_s = jnp.where(edge, 0.0, pltpu.roll(b, shift, 0))
        b = a * b_s + b
        a = a * a_s
        s *= 2
    return a, b


def _lru_direction(xm, ph, nh, pv, nv, cw, cb, w, bias, spl, carry_ref, first, reverse):
    T, C = xm.shape
    rows = lax.broadcasted_iota(jnp.int32, (T, C), 0)
    p6 = ph[6:7] * pv
    p7 = ph[7:8] * pv
    n0 = nh[0:1] * nv
    x_m1 = jnp.where(rows == 0, p7, pltpu.roll(xm, 1, 0))
    x_m2 = jnp.where(rows == 0, p6, jnp.where(rows == 1, p7, pltpu.roll(xm, 2, 0)))
    x_p1 = jnp.where(rows == T - 1, n0, pltpu.roll(xm, T - 1, 0))
    u = cw[0:1] * x_m2 + cw[1:2] * x_m1 + cw[2:3] * xm + cw[3:4] * x_p1 + cb
    zz = _dot(u.astype(BF16), w) + bias
    r = jax.nn.sigmoid(zz[:, :C])
    ig = jax.nn.sigmoid(zz[:, C:])
    log_a = -LRU_C * r * spl
    a = jnp.exp(log_a)
    b = jnp.sqrt(-jnp.tanh(log_a) * (a * a + 1.0)) * ig * u
    a, b = _chunk_scan(a, b, reverse)

    @pl.when(first)
    def _():
        carry_ref[...] = jnp.zeros(carry_ref.shape, F32)

    h = a * carry_ref[0:1] + b
    edge = h[0:1] if reverse else h[T - 1:T]
    carry_ref[...] = jnp.broadcast_to(edge, carry_ref.shape)
    return h


def _lru_kernel(xf_ref, pf_ref, nf_ref, xb_ref, pb_ref, nb_ref, cw_ref, cb_ref, w_ref, b_ref,
                lam_ref, hf_ref, hb_ref, cf_ref, cbk_ref, *, n_lat):
    j = pl.program_id(0)
    C = D_LRU
    fblk = jnp.where(j == 0, n_lat, j - 1)
    bblk = jnp.where(j == 0, n_lat, n_lat - j)
    lam = lam_ref[...]
    spl = jnp.maximum(-lam, 0.0) + jnp.log1p(jnp.exp(-jnp.abs(lam)))
    cw = cw_ref[...]
    cb = cb_ref[...]

    def halo_valid(blk):
        pv = jnp.logical_and(blk != 0, blk != n_lat).astype(F32)
        nv = jnp.logical_and(blk != n_lat - 1, blk != n_lat).astype(F32)
        return pv, nv

    pv, nv = halo_valid(fblk)
    hf_ref[...] = _lru_direction(xf_ref[...], pf_ref[...], nf_ref[...], pv, nv, cw, cb,
                                 w_ref[:, 0:2 * C], b_ref[:, 0:2 * C], spl[0:1], cf_ref, j == 0, False)
    pv, nv = halo_valid(bblk)
    hb_ref[...] = _lru_direction(xb_ref[...], pb_ref[...], nb_ref[...], pv, nv, cw, cb,
                                 w_ref[:, 2 * C:4 * C], b_ref[:, 2 * C:4 * C], spl[1:2], cbk_ref, j == 0, True)


def _lru(xa, conv_w, conv_b, w_gates, b_gates, lam, n_lat):
    n, C = xa.shape
    T = TOK_TILE
    sub = T // 8
    nblk8 = n // 8
    fblk = lambda j: jnp.where(j == 0, n_lat, j - 1)
    bblk = lambda j: jnp.where(j == 0, n_lat, n_lat - j)
    prev8 = lambda blk: jnp.maximum(blk * sub - 1, 0)
    next8 = lambda blk: jnp.minimum((blk + 1) * sub, nblk8 - 1)
    full = lambda a: pl.BlockSpec(a.shape, lambda j: (0,) * a.ndim)
    return pl.pallas_call(
        functools.partial(_lru_kernel, n_lat=n_lat),
        out_shape=[jax.ShapeDtypeStruct((n, C), F32)] * 2,
        grid=(n_lat + 1,),
        in_specs=[pl.BlockSpec((T, C), lambda j: (fblk(j), 0)),
                  pl.BlockSpec((8, C), lambda j: (prev8(fblk(j)), 0)),
                  pl.BlockSpec((8, C), lambda j: (next8(fblk(j)), 0)),
                  pl.BlockSpec((T, C), lambda j: (bblk(j), 0)),
                  pl.BlockSpec((8, C), lambda j: (prev8(bblk(j)), 0)),
                  pl.BlockSpec((8, C), lambda j: (next8(bblk(j)), 0)),
                  full(conv_w), full(conv_b), full(w_gates), full(b_gates), full(lam)],
        out_specs=[pl.BlockSpec((T, C), lambda j: (fblk(j), 0)),
                   pl.BlockSpec((T, C), lambda j: (bblk(j), 0))],
        scratch_shapes=[pltpu.VMEM((8, C), F32), pltpu.VMEM((8, C), F32)],
        compiler_params=_cparams("arbitrary"),
        name="rglru",
    )(xa, xa, xa, xa, xa, xa, conv_w, conv_b, w_gates, b_gates, lam)


def _win_kernel(sink_ref, q_ref, kp_ref, kc_ref, kn_ref, vp_ref, vc_ref, vn_ref, kx_ref, vx_ref,
                o_ref, *, nb):
    W = WIN_BLOCK
    R = WIN_Q_HEADS // WIN_KV_HEADS
    b = pl.program_id(0)
    blk = jnp.where(b < nb, b, -4)
    rows = lax.broadcasted_iota(jnp.int32, (R * W, 3 * W), 0)
    cols = lax.broadcasted_iota(jnp.int32, (R * W, 3 * W), 1)
    kpos = (blk - 1) * W + cols
    qpos = blk * W + (rows & (W - 1))
    valid = jnp.logical_and(jnp.abs(kpos - qpos) <= W, jnp.logical_and(kpos >= 0, kpos < nb * W))
    row1 = lax.broadcasted_iota(jnp.int32, (R * W, 1), 0)
    q = q_ref[...]
    for g in range(WIN_KV_HEADS):
        ls = slice(g * HEAD_DIM, (g + 1) * HEAD_DIM)
        qs = jnp.concatenate([q[:, (g * R + r) * HEAD_DIM:(g * R + r + 1) * HEAD_DIM] for r in range(R)], axis=0)
        kl = jnp.concatenate([kp_ref[:, ls], kc_ref[:, ls], kn_ref[:, ls]], axis=0)
        vl = jnp.concatenate([vp_ref[:, ls], vc_ref[:, ls], vn_ref[:, ls]], axis=0)
        s_loc = jnp.where(valid, _dot_nt(qs, kl), NEG)
        s_ctx = _dot_nt(qs, kx_ref[:, ls])
        sink = jnp.full((R * W, 1), sink_ref[g * R], F32)
        for r in range(1, R):
            sink = jnp.where(row1 >= r * W, sink_ref[g * R + r], sink)
        m = jnp.maximum(jnp.maximum(jnp.max(s_loc, axis=-1, keepdims=True),
                                    jnp.max(s_ctx, axis=-1, keepdims=True)), sink)
        p_loc = jnp.exp(s_loc - m)
        p_ctx = jnp.exp(s_ctx - m)
        den = (jnp.sum(p_loc, axis=-1, keepdims=True) + jnp.sum(p_ctx, axis=-1, keepdims=True)
               + jnp.exp(sink - m))
        o = (_dot(p_loc.astype(BF16), vl) + _dot(p_ctx.astype(BF16), vx_ref[:, ls])) / den
        for r in range(R):
            hh = g * R + r
            o_ref[:, hh * HEAD_DIM:(hh + 1) * HEAD_DIM] = o[r * W:(r + 1) * W].astype(o_ref.dtype)


def _window_attention(sink, qw, kw, vw, L, with_ctx):
    n = qw.shape[0]
    W = WIN_BLOCK
    nb = L // W
    nq = n // W if with_ctx else nb
    ctx_blk = L // TOK_TILE
    lat = lambda b: jnp.minimum(b, nb - 1)
    kv = lambda f: pl.BlockSpec((W, D_WIN_KV), lambda b: (f(b), 0))
    prv = lambda b: jnp.maximum(lat(b) - 1, 0)
    nxt = lambda b: jnp.minimum(lat(b) + 1, nb - 1)
    ctx = pl.BlockSpec((n - L, D_WIN_KV), lambda b: (ctx_blk, 0))
    return pl.pallas_call(
        functools.partial(_win_kernel, nb=nb),
        out_shape=jax.ShapeDtypeStruct((nq * W, D_WIN_Q), BF16),
        grid=(nq,),
        in_specs=[pl.BlockSpec(memory_space=pltpu.SMEM),
                  pl.BlockSpec((W, D_WIN_Q), lambda b: (b, 0)),
                  kv(prv), kv(lat), kv(nxt), kv(prv), kv(lat), kv(nxt), ctx, ctx],
        out_specs=pl.BlockSpec((W, D_WIN_Q), lambda b: (b, 0)),
        compiler_params=_cparams("arbitrary"),
        name="window_attn",
    )(sink, qw, kw, kw, kw, vw, vw, vw, kw, vw)


def _na_kernel(q_ref, kl_ref, vl_ref, kx_ref, vx_ref, bias_ref, o_ref, *, rows):
    b = pl.program_id(0)
    latent = jnp.full((GRID_W, 1), (b < rows).astype(jnp.int32)) == 1
    q = q_ref[...]
    for h in range(NA_HEADS):
        ls = slice(h * HEAD_DIM, (h + 1) * HEAD_DIM)
        qh = q[:, ls]
        s_loc = jnp.where(latent, _dot_nt(qh, kl_ref[:, ls]) + bias_ref[0, h], NEG)
        s_ctx = _dot_nt(qh, kx_ref[:, ls])
        m = jnp.maximum(jnp.max(s_loc, axis=-1, keepdims=True), jnp.max(s_ctx, axis=-1, keepdims=True))
        p_loc = jnp.exp(s_loc - m)
        p_ctx = jnp.exp(s_ctx - m)
        den = jnp.sum(p_loc, axis=-1, keepdims=True) + jnp.sum(p_ctx, axis=-1, keepdims=True)
        o = (_dot(p_loc.astype(BF16), vl_ref[:, ls]) + _dot(p_ctx.astype(BF16), vx_ref[:, ls])) / den
        o_ref[:, ls] = o.astype(o_ref.dtype)


def _na_bias_table(rpb, rows):
    qc = np.arange(GRID_W)[:, None]
    kc = np.arange(GRID_W)[None, :]
    qstart = np.clip(qc - NA_KW // 2, 0, GRID_W - NA_KW)
    inside = (kc - qstart >= 0) & (kc - qstart < NA_KW)
    dc = np.clip(kc - qc + NA_KW - 1, 0, 2 * NA_KW - 2)
    per_dr = jnp.where(inside[None, None], rpb[:, :, dc].astype(F32), NEG)
    cases = []
    for c in range(NA_KH):
        if c < NA_KH // 2:
            dr = np.arange(NA_KH) - c + NA_KH - 1
        elif c == NA_KH // 2:
            dr = np.arange(NA_KH) + NA_KH // 2 - 1
        else:
            dr = np.arange(NA_KH) + NA_KH - 1 - c
        blk = per_dr[:, dr]
        cases.append(jnp.transpose(blk, (0, 2, 1, 3)).reshape(NA_HEADS, GRID_W, NA_KH * GRID_W))
    return jnp.stack(cases)


def _neighborhood_attention(qn, kn, vn, bias, L, with_ctx):
    n = qn.shape[0]
    rows = L // GRID_W
    nr = n // GRID_W if with_ctx else rows
    ctx_blk = L // TOK_TILE
    half = NA_KH // 2

    def row_of(b):
        return jnp.minimum(b, rows - 1)

    def key_start(b):
        return jnp.clip(row_of(b) - half, 0, rows - NA_KH) * GRID_W

    def case_of(b):
        r = row_of(b)
        return jnp.where(r < half, r, jnp.where(r <= rows - half, half, r - (rows - NA_KH)))

    loc = pl.BlockSpec((pl.Element(NA_KH * GRID_W), pl.Element(D_NA)), lambda b: (key_start(b), 0))
    ctx = pl.BlockSpec((n - L, D_NA), lambda b: (ctx_blk, 0))
    return pl.pallas_call(
        functools.partial(_na_kernel, rows=rows),
        out_shape=jax.ShapeDtypeStruct((nr * GRID_W, D_NA), BF16),
        grid=(nr,),
        in_specs=[pl.BlockSpec((GRID_W, D_NA), lambda b: (b, 0)), loc, loc, ctx, ctx,
                  pl.BlockSpec((1, NA_HEADS, GRID_W, NA_KH * GRID_W), lambda b: (case_of(b), 0, 0, 0))],
        out_specs=pl.BlockSpec((GRID_W, D_NA), lambda b: (b, 0)),
        compiler_params=_cparams("arbitrary"),
        name="neighborhood_attn",
    )(qn, kn, vn, kn, vn, bias)


def _out_kernel(xl_ref, xc_ref, hf_ref, hb_ref, ga_ref, yb_ref, yc_ref, w_ref, mod_ref, g_ref,
                xo_ref, ht_ref, *, n_lat):
    d = xl_ref.shape[1]
    i = pl.program_id(0)
    row = (i == n_lat).astype(jnp.int32)
    is_ctx = jnp.full((TOK_TILE, 1), row) == 1
    x = jnp.where(is_ctx, xc_ref[...], xl_ref[...])
    ya = ((hf_ref[...] + hb_ref[...]) * jax.nn.gelu(ga_ref[...])).astype(BF16)
    o1 = D_LRU
    o2 = D_LRU + D_WIN_Q
    mix = (_dot(ya, w_ref[0:o1]) + _dot(yb_ref[...], w_ref[o1:o2]) + _dot(yc_ref[...], w_ref[o2:o2 + D_NA]))
    xn = x + mod_ref[pl.ds(row, 1), 2 * d:3 * d] * mix
    xo_ref[...] = xn
    ms = jnp.mean(xn * xn, axis=-1, keepdims=True)
    y = xn * lax.rsqrt(ms + EPS) * g_ref[...]
    h2 = y * (1.0 + mod_ref[pl.ds(row, 1), 4 * d:5 * d]) + mod_ref[pl.ds(row, 1), 3 * d:4 * d]
    ht_ref[...] = h2.T.astype(BF16)


def _out_proj(xl, xc, hf, hb, ga, yb, yc, w_bf, mod, g, with_ctx):
    L, d = xl.shape
    T = TOK_TILE
    n_lat = L // T
    nt = n_lat + 1 if with_ctx else n_lat
    tok = lambda w: pl.BlockSpec((T, w), lambda i: (i, 0))
    full = lambda a: pl.BlockSpec(a.shape, lambda i: (0,) * a.ndim)
    return pl.pallas_call(
        functools.partial(_out_kernel, n_lat=n_lat),
        out_shape=[jax.ShapeDtypeStruct((nt * T, d), F32), jax.ShapeDtypeStruct((d, nt * T), BF16)],
        grid=(nt,),
        in_specs=[pl.BlockSpec((T, d), lambda i: (jnp.minimum(i, n_lat - 1), 0)),
                  pl.BlockSpec((T, d), lambda i: (0, 0)),
                  tok(D_LRU), tok(D_LRU), tok(D_LRU), tok(D_WIN_Q), tok(D_NA),
                  full(w_bf), full(mod), full(g)],
        out_specs=[tok(d), pl.BlockSpec((d, T), lambda i: (0, i))],
        compiler_params=_cparams("arbitrary"),
        name="out_proj",
    )(xl, xc, hf, hb, ga, yb, yc, w_bf, mod, g)


SUBLANES = 8


def _sorting_pairs(n):
    pairs, p = [], 1
    while p < n:
        k = p
        while k >= 1:
            for j in range(k % p, n - k, 2 * k):
                for i in range(min(k, n - j - k)):
                    if (i + j) // (2 * p) == (i + j + k) // (2 * p):
                        pairs.append((i + j, i + j + k))
            k //= 2
        p *= 2
    return pairs


def _vmax(a, b):
    if a is None:
        return b
    if b is None:
        return a
    return jnp.maximum(a, b)


def _vmin(a, b):
    if a is None or b is None:
        return None
    return jnp.minimum(a, b)


def _top16_sorted(slabs):
    K = PEER_TOPK
    w = list(slabs)
    for a, b in _sorting_pairs(K):
        w[a], w[b] = _vmax(w[a], w[b]), _vmin(w[a], w[b])
    shift = SUBLANES // 2
    while shift >= 1:
        partner = [None if w[K - 1 - i] is None else pltpu.roll(w[K - 1 - i], shift, 0) for i in range(K)]
        w = [_vmax(w[i], partner[i]) for i in range(K)]
        stride = K // 2
        while stride >= 1:
            for i in range(K):
                if i & stride == 0:
                    w[i], w[i + stride] = _vmax(w[i], w[i + stride]), _vmin(w[i], w[i + stride])
            stride //= 2
        shift //= 2
    return w


def _allsum8(x):
    x = x + pltpu.roll(x, 4, 0)
    x = x + pltpu.roll(x, 2, 0)
    return x + pltpu.roll(x, 1, 0)


def _route_tile(s1, s2):
    K = PEER_TOPK
    S = SUBLANES
    n_slab = s1.shape[0] // S
    a1 = [s1[j * S:(j + 1) * S] for j in range(n_slab)]
    a2 = [s2[j * S:(j + 1) * S] for j in range(n_slab)]
    v1 = _top16_sorted(a1)
    v2 = _top16_sorted(a2)
    sub = lax.broadcasted_iota(jnp.int32, a1[0].shape, 0)

    def as_rows(v, lo):
        out = v[lo]
        for b in range(1, S):
            out = jnp.where(sub == b, v[lo + b], out)
        return out

    v2_lo, v2_hi, v1_hi = as_rows(v2, 0), as_rows(v2, S), as_rows(v1, S)
    lens = [K // (a + 1) for a in range(S)]
    cands = [v1[0] + v2_lo, v1[0] + v2_hi]
    for a in range(1, S):
        ca = v1[a] + v2_lo
        cands.append(ca if lens[a] >= S else jnp.where(sub < lens[a], ca, -jnp.inf))
    cands.append(v1_hi + v2[0])
    thr = _top16_sorted(cands + [None] * (K - len(cands)))[K - 1]
    x2_lo, x2_hi = jnp.exp(v2_lo - v2[0]), jnp.exp(v2_hi - v2[0])
    sel_lo, sel_hi = cands[0] >= thr, cands[1] >= thr
    zsum = jnp.where(sel_lo, x2_lo, 0.0) + jnp.where(sel_hi, x2_hi, 0.0)
    cnt = [_allsum8(jnp.where(sel_lo, 1.0, 0.0) + jnp.where(sel_hi, 1.0, 0.0))]
    for a in range(1, S):
        sel = cands[a + 1] >= thr
        zsum = zsum + jnp.where(sel, jnp.exp(v1[a] - v1[0]) * x2_lo, 0.0)
        cnt.append(_allsum8(jnp.where(sel, 1.0, 0.0)))
    zsum = zsum + jnp.where(cands[S + 1] >= thr, jnp.exp(v1_hi - v1[0]), 0.0)
    inv_z = 1.0 / _allsum8(zsum)
    r2, e2, c1, e1 = [], [], [], []
    for j in range(n_slab):
        r = jnp.full(a2[j].shape, float(K), F32)
        for k in reversed(range(K)):
            r = jnp.where(a2[j] >= v2[k], float(k), r)
        r2.append(r)
        e2.append(jnp.exp(a2[j] - v2[0]) * inv_z)
        c = jnp.where(a1[j] + v2[0] >= thr, 1.0, 0.0)
        for a in reversed(range(S)):
            c = jnp.where(a1[j] >= v1[a], cnt[a], c)
        c1.append(c)
        e1.append(jnp.exp(a1[j] - v1[0]))
    cat = lambda xs: jnp.concatenate(xs, axis=0)
    return cat(r2), cat(e2), cat(c1), cat(e1)


def _prep_kernel(ht_ref, wq_ref, k1_ref, k2_ref, r2_ref, e2_ref, c1_ref, e1_ref, q_scr):
    nk = PEER_NKEYS
    q_scr[...] = _dot(wq_ref[...], ht_ref[...])

    def head(h, carry):
        base = pl.multiple_of(h * 2 * nk, 2 * nk)
        s1 = _dot(k1_ref[...], q_scr[pl.ds(base, nk), :].astype(BF16))
        s2 = _dot(k2_ref[...], q_scr[pl.ds(base + nk, nk), :].astype(BF16))
        for t in range(s1.shape[1] // LANES):
            ts = slice(t * LANES, (t + 1) * LANES)
            r2, e2, c1, e1 = _route_tile(s1[:, ts], s2[:, ts])
            r2_ref[h, :, ts] = _pack_rows(r2)
            e2_ref[h, :, ts] = _pack_rows(e2)
            c1_ref[h, :, ts] = c1
            e1_ref[h, :, ts] = e1
        return carry

    lax.fori_loop(0, PEER_HEADS, head, 0)


def _peer_prep(ht, wq_t, k1, k2):
    d, n = ht.shape
    T = PREP_TOK_TILE
    nk = PEER_NKEYS
    full = lambda a: pl.BlockSpec(a.shape, lambda i: (0,) * a.ndim)
    spec = lambda rows: pl.BlockSpec((PEER_HEADS, rows, T), lambda i: (0, 0, i))
    return pl.pallas_call(
        _prep_kernel,
        out_shape=[jax.ShapeDtypeStruct((PEER_HEADS, nk // 2, n), U32)] * 2
        + [jax.ShapeDtypeStruct((PEER_HEADS, nk, n), F32)] * 2,
        grid=(n // T,),
        in_specs=[pl.BlockSpec((d, T), lambda i: (0, i)), full(wq_t), full(k1), full(k2)],
        out_specs=[spec(nk // 2), spec(nk // 2), spec(nk), spec(nk)],
        scratch_shapes=[pltpu.VMEM((wq_t.shape[0], T), F32)],
        compiler_params=_cparams("arbitrary"),
        name="peer_prep",
    )(ht, wq_t, k1, k2)


def _peer_kernel(ht_ref, u_ref, un_ref, vt_ref, vp_ref, r2_ref, e2_ref, c1_ref, e1_ref, x_ref, g_ref,
                 o_ref, acc_ref, *ap_refs):
    nk = PEER_NKEYS
    c = pl.program_id(1)
    n_sub = len(ap_refs) // 2
    a_refs, p_refs = ap_refs[:n_sub], ap_refs[n_sub:]
    sub = 2 * a_refs[0].shape[0]
    T = a_refs[0].shape[1]
    rows_per_sub = sub // nk
    n_rows = n_sub * rows_per_sub
    PK = 16
    WK = PK // 2
    base = pl.multiple_of(c * n_rows, n_rows)

    def activations(s):
        a_refs[s][...] = _pack_rows(_dot(u_ref[s * sub:(s + 1) * sub, :], ht_ref[...]))

    def gates(s):
        for j in range(rows_per_sub):
            row = s * rows_per_sub + j
            for t in range(T // LANES):
                ts = slice(t * LANES, (t + 1) * LANES)
                gate = [None] * (nk // PK)
                for h in range(PEER_HEADS):
                    c1 = c1_ref[h, pl.ds(base, n_rows), ts][row:row + 1]
                    e1 = e1_ref[h, pl.ds(base, n_rows), ts][row:row + 1]
                    c1 = jnp.broadcast_to(c1, (PK, LANES)).astype(BF16)
                    e1 = jnp.broadcast_to(e1, (PK, LANES)).astype(BF16)
                    for v in range(nk // PK):
                        ws = slice(v * WK, (v + 1) * WK)
                        r2 = _unpack_rows(r2_ref[h, ws, ts])
                        e2 = _unpack_rows(e2_ref[h, ws, ts])
                        g = jnp.where(r2 < c1, e2 * e1, 0.0)
                        gate[v] = g if h == 0 else gate[v] + g
                for v in range(nk // PK):
                    ws = slice((j * nk + v * PK) // 2, (j * nk + (v + 1) * PK) // 2)
                    a = _unpack_rows(a_refs[s][ws, ts])
                    th = jnp.tanh(a * (GELU_K0 + GELU_K1 * (a * a)))
                    p_refs[s][ws, ts] = pltpu.bitcast(gate[v] * (a * (0.5 + 0.5 * th)), U32)

    def project(s):
        acc_ref[...] += _dot(vt_ref[:, s * sub:(s + 1) * sub], _unpack_rows(p_refs[s][...]))

    last = n_sub - 1

    def project_deferred():
        acc_ref[...] += _dot(vp_ref[...], _unpack_rows(p_refs[last][...]))

    @pl.when(c == 0)
    def _():
        acc_ref[...] = jnp.zeros(acc_ref.shape, F32)
        p_refs[last][...] = jnp.zeros(p_refs[last].shape, U32)
        activations(0)

    for s in range(n_sub):
        gates(s)
        if s == 0:
            project_deferred()
        else:
            project(s - 1)
        if s < last:
            activations(s + 1)
    a_refs[0][...] = _pack_rows(_dot(un_ref[...], ht_ref[...]))

    @pl.when(c == pl.num_programs(1) - 1)
    def _():
        project(last)
        o_ref[...] = x_ref[...] + g_ref[...] * acc_ref[...].T


def _peer_dense(ht, u_bf, vt_bf, r2, e2, c1, e1, x, g2, tok_tile, tok_off, n_tok):
    d = ht.shape[0]
    n_exp = u_bf.shape[0]
    T = tok_tile
    NC = PEER_EXPERT_CHUNK
    nk = PEER_NKEYS
    SUB = PEER_SUB_CHUNK
    n_sub = NC // SUB
    rt = lambda rows: pl.BlockSpec((PEER_HEADS, rows, T), lambda i, c: (0, 0, i + tok_off))
    return pl.pallas_call(
        _peer_kernel,
        out_shape=jax.ShapeDtypeStruct((n_tok, d), F32),
        grid=(n_tok // T, n_exp // NC),
        in_specs=[pl.BlockSpec((d, T), lambda i, c: (0, i + tok_off)),
                  pl.BlockSpec((NC, d), lambda i, c: (c, 0)),
                  pl.BlockSpec((SUB, d), lambda i, c: (jnp.minimum((c + 1) * n_sub, n_exp // SUB - 1), 0)),
                  pl.BlockSpec((d, NC), lambda i, c: (0, c)),
                  pl.BlockSpec((d, SUB), lambda i, c: (0, jnp.maximum(c * n_sub - 1, 0))),
                  rt(nk // 2), rt(nk // 2), rt(nk), rt(nk),
                  pl.BlockSpec((T, d), lambda i, c: (i + tok_off, 0)),
                  pl.BlockSpec((1, d), lambda i, c: (0, 0))],
        out_specs=pl.BlockSpec((T, d), lambda i, c: (i, 0)),
        scratch_shapes=[pltpu.VMEM((d, T), F32)] + [pltpu.VMEM((PEER_SUB_CHUNK // 2, T), U32)] * (2 * n_sub),
        compiler_params=_cparams("arbitrary", "arbitrary"),
        name="peer_dense",
    )(ht, u_bf, u_bf, vt_bf, vt_bf, r2, e2, c1, e1, x, g2)


def _rope_tables(L, n):
    t = jnp.arange(L)
    row = (t // GRID_W).astype(F32)
    col = (t % GRID_W).astype(F32)
    q = HEAD_DIM // 4
    inv = ROPE_THETA ** (-jnp.arange(q, dtype=F32) / q)
    ar = row[:, None] * inv
    ac = col[:, None] * inv
    cos = jnp.concatenate([jnp.cos(ar), jnp.cos(ar), jnp.cos(ac), jnp.cos(ac)], axis=-1)
    sin = jnp.concatenate([-jnp.sin(ar), jnp.sin(ar), -jnp.sin(ac), jnp.sin(ac)], axis=-1)
    cos = jnp.concatenate([cos, jnp.ones((n - L, HEAD_DIM), F32)], axis=0)
    sin = jnp.concatenate([sin, jnp.zeros((n - L, HEAD_DIM), F32)], axis=0)
    return jnp.tile(cos, (1, LANES // HEAD_DIM)), jnp.tile(sin, (1, LANES // HEAD_DIM))


def _block_diag(w):
    nb, di, do = w.shape
    eye = jnp.eye(nb, dtype=w.dtype)
    return (eye[:, None, :, None] * w[:, :, None, :]).reshape(nb * di, nb * do)


def kernel(x, c, ctx, c_ctx, w_mod, b_mod, norm1_g, norm2_g, w_in, w_out, lru_conv_w, lru_conv_b, lru_wa, lru_ba, lru_wx, lru_bx, lru_lam, win_qn_g, win_kn_g, win_sink, na_qn_g, na_kn_g, na_rpb, peer_wq, peer_k1, peer_k2, peer_u, peer_v):
    B, L, D = x.shape
    Lc = ctx.shape[1]
    depth = w_mod.shape[0]
    assert B == 1 and Lc == TOK_TILE and L % PEER_TOK_TILE == 0 and L // GRID_W >= NA_KH
    n = L + Lc
    n_lat = L // TOK_TILE
    rows = L // GRID_W

    cvec = jnp.zeros((8, D), F32).at[0].set(c[0]).at[1].set(c_ctx)
    mods = _modulation(cvec, w_mod, b_mod)
    cos_t, sin_t = _rope_tables(L, n)
    tile2 = lambda g: jnp.tile(g, LANES // HEAD_DIM)

    xl, xc = x[0], ctx[0]
    for l in range(depth):
        with_ctx = l < depth - 1
        mod = mods[l]
        hg = jnp.zeros((8, LANES), F32)
        hg = hg.at[0].set(tile2(win_qn_g[l])).at[1].set(tile2(win_kn_g[l]))
        hg = hg.at[2].set(tile2(na_qn_g[l])).at[3].set(tile2(na_kn_g[l]))
        xa, ga, qw, kw, vw, qn, kn, vn = _in_proj(xl, xc, mod, norm1_g[l][None], w_in[l].astype(BF16),
                                                  cos_t, sin_t, hg)

        w_gates = jnp.concatenate([_block_diag(lru_wa[l, 0]), _block_diag(lru_wx[l, 0]),
                                   _block_diag(lru_wa[l, 1]), _block_diag(lru_wx[l, 1])], axis=1).astype(BF16)
        b_gates = jnp.concatenate([lru_ba[l, 0], lru_bx[l, 0], lru_ba[l, 1], lru_bx[l, 1]])[None]
        conv_w = jnp.zeros((8, D_LRU), F32).at[0:lru_conv_w.shape[1]].set(lru_conv_w[l])
        lam = jnp.zeros((8, D_LRU), F32).at[0:2].set(lru_lam[l])
        hf, hb = _lru(xa, conv_w, lru_conv_b[l][None], w_gates, b_gates, lam, n_lat)

        yb = _window_attention(win_sink[l], qw, kw, vw, L, with_ctx)
        yc = _neighborhood_attention(qn, kn, vn, _na_bias_table(na_rpb[l], rows), L, with_ctx)

        xn, ht = _out_proj(xl, xc, hf, hb, ga, yb, yc, w_out[l].astype(BF16), mod, norm2_g[l][None], with_ctx)

        r2, e2, c1, e1 = _peer_prep(ht, peer_wq[l].T.astype(BF16), peer_k1[l].astype(BF16),
                                    peer_k2[l].astype(BF16))
        u_bf = peer_u[l].astype(BF16)
        vt_bf = peer_v[l].T.astype(BF16)
        g2 = mod[:, 5 * D:6 * D]
        xl_new = _peer_dense(ht, u_bf, vt_bf, r2, e2, c1, e1, xn, g2[0:1], PEER_TOK_TILE, 0, L)
        if with_ctx:
            xc = _peer_dense(ht, u_bf, vt_bf, r2, e2, c1, e1, xn, g2[1:2], Lc, L // Lc, Lc)
        xl = xl_new
    return xl[None]
```

```python
import functools

import numpy as np
import jax
import jax.numpy as jnp
from jax import lax
from jax.experimental import pallas as pl
from jax.experimental.pallas import tpu as pltpu

F32 = jnp.float32
BF16 = jnp.bfloat16
U32 = jnp.uint32

HEAD_DIM = 64
GRID_W = 64
EPS = 1e-6
ROPE_THETA = 10000.0
D_LRU = 256
LRU_BLOCKS = 4
LRU_C = 8.0
WIN_Q_HEADS = 6
WIN_KV_HEADS = 2
WIN_BLOCK = 128
NA_HEADS = 6
NA_KH = 8
NA_KW = 16
NA_ROW_BLOCK = 4
NA_KEY_ROWS = NA_ROW_BLOCK + NA_KH - 1
PEER_HEADS = 8
PEER_NKEYS = 128
PEER_TOPK = 16
D_WIN_Q = WIN_Q_HEADS * HEAD_DIM
D_WIN_KV = WIN_KV_HEADS * HEAD_DIM
D_NA = NA_HEADS * HEAD_DIM

LANES = 128
TOK_TILE = 256
PEER_TOK_TILE = 512
PEER_EXPERT_CHUNK = 2048
PEER_SUB_CHUNK = 512
PEER_MXU_TOKENS = 256
PREP_TOK_TILE = 256
VMEM_LIMIT = 56 * 1024 * 1024
NEG = -1e30
GELU_K0 = float(np.sqrt(2.0 / np.pi))
GELU_K1 = 0.044715 * GELU_K0


def _cparams(*sem):
    return pltpu.CompilerParams(dimension_semantics=sem, vmem_limit_bytes=VMEM_LIMIT)


def _dot(a, b):
    return jnp.dot(a, b, preferred_element_type=F32)


def _pack_rows(x):
    return pltpu.bitcast(x.astype(BF16), U32)


def _unpack_rows(w):
    return pltpu.bitcast(w, BF16)


def _dot_nt(a, b):
    return lax.dot_general(a, b, (((1,), (1,)), ((), ())), preferred_element_type=F32)


def _mod_kernel(c_ref, w_ref, b_ref, o_ref):
    c = c_ref[...]
    a = (c * jax.nn.sigmoid(c)).astype(BF16)
    o_ref[0] = _dot(a, w_ref[0].astype(BF16)) + b_ref[0]


def _modulation(cvec, w_mod, b_mod):
    depth, d, d6 = w_mod.shape
    return pl.pallas_call(
        _mod_kernel,
        out_shape=jax.ShapeDtypeStruct((depth, 8, d6), F32),
        grid=(depth, d6 // d),
        in_specs=[pl.BlockSpec((8, d), lambda l, j: (0, 0)),
                  pl.BlockSpec((1, d, d), lambda l, j: (l, 0, j)),
                  pl.BlockSpec((1, 1, d), lambda l, j: (l, 0, j))],
        out_specs=pl.BlockSpec((1, 8, d), lambda l, j: (l, 0, j)),
        compiler_params=_cparams("arbitrary", "arbitrary"),
        name="modulation",
    )(cvec, w_mod, b_mod.reshape(depth, 1, d6))


def _pair_mean_matrix():
    r = lax.broadcasted_iota(jnp.int32, (LANES, LANES), 0) < HEAD_DIM
    c = lax.broadcasted_iota(jnp.int32, (LANES, LANES), 1) < HEAD_DIM
    return jnp.where(r == c, 1.0 / HEAD_DIM, 0.0).astype(BF16)


def _head_norm(z, gain, bd):
    z2 = z * z
    hi = z2.astype(BF16)
    lo = (z2 - hi.astype(F32)).astype(BF16)
    ms = _dot(hi, bd) + _dot(lo, bd)
    return z * lax.rsqrt(ms + EPS) * gain


def _rope(z, cos, sin):
    lane = lax.broadcasted_iota(jnp.int32, z.shape, 1)
    first = (lane & 16) == 0
    partner = jnp.where(first, pltpu.roll(z, LANES - 16, 1), pltpu.roll(z, 16, 1))
    return z * cos + partner * sin


def _in_kernel(xl_ref, xc_ref, mod_ref, g_ref, w_ref, cos_ref, sin_ref, hg_ref,
               xa_ref, ga_ref, qw_ref, kw_ref, vw_ref, qn_ref, kn_ref, vn_ref, *, n_lat):
    d = xl_ref.shape[1]
    i = pl.program_id(0)
    row = (i == n_lat).astype(jnp.int32)
    is_ctx = jnp.full((TOK_TILE, 1), row) == 1
    x = jnp.where(is_ctx, xc_ref[...], xl_ref[...])
    ms = jnp.mean(x * x, axis=-1, keepdims=True)
    y = x * lax.rsqrt(ms + EPS) * g_ref[...]
    sh = mod_ref[pl.ds(row, 1), 0:d]
    sc = mod_ref[pl.ds(row, 1), d:2 * d]
    h = (y * (1.0 + sc) + sh).astype(BF16)
    z = _dot(h, w_ref[...])

    bd = _pair_mean_matrix()
    cos = cos_ref[...]
    sin = sin_ref[...]
    scale = HEAD_DIM ** -0.5
    o = 0
    xa_ref[...] = z[:, o:o + D_LRU]
    o += D_LRU
    ga_ref[...] = z[:, o:o + D_LRU]
    o += D_LRU
    for g in range(D_WIN_Q // LANES):
        zz = _rope(_head_norm(z[:, o:o + LANES], hg_ref[0:1], bd), cos, sin)
        qw_ref[:, g * LANES:(g + 1) * LANES] = (zz * scale).astype(BF16)
        o += LANES
    kw_ref[...] = _rope(_head_norm(z[:, o:o + LANES], hg_ref[1:2], bd), cos, sin).astype(BF16)
    o += LANES
    vw_ref[...] = z[:, o:o + LANES].astype(BF16)
    o += LANES
    for g in range(D_NA // LANES):
        zz = _head_norm(z[:, o:o + LANES], hg_ref[2:3], bd)
        qn_ref[:, g * LANES:(g + 1) * LANES] = (zz * scale).astype(BF16)
        o += LANES
    for g in range(D_NA // LANES):
        zz = _head_norm(z[:, o:o + LANES], hg_ref[3:4], bd)
        kn_ref[:, g * LANES:(g + 1) * LANES] = zz.astype(BF16)
        o += LANES
    vn_ref[...] = z[:, o:o + D_NA].astype(BF16)


def _in_proj(xl, xc, mod, g, w_bf, cos_t, sin_t, hg):
    L, d = xl.shape
    n_lat = L // TOK_TILE
    n = L + xc.shape[0]
    d_in = w_bf.shape[1]
    T = TOK_TILE
    tok = lambda w: pl.BlockSpec((T, w), lambda i: (i, 0))
    full = lambda a: pl.BlockSpec(a.shape, lambda i: (0,) * a.ndim)
    widths = (D_LRU, D_LRU, D_WIN_Q, D_WIN_KV, D_WIN_KV, D_NA, D_NA, D_NA)
    dtypes = (F32, F32, BF16, BF16, BF16, BF16, BF16, BF16)
    return pl.pallas_call(
        functools.partial(_in_kernel, n_lat=n_lat),
        out_shape=[jax.ShapeDtypeStruct((n, w), t) for w, t in zip(widths, dtypes)],
        grid=(n_lat + 1,),
        in_specs=[pl.BlockSpec((T, d), lambda i: (jnp.minimum(i, n_lat - 1), 0)),
                  pl.BlockSpec((T, d), lambda i: (0, 0)),
                  full(mod), full(g), full(w_bf), tok(LANES), tok(LANES), full(hg)],
        out_specs=[tok(w) for w in widths],
        compiler_params=_cparams("arbitrary"),
        name="in_proj",
    )(xl, xc, mod, g, w_bf, cos_t, sin_t, hg)


def _chunk_scan(a, b, reverse):
    T = a.shape[0]
    rows = lax.broadcasted_iota(jnp.int32, a.shape, 0)
    s = 1
    while s < T:
        if reverse:
            edge = rows >= T - s
            shift = T - s
        else:
            edge = rows < s
            shift = s
        a_s = jnp.where(edge, 1.0, pltpu.roll(a, shift, 0))
        b_s = jnp.where(edge, 0.0, pltpu.roll(b, shift, 0))
        b = a * b_s + b
        a = a * a_s
        s *= 2
    return a, b


def _lru_direction(xm, ph, nh, pv, nv, cw, cb, w, bias, spl, carry_ref, first, reverse):
    T, C = xm.shape
    rows = lax.broadcasted_iota(jnp.int32, (T, C), 0)
    p6 = ph[6:7] * pv
    p7 = ph[7:8] * pv
    n0 = nh[0:1] * nv
    x_m1 = jnp.where(rows == 0, p7, pltpu.roll(xm, 1, 0))
    x_m2 = jnp.where(rows == 0, p6, jnp.where(rows == 1, p7, pltpu.roll(xm, 2, 0)))
    x_p1 = jnp.where(rows == T - 1, n0, pltpu.roll(xm, T - 1, 0))
    u = cw[0:1] * x_m2 + cw[1:2] * x_m1 + cw[2:3] * xm + cw[3:4] * x_p1 + cb
    zz = _dot(u.astype(BF16), w) + bias
    r = jax.nn.sigmoid(zz[:, :C])
    ig = jax.nn.sigmoid(zz[:, C:])
    log_a = -LRU_C * r * spl
    a = jnp.exp(log_a)
    b = jnp.sqrt(-jnp.tanh(log_a) * (a * a + 1.0)) * ig * u
    a, b = _chunk_scan(a, b, reverse)

    @pl.when(first)
    def _():
        carry_ref[...] = jnp.zeros(carry_ref.shape, F32)

    h = a * carry_ref[0:1] + b
    edge = h[0:1] if reverse else h[T - 1:T]
    carry_ref[...] = jnp.broadcast_to(edge, carry_ref.shape)
    return h


def _lru_kernel(xf_ref, pf_ref, nf_ref, xb_ref, pb_ref, nb_ref, cw_ref, cb_ref, w_ref, b_ref,
                lam_ref, hf_ref, hb_ref, cf_ref, cbk_ref, *, n_lat):
    j = pl.program_id(0)
    C = D_LRU
    fblk = jnp.where(j == 0, n_lat, j - 1)
    bblk = jnp.where(j == 0, n_lat, n_lat - j)
    lam = lam_ref[...]
    spl = jnp.maximum(-lam, 0.0) + jnp.log1p(jnp.exp(-jnp.abs(lam)))
    cw = cw_ref[...]
    cb = cb_ref[...]

    def halo_valid(blk):
        pv = jnp.logical_and(blk != 0, blk != n_lat).astype(F32)
        nv = jnp.logical_and(blk != n_lat - 1, blk != n_lat).astype(F32)
        return pv, nv

    pv, nv = halo_valid(fblk)
    hf_ref[...] = _lru_direction(xf_ref[...], pf_ref[...], nf_ref[...], pv, nv, cw, cb,
                                 w_ref[:, 0:2 * C], b_ref[:, 0:2 * C], spl[0:1], cf_ref, j == 0, False)
    pv, nv = halo_valid(bblk)
    hb_ref[...] = _lru_direction(xb_ref[...], pb_ref[...], nb_ref[...], pv, nv, cw, cb,
                                 w_ref[:, 2 * C:4 * C], b_ref[:, 2 * C:4 * C], spl[1:2], cbk_ref, j == 0, True)


def _lru(xa, conv_w, conv_b, w_gates, b_gates, lam, n_lat):
    n, C = xa.shape
    T = TOK_TILE
    sub = T // 8
    nblk8 = n // 8
    fblk = lambda j: jnp.where(j == 0, n_lat, j - 1)
    bblk = lambda j: jnp.where(j == 0, n_lat, n_lat - j)
    prev8 = lambda blk: jnp.maximum(blk * sub - 1, 0)
    next8 = lambda blk: jnp.minimum((blk + 1) * sub, nblk8 - 1)
    full = lambda a: pl.BlockSpec(a.shape, lambda j: (0,) * a.ndim)
    return pl.pallas_call(
        functools.partial(_lru_kernel, n_lat=n_lat),
        out_shape=[jax.ShapeDtypeStruct((n, C), F32)] * 2,
        grid=(n_lat + 1,),
        in_specs=[pl.BlockSpec((T, C), lambda j: (fblk(j), 0)),
                  pl.BlockSpec((8, C), lambda j: (prev8(fblk(j)), 0)),
                  pl.BlockSpec((8, C), lambda j: (next8(fblk(j)), 0)),
                  pl.BlockSpec((T, C), lambda j: (bblk(j), 0)),
                  pl.BlockSpec((8, C), lambda j: (prev8(bblk(j)), 0)),
                  pl.BlockSpec((8, C), lambda j: (next8(bblk(j)), 0)),
                  full(conv_w), full(conv_b), full(w_gates), full(b_gates), full(lam)],
        out_specs=[pl.BlockSpec((T, C), lambda j: (fblk(j), 0)),
                   pl.BlockSpec((T, C), lambda j: (bblk(j), 0))],
        scratch_shapes=[pltpu.VMEM((8, C), F32), pltpu.VMEM((8, C), F32)],
        compiler_params=_cparams("arbitrary"),
        name="rglru",
    )(xa, xa, xa, xa, xa, xa, conv_w, conv_b, w_gates, b_gates, lam)


def _win_kernel(sink_ref, q_ref, kp_ref, kc_ref, kn_ref, vp_ref, vc_ref, vn_ref, kx_ref, vx_ref,
                o_ref, *, nb):
    W = WIN_BLOCK
    R = WIN_Q_HEADS // WIN_KV_HEADS
    b = pl.program_id(0)
    blk = jnp.where(b < nb, b, -4)
    rows = lax.broadcasted_iota(jnp.int32, (R * W, 3 * W), 0)
    cols = lax.broadcasted_iota(jnp.int32, (R * W, 3 * W), 1)
    kpos = (blk - 1) * W + cols
    qpos = blk * W + (rows & (W - 1))
    valid = jnp.logical_and(jnp.abs(kpos - qpos) <= W, jnp.logical_and(kpos >= 0, kpos < nb * W))
    row1 = lax.broadcasted_iota(jnp.int32, (R * W, 1), 0)
    q = q_ref[...]
    for g in range(WIN_KV_HEADS):
        ls = slice(g * HEAD_DIM, (g + 1) * HEAD_DIM)
        qs = jnp.concatenate([q[:, (g * R + r) * HEAD_DIM:(g * R + r + 1) * HEAD_DIM] for r in range(R)], axis=0)
        kl = jnp.concatenate([kp_ref[:, ls], kc_ref[:, ls], kn_ref[:, ls]], axis=0)
        vl = jnp.concatenate([vp_ref[:, ls], vc_ref[:, ls], vn_ref[:, ls]], axis=0)
        s_loc = jnp.where(valid, _dot_nt(qs, kl), NEG)
        s_ctx = _dot_nt(qs, kx_ref[:, ls])
        sink = jnp.full((R * W, 1), sink_ref[g * R], F32)
        for r in range(1, R):
            sink = jnp.where(row1 >= r * W, sink_ref[g * R + r], sink)
        m = jnp.maximum(jnp.maximum(jnp.max(s_loc, axis=-1, keepdims=True),
                                    jnp.max(s_ctx, axis=-1, keepdims=True)), sink)
        p_loc = jnp.exp(s_loc - m)
        p_ctx = jnp.exp(s_ctx - m)
        den = (jnp.sum(p_loc, axis=-1, keepdims=True) + jnp.sum(p_ctx, axis=-1, keepdims=True)
               + jnp.exp(sink - m))
        o = (_dot(p_loc.astype(BF16), vl) + _dot(p_ctx.astype(BF16), vx_ref[:, ls])) / den
        for r in range(R):
            hh = g * R + r
            o_ref[:, hh * HEAD_DIM:(hh + 1) * HEAD_DIM] = o[r * W:(r + 1) * W].astype(o_ref.dtype)


def _window_attention(sink, qw, kw, vw, L, with_ctx):
    n = qw.shape[0]
    W = WIN_BLOCK
    nb = L // W
    nq = n // W if with_ctx else nb
    ctx_blk = L // TOK_TILE
    lat = lambda b: jnp.minimum(b, nb - 1)
    kv = lambda f: pl.BlockSpec((W, D_WIN_KV), lambda b: (f(b), 0))
    prv = lambda b: jnp.maximum(lat(b) - 1, 0)
    nxt = lambda b: jnp.minimum(lat(b) + 1, nb - 1)
    ctx = pl.BlockSpec((n - L, D_WIN_KV), lambda b: (ctx_blk, 0))
    return pl.pallas_call(
        functools.partial(_win_kernel, nb=nb),
        out_shape=jax.ShapeDtypeStruct((nq * W, D_WIN_Q), BF16),
        grid=(nq,),
        in_specs=[pl.BlockSpec(memory_space=pltpu.SMEM),
                  pl.BlockSpec((W, D_WIN_Q), lambda b: (b, 0)),
                  kv(prv), kv(lat), kv(nxt), kv(prv), kv(lat), kv(nxt), ctx, ctx],
        out_specs=pl.BlockSpec((W, D_WIN_Q), lambda b: (b, 0)),
        compiler_params=_cparams("arbitrary"),
        name="window_attn",
    )(sink, qw, kw, kw, kw, vw, vw, vw, kw, vw)


def _na_kernel(q_ref, kl_ref, vl_ref, kx_ref, vx_ref, bias_ref, o_ref, *, n_blocks):
    b = pl.program_id(0)
    nq = q_ref.shape[0]
    latent = jnp.full((nq, 1), (b < n_blocks).astype(jnp.int32)) == 1
    q = q_ref[...]
    for h in range(NA_HEADS):
        ls = slice(h * HEAD_DIM, (h + 1) * HEAD_DIM)
        qh = q[:, ls]
        s_loc = jnp.where(latent, _dot_nt(qh, kl_ref[:, ls]) + bias_ref[0, h], NEG)
        s_ctx = _dot_nt(qh, kx_ref[:, ls])
        m = jnp.maximum(jnp.max(s_loc, axis=-1, keepdims=True), jnp.max(s_ctx, axis=-1, keepdims=True))
        p_loc = jnp.exp(s_loc - m)
        p_ctx = jnp.exp(s_ctx - m)
        den = jnp.sum(p_loc, axis=-1, keepdims=True) + jnp.sum(p_ctx, axis=-1, keepdims=True)
        o = (_dot(p_loc.astype(BF16), vl_ref[:, ls]) + _dot(p_ctx.astype(BF16), vx_ref[:, ls])) / den
        o_ref[:, ls] = o.astype(o_ref.dtype)


def _na_bias_table(rpb, rows):
    R, KR = NA_ROW_BLOCK, NA_KEY_ROWS
    qc = np.arange(GRID_W)[:, None]
    kc = np.arange(GRID_W)[None, :]
    qstart = np.clip(qc - NA_KW // 2, 0, GRID_W - NA_KW)
    inside = (kc - qstart >= 0) & (kc - qstart < NA_KW)
    dc = np.clip(kc - qc + NA_KW - 1, 0, 2 * NA_KW - 2)
    per_dr = jnp.where(inside[None, None], rpb[:, :, dc].astype(F32), NEG)
    half = NA_KH // 2
    dr = np.zeros((3, R, KR), np.int64)
    ok = np.zeros((3, R, KR), bool)
    for case, (r0, ks) in enumerate(((0, 0), (half, 0), (rows - R, rows - KR))):
        for rr in range(R):
            r = r0 + rr
            kr = min(max(r - half, 0), rows - NA_KH)
            for kk in range(KR):
                d = ks + kk - r + NA_KH - 1
                ok[case, rr, kk] = kr <= ks + kk < kr + NA_KH
                dr[case, rr, kk] = min(max(d, 0), 2 * NA_KH - 2)
    t = per_dr[:, dr]
    t = jnp.where(ok[None, :, :, :, None, None], t, NEG)
    t = jnp.transpose(t, (1, 0, 2, 4, 3, 5))
    return t.reshape(3, NA_HEADS, R * GRID_W, KR * GRID_W)


def _neighborhood_attention(qn, kn, vn, bias, L, with_ctx):
    n = qn.shape[0]
    R, KR = NA_ROW_BLOCK, NA_KEY_ROWS
    rows = L // GRID_W
    nb = rows // R
    nq = R * GRID_W
    steps = n // nq if with_ctx else nb
    ctx_blk = L // TOK_TILE
    half = NA_KH // 2

    def blk_of(b):
        return jnp.minimum(b, nb - 1)

    def key_start(b):
        return jnp.clip(blk_of(b) * R - half, 0, rows - KR) * GRID_W

    def case_of(b):
        return jnp.where(blk_of(b) == 0, 0, jnp.where(blk_of(b) == nb - 1, 2, 1))

    loc = pl.BlockSpec((pl.Element(KR * GRID_W), pl.Element(D_NA)), lambda b: (key_start(b), 0))
    ctx = pl.BlockSpec((n - L, D_NA), lambda b: (ctx_blk, 0))
    return pl.pallas_call(
        functools.partial(_na_kernel, n_blocks=nb),
        out_shape=jax.ShapeDtypeStruct((steps * nq, D_NA), BF16),
        grid=(steps,),
        in_specs=[pl.BlockSpec((nq, D_NA), lambda b: (b, 0)), loc, loc, ctx, ctx,
                  pl.BlockSpec((1, NA_HEADS, nq, KR * GRID_W), lambda b: (case_of(b), 0, 0, 0))],
        out_specs=pl.BlockSpec((nq, D_NA), lambda b: (b, 0)),
        compiler_params=_cparams("arbitrary"),
        name="neighborhood_attn",
    )(qn, kn, vn, kn, vn, bias)


def _out_kernel(xl_ref, xc_ref, hf_ref, hb_ref, ga_ref, yb_ref, yc_ref, w_ref, mod_ref, g_ref,
                xo_ref, ht_ref, *, n_lat):
    d = xl_ref.shape[1]
    i = pl.program_id(0)
    row = (i == n_lat).astype(jnp.int32)
    is_ctx = jnp.full((TOK_TILE, 1), row) == 1
    x = jnp.where(is_ctx, xc_ref[...], xl_ref[...])
    ya = ((hf_ref[...] + hb_ref[...]) * jax.nn.gelu(ga_ref[...])).astype(BF16)
    o1 = D_LRU
    o2 = D_LRU + D_WIN_Q
    mix = (_dot(ya, w_ref[0:o1]) + _dot(yb_ref[...], w_ref[o1:o2]) + _dot(yc_ref[...], w_ref[o2:o2 + D_NA]))
    xn = x + mod_ref[pl.ds(row, 1), 2 * d:3 * d] * mix
    xo_ref[...] = xn
    ms = jnp.mean(xn * xn, axis=-1, keepdims=True)
    y = xn * lax.rsqrt(ms + EPS) * g_ref[...]
    h2 = y * (1.0 + mod_ref[pl.ds(row, 1), 4 * d:5 * d]) + mod_ref[pl.ds(row, 1), 3 * d:4 * d]
    ht_ref[...] = h2.T.astype(BF16)


def _out_proj(xl, xc, hf, hb, ga, yb, yc, w_bf, mod, g, with_ctx):
    L, d = xl.shape
    T = TOK_TILE
    n_lat = L // T
    nt = n_lat + 1 if with_ctx else n_lat
    tok = lambda w: pl.BlockSpec((T, w), lambda i: (i, 0))
    full = lambda a: pl.BlockSpec(a.shape, lambda i: (0,) * a.ndim)
    return pl.pallas_call(
        functools.partial(_out_kernel, n_lat=n_lat),
        out_shape=[jax.ShapeDtypeStruct((nt * T, d), F32), jax.ShapeDtypeStruct((d, nt * T), BF16)],
        grid=(nt,),
        in_specs=[pl.BlockSpec((T, d), lambda i: (jnp.minimum(i, n_lat - 1), 0)),
                  pl.BlockSpec((T, d), lambda i: (0, 0)),
                  tok(D_LRU), tok(D_LRU), tok(D_LRU), tok(D_WIN_Q), tok(D_NA),
                  full(w_bf), full(mod), full(g)],
        out_specs=[tok(d), pl.BlockSpec((d, T), lambda i: (0, i))],
        compiler_params=_cparams("arbitrary"),
        name="out_proj",
    )(xl, xc, hf, hb, ga, yb, yc, w_bf, mod, g)


SUBLANES = 8


def _sorting_pairs(n):
    pairs, p = [], 1
    while p < n:
        k = p
        while k >= 1:
            for j in range(k % p, n - k, 2 * k):
                for i in range(min(k, n - j - k)):
                    if (i + j) // (2 * p) == (i + j + k) // (2 * p):
                        pairs.append((i + j, i + j + k))
            k //= 2
        p *= 2
    return pairs


def _vmax(a, b):
    if a is None:
        return b
    if b is None:
        return a
    return jnp.maximum(a, b)


def _vmin(a, b):
    if a is None or b is None:
        return None
    return jnp.minimum(a, b)


def _top16_sorted(slabs):
    K = PEER_TOPK
    w = list(slabs)
    for a, b in _sorting_pairs(K):
        w[a], w[b] = _vmax(w[a], w[b]), _vmin(w[a], w[b])
    shift = SUBLANES // 2
    while shift >= 1:
        partner = [None if w[K - 1 - i] is None else pltpu.roll(w[K - 1 - i], shift, 0) for i in range(K)]
        w = [_vmax(w[i], partner[i]) for i in range(K)]
        stride = K // 2
        while stride >= 1:
            for i in range(K):
                if i & stride == 0:
                    w[i], w[i + stride] = _vmax(w[i], w[i + stride]), _vmin(w[i], w[i + stride])
            stride //= 2
        shift //= 2
    return w


def _allsum8(x):
    x = x + pltpu.roll(x, 4, 0)
    x = x + pltpu.roll(x, 2, 0)
    return x + pltpu.roll(x, 1, 0)


def _route_tile(s1, s2):
    K = PEER_TOPK
    S = SUBLANES
    n_slab = s1.shape[0] // S
    a1 = [s1[j * S:(j + 1) * S] for j in range(n_slab)]
    a2 = [s2[j * S:(j + 1) * S] for j in range(n_slab)]
    v1 = _top16_sorted(a1)
    v2 = _top16_sorted(a2)
    sub = lax.broadcasted_iota(jnp.int32, a1[0].shape, 0)

    def as_rows(v, lo):
        out = v[lo]
        for b in range(1, S):
            out = jnp.where(sub == b, v[lo + b], out)
        return out

    v2_lo, v2_hi, v1_hi = as_rows(v2, 0), as_rows(v2, S), as_rows(v1, S)
    lens = [K // (a + 1) for a in range(S)]
    cands = [v1[0] + v2_lo, v1[0] + v2_hi]
    for a in range(1, S):
        ca = v1[a] + v2_lo
        cands.append(ca if lens[a] >= S else jnp.where(sub < lens[a], ca, -jnp.inf))
    cands.append(v1_hi + v2[0])
    thr = _top16_sorted(cands + [None] * (K - len(cands)))[K - 1]
    x2_lo, x2_hi = jnp.exp(v2_lo - v2[0]), jnp.exp(v2_hi - v2[0])
    sel_lo, sel_hi = cands[0] >= thr, cands[1] >= thr
    zsum = jnp.where(sel_lo, x2_lo, 0.0) + jnp.where(sel_hi, x2_hi, 0.0)
    cnt = [_allsum8(jnp.where(sel_lo, 1.0, 0.0) + jnp.where(sel_hi, 1.0, 0.0))]
    for a in range(1, S):
        sel = cands[a + 1] >= thr
        zsum = zsum + jnp.where(sel, jnp.exp(v1[a] - v1[0]) * x2_lo, 0.0)
        cnt.append(_allsum8(jnp.where(sel, 1.0, 0.0)))
    zsum = zsum + jnp.where(cands[S + 1] >= thr, jnp.exp(v1_hi - v1[0]), 0.0)
    inv_z = 1.0 / _allsum8(zsum)
    r2, e2, c1, e1 = [], [], [], []
    for j in range(n_slab):
        r = jnp.full(a2[j].shape, float(K), F32)
        for k in reversed(range(K)):
            r = jnp.where(a2[j] >= v2[k], float(k), r)
        r2.append(r)
        e2.append(jnp.exp(a2[j] - v2[0]) * inv_z)
        c = jnp.where(a1[j] + v2[0] >= thr, 1.0, 0.0)
        for a in reversed(range(S)):
            c = jnp.where(a1[j] >= v1[a], cnt[a], c)
        c1.append(c)
        e1.append(jnp.exp(a1[j] - v1[0]))
    cat = lambda xs: jnp.concatenate(xs, axis=0)
    return cat(r2), cat(e2), cat(c1), cat(e1)


def _prep_kernel(ht_ref, wq_ref, k1_ref, k2_ref, r2_ref, e2_ref, c1_ref, e1_ref, q_scr):
    nk = PEER_NKEYS
    q_scr[...] = _dot(wq_ref[...], ht_ref[...])

    def head(h, carry):
        base = pl.multiple_of(h * 2 * nk, 2 * nk)
        s1 = _dot(k1_ref[...], q_scr[pl.ds(base, nk), :].astype(BF16))
        s2 = _dot(k2_ref[...], q_scr[pl.ds(base + nk, nk), :].astype(BF16))
        for t in range(s1.shape[1] // LANES):
            ts = slice(t * LANES, (t + 1) * LANES)
            r2, e2, c1, e1 = _route_tile(s1[:, ts], s2[:, ts])
            r2_ref[h, :, ts] = _pack_rows(r2)
            e2_ref[h, :, ts] = _pack_rows(e2)
            c1_ref[h, :, ts] = c1
            e1_ref[h, :, ts] = e1
        return carry

    lax.fori_loop(0, PEER_HEADS, head, 0)


def _peer_prep(ht, wq_t, k1, k2):
    d, n = ht.shape
    T = PREP_TOK_TILE
    nk = PEER_NKEYS
    full = lambda a: pl.BlockSpec(a.shape, lambda i: (0,) * a.ndim)
    spec = lambda rows: pl.BlockSpec((PEER_HEADS, rows, T), lambda i: (0, 0, i))
    return pl.pallas_call(
        _prep_kernel,
        out_shape=[jax.ShapeDtypeStruct((PEER_HEADS, nk // 2, n), U32)] * 2
        + [jax.ShapeDtypeStruct((PEER_HEADS, nk, n), F32)] * 2,
        grid=(n // T,),
        in_specs=[pl.BlockSpec((d, T), lambda i: (0, i)), full(wq_t), full(k1), full(k2)],
        out_specs=[spec(nk // 2), spec(nk // 2), spec(nk), spec(nk)],
        scratch_shapes=[pltpu.VMEM((wq_t.shape[0], T), F32)],
        compiler_params=_cparams("arbitrary"),
        name="peer_prep",
    )(ht, wq_t, k1, k2)


def _peer_kernel(ht_ref, u_ref, un_ref, vt_ref, vp_ref, r2_ref, e2_ref, c1_ref, e1_ref, x_ref, g_ref,
                 o_ref, acc_ref, *ap_refs):
    nk = PEER_NKEYS
    c = pl.program_id(1)
    n_sub = len(ap_refs) // 2
    a_refs, p_refs = ap_refs[:n_sub], ap_refs[n_sub:]
    sub = 2 * a_refs[0].shape[0]
    T = a_refs[0].shape[1]
    rows_per_sub = sub // nk
    n_rows = n_sub * rows_per_sub
    PK = 16
    WK = PK // 2
    base = pl.multiple_of(c * n_rows, n_rows)

    tok_blocks = [slice(n, min(n + PEER_MXU_TOKENS, T)) for n in range(0, T, PEER_MXU_TOKENS)]

    def activations_from(rows_ref, dst_ref):
        for tb in tok_blocks:
            dst_ref[:, tb] = _pack_rows(_dot(rows_ref, ht_ref[:, tb]))

    def activations(s):
        activations_from(u_ref[s * sub:(s + 1) * sub, :], a_refs[s])

    def gates(s):
        for t in range(T // LANES):
            ts = slice(t * LANES, (t + 1) * LANES)
            for j in range(rows_per_sub):
                row = s * rows_per_sub + j
                gate = [None] * (nk // PK)
                for h in range(PEER_HEADS):
                    c1 = c1_ref[h, pl.ds(base, n_rows), ts][row:row + 1]
                    e1 = e1_ref[h, pl.ds(base, n_rows), ts][row:row + 1]
                    c1 = jnp.broadcast_to(c1, (PK, LANES)).astype(BF16)
                    e1 = jnp.broadcast_to(e1, (PK, LANES)).astype(BF16)
                    for v in range(nk // PK):
                        ws = slice(v * WK, (v + 1) * WK)
                        r2 = _unpack_rows(r2_ref[h, ws, ts])
                        e2 = _unpack_rows(e2_ref[h, ws, ts])
                        g = jnp.where(r2 < c1, e2 * e1, 0.0)
                        gate[v] = g if h == 0 else gate[v] + g
                for v in range(nk // PK):
                    ws = slice((j * nk + v * PK) // 2, (j * nk + (v + 1) * PK) // 2)
                    a = _unpack_rows(a_refs[s][ws, ts])
                    th = jnp.tanh(a * (GELU_K0 + GELU_K1 * (a * a)))
                    p_refs[s][ws, ts] = pltpu.bitcast(gate[v] * (a * (0.5 + 0.5 * th)), U32)

    def project_from(cols_ref, src_ref):
        for tb in tok_blocks:
            acc_ref[:, tb] += _dot(cols_ref, _unpack_rows(src_ref[:, tb]))

    def project(s):
        project_from(vt_ref[:, s * sub:(s + 1) * sub], p_refs[s])

    last = n_sub - 1

    def project_deferred():
        project_from(vp_ref[...], p_refs[last])

    @pl.when(c == 0)
    def _():
        acc_ref[...] = jnp.zeros(acc_ref.shape, F32)
        p_refs[last][...] = jnp.zeros(p_refs[last].shape, U32)
        activations(0)

    for s in range(n_sub):
        gates(s)
        if s == 0:
            project_deferred()
        else:
            project(s - 1)
        if s < last:
            activations(s + 1)
    activations_from(un_ref[...], a_refs[0])

    @pl.when(c == pl.num_programs(1) - 1)
    def _():
        project(last)
        o_ref[...] = x_ref[...] + g_ref[...] * acc_ref[...].T


def _peer_dense(ht, u_bf, vt_bf, r2, e2, c1, e1, x, g2, tok_tile, tok_off, n_tok):
    d = ht.shape[0]
    n_exp = u_bf.shape[0]
    T = tok_tile
    NC = PEER_EXPERT_CHUNK
    nk = PEER_NKEYS
    SUB = PEER_SUB_CHUNK
    n_sub = NC // SUB
    rt = lambda rows: pl.BlockSpec((PEER_HEADS, rows, T), lambda i, c: (0, 0, i + tok_off))
    return pl.pallas_call(
        _peer_kernel,
        out_shape=jax.ShapeDtypeStruct((n_tok, d), F32),
        grid=(n_tok // T, n_exp // NC),
        in_specs=[pl.BlockSpec((d, T), lambda i, c: (0, i + tok_off)),
                  pl.BlockSpec((NC, d), lambda i, c: (c, 0)),
                  pl.BlockSpec((SUB, d), lambda i, c: (jnp.minimum((c + 1) * n_sub, n_exp // SUB - 1), 0)),
                  pl.BlockSpec((d, NC), lambda i, c: (0, c)),
                  pl.BlockSpec((d, SUB), lambda i, c: (0, jnp.maximum(c * n_sub - 1, 0))),
                  rt(nk // 2), rt(nk // 2), rt(nk), rt(nk),
                  pl.BlockSpec((T, d), lambda i, c: (i + tok_off, 0)),
                  pl.BlockSpec((1, d), lambda i, c: (0, 0))],
        out_specs=pl.BlockSpec((T, d), lambda i, c: (i, 0)),
        scratch_shapes=[pltpu.VMEM((d, T), F32)] + [pltpu.VMEM((PEER_SUB_CHUNK // 2, T), U32)] * (2 * n_sub),
        compiler_params=_cparams("arbitrary", "arbitrary"),
        name="peer_dense",
    )(ht, u_bf, u_bf, vt_bf, vt_bf, r2, e2, c1, e1, x, g2)


def _rope_tables(L, n):
    t = jnp.arange(L)
    row = (t // GRID_W).astype(F32)
    col = (t % GRID_W).astype(F32)
    q = HEAD_DIM // 4
    inv = ROPE_THETA ** (-jnp.arange(q, dtype=F32) / q)
    ar = row[:, None] * inv
    ac = col[:, None] * inv
    cos = jnp.concatenate([jnp.cos(ar), jnp.cos(ar), jnp.cos(ac), jnp.cos(ac)], axis=-1)
    sin = jnp.concatenate([-jnp.sin(ar), jnp.sin(ar), -jnp.sin(ac), jnp.sin(ac)], axis=-1)
    cos = jnp.concatenate([cos, jnp.ones((n - L, HEAD_DIM), F32)], axis=0)
    sin = jnp.concatenate([sin, jnp.zeros((n - L, HEAD_DIM), F32)], axis=0)
    return jnp.tile(cos, (1, LANES // HEAD_DIM)), jnp.tile(sin, (1, LANES // HEAD_DIM))


def _block_diag(w):
    nb, di, do = w.shape
    eye = jnp.eye(nb, dtype=w.dtype)
    return (eye[:, None, :, None] * w[:, :, None, :]).reshape(nb * di, nb * do)


def kernel(x, c, ctx, c_ctx, w_mod, b_mod, norm1_g, norm2_g, w_in, w_out, lru_conv_w, lru_conv_b, lru_wa, lru_ba, lru_wx, lru_bx, lru_lam, win_qn_g, win_kn_g, win_sink, na_qn_g, na_kn_g, na_rpb, peer_wq, peer_k1, peer_k2, peer_u, peer_v):
    B, L, D = x.shape
    Lc = ctx.shape[1]
    depth = w_mod.shape[0]
    assert B == 1 and Lc == TOK_TILE and L % PEER_TOK_TILE == 0 and L // GRID_W >= 3 * NA_ROW_BLOCK
    n = L + Lc
    n_lat = L // TOK_TILE
    rows = L // GRID_W

    cvec = jnp.zeros((8, D), F32).at[0].set(c[0]).at[1].set(c_ctx)
    mods = _modulation(cvec, w_mod, b_mod)
    cos_t, sin_t = _rope_tables(L, n)
    tile2 = lambda g: jnp.tile(g, LANES // HEAD_DIM)

    xl, xc = x[0], ctx[0]
    for l in range(depth):
        with_ctx = l < depth - 1
        mod = mods[l]
        hg = jnp.zeros((8, LANES), F32)
        hg = hg.at[0].set(tile2(win_qn_g[l])).at[1].set(tile2(win_kn_g[l]))
        hg = hg.at[2].set(tile2(na_qn_g[l])).at[3].set(tile2(na_kn_g[l]))
        xa, ga, qw, kw, vw, qn, kn, vn = _in_proj(xl, xc, mod, norm1_g[l][None], w_in[l].astype(BF16),
                                                  cos_t, sin_t, hg)

        w_gates = jnp.concatenate([_block_diag(lru_wa[l, 0]), _block_diag(lru_wx[l, 0]),
                                   _block_diag(lru_wa[l, 1]), _block_diag(lru_wx[l, 1])], axis=1).astype(BF16)
        b_gates = jnp.concatenate([lru_ba[l, 0], lru_bx[l, 0], lru_ba[l, 1], lru_bx[l, 1]])[None]
        conv_w = jnp.zeros((8, D_LRU), F32).at[0:lru_conv_w.shape[1]].set(lru_conv_w[l])
        lam = jnp.zeros((8, D_LRU), F32).at[0:2].set(lru_lam[l])
        hf, hb = _lru(xa, conv_w, lru_conv_b[l][None], w_gates, b_gates, lam, n_lat)

        yb = _window_attention(win_sink[l], qw, kw, vw, L, with_ctx)
        yc = _neighborhood_attention(qn, kn, vn, _na_bias_table(na_rpb[l], rows), L, with_ctx)

        xn, ht = _out_proj(xl, xc, hf, hb, ga, yb, yc, w_out[l].astype(BF16), mod, norm2_g[l][None], with_ctx)

        r2, e2, c1, e1 = _peer_prep(ht, peer_wq[l].T.astype(BF16), peer_k1[l].astype(BF16),
                                    peer_k2[l].astype(BF16))
        u_bf = peer_u[l].astype(BF16)
        vt_bf = peer_v[l].T.astype(BF16)
        g2 = mod[:, 5 * D:6 * D]
        xl_new = _peer_dense(ht, u_bf, vt_bf, r2, e2, c1, e1, xn, g2[0:1], PEER_TOK_TILE, 0, L)
        if with_ctx:
            xc = _peer_dense(ht, u_bf, vt_bf, r2, e2, c1, e1, xn, g2[1:2], Lc, L // Lc, Lc)
        xl = xl_new
    return xl[None]
```

```python
import functools

import numpy as np
import jax
import jax.numpy as jnp
from jax import lax
from jax.experimental import pallas as pl
from jax.experimental.pallas import tpu as pltpu

F32 = jnp.float32
BF16 = jnp.bfloat16
U32 = jnp.uint32

HEAD_DIM = 64
GRID_W = 64
EPS = 1e-6
ROPE_THETA = 10000.0
D_LRU = 256
LRU_BLOCKS = 4
LRU_C = 8.0
WIN_Q_HEADS = 6
WIN_KV_HEADS = 2
WIN_BLOCK = 128
NA_HEADS = 6
NA_KH = 8
NA_KW = 16
NA_ROW_BLOCK = 4
NA_KEY_ROWS = NA_ROW_BLOCK + NA_KH - 1
PEER_HEADS = 8
PEER_NKEYS = 128
PEER_TOPK = 16
D_WIN_Q = WIN_Q_HEADS * HEAD_DIM
D_WIN_KV = WIN_KV_HEADS * HEAD_DIM
D_NA = NA_HEADS * HEAD_DIM

LANES = 128
TOK_TILE = 256
PEER_TOK_TILE = 512
PEER_EXPERT_CHUNK = 2048
PEER_SUB_CHUNK = 512
PEER_MXU_TOKENS = 256
PREP_TOK_TILE = 256
VMEM_LIMIT = 56 * 1024 * 1024
NEG = -1e30
GELU_K0 = float(np.sqrt(2.0 / np.pi))
GELU_K1 = 0.044715 * GELU_K0


def _cparams(*sem):
    return pltpu.CompilerParams(dimension_semantics=sem, vmem_limit_bytes=VMEM_LIMIT)


def _dot(a, b):
    return jnp.dot(a, b, preferred_element_type=F32)


def _pack_rows(x):
    return pltpu.bitcast(x.astype(BF16), U32)


def _unpack_rows(w):
    return pltpu.bitcast(w, BF16)


def _dot_nt(a, b):
    return lax.dot_general(a, b, (((1,), (1,)), ((), ())), preferred_element_type=F32)


def _mod_kernel(c_ref, w_ref, b_ref, o_ref):
    c = c_ref[...]
    a = (c * jax.nn.sigmoid(c)).astype(BF16)
    o_ref[0] = _dot(a, w_ref[0].astype(BF16)) + b_ref[0]


def _modulation(cvec, w_mod, b_mod):
    depth, d, d6 = w_mod.shape
    return pl.pallas_call(
        _mod_kernel,
        out_shape=jax.ShapeDtypeStruct((depth, 8, d6), F32),
        grid=(depth, d6 // d),
        in_specs=[pl.BlockSpec((8, d), lambda l, j: (0, 0)),
                  pl.BlockSpec((1, d, d), lambda l, j: (l, 0, j)),
                  pl.BlockSpec((1, 1, d), lambda l, j: (l, 0, j))],
        out_specs=pl.BlockSpec((1, 8, d), lambda l, j: (l, 0, j)),
        compiler_params=_cparams("arbitrary", "arbitrary"),
        name="modulation",
    )(cvec, w_mod, b_mod.reshape(depth, 1, d6))


def _pair_mean_matrix():
    r = lax.broadcasted_iota(jnp.int32, (LANES, LANES), 0) < HEAD_DIM
    c = lax.broadcasted_iota(jnp.int32, (LANES, LANES), 1) < HEAD_DIM
    return jnp.where(r == c, 1.0 / HEAD_DIM, 0.0).astype(BF16)


def _head_norm(z, gain, bd):
    z2 = z * z
    hi = z2.astype(BF16)
    lo = (z2 - hi.astype(F32)).astype(BF16)
    ms = _dot(hi, bd) + _dot(lo, bd)
    return z * lax.rsqrt(ms + EPS) * gain


def _rope(z, cos, sin):
    lane = lax.broadcasted_iota(jnp.int32, z.shape, 1)
    first = (lane & 16) == 0
    partner = jnp.where(first, pltpu.roll(z, LANES - 16, 1), pltpu.roll(z, 16, 1))
    return z * cos + partner * sin


def _in_kernel(xl_ref, xc_ref, mod_ref, g_ref, w_ref, cos_ref, sin_ref, hg_ref,
               xa_ref, ga_ref, qw_ref, kw_ref, vw_ref, qn_ref, kn_ref, vn_ref, *, n_lat):
    d = xl_ref.shape[1]
    i = pl.program_id(0)
    row = (i == n_lat).astype(jnp.int32)
    is_ctx = jnp.full((TOK_TILE, 1), row) == 1
    x = jnp.where(is_ctx, xc_ref[...], xl_ref[...])
    ms = jnp.mean(x * x, axis=-1, keepdims=True)
    y = x * lax.rsqrt(ms + EPS) * g_ref[...]
    sh = mod_ref[pl.ds(row, 1), 0:d]
    sc = mod_ref[pl.ds(row, 1), d:2 * d]
    h = (y * (1.0 + sc) + sh).astype(BF16)
    z = _dot(h, w_ref[...])

    bd = _pair_mean_matrix()
    cos = cos_ref[...]
    sin = sin_ref[...]
    scale = HEAD_DIM ** -0.5
    o = 0
    xa_ref[...] = z[:, o:o + D_LRU]
    o += D_LRU
    ga_ref[...] = z[:, o:o + D_LRU]
    o += D_LRU
    for g in range(D_WIN_Q // LANES):
        zz = _rope(_head_norm(z[:, o:o + LANES], hg_ref[0:1], bd), cos, sin)
        qw_ref[:, g * LANES:(g + 1) * LANES] = (zz * scale).astype(BF16)
        o += LANES
    kw_ref[...] = _rope(_head_norm(z[:, o:o + LANES], hg_ref[1:2], bd), cos, sin).astype(BF16)
    o += LANES
    vw_ref[...] = z[:, o:o + LANES].astype(BF16)
    o += LANES
    for g in range(D_NA // LANES):
        zz = _head_norm(z[:, o:o + LANES], hg_ref[2:3], bd)
        qn_ref[:, g * LANES:(g + 1) * LANES] = (zz * scale).astype(BF16)
        o += LANES
    for g in range(D_NA // LANES):
        zz = _head_norm(z[:, o:o + LANES], hg_ref[3:4], bd)
        kn_ref[:, g * LANES:(g + 1) * LANES] = zz.astype(BF16)
        o += LANES
    vn_ref[...] = z[:, o:o + D_NA].astype(BF16)


def _in_proj(xl, xc, mod, g, w_bf, cos_t, sin_t, hg):
    L, d = xl.shape
    n_lat = L // TOK_TILE
    n = L + xc.shape[0]
    d_in = w_bf.shape[1]
    T = TOK_TILE
    tok = lambda w: pl.BlockSpec((T, w), lambda i: (i, 0))
    full = lambda a: pl.BlockSpec(a.shape, lambda i: (0,) * a.ndim)
    widths = (D_LRU, D_LRU, D_WIN_Q, D_WIN_KV, D_WIN_KV, D_NA, D_NA, D_NA)
    dtypes = (F32, F32, BF16, BF16, BF16, BF16, BF16, BF16)
    return pl.pallas_call(
        functools.partial(_in_kernel, n_lat=n_lat),
        out_shape=[jax.ShapeDtypeStruct((n, w), t) for w, t in zip(widths, dtypes)],
        grid=(n_lat + 1,),
        in_specs=[pl.BlockSpec((T, d), lambda i: (jnp.minimum(i, n_lat - 1), 0)),
                  pl.BlockSpec((T, d), lambda i: (0, 0)),
                  full(mod), full(g), full(w_bf), tok(LANES), tok(LANES), full(hg)],
        out_specs=[tok(w) for w in widths],
        compiler_params=_cparams("arbitrary"),
        name="in_proj",
    )(xl, xc, mod, g, w_bf, cos_t, sin_t, hg)


def _chunk_scan(a, b, reverse):
    T = a.shape[0]
    rows = lax.broadcasted_iota(jnp.int32, a.shape, 0)
    s = 1
    while s < T:
        if reverse:
            edge = rows >= T - s
            shift = T - s
        else:
            edge = rows < s
            shift = s
        a_s = jnp.where(edge, 1.0, pltpu.roll(a, shift, 0))
        b_s = jnp.where(edge, 0.0, pltpu.roll(b, shift, 0))
        b = a * b_s + b
        a = a * a_s
        s *= 2
    return a, b


def _lru_direction(xm, ph, nh, pv, nv, cw, cb, w, bias, spl, carry_ref, first, reverse):
    T, C = xm.shape
    rows = lax.broadcasted_iota(jnp.int32, (T, C), 0)
    p6 = ph[6:7] * pv
    p7 = ph[7:8] * pv
    n0 = nh[0:1] * nv
    x_m1 = jnp.where(rows == 0, p7, pltpu.roll(xm, 1, 0))
    x_m2 = jnp.where(rows == 0, p6, jnp.where(rows == 1, p7, pltpu.roll(xm, 2, 0)))
    x_p1 = jnp.where(rows == T - 1, n0, pltpu.roll(xm, T - 1, 0))
    u = cw[0:1] * x_m2 + cw[1:2] * x_m1 + cw[2:3] * xm + cw[3:4] * x_p1 + cb
    zz = _dot(u.astype(BF16), w) + bias
    r = jax.nn.sigmoid(zz[:, :C])
    ig = jax.nn.sigmoid(zz[:, C:])
    log_a = -LRU_C * r * spl
    a = jnp.exp(log_a)
    b = jnp.sqrt(-jnp.tanh(log_a) * (a * a + 1.0)) * ig * u
    a, b = _chunk_scan(a, b, reverse)

    @pl.when(first)
    def _():
        carry_ref[...] = jnp.zeros(carry_ref.shape, F32)

    h = a * carry_ref[0:1] + b
    edge = h[0:1] if reverse else h[T - 1:T]
    carry_ref[...] = jnp.broadcast_to(edge, carry_ref.shape)
    return h


def _lru_kernel(xf_ref, pf_ref, nf_ref, xb_ref, pb_ref, nb_ref, cw_ref, cb_ref, w_ref, b_ref,
                lam_ref, hf_ref, hb_ref, cf_ref, cbk_ref, *, n_lat):
    j = pl.program_id(0)
    C = D_LRU
    fblk = jnp.where(j == 0, n_lat, j - 1)
    bblk = jnp.where(j == 0, n_lat, n_lat - j)
    lam = lam_ref[...]
    spl = jnp.maximum(-lam, 0.0) + jnp.log1p(jnp.exp(-jnp.abs(lam)))
    cw = cw_ref[...]
    cb = cb_ref[...]

    def halo_valid(blk):
        pv = jnp.logical_and(blk != 0, blk != n_lat).astype(F32)
        nv = jnp.logical_and(blk != n_lat - 1, blk != n_lat).astype(F32)
        return pv, nv

    pv, nv = halo_valid(fblk)
    hf_ref[...] = _lru_direction(xf_ref[...], pf_ref[...], nf_ref[...], pv, nv, cw, cb,
                                 w_ref[:, 0:2 * C], b_ref[:, 0:2 * C], spl[0:1], cf_ref, j == 0, False)
    pv, nv = halo_valid(bblk)
    hb_ref[...] = _lru_direction(xb_ref[...], pb_ref[...], nb_ref[...], pv, nv, cw, cb,
                                 w_ref[:, 2 * C:4 * C], b_ref[:, 2 * C:4 * C], spl[1:2], cbk_ref, j == 0, True)


def _lru(xa, conv_w, conv_b, w_gates, b_gates, lam, n_lat):
    n, C = xa.shape
    T = TOK_TILE
    sub = T // 8
    nblk8 = n // 8
    fblk = lambda j: jnp.where(j == 0, n_lat, j - 1)
    bblk = lambda j: jnp.where(j == 0, n_lat, n_lat - j)
    prev8 = lambda blk: jnp.maximum(blk * sub - 1, 0)
    next8 = lambda blk: jnp.minimum((blk + 1) * sub, nblk8 - 1)
    full = lambda a: pl.BlockSpec(a.shape, lambda j: (0,) * a.ndim)
    return pl.pallas_call(
        functools.partial(_lru_kernel, n_lat=n_lat),
        out_shape=[jax.ShapeDtypeStruct((n, C), F32)] * 2,
        grid=(n_lat + 1,),
        in_specs=[pl.BlockSpec((T, C), lambda j: (fblk(j), 0)),
                  pl.BlockSpec((8, C), lambda j: (prev8(fblk(j)), 0)),
                  pl.BlockSpec((8, C), lambda j: (next8(fblk(j)), 0)),
                  pl.BlockSpec((T, C), lambda j: (bblk(j), 0)),
                  pl.BlockSpec((8, C), lambda j: (prev8(bblk(j)), 0)),
                  pl.BlockSpec((8, C), lambda j: (next8(bblk(j)), 0)),
                  full(conv_w), full(conv_b), full(w_gates), full(b_gates), full(lam)],
        out_specs=[pl.BlockSpec((T, C), lambda j: (fblk(j), 0)),
                   pl.BlockSpec((T, C), lambda j: (bblk(j), 0))],
        scratch_shapes=[pltpu.VMEM((8, C), F32), pltpu.VMEM((8, C), F32)],
        compiler_params=_cparams("arbitrary"),
        name="rglru",
    )(xa, xa, xa, xa, xa, xa, conv_w, conv_b, w_gates, b_gates, lam)


def _win_kernel(sink_ref, q_ref, kp_ref, kc_ref, kn_ref, vp_ref, vc_ref, vn_ref, kx_ref, vx_ref,
                o_ref, *, nb):
    W = WIN_BLOCK
    R = WIN_Q_HEADS // WIN_KV_HEADS
    b = pl.program_id(0)
    blk = jnp.where(b < nb, b, -4)
    rows = lax.broadcasted_iota(jnp.int32, (R * W, 3 * W), 0)
    cols = lax.broadcasted_iota(jnp.int32, (R * W, 3 * W), 1)
    kpos = (blk - 1) * W + cols
    qpos = blk * W + (rows & (W - 1))
    valid = jnp.logical_and(jnp.abs(kpos - qpos) <= W, jnp.logical_and(kpos >= 0, kpos < nb * W))
    row1 = lax.broadcasted_iota(jnp.int32, (R * W, 1), 0)
    q = q_ref[...]
    for g in range(WIN_KV_HEADS):
        ls = slice(g * HEAD_DIM, (g + 1) * HEAD_DIM)
        qs = jnp.concatenate([q[:, (g * R + r) * HEAD_DIM:(g * R + r + 1) * HEAD_DIM] for r in range(R)], axis=0)
        kl = jnp.concatenate([kp_ref[:, ls], kc_ref[:, ls], kn_ref[:, ls]], axis=0)
        vl = jnp.concatenate([vp_ref[:, ls], vc_ref[:, ls], vn_ref[:, ls]], axis=0)
        s_loc = jnp.where(valid, _dot_nt(qs, kl), NEG)
        s_ctx = _dot_nt(qs, kx_ref[:, ls])
        sink = jnp.full((R * W, 1), sink_ref[g * R], F32)
        for r in range(1, R):
            sink = jnp.where(row1 >= r * W, sink_ref[g * R + r], sink)
        m = jnp.maximum(jnp.maximum(jnp.max(s_loc, axis=-1, keepdims=True),
                                    jnp.max(s_ctx, axis=-1, keepdims=True)), sink)
        p_loc = jnp.exp(s_loc - m)
        p_ctx = jnp.exp(s_ctx - m)
        den = (jnp.sum(p_loc, axis=-1, keepdims=True) + jnp.sum(p_ctx, axis=-1, keepdims=True)
               + jnp.exp(sink - m))
        o = (_dot(p_loc.astype(BF16), vl) + _dot(p_ctx.astype(BF16), vx_ref[:, ls])) / den
        for r in range(R):
            hh = g * R + r
            o_ref[:, hh * HEAD_DIM:(hh + 1) * HEAD_DIM] = o[r * W:(r + 1) * W].astype(o_ref.dtype)


def _window_attention(sink, qw, kw, vw, L, with_ctx):
    n = qw.shape[0]
    W = WIN_BLOCK
    nb = L // W
    nq = n // W if with_ctx else nb
    ctx_blk = L // TOK_TILE
    lat = lambda b: jnp.minimum(b, nb - 1)
    kv = lambda f: pl.BlockSpec((W, D_WIN_KV), lambda b: (f(b), 0))
    prv = lambda b: jnp.maximum(lat(b) - 1, 0)
    nxt = lambda b: jnp.minimum(lat(b) + 1, nb - 1)
    ctx = pl.BlockSpec((n - L, D_WIN_KV), lambda b: (ctx_blk, 0))
    return pl.pallas_call(
        functools.partial(_win_kernel, nb=nb),
        out_shape=jax.ShapeDtypeStruct((nq * W, D_WIN_Q), BF16),
        grid=(nq,),
        in_specs=[pl.BlockSpec(memory_space=pltpu.SMEM),
                  pl.BlockSpec((W, D_WIN_Q), lambda b: (b, 0)),
                  kv(prv), kv(lat), kv(nxt), kv(prv), kv(lat), kv(nxt), ctx, ctx],
        out_specs=pl.BlockSpec((W, D_WIN_Q), lambda b: (b, 0)),
        compiler_params=_cparams("arbitrary"),
        name="window_attn",
    )(sink, qw, kw, kw, kw, vw, vw, vw, kw, vw)


def _na_kernel(q_ref, kl_ref, vl_ref, kx_ref, vx_ref, bias_ref, o_ref, *, n_blocks):
    b = pl.program_id(0)
    nq = q_ref.shape[0]
    latent = jnp.full((nq, 1), (b < n_blocks).astype(jnp.int32)) == 1
    q = q_ref[...]
    for h in range(NA_HEADS):
        ls = slice(h * HEAD_DIM, (h + 1) * HEAD_DIM)
        qh = q[:, ls]
        s_loc = jnp.where(latent, _dot_nt(qh, kl_ref[:, ls]) + bias_ref[0, h], NEG)
        s_ctx = _dot_nt(qh, kx_ref[:, ls])
        m = jnp.maximum(jnp.max(s_loc, axis=-1, keepdims=True), jnp.max(s_ctx, axis=-1, keepdims=True))
        p_loc = jnp.exp(s_loc - m)
        p_ctx = jnp.exp(s_ctx - m)
        den = jnp.sum(p_loc, axis=-1, keepdims=True) + jnp.sum(p_ctx, axis=-1, keepdims=True)
        o = (_dot(p_loc.astype(BF16), vl_ref[:, ls]) + _dot(p_ctx.astype(BF16), vx_ref[:, ls])) / den
        o_ref[:, ls] = o.astype(o_ref.dtype)


def _na_bias_table(rpb, rows):
    R, KR = NA_ROW_BLOCK, NA_KEY_ROWS
    qc = np.arange(GRID_W)[:, None]
    kc = np.arange(GRID_W)[None, :]
    qstart = np.clip(qc - NA_KW // 2, 0, GRID_W - NA_KW)
    inside = (kc - qstart >= 0) & (kc - qstart < NA_KW)
    dc = np.clip(kc - qc + NA_KW - 1, 0, 2 * NA_KW - 2)
    per_dr = jnp.where(inside[None, None], rpb[:, :, dc].astype(F32), NEG)
    half = NA_KH // 2
    dr = np.zeros((3, R, KR), np.int64)
    ok = np.zeros((3, R, KR), bool)
    for case, (r0, ks) in enumerate(((0, 0), (half, 0), (rows - R, rows - KR))):
        for rr in range(R):
            r = r0 + rr
            kr = min(max(r - half, 0), rows - NA_KH)
            for kk in range(KR):
                d = ks + kk - r + NA_KH - 1
                ok[case, rr, kk] = kr <= ks + kk < kr + NA_KH
                dr[case, rr, kk] = min(max(d, 0), 2 * NA_KH - 2)
    t = per_dr[:, dr]
    t = jnp.where(ok[None, :, :, :, None, None], t, NEG)
    t = jnp.transpose(t, (1, 0, 2, 4, 3, 5))
    return t.reshape(3, NA_HEADS, R * GRID_W, KR * GRID_W)


def _neighborhood_attention(qn, kn, vn, bias, L, with_ctx):
    n = qn.shape[0]
    R, KR = NA_ROW_BLOCK, NA_KEY_ROWS
    rows = L // GRID_W
    nb = rows // R
    nq = R * GRID_W
    steps = n // nq if with_ctx else nb
    ctx_blk = L // TOK_TILE
    half = NA_KH // 2

    def blk_of(b):
        return jnp.minimum(b, nb - 1)

    def key_start(b):
        return jnp.clip(blk_of(b) * R - half, 0, rows - KR) * GRID_W

    def case_of(b):
        return jnp.where(blk_of(b) == 0, 0, jnp.where(blk_of(b) == nb - 1, 2, 1))

    loc = pl.BlockSpec((pl.Element(KR * GRID_W), pl.Element(D_NA)), lambda b: (key_start(b), 0))
    ctx = pl.BlockSpec((n - L, D_NA), lambda b: (ctx_blk, 0))
    return pl.pallas_call(
        functools.partial(_na_kernel, n_blocks=nb),
        out_shape=jax.ShapeDtypeStruct((steps * nq, D_NA), BF16),
        grid=(steps,),
        in_specs=[pl.BlockSpec((nq, D_NA), lambda b: (b, 0)), loc, loc, ctx, ctx,
                  pl.BlockSpec((1, NA_HEADS, nq, KR * GRID_W), lambda b: (case_of(b), 0, 0, 0))],
        out_specs=pl.BlockSpec((nq, D_NA), lambda b: (b, 0)),
        compiler_params=_cparams("arbitrary"),
        name="neighborhood_attn",
    )(qn, kn, vn, kn, vn, bias)


def _out_kernel(xl_ref, xc_ref, hf_ref, hb_ref, ga_ref, yb_ref, yc_ref, w_ref, mod_ref, g_ref,
                xo_ref, ht_ref, *, n_lat):
    d = xl_ref.shape[1]
    i = pl.program_id(0)
    row = (i == n_lat).astype(jnp.int32)
    is_ctx = jnp.full((TOK_TILE, 1), row) == 1
    x = jnp.where(is_ctx, xc_ref[...], xl_ref[...])
    ya = ((hf_ref[...] + hb_ref[...]) * jax.nn.gelu(ga_ref[...])).astype(BF16)
    o1 = D_LRU
    o2 = D_LRU + D_WIN_Q
    mix = (_dot(ya, w_ref[0:o1]) + _dot(yb_ref[...], w_ref[o1:o2]) + _dot(yc_ref[...], w_ref[o2:o2 + D_NA]))
    xn = x + mod_ref[pl.ds(row, 1), 2 * d:3 * d] * mix
    xo_ref[...] = xn
    ms = jnp.mean(xn * xn, axis=-1, keepdims=True)
    y = xn * lax.rsqrt(ms + EPS) * g_ref[...]
    h2 = y * (1.0 + mod_ref[pl.ds(row, 1), 4 * d:5 * d]) + mod_ref[pl.ds(row, 1), 3 * d:4 * d]
    ht_ref[...] = h2.T.astype(BF16)


def _out_proj(xl, xc, hf, hb, ga, yb, yc, w_bf, mod, g, with_ctx):
    L, d = xl.shape
    T = TOK_TILE
    n_lat = L // T
    nt = n_lat + 1 if with_ctx else n_lat
    tok = lambda w: pl.BlockSpec((T, w), lambda i: (i, 0))
    full = lambda a: pl.BlockSpec(a.shape, lambda i: (0,) * a.ndim)
    return pl.pallas_call(
        functools.partial(_out_kernel, n_lat=n_lat),
        out_shape=[jax.ShapeDtypeStruct((nt * T, d), F32), jax.ShapeDtypeStruct((d, nt * T), BF16)],
        grid=(nt,),
        in_specs=[pl.BlockSpec((T, d), lambda i: (jnp.minimum(i, n_lat - 1), 0)),
                  pl.BlockSpec((T, d), lambda i: (0, 0)),
                  tok(D_LRU), tok(D_LRU), tok(D_LRU), tok(D_WIN_Q), tok(D_NA),
                  full(w_bf), full(mod), full(g)],
        out_specs=[tok(d), pl.BlockSpec((d, T), lambda i: (0, i))],
        compiler_params=_cparams("arbitrary"),
        name="out_proj",
    )(xl, xc, hf, hb, ga, yb, yc, w_bf, mod, g)


SUBLANES = 8


def _sorting_pairs(n):
    pairs, p = [], 1
    while p < n:
        k = p
        while k >= 1:
            for j in range(k % p, n - k, 2 * k):
                for i in range(min(k, n - j - k)):
                    if (i + j) // (2 * p) == (i + j + k) // (2 * p):
                        pairs.append((i + j, i + j + k))
            k //= 2
        p *= 2
    return pairs


def _vmax(a, b):
    if a is None:
        return b
    if b is None:
        return a
    return jnp.maximum(a, b)


def _vmin(a, b):
    if a is None or b is None:
        return None
    return jnp.minimum(a, b)


def _top16_sorted(slabs):
    K = PEER_TOPK
    w = list(slabs)
    for a, b in _sorting_pairs(K):
        w[a], w[b] = _vmax(w[a], w[b]), _vmin(w[a], w[b])
    shift = SUBLANES // 2
    while shift >= 1:
        partner = [None if w[K - 1 - i] is None else pltpu.roll(w[K - 1 - i], shift, 0) for i in range(K)]
        w = [_vmax(w[i], partner[i]) for i in range(K)]
        stride = K // 2
        while stride >= 1:
            for i in range(K):
                if i & stride == 0:
                    w[i], w[i + stride] = _vmax(w[i], w[i + stride]), _vmin(w[i], w[i + stride])
            stride //= 2
        shift //= 2
    return w


def _allsum8(x):
    x = x + pltpu.roll(x, 4, 0)
    x = x + pltpu.roll(x, 2, 0)
    return x + pltpu.roll(x, 1, 0)


def _route_tile(s1, s2):
    K = PEER_TOPK
    S = SUBLANES
    n_slab = s1.shape[0] // S
    a1 = [s1[j * S:(j + 1) * S] for j in range(n_slab)]
    a2 = [s2[j * S:(j + 1) * S] for j in range(n_slab)]
    v1 = _top16_sorted(a1)
    v2 = _top16_sorted(a2)
    sub = lax.broadcasted_iota(jnp.int32, a1[0].shape, 0)

    def as_rows(v, lo):
        out = v[lo]
        for b in range(1, S):
            out = jnp.where(sub == b, v[lo + b], out)
        return out

    v2_lo, v2_hi, v1_hi = as_rows(v2, 0), as_rows(v2, S), as_rows(v1, S)
    lens = [K // (a + 1) for a in range(S)]
    cands = [v1[0] + v2_lo, v1[0] + v2_hi]
    for a in range(1, S):
        ca = v1[a] + v2_lo
        cands.append(ca if lens[a] >= S else jnp.where(sub < lens[a], ca, -jnp.inf))
    cands.append(v1_hi + v2[0])
    thr = _top16_sorted(cands + [None] * (K - len(cands)))[K - 1]
    x2_lo, x2_hi = jnp.exp(v2_lo - v2[0]), jnp.exp(v2_hi - v2[0])
    sel_lo, sel_hi = cands[0] >= thr, cands[1] >= thr
    zsum = jnp.where(sel_lo, x2_lo, 0.0) + jnp.where(sel_hi, x2_hi, 0.0)
    cnt = [_allsum8(jnp.where(sel_lo, 1.0, 0.0) + jnp.where(sel_hi, 1.0, 0.0))]
    for a in range(1, S):
        sel = cands[a + 1] >= thr
        zsum = zsum + jnp.where(sel, jnp.exp(v1[a] - v1[0]) * x2_lo, 0.0)
        cnt.append(_allsum8(jnp.where(sel, 1.0, 0.0)))
    zsum = zsum + jnp.where(cands[S + 1] >= thr, jnp.exp(v1_hi - v1[0]), 0.0)
    inv_z = 1.0 / _allsum8(zsum)
    r2, e2, c1, e1 = [], [], [], []
    for j in range(n_slab):
        r = jnp.full(a2[j].shape, float(K), F32)
        for k in reversed(range(K)):
            r = jnp.where(a2[j] >= v2[k], float(k), r)
        r2.append(r)
        e2.append(jnp.exp(a2[j] - v2[0]) * inv_z)
        c = jnp.where(a1[j] + v2[0] >= thr, 1.0, 0.0)
        for a in reversed(range(S)):
            c = jnp.where(a1[j] >= v1[a], cnt[a], c)
        c1.append(c)
        e1.append(jnp.exp(a1[j] - v1[0]))
    cat = lambda xs: jnp.concatenate(xs, axis=0)
    return cat(r2), cat(e2), cat(c1), cat(e1)


def _prep_kernel(ht_ref, wq_ref, k1_ref, k2_ref, r2_ref, e2_ref, c1_ref, e1_ref, q_scr):
    nk = PEER_NKEYS
    q_scr[...] = _dot(wq_ref[...], ht_ref[...])

    def head(h, carry):
        base = pl.multiple_of(h * 2 * nk, 2 * nk)
        s1 = _dot(k1_ref[...], q_scr[pl.ds(base, nk), :].astype(BF16))
        s2 = _dot(k2_ref[...], q_scr[pl.ds(base + nk, nk), :].astype(BF16))
        for t in range(s1.shape[1] // LANES):
            ts = slice(t * LANES, (t + 1) * LANES)
            r2, e2, c1, e1 = _route_tile(s1[:, ts], s2[:, ts])
            r2_ref[h, :, ts] = _pack_rows(r2)
            e2_ref[h, :, ts] = _pack_rows(e2)
            c1_ref[h, :, ts] = c1
            e1_ref[h, :, ts] = e1
        return carry

    lax.fori_loop(0, PEER_HEADS, head, 0)


def _peer_prep(ht, wq_t, k1, k2):
    d, n = ht.shape
    T = PREP_TOK_TILE
    nk = PEER_NKEYS
    full = lambda a: pl.BlockSpec(a.shape, lambda i: (0,) * a.ndim)
    spec = lambda rows: pl.BlockSpec((PEER_HEADS, rows, T), lambda i: (0, 0, i))
    return pl.pallas_call(
        _prep_kernel,
        out_shape=[jax.ShapeDtypeStruct((PEER_HEADS, nk // 2, n), U32)] * 2
        + [jax.ShapeDtypeStruct((PEER_HEADS, nk, n), F32)] * 2,
        grid=(n // T,),
        in_specs=[pl.BlockSpec((d, T), lambda i: (0, i)), full(wq_t), full(k1), full(k2)],
        out_specs=[spec(nk // 2), spec(nk // 2), spec(nk), spec(nk)],
        scratch_shapes=[pltpu.VMEM((wq_t.shape[0], T), F32)],
        compiler_params=_cparams("arbitrary"),
        name="peer_prep",
    )(ht, wq_t, k1, k2)


def _peer_kernel(ht_ref, u0_ref, us_ref, vs_ref, vl_ref, r2_ref, e2_ref, c1_ref, e1_ref, x_ref, g_ref,
                 o_ref, acc_ref, *ap_refs):
    nk = PEER_NKEYS
    c = pl.program_id(1)
    n_sub = len(ap_refs) // 2
    a_refs, p_refs = ap_refs[:n_sub], ap_refs[n_sub:]
    sub = 2 * a_refs[0].shape[0]
    T = a_refs[0].shape[1]
    rows_per_sub = sub // nk
    n_rows = n_sub * rows_per_sub
    PK = 16
    WK = PK // 2
    base = pl.multiple_of(c * n_rows, n_rows)

    tok_blocks = [slice(n, min(n + PEER_MXU_TOKENS, T)) for n in range(0, T, PEER_MXU_TOKENS)]

    def activations_from(rows_ref, dst_ref):
        for tb in tok_blocks:
            dst_ref[:, tb] = _pack_rows(_dot(rows_ref, ht_ref[:, tb]))

    def activations(s):
        activations_from(us_ref[(s - 1) * sub:s * sub, :], a_refs[s])

    def gates(s):
        for t in range(T // LANES):
            ts = slice(t * LANES, (t + 1) * LANES)
            for j in range(rows_per_sub):
                row = s * rows_per_sub + j
                gate = [None] * (nk // PK)
                for h in range(PEER_HEADS):
                    c1 = c1_ref[h, pl.ds(base, n_rows), ts][row:row + 1]
                    e1 = e1_ref[h, pl.ds(base, n_rows), ts][row:row + 1]
                    c1 = jnp.broadcast_to(c1, (PK, LANES)).astype(BF16)
                    e1 = jnp.broadcast_to(e1, (PK, LANES)).astype(BF16)
                    for v in range(nk // PK):
                        ws = slice(v * WK, (v + 1) * WK)
                        r2 = _unpack_rows(r2_ref[h, ws, ts])
                        e2 = _unpack_rows(e2_ref[h, ws, ts])
                        g = jnp.where(r2 < c1, e2 * e1, 0.0)
                        gate[v] = g if h == 0 else gate[v] + g
                for v in range(nk // PK):
                    ws = slice((j * nk + v * PK) // 2, (j * nk + (v + 1) * PK) // 2)
                    a = _unpack_rows(a_refs[s][ws, ts])
                    th = jnp.tanh(a * (GELU_K0 + GELU_K1 * (a * a)))
                    p_refs[s][ws, ts] = pltpu.bitcast(gate[v] * (a * (0.5 + 0.5 * th)), U32)

    def project_from(cols_ref, src_ref):
        for tb in tok_blocks:
            acc_ref[:, tb] += _dot(cols_ref, _unpack_rows(src_ref[:, tb]))

    def project(s):
        project_from(vs_ref[:, (s + 1) * sub:(s + 2) * sub], p_refs[s])

    last = n_sub - 1

    def project_deferred():
        project_from(vs_ref[:, 0:sub], p_refs[last])

    @pl.when(c == 0)
    def _():
        acc_ref[...] = jnp.zeros(acc_ref.shape, F32)
        p_refs[last][...] = jnp.zeros(p_refs[last].shape, U32)
        activations_from(u0_ref[...], a_refs[0])

    for s in range(n_sub):
        gates(s)
        if s == 0:
            project_deferred()
        else:
            project(s - 1)
        if s < last:
            activations(s + 1)
    activations_from(us_ref[last * sub:(last + 1) * sub, :], a_refs[0])

    @pl.when(c == pl.num_programs(1) - 1)
    def _():
        project_from(vl_ref[...], p_refs[last])
        o_ref[...] = x_ref[...] + g_ref[...] * acc_ref[...].T


def _peer_weights(u, v):
    SUB = PEER_SUB_CHUNK
    u_bf = u.astype(BF16)
    u0 = u_bf[:SUB]
    us = jnp.concatenate([u_bf[SUB:], jnp.zeros((SUB, u.shape[1]), BF16)], axis=0)
    vs = jnp.concatenate([jnp.zeros((v.shape[1], SUB), BF16), v.T.astype(BF16)], axis=1)
    return u0, us, vs


def _peer_dense(ht, u0, us, vs, r2, e2, c1, e1, x, g2, tok_tile, tok_off, n_tok):
    d = ht.shape[0]
    n_exp = us.shape[0]
    T = tok_tile
    NC = PEER_EXPERT_CHUNK
    nk = PEER_NKEYS
    SUB = PEER_SUB_CHUNK
    n_sub = NC // SUB
    rt = lambda rows: pl.BlockSpec((PEER_HEADS, rows, T), lambda i, c: (0, 0, i + tok_off))
    return pl.pallas_call(
        _peer_kernel,
        out_shape=jax.ShapeDtypeStruct((n_tok, d), F32),
        grid=(n_tok // T, n_exp // NC),
        in_specs=[pl.BlockSpec((d, T), lambda i, c: (0, i + tok_off)),
                  pl.BlockSpec((SUB, d), lambda i, c: (0, 0)),
                  pl.BlockSpec((NC, d), lambda i, c: (c, 0)),
                  pl.BlockSpec((d, NC), lambda i, c: (0, c)),
                  pl.BlockSpec((d, SUB), lambda i, c: (0, n_exp // SUB)),
                  rt(nk // 2), rt(nk // 2), rt(nk), rt(nk),
                  pl.BlockSpec((T, d), lambda i, c: (i + tok_off, 0)),
                  pl.BlockSpec((1, d), lambda i, c: (0, 0))],
        out_specs=pl.BlockSpec((T, d), lambda i, c: (i, 0)),
        scratch_shapes=[pltpu.VMEM((d, T), F32)] + [pltpu.VMEM((PEER_SUB_CHUNK // 2, T), U32)] * (2 * n_sub),
        compiler_params=_cparams("arbitrary", "arbitrary"),
        name="peer_dense",
    )(ht, u0, us, vs, vs, r2, e2, c1, e1, x, g2)


def _rope_tables(L, n):
    t = jnp.arange(L)
    row = (t // GRID_W).astype(F32)
    col = (t % GRID_W).astype(F32)
    q = HEAD_DIM // 4
    inv = ROPE_THETA ** (-jnp.arange(q, dtype=F32) / q)
    ar = row[:, None] * inv
    ac = col[:, None] * inv
    cos = jnp.concatenate([jnp.cos(ar), jnp.cos(ar), jnp.cos(ac), jnp.cos(ac)], axis=-1)
    sin = jnp.concatenate([-jnp.sin(ar), jnp.sin(ar), -jnp.sin(ac), jnp.sin(ac)], axis=-1)
    cos = jnp.concatenate([cos, jnp.ones((n - L, HEAD_DIM), F32)], axis=0)
    sin = jnp.concatenate([sin, jnp.zeros((n - L, HEAD_DIM), F32)], axis=0)
    return jnp.tile(cos, (1, LANES // HEAD_DIM)), jnp.tile(sin, (1, LANES // HEAD_DIM))


def _block_diag(w):
    nb, di, do = w.shape
    eye = jnp.eye(nb, dtype=w.dtype)
    return (eye[:, None, :, None] * w[:, :, None, :]).reshape(nb * di, nb * do)


def kernel(x, c, ctx, c_ctx, w_mod, b_mod, norm1_g, norm2_g, w_in, w_out, lru_conv_w, lru_conv_b, lru_wa, lru_ba, lru_wx, lru_bx, lru_lam, win_qn_g, win_kn_g, win_sink, na_qn_g, na_kn_g, na_rpb, peer_wq, peer_k1, peer_k2, peer_u, peer_v):
    B, L, D = x.shape
    Lc = ctx.shape[1]
    depth = w_mod.shape[0]
    assert B == 1 and Lc == TOK_TILE and L % PEER_TOK_TILE == 0 and L // GRID_W >= 3 * NA_ROW_BLOCK
    n = L + Lc
    n_lat = L // TOK_TILE
    rows = L // GRID_W

    cvec = jnp.zeros((8, D), F32).at[0].set(c[0]).at[1].set(c_ctx)
    mods = _modulation(cvec, w_mod, b_mod)
    cos_t, sin_t = _rope_tables(L, n)
    tile2 = lambda g: jnp.tile(g, LANES // HEAD_DIM)

    xl, xc = x[0], ctx[0]
    for l in range(depth):
        with_ctx = l < depth - 1
        mod = mods[l]
        hg = jnp.zeros((8, LANES), F32)
        hg = hg.at[0].set(tile2(win_qn_g[l])).at[1].set(tile2(win_kn_g[l]))
        hg = hg.at[2].set(tile2(na_qn_g[l])).at[3].set(tile2(na_kn_g[l]))
        xa, ga, qw, kw, vw, qn, kn, vn = _in_proj(xl, xc, mod, norm1_g[l][None], w_in[l].astype(BF16),
                                                  cos_t, sin_t, hg)

        w_gates = jnp.concatenate([_block_diag(lru_wa[l, 0]), _block_diag(lru_wx[l, 0]),
                                   _block_diag(lru_wa[l, 1]), _block_diag(lru_wx[l, 1])], axis=1).astype(BF16)
        b_gates = jnp.concatenate([lru_ba[l, 0], lru_bx[l, 0], lru_ba[l, 1], lru_bx[l, 1]])[None]
        conv_w = jnp.zeros((8, D_LRU), F32).at[0:lru_conv_w.shape[1]].set(lru_conv_w[l])
        lam = jnp.zeros((8, D_LRU), F32).at[0:2].set(lru_lam[l])
        hf, hb = _lru(xa, conv_w, lru_conv_b[l][None], w_gates, b_gates, lam, n_lat)

        yb = _window_attention(win_sink[l], qw, kw, vw, L, with_ctx)
        yc = _neighborhood_attention(qn, kn, vn, _na_bias_table(na_rpb[l], rows), L, with_ctx)

        xn, ht = _out_proj(xl, xc, hf, hb, ga, yb, yc, w_out[l].astype(BF16), mod, norm2_g[l][None], with_ctx)

        r2, e2, c1, e1 = _peer_prep(ht, peer_wq[l].T.astype(BF16), peer_k1[l].astype(BF16),
                                    peer_k2[l].astype(BF16))
        u0, us, vs = _peer_weights(peer_u[l], peer_v[l])
        g2 = mod[:, 5 * D:6 * D]
        xl_new = _peer_dense(ht, u0, us, vs, r2, e2, c1, e1, xn, g2[0:1], PEER_TOK_TILE, 0, L)
        if with_ctx:
            xc = _peer_dense(ht, u0, us, vs, r2, e2, c1, e1, xn, g2[1:2], Lc, L // Lc, Lc)
        xl = xl_new
    return xl[None]
```

```python
import functools

import numpy as np
import jax
import jax.numpy as jnp
from jax import lax
from jax.experimental import pallas as pl
from jax.experimental.pallas import tpu as pltpu

F32 = jnp.float32
BF16 = jnp.bfloat16
U32 = jnp.uint32

HEAD_DIM = 64
GRID_W = 64
EPS = 1e-6
ROPE_THETA = 10000.0
D_LRU = 256
LRU_BLOCKS = 4
LRU_C = 8.0
WIN_Q_HEADS = 6
WIN_KV_HEADS = 2
WIN_BLOCK = 128
NA_HEADS = 6
NA_KH = 8
NA_KW = 16
NA_ROW_BLOCK = 4
NA_KEY_ROWS = NA_ROW_BLOCK + NA_KH - 1
PEER_HEADS = 8
PEER_NKEYS = 128
PEER_TOPK = 16
D_WIN_Q = WIN_Q_HEADS * HEAD_DIM
D_WIN_KV = WIN_KV_HEADS * HEAD_DIM
D_NA = NA_HEADS * HEAD_DIM

LANES = 128
TOK_TILE = 256
PEER_TOK_TILE = 512
PEER_EXPERT_CHUNK = 2048
PEER_SUB_CHUNK = 512
PEER_MXU_TOKENS = 256
PREP_TOK_TILE = 256
VMEM_LIMIT = 56 * 1024 * 1024
NEG = -1e30
GELU_K0 = float(np.sqrt(2.0 / np.pi))
GELU_K1 = 0.044715 * GELU_K0


def _cparams(*sem):
    return pltpu.CompilerParams(dimension_semantics=sem, vmem_limit_bytes=VMEM_LIMIT)


def _dot(a, b):
    return jnp.dot(a, b, preferred_element_type=F32)


def _pack_rows(x):
    return pltpu.bitcast(x.astype(BF16), U32)


def _unpack_rows(w):
    return pltpu.bitcast(w, BF16)


def _dup_halves(x):
    w = pltpu.bitcast(x.astype(BF16).astype(F32), U32)
    return w | (w >> 16)


def _dot_nt(a, b):
    return lax.dot_general(a, b, (((1,), (1,)), ((), ())), preferred_element_type=F32)


def _mod_kernel(c_ref, w_ref, b_ref, o_ref):
    c = c_ref[...]
    a = (c * jax.nn.sigmoid(c)).astype(BF16)
    o_ref[0] = _dot(a, w_ref[0].astype(BF16)) + b_ref[0]


def _modulation(cvec, w_mod, b_mod):
    depth, d, d6 = w_mod.shape
    return pl.pallas_call(
        _mod_kernel,
        out_shape=jax.ShapeDtypeStruct((depth, 8, d6), F32),
        grid=(depth, d6 // d),
        in_specs=[pl.BlockSpec((8, d), lambda l, j: (0, 0)),
                  pl.BlockSpec((1, d, d), lambda l, j: (l, 0, j)),
                  pl.BlockSpec((1, 1, d), lambda l, j: (l, 0, j))],
        out_specs=pl.BlockSpec((1, 8, d), lambda l, j: (l, 0, j)),
        compiler_params=_cparams("arbitrary", "arbitrary"),
        name="modulation",
    )(cvec, w_mod, b_mod.reshape(depth, 1, d6))


def _pair_mean_matrix():
    r = lax.broadcasted_iota(jnp.int32, (LANES, LANES), 0) < HEAD_DIM
    c = lax.broadcasted_iota(jnp.int32, (LANES, LANES), 1) < HEAD_DIM
    return jnp.where(r == c, 1.0 / HEAD_DIM, 0.0).astype(BF16)


def _head_norm(z, gain, bd):
    z2 = z * z
    hi = z2.astype(BF16)
    lo = (z2 - hi.astype(F32)).astype(BF16)
    ms = _dot(hi, bd) + _dot(lo, bd)
    return z * lax.rsqrt(ms + EPS) * gain


def _rope(z, cos, sin):
    lane = lax.broadcasted_iota(jnp.int32, z.shape, 1)
    first = (lane & 16) == 0
    partner = jnp.where(first, pltpu.roll(z, LANES - 16, 1), pltpu.roll(z, 16, 1))
    return z * cos + partner * sin


def _in_kernel(xl_ref, xc_ref, mod_ref, g_ref, w_ref, cos_ref, sin_ref, hg_ref,
               xa_ref, ga_ref, qw_ref, kw_ref, vw_ref, qn_ref, kn_ref, vn_ref, *, n_lat):
    d = xl_ref.shape[1]
    i = pl.program_id(0)
    row = (i == n_lat).astype(jnp.int32)
    is_ctx = jnp.full((TOK_TILE, 1), row) == 1
    x = jnp.where(is_ctx, xc_ref[...], xl_ref[...])
    ms = jnp.mean(x * x, axis=-1, keepdims=True)
    y = x * lax.rsqrt(ms + EPS) * g_ref[...]
    sh = mod_ref[pl.ds(row, 1), 0:d]
    sc = mod_ref[pl.ds(row, 1), d:2 * d]
    h = (y * (1.0 + sc) + sh).astype(BF16)
    z = _dot(h, w_ref[...])

    bd = _pair_mean_matrix()
    cos = cos_ref[...]
    sin = sin_ref[...]
    scale = HEAD_DIM ** -0.5
    o = 0
    xa_ref[...] = z[:, o:o + D_LRU]
    o += D_LRU
    ga_ref[...] = z[:, o:o + D_LRU]
    o += D_LRU
    for g in range(D_WIN_Q // LANES):
        zz = _rope(_head_norm(z[:, o:o + LANES], hg_ref[0:1], bd), cos, sin)
        qw_ref[:, g * LANES:(g + 1) * LANES] = (zz * scale).astype(BF16)
        o += LANES
    kw_ref[...] = _rope(_head_norm(z[:, o:o + LANES], hg_ref[1:2], bd), cos, sin).astype(BF16)
    o += LANES
    vw_ref[...] = z[:, o:o + LANES].astype(BF16)
    o += LANES
    for g in range(D_NA // LANES):
        zz = _head_norm(z[:, o:o + LANES], hg_ref[2:3], bd)
        qn_ref[:, g * LANES:(g + 1) * LANES] = (zz * scale).astype(BF16)
        o += LANES
    for g in range(D_NA // LANES):
        zz = _head_norm(z[:, o:o + LANES], hg_ref[3:4], bd)
        kn_ref[:, g * LANES:(g + 1) * LANES] = zz.astype(BF16)
        o += LANES
    vn_ref[...] = z[:, o:o + D_NA].astype(BF16)


def _in_proj(xl, xc, mod, g, w_bf, cos_t, sin_t, hg):
    L, d = xl.shape
    n_lat = L // TOK_TILE
    n = L + xc.shape[0]
    d_in = w_bf.shape[1]
    T = TOK_TILE
    tok = lambda w: pl.BlockSpec((T, w), lambda i: (i, 0))
    full = lambda a: pl.BlockSpec(a.shape, lambda i: (0,) * a.ndim)
    widths = (D_LRU, D_LRU, D_WIN_Q, D_WIN_KV, D_WIN_KV, D_NA, D_NA, D_NA)
    dtypes = (F32, F32, BF16, BF16, BF16, BF16, BF16, BF16)
    return pl.pallas_call(
        functools.partial(_in_kernel, n_lat=n_lat),
        out_shape=[jax.ShapeDtypeStruct((n, w), t) for w, t in zip(widths, dtypes)],
        grid=(n_lat + 1,),
        in_specs=[pl.BlockSpec((T, d), lambda i: (jnp.minimum(i, n_lat - 1), 0)),
                  pl.BlockSpec((T, d), lambda i: (0, 0)),
                  full(mod), full(g), full(w_bf), tok(LANES), tok(LANES), full(hg)],
        out_specs=[tok(w) for w in widths],
        compiler_params=_cparams("arbitrary"),
        name="in_proj",
    )(xl, xc, mod, g, w_bf, cos_t, sin_t, hg)


def _chunk_scan(a, b, reverse):
    T = a.shape[0]
    rows = lax.broadcasted_iota(jnp.int32, a.shape, 0)
    s = 1
    while s < T:
        if reverse:
            edge = rows >= T - s
            shift = T - s
        else:
            edge = rows < s
            shift = s
        a_s = jnp.where(edge, 1.0, pltpu.roll(a, shift, 0))
        b_s = jnp.where(edge, 0.0, pltpu.roll(b, shift, 0))
        b = a * b_s + b
        a = a * a_s
        s *= 2
    return a, b


def _lru_direction(xm, ph, nh, pv, nv, cw, cb, w, bias, spl, carry_ref, first, reverse):
    T, C = xm.shape
    rows = lax.broadcasted_iota(jnp.int32, (T, C), 0)
    p6 = ph[6:7] * pv
    p7 = ph[7:8] * pv
    n0 = nh[0:1] * nv
    x_m1 = jnp.where(rows == 0, p7, pltpu.roll(xm, 1, 0))
    x_m2 = jnp.where(rows == 0, p6, jnp.where(rows == 1, p7, pltpu.roll(xm, 2, 0)))
    x_p1 = jnp.where(rows == T - 1, n0, pltpu.roll(xm, T - 1, 0))
    u = cw[0:1] * x_m2 + cw[1:2] * x_m1 + cw[2:3] * xm + cw[3:4] * x_p1 + cb
    zz = _dot(u.astype(BF16), w) + bias
    r = jax.nn.sigmoid(zz[:, :C])
    ig = jax.nn.sigmoid(zz[:, C:])
    log_a = -LRU_C * r * spl
    a = jnp.exp(log_a)
    b = jnp.sqrt(-jnp.tanh(log_a) * (a * a + 1.0)) * ig * u
    a, b = _chunk_scan(a, b, reverse)

    @pl.when(first)
    def _():
        carry_ref[...] = jnp.zeros(carry_ref.shape, F32)

    h = a * carry_ref[0:1] + b
    edge = h[0:1] if reverse else h[T - 1:T]
    carry_ref[...] = jnp.broadcast_to(edge, carry_ref.shape)
    return h


def _lru_kernel(xf_ref, pf_ref, nf_ref, xb_ref, pb_ref, nb_ref, cw_ref, cb_ref, w_ref, b_ref,
                lam_ref, hf_ref, hb_ref, cf_ref, cbk_ref, *, n_lat):
    j = pl.program_id(0)
    C = D_LRU
    fblk = jnp.where(j == 0, n_lat, j - 1)
    bblk = jnp.where(j == 0, n_lat, n_lat - j)
    lam = lam_ref[...]
    spl = jnp.maximum(-lam, 0.0) + jnp.log1p(jnp.exp(-jnp.abs(lam)))
    cw = cw_ref[...]
    cb = cb_ref[...]

    def halo_valid(blk):
        pv = jnp.logical_and(blk != 0, blk != n_lat).astype(F32)
        nv = jnp.logical_and(blk != n_lat - 1, blk != n_lat).astype(F32)
        return pv, nv

    pv, nv = halo_valid(fblk)
    hf_ref[...] = _lru_direction(xf_ref[...], pf_ref[...], nf_ref[...], pv, nv, cw, cb,
                                 w_ref[:, 0:2 * C], b_ref[:, 0:2 * C], spl[0:1], cf_ref, j == 0, False)
    pv, nv = halo_valid(bblk)
    hb_ref[...] = _lru_direction(xb_ref[...], pb_ref[...], nb_ref[...], pv, nv, cw, cb,
                                 w_ref[:, 2 * C:4 * C], b_ref[:, 2 * C:4 * C], spl[1:2], cbk_ref, j == 0, True)


def _lru(xa, conv_w, conv_b, w_gates, b_gates, lam, n_lat):
    n, C = xa.shape
    T = TOK_TILE
    sub = T // 8
    nblk8 = n // 8
    fblk = lambda j: jnp.where(j == 0, n_lat, j - 1)
    bblk = lambda j: jnp.where(j == 0, n_lat, n_lat - j)
    prev8 = lambda blk: jnp.maximum(blk * sub - 1, 0)
    next8 = lambda blk: jnp.minimum((blk + 1) * sub, nblk8 - 1)
    full = lambda a: pl.BlockSpec(a.shape, lambda j: (0,) * a.ndim)
    return pl.pallas_call(
        functools.partial(_lru_kernel, n_lat=n_lat),
        out_shape=[jax.ShapeDtypeStruct((n, C), F32)] * 2,
        grid=(n_lat + 1,),
        in_specs=[pl.BlockSpec((T, C), lambda j: (fblk(j), 0)),
                  pl.BlockSpec((8, C), lambda j: (prev8(fblk(j)), 0)),
                  pl.BlockSpec((8, C), lambda j: (next8(fblk(j)), 0)),
                  pl.BlockSpec((T, C), lambda j: (bblk(j), 0)),
                  pl.BlockSpec((8, C), lambda j: (prev8(bblk(j)), 0)),
                  pl.BlockSpec((8, C), lambda j: (next8(bblk(j)), 0)),
                  full(conv_w), full(conv_b), full(w_gates), full(b_gates), full(lam)],
        out_specs=[pl.BlockSpec((T, C), lambda j: (fblk(j), 0)),
                   pl.BlockSpec((T, C), lambda j: (bblk(j), 0))],
        scratch_shapes=[pltpu.VMEM((8, C), F32), pltpu.VMEM((8, C), F32)],
        compiler_params=_cparams("arbitrary"),
        name="rglru",
    )(xa, xa, xa, xa, xa, xa, conv_w, conv_b, w_gates, b_gates, lam)


def _win_kernel(sink_ref, q_ref, kp_ref, kc_ref, kn_ref, vp_ref, vc_ref, vn_ref, kx_ref, vx_ref,
                o_ref, *, nb):
    W = WIN_BLOCK
    R = WIN_Q_HEADS // WIN_KV_HEADS
    b = pl.program_id(0)
    blk = jnp.where(b < nb, b, -4)
    rows = lax.broadcasted_iota(jnp.int32, (R * W, 3 * W), 0)
    cols = lax.broadcasted_iota(jnp.int32, (R * W, 3 * W), 1)
    kpos = (blk - 1) * W + cols
    qpos = blk * W + (rows & (W - 1))
    valid = jnp.logical_and(jnp.abs(kpos - qpos) <= W, jnp.logical_and(kpos >= 0, kpos < nb * W))
    row1 = lax.broadcasted_iota(jnp.int32, (R * W, 1), 0)
    q = q_ref[...]
    for g in range(WIN_KV_HEADS):
        ls = slice(g * HEAD_DIM, (g + 1) * HEAD_DIM)
        qs = jnp.concatenate([q[:, (g * R + r) * HEAD_DIM:(g * R + r + 1) * HEAD_DIM] for r in range(R)], axis=0)
        kl = jnp.concatenate([kp_ref[:, ls], kc_ref[:, ls], kn_ref[:, ls]], axis=0)
        vl = jnp.concatenate([vp_ref[:, ls], vc_ref[:, ls], vn_ref[:, ls]], axis=0)
        s_loc = jnp.where(valid, _dot_nt(qs, kl), NEG)
        s_ctx = _dot_nt(qs, kx_ref[:, ls])
        sink = jnp.full((R * W, 1), sink_ref[g * R], F32)
        for r in range(1, R):
            sink = jnp.where(row1 >= r * W, sink_ref[g * R + r], sink)
        m = jnp.maximum(jnp.maximum(jnp.max(s_loc, axis=-1, keepdims=True),
                                    jnp.max(s_ctx, axis=-1, keepdims=True)), sink)
        p_loc = jnp.exp(s_loc - m)
        p_ctx = jnp.exp(s_ctx - m)
        den = (jnp.sum(p_loc, axis=-1, keepdims=True) + jnp.sum(p_ctx, axis=-1, keepdims=True)
               + jnp.exp(sink - m))
        o = (_dot(p_loc.astype(BF16), vl) + _dot(p_ctx.astype(BF16), vx_ref[:, ls])) / den
        for r in range(R):
            hh = g * R + r
            o_ref[:, hh * HEAD_DIM:(hh + 1) * HEAD_DIM] = o[r * W:(r + 1) * W].astype(o_ref.dtype)


def _window_attention(sink, qw, kw, vw, L, with_ctx):
    n = qw.shape[0]
    W = WIN_BLOCK
    nb = L // W
    nq = n // W if with_ctx else nb
    ctx_blk = L // TOK_TILE
    lat = lambda b: jnp.minimum(b, nb - 1)
    kv = lambda f: pl.BlockSpec((W, D_WIN_KV), lambda b: (f(b), 0))
    prv = lambda b: jnp.maximum(lat(b) - 1, 0)
    nxt = lambda b: jnp.minimum(lat(b) + 1, nb - 1)
    ctx = pl.BlockSpec((n - L, D_WIN_KV), lambda b: (ctx_blk, 0))
    return pl.pallas_call(
        functools.partial(_win_kernel, nb=nb),
        out_shape=jax.ShapeDtypeStruct((nq * W, D_WIN_Q), BF16),
        grid=(nq,),
        in_specs=[pl.BlockSpec(memory_space=pltpu.SMEM),
                  pl.BlockSpec((W, D_WIN_Q), lambda b: (b, 0)),
                  kv(prv), kv(lat), kv(nxt), kv(prv), kv(lat), kv(nxt), ctx, ctx],
        out_specs=pl.BlockSpec((W, D_WIN_Q), lambda b: (b, 0)),
        compiler_params=_cparams("arbitrary"),
        name="window_attn",
    )(sink, qw, kw, kw, kw, vw, vw, vw, kw, vw)


def _na_kernel(q_ref, kl_ref, vl_ref, kx_ref, vx_ref, bias_ref, o_ref, *, n_blocks):
    b = pl.program_id(0)
    nq = q_ref.shape[0]
    latent = jnp.full((nq, 1), (b < n_blocks).astype(jnp.int32)) == 1
    q = q_ref[...]
    for h in range(NA_HEADS):
        ls = slice(h * HEAD_DIM, (h + 1) * HEAD_DIM)
        qh = q[:, ls]
        s_loc = jnp.where(latent, _dot_nt(qh, kl_ref[:, ls]) + bias_ref[0, h], NEG)
        s_ctx = _dot_nt(qh, kx_ref[:, ls])
        m = jnp.maximum(jnp.max(s_loc, axis=-1, keepdims=True), jnp.max(s_ctx, axis=-1, keepdims=True))
        p_loc = jnp.exp(s_loc - m)
        p_ctx = jnp.exp(s_ctx - m)
        den = jnp.sum(p_loc, axis=-1, keepdims=True) + jnp.sum(p_ctx, axis=-1, keepdims=True)
        o = (_dot(p_loc.astype(BF16), vl_ref[:, ls]) + _dot(p_ctx.astype(BF16), vx_ref[:, ls])) / den
        o_ref[:, ls] = o.astype(o_ref.dtype)


def _na_bias_table(rpb, rows):
    R, KR = NA_ROW_BLOCK, NA_KEY_ROWS
    qc = np.arange(GRID_W)[:, None]
    kc = np.arange(GRID_W)[None, :]
    qstart = np.clip(qc - NA_KW // 2, 0, GRID_W - NA_KW)
    inside = (kc - qstart >= 0) & (kc - qstart < NA_KW)
    dc = np.clip(kc - qc + NA_KW - 1, 0, 2 * NA_KW - 2)
    per_dr = jnp.where(inside[None, None], rpb[:, :, dc].astype(F32), NEG)
    n_dr = 2 * NA_KH - 1
    src = jnp.concatenate([per_dr, jnp.full_like(per_dr[:, :1], NEG)], axis=1).transpose(0, 2, 1, 3)
    half = NA_KH // 2
    dr = np.full((3, R, KR), n_dr, np.int64)
    for case, (r0, ks) in enumerate(((0, 0), (half, 0), (rows - R, rows - KR))):
        for rr in range(R):
            r = r0 + rr
            kr = min(max(r - half, 0), rows - NA_KH)
            for kk in range(KR):
                if kr <= ks + kk < kr + NA_KH:
                    dr[case, rr, kk] = ks + kk - r + NA_KH - 1
    h_idx = np.arange(NA_HEADS)[None, :, None, None, None]
    q_idx = np.arange(GRID_W)[None, None, None, :, None]
    t = src[h_idx, q_idx, dr[:, None, :, None, :]]
    return t.reshape(3, NA_HEADS, R * GRID_W, KR * GRID_W)


def _neighborhood_attention(qn, kn, vn, bias, L, with_ctx):
    n = qn.shape[0]
    R, KR = NA_ROW_BLOCK, NA_KEY_ROWS
    rows = L // GRID_W
    nb = rows // R
    nq = R * GRID_W
    steps = n // nq if with_ctx else nb
    ctx_blk = L // TOK_TILE
    half = NA_KH // 2

    def blk_of(b):
        return jnp.minimum(b, nb - 1)

    def key_start(b):
        return jnp.clip(blk_of(b) * R - half, 0, rows - KR) * GRID_W

    def case_of(b):
        return jnp.where(blk_of(b) == 0, 0, jnp.where(blk_of(b) == nb - 1, 2, 1))

    loc = pl.BlockSpec((pl.Element(KR * GRID_W), pl.Element(D_NA)), lambda b: (key_start(b), 0))
    ctx = pl.BlockSpec((n - L, D_NA), lambda b: (ctx_blk, 0))
    return pl.pallas_call(
        functools.partial(_na_kernel, n_blocks=nb),
        out_shape=jax.ShapeDtypeStruct((steps * nq, D_NA), BF16),
        grid=(steps,),
        in_specs=[pl.BlockSpec((nq, D_NA), lambda b: (b, 0)), loc, loc, ctx, ctx,
                  pl.BlockSpec((1, NA_HEADS, nq, KR * GRID_W), lambda b: (case_of(b), 0, 0, 0))],
        out_specs=pl.BlockSpec((nq, D_NA), lambda b: (b, 0)),
        compiler_params=_cparams("arbitrary"),
        name="neighborhood_attn",
    )(qn, kn, vn, kn, vn, bias)


def _out_kernel(xl_ref, xc_ref, hf_ref, hb_ref, ga_ref, yb_ref, yc_ref, w_ref, mod_ref, g_ref,
                xo_ref, ht_ref, *, n_lat):
    d = xl_ref.shape[1]
    i = pl.program_id(0)
    row = (i == n_lat).astype(jnp.int32)
    is_ctx = jnp.full((TOK_TILE, 1), row) == 1
    x = jnp.where(is_ctx, xc_ref[...], xl_ref[...])
    ya = ((hf_ref[...] + hb_ref[...]) * jax.nn.gelu(ga_ref[...])).astype(BF16)
    o1 = D_LRU
    o2 = D_LRU + D_WIN_Q
    mix = (_dot(ya, w_ref[0:o1]) + _dot(yb_ref[...], w_ref[o1:o2]) + _dot(yc_ref[...], w_ref[o2:o2 + D_NA]))
    xn = x + mod_ref[pl.ds(row, 1), 2 * d:3 * d] * mix
    xo_ref[...] = xn
    ms = jnp.mean(xn * xn, axis=-1, keepdims=True)
    y = xn * lax.rsqrt(ms + EPS) * g_ref[...]
    h2 = y * (1.0 + mod_ref[pl.ds(row, 1), 4 * d:5 * d]) + mod_ref[pl.ds(row, 1), 3 * d:4 * d]
    ht_ref[...] = h2.T.astype(BF16)


def _out_proj(xl, xc, hf, hb, ga, yb, yc, w_bf, mod, g, with_ctx):
    L, d = xl.shape
    T = TOK_TILE
    n_lat = L // T
    nt = n_lat + 1 if with_ctx else n_lat
    tok = lambda w: pl.BlockSpec((T, w), lambda i: (i, 0))
    full = lambda a: pl.BlockSpec(a.shape, lambda i: (0,) * a.ndim)
    return pl.pallas_call(
        functools.partial(_out_kernel, n_lat=n_lat),
        out_shape=[jax.ShapeDtypeStruct((nt * T, d), F32), jax.ShapeDtypeStruct((d, nt * T), BF16)],
        grid=(nt,),
        in_specs=[pl.BlockSpec((T, d), lambda i: (jnp.minimum(i, n_lat - 1), 0)),
                  pl.BlockSpec((T, d), lambda i: (0, 0)),
                  tok(D_LRU), tok(D_LRU), tok(D_LRU), tok(D_WIN_Q), tok(D_NA),
                  full(w_bf), full(mod), full(g)],
        out_specs=[tok(d), pl.BlockSpec((d, T), lambda i: (0, i))],
        compiler_params=_cparams("arbitrary"),
        name="out_proj",
    )(xl, xc, hf, hb, ga, yb, yc, w_bf, mod, g)


SUBLANES = 8


def _sorting_pairs(n):
    pairs, p = [], 1
    while p < n:
        k = p
        while k >= 1:
            for j in range(k % p, n - k, 2 * k):
                for i in range(min(k, n - j - k)):
                    if (i + j) // (2 * p) == (i + j + k) // (2 * p):
                        pairs.append((i + j, i + j + k))
            k //= 2
        p *= 2
    return pairs


def _vmax(a, b):
    if a is None:
        return b
    if b is None:
        return a
    return jnp.maximum(a, b)


def _vmin(a, b):
    if a is None or b is None:
        return None
    return jnp.minimum(a, b)


def _top16_sorted(slabs):
    K = PEER_TOPK
    w = list(slabs)
    for a, b in _sorting_pairs(K):
        w[a], w[b] = _vmax(w[a], w[b]), _vmin(w[a], w[b])
    shift = SUBLANES // 2
    while shift >= 1:
        partner = [None if w[K - 1 - i] is None else pltpu.roll(w[K - 1 - i], shift, 0) for i in range(K)]
        w = [_vmax(w[i], partner[i]) for i in range(K)]
        stride = K // 2
        while stride >= 1:
            for i in range(K):
                if i & stride == 0:
                    w[i], w[i + stride] = _vmax(w[i], w[i + stride]), _vmin(w[i], w[i + stride])
            stride //= 2
        shift //= 2
    return w


def _allsum8(x):
    x = x + pltpu.roll(x, 4, 0)
    x = x + pltpu.roll(x, 2, 0)
    return x + pltpu.roll(x, 1, 0)


def _route_tile(s1, s2):
    K = PEER_TOPK
    S = SUBLANES
    n_slab = s1.shape[0] // S
    a1 = [s1[j * S:(j + 1) * S] for j in range(n_slab)]
    a2 = [s2[j * S:(j + 1) * S] for j in range(n_slab)]
    v1 = _top16_sorted(a1)
    v2 = _top16_sorted(a2)
    sub = lax.broadcasted_iota(jnp.int32, a1[0].shape, 0)

    def as_rows(v, lo):
        out = v[lo]
        for b in range(1, S):
            out = jnp.where(sub == b, v[lo + b], out)
        return out

    v2_lo, v2_hi, v1_hi = as_rows(v2, 0), as_rows(v2, S), as_rows(v1, S)
    lens = [K // (a + 1) for a in range(S)]
    cands = [v1[0] + v2_lo, v1[0] + v2_hi]
    for a in range(1, S):
        ca = v1[a] + v2_lo
        cands.append(ca if lens[a] >= S else jnp.where(sub < lens[a], ca, -jnp.inf))
    cands.append(v1_hi + v2[0])
    thr = _top16_sorted(cands + [None] * (K - len(cands)))[K - 1]
    x2_lo, x2_hi = jnp.exp(v2_lo - v2[0]), jnp.exp(v2_hi - v2[0])
    sel_lo, sel_hi = cands[0] >= thr, cands[1] >= thr
    zsum = jnp.where(sel_lo, x2_lo, 0.0) + jnp.where(sel_hi, x2_hi, 0.0)
    cnt = [_allsum8(jnp.where(sel_lo, 1.0, 0.0) + jnp.where(sel_hi, 1.0, 0.0))]
    for a in range(1, S):
        sel = cands[a + 1] >= thr
        zsum = zsum + jnp.where(sel, jnp.exp(v1[a] - v1[0]) * x2_lo, 0.0)
        cnt.append(_allsum8(jnp.where(sel, 1.0, 0.0)))
    zsum = zsum + jnp.where(cands[S + 1] >= thr, jnp.exp(v1_hi - v1[0]), 0.0)
    inv_z = 1.0 / _allsum8(zsum)
    r2, e2, c1, e1 = [], [], [], []
    for j in range(n_slab):
        r = jnp.full(a2[j].shape, float(K), F32)
        for k in reversed(range(K)):
            r = jnp.where(a2[j] >= v2[k], float(k), r)
        r2.append(r)
        e2.append(jnp.exp(a2[j] - v2[0]) * inv_z)
        c = jnp.where(a1[j] + v2[0] >= thr, 1.0, 0.0)
        for a in reversed(range(S)):
            c = jnp.where(a1[j] >= v1[a], cnt[a], c)
        c1.append(c)
        e1.append(jnp.exp(a1[j] - v1[0]))
    cat = lambda xs: jnp.concatenate(xs, axis=0)
    return cat(r2), cat(e2), cat(c1), cat(e1)


def _prep_kernel(ht_ref, wq_ref, k1_ref, k2_ref, r2_ref, e2_ref, c1_ref, e1_ref, q_scr):
    nk = PEER_NKEYS
    q_scr[...] = _dot(wq_ref[...], ht_ref[...])

    def head(h, carry):
        base = pl.multiple_of(h * 2 * nk, 2 * nk)
        s1 = _dot(k1_ref[...], q_scr[pl.ds(base, nk), :].astype(BF16))
        s2 = _dot(k2_ref[...], q_scr[pl.ds(base + nk, nk), :].astype(BF16))
        for t in range(s1.shape[1] // LANES):
            ts = slice(t * LANES, (t + 1) * LANES)
            r2, e2, c1, e1 = _route_tile(s1[:, ts], s2[:, ts])
            r2_ref[h, :, ts] = _pack_rows(r2)
            e2_ref[h, :, ts] = _pack_rows(e2)
            c1_ref[h, :, ts] = _dup_halves(c1)
            e1_ref[h, :, ts] = _dup_halves(e1)
        return carry

    lax.fori_loop(0, PEER_HEADS, head, 0)


def _peer_prep(ht, wq_t, k1, k2):
    d, n = ht.shape
    T = PREP_TOK_TILE
    nk = PEER_NKEYS
    full = lambda a: pl.BlockSpec(a.shape, lambda i: (0,) * a.ndim)
    spec = lambda rows: pl.BlockSpec((PEER_HEADS, rows, T), lambda i: (0, 0, i))
    return pl.pallas_call(
        _prep_kernel,
        out_shape=[jax.ShapeDtypeStruct((PEER_HEADS, nk // 2, n), U32)] * 2
        + [jax.ShapeDtypeStruct((PEER_HEADS, nk, n), U32)] * 2,
        grid=(n // T,),
        in_specs=[pl.BlockSpec((d, T), lambda i: (0, i)), full(wq_t), full(k1), full(k2)],
        out_specs=[spec(nk // 2), spec(nk // 2), spec(nk), spec(nk)],
        scratch_shapes=[pltpu.VMEM((wq_t.shape[0], T), F32)],
        compiler_params=_cparams("arbitrary"),
        name="peer_prep",
    )(ht, wq_t, k1, k2)


def _peer_kernel(ht_ref, u_ref, un_ref, vt_ref, vp_ref, r2_ref, e2_ref, c1_ref, e1_ref, x_ref, g_ref,
                 o_ref, acc_ref, *ap_refs):
    nk = PEER_NKEYS
    c = pl.program_id(1)
    n_sub = len(ap_refs) // 2
    a_refs, p_refs = ap_refs[:n_sub], ap_refs[n_sub:]
    sub = 2 * a_refs[0].shape[0]
    T = a_refs[0].shape[1]
    rows_per_sub = sub // nk
    n_rows = n_sub * rows_per_sub
    PK = 16
    WK = PK // 2
    base = pl.multiple_of(c * n_rows, n_rows)

    tok_blocks = [slice(n, min(n + PEER_MXU_TOKENS, T)) for n in range(0, T, PEER_MXU_TOKENS)]

    def activations_from(rows_ref, dst_ref):
        for tb in tok_blocks:
            dst_ref[:, tb] = _pack_rows(_dot(rows_ref, ht_ref[:, tb]))

    def activations(s):
        activations_from(u_ref[s * sub:(s + 1) * sub, :], a_refs[s])

    def gates(s):
        for t in range(T // LANES):
            ts = slice(t * LANES, (t + 1) * LANES)
            for j in range(rows_per_sub):
                row = s * rows_per_sub + j
                gate = [None] * (nk // PK)
                for h in range(PEER_HEADS):
                    c1 = c1_ref[h, pl.ds(base, n_rows), ts][row:row + 1]
                    e1 = e1_ref[h, pl.ds(base, n_rows), ts][row:row + 1]
                    c1 = _unpack_rows(jnp.broadcast_to(c1, (WK, LANES)))
                    e1 = _unpack_rows(jnp.broadcast_to(e1, (WK, LANES)))
                    for v in range(nk // PK):
                        ws = slice(v * WK, (v + 1) * WK)
                        r2 = _unpack_rows(r2_ref[h, ws, ts])
                        e2 = _unpack_rows(e2_ref[h, ws, ts])
                        g = jnp.where(r2 < c1, e2 * e1, 0.0)
                        gate[v] = g if h == 0 else gate[v] + g
                for v in range(nk // PK):
                    ws = slice((j * nk + v * PK) // 2, (j * nk + (v + 1) * PK) // 2)
                    a = _unpack_rows(a_refs[s][ws, ts])
                    th = jnp.tanh(a * (GELU_K0 + GELU_K1 * (a * a)))
                    p_refs[s][ws, ts] = pltpu.bitcast(gate[v] * (a * (0.5 + 0.5 * th)), U32)

    def project_from(cols_ref, src_ref):
        for tb in tok_blocks:
            acc_ref[:, tb] += _dot(cols_ref, _unpack_rows(src_ref[:, tb]))

    def project(s):
        project_from(vt_ref[:, s * sub:(s + 1) * sub], p_refs[s])

    last = n_sub - 1

    def project_deferred():
        project_from(vp_ref[...], p_refs[last])

    @pl.when(c == 0)
    def _():
        acc_ref[...] = jnp.zeros(acc_ref.shape, F32)
        p_refs[last][...] = jnp.zeros(p_refs[last].shape, U32)
        activations(0)

    for s in range(n_sub):
        gates(s)
        if s == 0:
            project_deferred()
        else:
            project(s - 1)
        if s < last:
            activations(s + 1)
    activations_from(un_ref[...], a_refs[0])

    @pl.when(c == pl.num_programs(1) - 1)
    def _():
        project(last)
        o_ref[...] = x_ref[...] + g_ref[...] * acc_ref[...].T


def _peer_dense(ht, u_bf, vt_bf, r2, e2, c1, e1, x, g2, tok_tile, tok_off, n_tok):
    d = ht.shape[0]
    n_exp = u_bf.shape[0]
    T = tok_tile
    NC = PEER_EXPERT_CHUNK
    nk = PEER_NKEYS
    SUB = PEER_SUB_CHUNK
    n_sub = NC // SUB
    rt = lambda rows: pl.BlockSpec((PEER_HEADS, rows, T), lambda i, c: (0, 0, i + tok_off))
    return pl.pallas_call(
        _peer_kernel,
        out_shape=jax.ShapeDtypeStruct((n_tok, d), F32),
        grid=(n_tok // T, n_exp // NC),
        in_specs=[pl.BlockSpec((d, T), lambda i, c: (0, i + tok_off)),
                  pl.BlockSpec((NC, d), lambda i, c: (c, 0)),
                  pl.BlockSpec((SUB, d), lambda i, c: (jnp.minimum((c + 1) * n_sub, n_exp // SUB - 1), 0)),
                  pl.BlockSpec((d, NC), lambda i, c: (0, c)),
                  pl.BlockSpec((d, SUB), lambda i, c: (0, jnp.maximum(c * n_sub - 1, 0))),
                  rt(nk // 2), rt(nk // 2), rt(nk), rt(nk),
                  pl.BlockSpec((T, d), lambda i, c: (i + tok_off, 0)),
                  pl.BlockSpec((1, d), lambda i, c: (0, 0))],
        out_specs=pl.BlockSpec((T, d), lambda i, c: (i, 0)),
        scratch_shapes=[pltpu.VMEM((d, T), F32)] + [pltpu.VMEM((PEER_SUB_CHUNK // 2, T), U32)] * (2 * n_sub),
        compiler_params=_cparams("arbitrary", "arbitrary"),
        name="peer_dense",
    )(ht, u_bf, u_bf, vt_bf, vt_bf, r2, e2, c1, e1, x, g2)


def _rope_tables(L, n):
    t = jnp.arange(L)
    row = (t // GRID_W).astype(F32)
    col = (t % GRID_W).astype(F32)
    q = HEAD_DIM // 4
    inv = ROPE_THETA ** (-jnp.arange(q, dtype=F32) / q)
    ar = row[:, None] * inv
    ac = col[:, None] * inv
    cos = jnp.concatenate([jnp.cos(ar), jnp.cos(ar), jnp.cos(ac), jnp.cos(ac)], axis=-1)
    sin = jnp.concatenate([-jnp.sin(ar), jnp.sin(ar), -jnp.sin(ac), jnp.sin(ac)], axis=-1)
    cos = jnp.concatenate([cos, jnp.ones((n - L, HEAD_DIM), F32)], axis=0)
    sin = jnp.concatenate([sin, jnp.zeros((n - L, HEAD_DIM), F32)], axis=0)
    return jnp.tile(cos, (1, LANES // HEAD_DIM)), jnp.tile(sin, (1, LANES // HEAD_DIM))


def _block_diag(w):
    nb, di, do = w.shape
    eye = jnp.eye(nb, dtype=w.dtype)
    return (eye[:, None, :, None] * w[:, :, None, :]).reshape(nb * di, nb * do)


def kernel(x, c, ctx, c_ctx, w_mod, b_mod, norm1_g, norm2_g, w_in, w_out, lru_conv_w, lru_conv_b, lru_wa, lru_ba, lru_wx, lru_bx, lru_lam, win_qn_g, win_kn_g, win_sink, na_qn_g, na_kn_g, na_rpb, peer_wq, peer_k1, peer_k2, peer_u, peer_v):
    B, L, D = x.shape
    Lc = ctx.shape[1]
    depth = w_mod.shape[0]
    assert B == 1 and Lc == TOK_TILE and L % PEER_TOK_TILE == 0 and L // GRID_W >= 3 * NA_ROW_BLOCK
    n = L + Lc
    n_lat = L // TOK_TILE
    rows = L // GRID_W

    cvec = jnp.zeros((8, D), F32).at[0].set(c[0]).at[1].set(c_ctx)
    mods = _modulation(cvec, w_mod, b_mod)
    cos_t, sin_t = _rope_tables(L, n)
    tile2 = lambda g: jnp.tile(g, LANES // HEAD_DIM)

    xl, xc = x[0], ctx[0]
    for l in range(depth):
        with_ctx = l < depth - 1
        mod = mods[l]
        hg = jnp.zeros((8, LANES), F32)
        hg = hg.at[0].set(tile2(win_qn_g[l])).at[1].set(tile2(win_kn_g[l]))
        hg = hg.at[2].set(tile2(na_qn_g[l])).at[3].set(tile2(na_kn_g[l]))
        xa, ga, qw, kw, vw, qn, kn, vn = _in_proj(xl, xc, mod, norm1_g[l][None], w_in[l].astype(BF16),
                                                  cos_t, sin_t, hg)

        w_gates = jnp.concatenate([_block_diag(lru_wa[l, 0]), _block_diag(lru_wx[l, 0]),
                                   _block_diag(lru_wa[l, 1]), _block_diag(lru_wx[l, 1])], axis=1).astype(BF16)
        b_gates = jnp.concatenate([lru_ba[l, 0], lru_bx[l, 0], lru_ba[l, 1], lru_bx[l, 1]])[None]
        conv_w = jnp.zeros((8, D_LRU), F32).at[0:lru_conv_w.shape[1]].set(lru_conv_w[l])
        lam = jnp.zeros((8, D_LRU), F32).at[0:2].set(lru_lam[l])
        hf, hb = _lru(xa, conv_w, lru_conv_b[l][None], w_gates, b_gates, lam, n_lat)

        yb = _window_attention(win_sink[l], qw, kw, vw, L, with_ctx)
        yc = _neighborhood_attention(qn, kn, vn, _na_bias_table(na_rpb[l], rows), L, with_ctx)

        xn, ht = _out_proj(xl, xc, hf, hb, ga, yb, yc, w_out[l].astype(BF16), mod, norm2_g[l][None], with_ctx)

        r2, e2, c1, e1 = _peer_prep(ht, peer_wq[l].T.astype(BF16), peer_k1[l].astype(BF16),
                                    peer_k2[l].astype(BF16))
        u_bf = peer_u[l].astype(BF16)
        vt_bf = peer_v[l].T.astype(BF16)
        g2 = mod[:, 5 * D:6 * D]
        xl_new = _peer_dense(ht, u_bf, vt_bf, r2, e2, c1, e1, xn, g2[0:1], PEER_TOK_TILE, 0, L)
        if with_ctx:
            xc = _peer_dense(ht, u_bf, vt_bf, r2, e2, c1, e1, xn, g2[1:2], Lc, L // Lc, Lc)
        xl = xl_new
    return xl[None]
```

```python
import functools

import numpy as np
import jax
import jax.numpy as jnp
from jax import lax
from jax.experimental import pallas as pl
from jax.experimental.pallas import tpu as pltpu

F32 = jnp.float32
BF16 = jnp.bfloat16
U32 = jnp.uint32

HEAD_DIM = 64
GRID_W = 64
EPS = 1e-6
ROPE_THETA = 10000.0
D_LRU = 256
LRU_BLOCKS = 4
LRU_C = 8.0
WIN_Q_HEADS = 6
WIN_KV_HEADS = 2
WIN_BLOCK = 128
NA_HEADS = 6
NA_KH = 8
NA_KW = 16
NA_ROW_BLOCK = 4
NA_KEY_ROWS = NA_ROW_BLOCK + NA_KH - 1
PEER_HEADS = 8
PEER_NKEYS = 128
PEER_TOPK = 16
D_WIN_Q = WIN_Q_HEADS * HEAD_DIM
D_WIN_KV = WIN_KV_HEADS * HEAD_DIM
D_NA = NA_HEADS * HEAD_DIM

LANES = 128
TOK_TILE = 256
PEER_TOK_TILE = 512
PEER_EXPERT_CHUNK = 2048
PEER_SUB_CHUNK = 512
PEER_MXU_TOKENS = 256
PREP_TOK_TILE = 256
VMEM_LIMIT = 56 * 1024 * 1024
NEG = -1e30
GELU_K0 = float(np.sqrt(2.0 / np.pi))
GELU_K1 = 0.044715 * GELU_K0


def _cparams(*sem):
    return pltpu.CompilerParams(dimension_semantics=sem, vmem_limit_bytes=VMEM_LIMIT)


def _dot(a, b):
    return jnp.dot(a, b, preferred_element_type=F32)


def _pack_rows(x):
    return pltpu.bitcast(x.astype(BF16), U32)


def _unpack_rows(w):
    return pltpu.bitcast(w, BF16)


def _dot_nt(a, b):
    return lax.dot_general(a, b, (((1,), (1,)), ((), ())), preferred_element_type=F32)


def _mod_kernel(c_ref, w_ref, b_ref, o_ref):
    c = c_ref[...]
    a = (c * jax.nn.sigmoid(c)).astype(BF16)
    o_ref[0] = _dot(a, w_ref[0].astype(BF16)) + b_ref[0]


def _modulation(cvec, w_mod, b_mod):
    depth, d, d6 = w_mod.shape
    return pl.pallas_call(
        _mod_kernel,
        out_shape=jax.ShapeDtypeStruct((depth, 8, d6), F32),
        grid=(depth, d6 // d),
        in_specs=[pl.BlockSpec((8, d), lambda l, j: (0, 0)),
                  pl.BlockSpec((1, d, d), lambda l, j: (l, 0, j)),
                  pl.BlockSpec((1, 1, d), lambda l, j: (l, 0, j))],
        out_specs=pl.BlockSpec((1, 8, d), lambda l, j: (l, 0, j)),
        compiler_params=_cparams("arbitrary", "arbitrary"),
        name="modulation",
    )(cvec, w_mod, b_mod.reshape(depth, 1, d6))


def _pair_mean_matrix():
    r = lax.broadcasted_iota(jnp.int32, (LANES, LANES), 0) < HEAD_DIM
    c = lax.broadcasted_iota(jnp.int32, (LANES, LANES), 1) < HEAD_DIM
    return jnp.where(r == c, 1.0 / HEAD_DIM, 0.0).astype(BF16)


def _head_norm(z, gain, bd):
    z2 = z * z
    hi = z2.astype(BF16)
    lo = (z2 - hi.astype(F32)).astype(BF16)
    ms = _dot(hi, bd) + _dot(lo, bd)
    return z * lax.rsqrt(ms + EPS) * gain


def _rope(z, cos, sin):
    lane = lax.broadcasted_iota(jnp.int32, z.shape, 1)
    first = (lane & 16) == 0
    partner = jnp.where(first, pltpu.roll(z, LANES - 16, 1), pltpu.roll(z, 16, 1))
    return z * cos + partner * sin


def _in_kernel(xl_ref, xc_ref, mod_ref, g_ref, w_ref, cos_ref, sin_ref, hg_ref,
               xa_ref, ga_ref, qw_ref, kw_ref, vw_ref, qn_ref, kn_ref, vn_ref, *, n_lat):
    d = xl_ref.shape[1]
    i = pl.program_id(0)
    row = (i == n_lat).astype(jnp.int32)
    is_ctx = jnp.full((TOK_TILE, 1), row) == 1
    x = jnp.where(is_ctx, xc_ref[...], xl_ref[...])
    ms = jnp.mean(x * x, axis=-1, keepdims=True)
    y = x * lax.rsqrt(ms + EPS) * g_ref[...]
    sh = mod_ref[pl.ds(row, 1), 0:d]
    sc = mod_ref[pl.ds(row, 1), d:2 * d]
    h = (y * (1.0 + sc) + sh).astype(BF16)
    z = _dot(h, w_ref[...])

    bd = _pair_mean_matrix()
    cos = cos_ref[...]
    sin = sin_ref[...]
    scale = HEAD_DIM ** -0.5
    o = 0
    xa_ref[...] = z[:, o:o + D_LRU]
    o += D_LRU
    ga_ref[...] = z[:, o:o + D_LRU]
    o += D_LRU
    for g in range(D_WIN_Q // LANES):
        zz = _rope(_head_norm(z[:, o:o + LANES], hg_ref[0:1], bd), cos, sin)
        qw_ref[:, g * LANES:(g + 1) * LANES] = (zz * scale).astype(BF16)
        o += LANES
    kw_ref[...] = _rope(_head_norm(z[:, o:o + LANES], hg_ref[1:2], bd), cos, sin).astype(BF16)
    o += LANES
    vw_ref[...] = z[:, o:o + LANES].astype(BF16)
    o += LANES
    for g in range(D_NA // LANES):
        zz = _head_norm(z[:, o:o + LANES], hg_ref[2:3], bd)
        qn_ref[:, g * LANES:(g + 1) * LANES] = (zz * scale).astype(BF16)
        o += LANES
    for g in range(D_NA // LANES):
        zz = _head_norm(z[:, o:o + LANES], hg_ref[3:4], bd)
        kn_ref[:, g * LANES:(g + 1) * LANES] = zz.astype(BF16)
        o += LANES
    vn_ref[...] = z[:, o:o + D_NA].astype(BF16)


def _in_proj(xl, xc, mod, g, w_bf, cos_t, sin_t, hg):
    L, d = xl.shape
    n_lat = L // TOK_TILE
    n = L + xc.shape[0]
    d_in = w_bf.shape[1]
    T = TOK_TILE
    tok = lambda w: pl.BlockSpec((T, w), lambda i: (i, 0))
    full = lambda a: pl.BlockSpec(a.shape, lambda i: (0,) * a.ndim)
    widths = (D_LRU, D_LRU, D_WIN_Q, D_WIN_KV, D_WIN_KV, D_NA, D_NA, D_NA)
    dtypes = (F32, F32, BF16, BF16, BF16, BF16, BF16, BF16)
    return pl.pallas_call(
        functools.partial(_in_kernel, n_lat=n_lat),
        out_shape=[jax.ShapeDtypeStruct((n, w), t) for w, t in zip(widths, dtypes)],
        grid=(n_lat + 1,),
        in_specs=[pl.BlockSpec((T, d), lambda i: (jnp.minimum(i, n_lat - 1), 0)),
                  pl.BlockSpec((T, d), lambda i: (0, 0)),
                  full(mod), full(g), full(w_bf), tok(LANES), tok(LANES), full(hg)],
        out_specs=[tok(w) for w in widths],
        compiler_params=_cparams("arbitrary"),
        name="in_proj",
    )(xl, xc, mod, g, w_bf, cos_t, sin_t, hg)


def _chunk_scan(a, b, reverse):
    T = a.shape[0]
    rows = lax.broadcasted_iota(jnp.int32, a.shape, 0)
    s = 1
    while s < T:
        if reverse:
            edge = rows >= T - s
            shift = T - s
        else:
            edge = rows < s
            shift = s
        a_s = jnp.where(edge, 1.0, pltpu.roll(a, shift, 0))
        b_s = jnp.where(edge, 0.0, pltpu.roll(b, shift, 0))
        b = a * b_s + b
        a = a * a_s
        s *= 2
    return a, b


def _lru_direction(xm, ph, nh, pv, nv, cw, cb, w, bias, spl, carry_ref, first, reverse):
    T, C = xm.shape
    rows = lax.broadcasted_iota(jnp.int32, (T, C), 0)
    p6 = ph[6:7] * pv
    p7 = ph[7:8] * pv
    n0 = nh[0:1] * nv
    x_m1 = jnp.where(rows == 0, p7, pltpu.roll(xm, 1, 0))
    x_m2 = jnp.where(rows == 0, p6, jnp.where(rows == 1, p7, pltpu.roll(xm, 2, 0)))
    x_p1 = jnp.where(rows == T - 1, n0, pltpu.roll(xm, T - 1, 0))
    u = cw[0:1] * x_m2 + cw[1:2] * x_m1 + cw[2:3] * xm + cw[3:4] * x_p1 + cb
    zz = _dot(u.astype(BF16), w) + bias
    r = jax.nn.sigmoid(zz[:, :C])
    ig = jax.nn.sigmoid(zz[:, C:])
    log_a = -LRU_C * r * spl
    a = jnp.exp(log_a)
    b = jnp.sqrt(-jnp.tanh(log_a) * (a * a + 1.0)) * ig * u
    a, b = _chunk_scan(a, b, reverse)

    @pl.when(first)
    def _():
        carry_ref[...] = jnp.zeros(carry_ref.shape, F32)

    h = a * carry_ref[0:1] + b
    edge = h[0:1] if reverse else h[T - 1:T]
    carry_ref[...] = jnp.broadcast_to(edge, carry_ref.shape)
    return h


def _lru_kernel(xf_ref, pf_ref, nf_ref, xb_ref, pb_ref, nb_ref, cw_ref, cb_ref, w_ref, b_ref,
                lam_ref, hf_ref, hb_ref, cf_ref, cbk_ref, *, n_lat):
    j = pl.program_id(0)
    C = D_LRU
    fblk = jnp.where(j == 0, n_lat, j - 1)
    bblk = jnp.where(j == 0, n_lat, n_lat - j)
    lam = lam_ref[...]
    spl = jnp.maximum(-lam, 0.0) + jnp.log1p(jnp.exp(-jnp.abs(lam)))
    cw = cw_ref[...]
    cb = cb_ref[...]

    def halo_valid(blk):
        pv = jnp.logical_and(blk != 0, blk != n_lat).astype(F32)
        nv = jnp.logical_and(blk != n_lat - 1, blk != n_lat).astype(F32)
        return pv, nv

    pv, nv = halo_valid(fblk)
    hf_ref[...] = _lru_direction(xf_ref[...], pf_ref[...], nf_ref[...], pv, nv, cw, cb,
                                 w_ref[:, 0:2 * C], b_ref[:, 0:2 * C], spl[0:1], cf_ref, j == 0, False)
    pv, nv = halo_valid(bblk)
    hb_ref[...] = _lru_direction(xb_ref[...], pb_ref[...], nb_ref[...], pv, nv, cw, cb,
                                 w_ref[:, 2 * C:4 * C], b_ref[:, 2 * C:4 * C], spl[1:2], cbk_ref, j == 0, True)


def _lru(xa, conv_w, conv_b, w_gates, b_gates, lam, n_lat):
    n, C = xa.shape
    T = TOK_TILE
    sub = T // 8
    nblk8 = n // 8
    fblk = lambda j: jnp.where(j == 0, n_lat, j - 1)
    bblk = lambda j: jnp.where(j == 0, n_lat, n_lat - j)
    prev8 = lambda blk: jnp.maximum(blk * sub - 1, 0)
    next8 = lambda blk: jnp.minimum((blk + 1) * sub, nblk8 - 1)
    full = lambda a: pl.BlockSpec(a.shape, lambda j: (0,) * a.ndim)
    return pl.pallas_call(
        functools.partial(_lru_kernel, n_lat=n_lat),
        out_shape=[jax.ShapeDtypeStruct((n, C), F32)] * 2,
        grid=(n_lat + 1,),
        in_specs=[pl.BlockSpec((T, C), lambda j: (fblk(j), 0)),
                  pl.BlockSpec((8, C), lambda j: (prev8(fblk(j)), 0)),
                  pl.BlockSpec((8, C), lambda j: (next8(fblk(j)), 0)),
                  pl.BlockSpec((T, C), lambda j: (bblk(j), 0)),
                  pl.BlockSpec((8, C), lambda j: (prev8(bblk(j)), 0)),
                  pl.BlockSpec((8, C), lambda j: (next8(bblk(j)), 0)),
                  full(conv_w), full(conv_b), full(w_gates), full(b_gates), full(lam)],
        out_specs=[pl.BlockSpec((T, C), lambda j: (fblk(j), 0)),
                   pl.BlockSpec((T, C), lambda j: (bblk(j), 0))],
        scratch_shapes=[pltpu.VMEM((8, C), F32), pltpu.VMEM((8, C), F32)],
        compiler_params=_cparams("arbitrary"),
        name="rglru",
    )(xa, xa, xa, xa, xa, xa, conv_w, conv_b, w_gates, b_gates, lam)


def _win_kernel(sink_ref, q_ref, kp_ref, kc_ref, kn_ref, vp_ref, vc_ref, vn_ref, kx_ref, vx_ref,
                o_ref, *, nb):
    W = WIN_BLOCK
    R = WIN_Q_HEADS // WIN_KV_HEADS
    b = pl.program_id(0)
    blk = jnp.where(b < nb, b, -4)
    rows = lax.broadcasted_iota(jnp.int32, (R * W, 3 * W), 0)
    cols = lax.broadcasted_iota(jnp.int32, (R * W, 3 * W), 1)
    kpos = (blk - 1) * W + cols
    qpos = blk * W + (rows & (W - 1))
    valid = jnp.logical_and(jnp.abs(kpos - qpos) <= W, jnp.logical_and(kpos >= 0, kpos < nb * W))
    row1 = lax.broadcasted_iota(jnp.int32, (R * W, 1), 0)
    q = q_ref[...]
    for g in range(WIN_KV_HEADS):
        ls = slice(g * HEAD_DIM, (g + 1) * HEAD_DIM)
        qs = jnp.concatenate([q[:, (g * R + r) * HEAD_DIM:(g * R + r + 1) * HEAD_DIM] for r in range(R)], axis=0)
        kl = jnp.concatenate([kp_ref[:, ls], kc_ref[:, ls], kn_ref[:, ls]], axis=0)
        vl = jnp.concatenate([vp_ref[:, ls], vc_ref[:, ls], vn_ref[:, ls]], axis=0)
        s_loc = jnp.where(valid, _dot_nt(qs, kl), NEG)
        s_ctx = _dot_nt(qs, kx_ref[:, ls])
        sink = jnp.full((R * W, 1), sink_ref[g * R], F32)
        for r in range(1, R):
            sink = jnp.where(row1 >= r * W, sink_ref[g * R + r], sink)
        m = jnp.maximum(jnp.maximum(jnp.max(s_loc, axis=-1, keepdims=True),
                                    jnp.max(s_ctx, axis=-1, keepdims=True)), sink)
        p_loc = jnp.exp(s_loc - m)
        p_ctx = jnp.exp(s_ctx - m)
        den = (jnp.sum(p_loc, axis=-1, keepdims=True) + jnp.sum(p_ctx, axis=-1, keepdims=True)
               + jnp.exp(sink - m))
        o = (_dot(p_loc.astype(BF16), vl) + _dot(p_ctx.astype(BF16), vx_ref[:, ls])) / den
        for r in range(R):
            hh = g * R + r
            o_ref[:, hh * HEAD_DIM:(hh + 1) * HEAD_DIM] = o[r * W:(r + 1) * W].astype(o_ref.dtype)


def _window_attention(sink, qw, kw, vw, L, with_ctx):
    n = qw.shape[0]
    W = WIN_BLOCK
    nb = L // W
    nq = n // W if with_ctx else nb
    ctx_blk = L // TOK_TILE
    lat = lambda b: jnp.minimum(b, nb - 1)
    kv = lambda f: pl.BlockSpec((W, D_WIN_KV), lambda b: (f(b), 0))
    prv = lambda b: jnp.maximum(lat(b) - 1, 0)
    nxt = lambda b: jnp.minimum(lat(b) + 1, nb - 1)
    ctx = pl.BlockSpec((n - L, D_WIN_KV), lambda b: (ctx_blk, 0))
    return pl.pallas_call(
        functools.partial(_win_kernel, nb=nb),
        out_shape=jax.ShapeDtypeStruct((nq * W, D_WIN_Q), BF16),
        grid=(nq,),
        in_specs=[pl.BlockSpec(memory_space=pltpu.SMEM),
                  pl.BlockSpec((W, D_WIN_Q), lambda b: (b, 0)),
                  kv(prv), kv(lat), kv(nxt), kv(prv), kv(lat), kv(nxt), ctx, ctx],
        out_specs=pl.BlockSpec((W, D_WIN_Q), lambda b: (b, 0)),
        compiler_params=_cparams("arbitrary"),
        name="window_attn",
    )(sink, qw, kw, kw, kw, vw, vw, vw, kw, vw)


def _na_kernel(q_ref, kl_ref, vl_ref, kx_ref, vx_ref, bias_ref, o_ref, *, n_blocks):
    b = pl.program_id(0)
    nq = q_ref.shape[0]
    latent = jnp.full((nq, 1), (b < n_blocks).astype(jnp.int32)) == 1
    q = q_ref[...]
    for h in range(NA_HEADS):
        ls = slice(h * HEAD_DIM, (h + 1) * HEAD_DIM)
        qh = q[:, ls]
        s_loc = jnp.where(latent, _dot_nt(qh, kl_ref[:, ls]) + bias_ref[0, h], NEG)
        s_ctx = _dot_nt(qh, kx_ref[:, ls])
        m = jnp.maximum(jnp.max(s_loc, axis=-1, keepdims=True), jnp.max(s_ctx, axis=-1, keepdims=True))
        p_loc = jnp.exp(s_loc - m)
        p_ctx = jnp.exp(s_ctx - m)
        den = jnp.sum(p_loc, axis=-1, keepdims=True) + jnp.sum(p_ctx, axis=-1, keepdims=True)
        o = (_dot(p_loc.astype(BF16), vl_ref[:, ls]) + _dot(p_ctx.astype(BF16), vx_ref[:, ls])) / den
        o_ref[:, ls] = o.astype(o_ref.dtype)


def _na_bias_table(rpb, rows):
    R, KR = NA_ROW_BLOCK, NA_KEY_ROWS
    qc = np.arange(GRID_W)[:, None]
    kc = np.arange(GRID_W)[None, :]
    qstart = np.clip(qc - NA_KW // 2, 0, GRID_W - NA_KW)
    inside = (kc - qstart >= 0) & (kc - qstart < NA_KW)
    pad = GRID_W - NA_KW
    rp = jnp.pad(rpb.astype(F32), ((0, 0), (0, 0), (pad, pad)))
    shifted = jnp.stack([rp[:, :, NA_KW - 1 - q + pad:NA_KW - 1 - q + pad + GRID_W] for q in range(GRID_W)], axis=2)
    per_dr = jnp.where(inside[None, None], shifted, NEG)
    masked = jnp.full((NA_HEADS, GRID_W, GRID_W), NEG, F32)
    half = NA_KH // 2
    cases = []
    for r0, ks in ((0, 0), (half, 0), (rows - R, rows - KR)):
        row_blocks = []
        for rr in range(R):
            r = r0 + rr
            kr = min(max(r - half, 0), rows - NA_KH)
            blocks = [per_dr[:, ks + kk - r + NA_KH - 1] if kr <= ks + kk < kr + NA_KH else masked
                      for kk in range(KR)]
            row_blocks.append(jnp.concatenate(blocks, axis=-1))
        cases.append(jnp.concatenate(row_blocks, axis=1))
    return jnp.stack(cases)


def _neighborhood_attention(qn, kn, vn, bias, L, with_ctx):
    n = qn.shape[0]
    R, KR = NA_ROW_BLOCK, NA_KEY_ROWS
    rows = L // GRID_W
    nb = rows // R
    nq = R * GRID_W
    steps = n // nq if with_ctx else nb
    ctx_blk = L // TOK_TILE
    half = NA_KH // 2

    def blk_of(b):
        return jnp.minimum(b, nb - 1)

    def key_start(b):
        return jnp.clip(blk_of(b) * R - half, 0, rows - KR) * GRID_W

    def case_of(b):
        return jnp.where(blk_of(b) == 0, 0, jnp.where(blk_of(b) == nb - 1, 2, 1))

    loc = pl.BlockSpec((pl.Element(KR * GRID_W), pl.Element(D_NA)), lambda b: (key_start(b), 0))
    ctx = pl.BlockSpec((n - L, D_NA), lambda b: (ctx_blk, 0))
    return pl.pallas_call(
        functools.partial(_na_kernel, n_blocks=nb),
        out_shape=jax.ShapeDtypeStruct((steps * nq, D_NA), BF16),
        grid=(steps,),
        in_specs=[pl.BlockSpec((nq, D_NA), lambda b: (b, 0)), loc, loc, ctx, ctx,
                  pl.BlockSpec((1, NA_HEADS, nq, KR * GRID_W), lambda b: (case_of(b), 0, 0, 0))],
        out_specs=pl.BlockSpec((nq, D_NA), lambda b: (b, 0)),
        compiler_params=_cparams("arbitrary"),
        name="neighborhood_attn",
    )(qn, kn, vn, kn, vn, bias)


def _out_kernel(xl_ref, xc_ref, hf_ref, hb_ref, ga_ref, yb_ref, yc_ref, w_ref, mod_ref, g_ref,
                xo_ref, ht_ref, *, n_lat):
    d = xl_ref.shape[1]
    i = pl.program_id(0)
    row = (i == n_lat).astype(jnp.int32)
    is_ctx = jnp.full((TOK_TILE, 1), row) == 1
    x = jnp.where(is_ctx, xc_ref[...], xl_ref[...])
    ya = ((hf_ref[...] + hb_ref[...]) * jax.nn.gelu(ga_ref[...])).astype(BF16)
    o1 = D_LRU
    o2 = D_LRU + D_WIN_Q
    mix = (_dot(ya, w_ref[0:o1]) + _dot(yb_ref[...], w_ref[o1:o2]) + _dot(yc_ref[...], w_ref[o2:o2 + D_NA]))
    xn = x + mod_ref[pl.ds(row, 1), 2 * d:3 * d] * mix
    xo_ref[...] = xn
    ms = jnp.mean(xn * xn, axis=-1, keepdims=True)
    y = xn * lax.rsqrt(ms + EPS) * g_ref[...]
    h2 = y * (1.0 + mod_ref[pl.ds(row, 1), 4 * d:5 * d]) + mod_ref[pl.ds(row, 1), 3 * d:4 * d]
    ht_ref[...] = h2.T.astype(BF16)


def _out_proj(xl, xc, hf, hb, ga, yb, yc, w_bf, mod, g, with_ctx):
    L, d = xl.shape
    T = TOK_TILE
    n_lat = L // T
    nt = n_lat + 1 if with_ctx else n_lat
    tok = lambda w: pl.BlockSpec((T, w), lambda i: (i, 0))
    full = lambda a: pl.BlockSpec(a.shape, lambda i: (0,) * a.ndim)
    return pl.pallas_call(
        functools.partial(_out_kernel, n_lat=n_lat),
        out_shape=[jax.ShapeDtypeStruct((nt * T, d), F32), jax.ShapeDtypeStruct((d, nt * T), BF16)],
        grid=(nt,),
        in_specs=[pl.BlockSpec((T, d), lambda i: (jnp.minimum(i, n_lat - 1), 0)),
                  pl.BlockSpec((T, d), lambda i: (0, 0)),
                  tok(D_LRU), tok(D_LRU), tok(D_LRU), tok(D_WIN_Q), tok(D_NA),
                  full(w_bf), full(mod), full(g)],
        out_specs=[tok(d), pl.BlockSpec((d, T), lambda i: (0, i))],
        compiler_params=_cparams("arbitrary"),
        name="out_proj",
    )(xl, xc, hf, hb, ga, yb, yc, w_bf, mod, g)


SUBLANES = 8


def _sorting_pairs(n):
    pairs, p = [], 1
    while p < n:
        k = p
        while k >= 1:
            for j in range(k % p, n - k, 2 * k):
                for i in range(min(k, n - j - k)):
                    if (i + j) // (2 * p) == (i + j + k) // (2 * p):
                        pairs.append((i + j, i + j + k))
            k //= 2
        p *= 2
    return pairs


def _vmax(a, b):
    if a is None:
        return b
    if b is None:
        return a
    return jnp.maximum(a, b)


def _vmin(a, b):
    if a is None or b is None:
        return None
    return jnp.minimum(a, b)


def _top16_sorted(slabs):
    K = PEER_TOPK
    w = list(slabs)
    for a, b in _sorting_pairs(K):
        w[a], w[b] = _vmax(w[a], w[b]), _vmin(w[a], w[b])
    shift = SUBLANES // 2
    while shift >= 1:
        partner = [None if w[K - 1 - i] is None else pltpu.roll(w[K - 1 - i], shift, 0) for i in range(K)]
        w = [_vmax(w[i], partner[i]) for i in range(K)]
        stride = K // 2
        while stride >= 1:
            for i in range(K):
                if i & stride == 0:
                    w[i], w[i + stride] = _vmax(w[i], w[i + stride]), _vmin(w[i], w[i + stride])
            stride //= 2
        shift //= 2
    return w


def _allsum8(x):
    x = x + pltpu.roll(x, 4, 0)
    x = x + pltpu.roll(x, 2, 0)
    return x + pltpu.roll(x, 1, 0)


def _route_tile(s1, s2):
    K = PEER_TOPK
    S = SUBLANES
    n_slab = s1.shape[0] // S
    a1 = [s1[j * S:(j + 1) * S] for j in range(n_slab)]
    a2 = [s2[j * S:(j + 1) * S] for j in range(n_slab)]
    v1 = _top16_sorted(a1)
    v2 = _top16_sorted(a2)
    sub = lax.broadcasted_iota(jnp.int32, a1[0].shape, 0)

    def as_rows(v, lo):
        out = v[lo]
        for b in range(1, S):
            out = jnp.where(sub == b, v[lo + b], out)
        return out

    v2_lo, v2_hi, v1_hi = as_rows(v2, 0), as_rows(v2, S), as_rows(v1, S)
    lens = [K // (a + 1) for a in range(S)]
    cands = [v1[0] + v2_lo, v1[0] + v2_hi]
    for a in range(1, S):
        ca = v1[a] + v2_lo
        cands.append(ca if lens[a] >= S else jnp.where(sub < lens[a], ca, -jnp.inf))
    cands.append(v1_hi + v2[0])
    thr = _top16_sorted(cands + [None] * (K - len(cands)))[K - 1]
    x2_lo, x2_hi = jnp.exp(v2_lo - v2[0]), jnp.exp(v2_hi - v2[0])
    sel_lo, sel_hi = cands[0] >= thr, cands[1] >= thr
    zsum = jnp.where(sel_lo, x2_lo, 0.0) + jnp.where(sel_hi, x2_hi, 0.0)
    cnt = [_allsum8(jnp.where(sel_lo, 1.0, 0.0) + jnp.where(sel_hi, 1.0, 0.0))]
    for a in range(1, S):
        sel = cands[a + 1] >= thr
        zsum = zsum + jnp.where(sel, jnp.exp(v1[a] - v1[0]) * x2_lo, 0.0)
        cnt.append(_allsum8(jnp.where(sel, 1.0, 0.0)))
    zsum = zsum + jnp.where(cands[S + 1] >= thr, jnp.exp(v1_hi - v1[0]), 0.0)
    inv_z = 1.0 / _allsum8(zsum)
    r2, e2, c1, e1 = [], [], [], []
    for j in range(n_slab):
        r = jnp.full(a2[j].shape, float(K), F32)
        for k in reversed(range(K)):
            r = jnp.where(a2[j] >= v2[k], float(k), r)
        r2.append(r)
        e2.append(jnp.exp(a2[j] - v2[0]) * inv_z)
        c = jnp.where(a1[j] + v2[0] >= thr, 1.0, 0.0)
        for a in reversed(range(S)):
            c = jnp.where(a1[j] >= v1[a], cnt[a], c)
        c1.append(c)
        e1.append(jnp.exp(a1[j] - v1[0]))
    cat = lambda xs: jnp.concatenate(xs, axis=0)
    return cat(r2), cat(e2), cat(c1), cat(e1)


def _prep_kernel(ht_ref, wq_ref, k1_ref, k2_ref, r2_ref, e2_ref, c1_ref, e1_ref, q_scr):
    nk = PEER_NKEYS
    q_scr[...] = _dot(wq_ref[...], ht_ref[...])

    def head(h, carry):
        base = pl.multiple_of(h * 2 * nk, 2 * nk)
        s1 = _dot(k1_ref[...], q_scr[pl.ds(base, nk), :].astype(BF16))
        s2 = _dot(k2_ref[...], q_scr[pl.ds(base + nk, nk), :].astype(BF16))
        for t in range(s1.shape[1] // LANES):
            ts = slice(t * LANES, (t + 1) * LANES)
            r2, e2, c1, e1 = _route_tile(s1[:, ts], s2[:, ts])
            r2_ref[h, :, ts] = _pack_rows(r2)
            e2_ref[h, :, ts] = _pack_rows(e2)
            c1_ref[h, :, ts] = c1
            e1_ref[h, :, ts] = e1
        return carry

    lax.fori_loop(0, PEER_HEADS, head, 0)


def _peer_prep(ht, wq_t, k1, k2):
    d, n = ht.shape
    T = PREP_TOK_TILE
    nk = PEER_NKEYS
    full = lambda a: pl.BlockSpec(a.shape, lambda i: (0,) * a.ndim)
    spec = lambda rows: pl.BlockSpec((PEER_HEADS, rows, T), lambda i: (0, 0, i))
    return pl.pallas_call(
        _prep_kernel,
        out_shape=[jax.ShapeDtypeStruct((PEER_HEADS, nk // 2, n), U32)] * 2
        + [jax.ShapeDtypeStruct((PEER_HEADS, nk, n), F32)] * 2,
        grid=(n // T,),
        in_specs=[pl.BlockSpec((d, T), lambda i: (0, i)), full(wq_t), full(k1), full(k2)],
        out_specs=[spec(nk // 2), spec(nk // 2), spec(nk), spec(nk)],
        scratch_shapes=[pltpu.VMEM((wq_t.shape[0], T), F32)],
        compiler_params=_cparams("arbitrary"),
        name="peer_prep",
    )(ht, wq_t, k1, k2)


def _peer_kernel(ht_ref, u_ref, un_ref, vt_ref, vp_ref, r2_ref, e2_ref, c1_ref, e1_ref, x_ref, g_ref,
                 o_ref, acc_ref, *ap_refs):
    nk = PEER_NKEYS
    c = pl.program_id(1)
    n_sub = len(ap_refs) // 2
    a_refs, p_refs = ap_refs[:n_sub], ap_refs[n_sub:]
    sub = 2 * a_refs[0].shape[0]
    T = a_refs[0].shape[1]
    rows_per_sub = sub // nk
    n_rows = n_sub * rows_per_sub
    PK = 16
    WK = PK // 2
    base = pl.multiple_of(c * n_rows, n_rows)

    tok_blocks = [slice(n, min(n + PEER_MXU_TOKENS, T)) for n in range(0, T, PEER_MXU_TOKENS)]

    def activations_from(rows_ref, dst_ref):
        for tb in tok_blocks:
            dst_ref[:, tb] = _pack_rows(_dot(rows_ref, ht_ref[:, tb]))

    def activations(s):
        activations_from(u_ref[s * sub:(s + 1) * sub, :], a_refs[s])

    def gates(s):
        for t in range(T // LANES):
            ts = slice(t * LANES, (t + 1) * LANES)
            for j in range(rows_per_sub):
                row = s * rows_per_sub + j
                gate = [None] * (nk // PK)
                for h in range(PEER_HEADS):
                    c1 = c1_ref[h, pl.ds(base, n_rows), ts][row:row + 1]
                    e1 = e1_ref[h, pl.ds(base, n_rows), ts][row:row + 1]
                    c1 = jnp.broadcast_to(c1, (PK, LANES)).astype(BF16)
                    e1 = jnp.broadcast_to(e1, (PK, LANES)).astype(BF16)
                    for v in range(nk // PK):
                        ws = slice(v * WK, (v + 1) * WK)
                        r2 = _unpack_rows(r2_ref[h, ws, ts])
                        e2 = _unpack_rows(e2_ref[h, ws, ts])
                        g = jnp.where(r2 < c1, e2 * e1, 0.0)
                        gate[v] = g if h == 0 else gate[v] + g
                for v in range(nk // PK):
                    ws = slice((j * nk + v * PK) // 2, (j * nk + (v + 1) * PK) // 2)
                    a = _unpack_rows(a_refs[s][ws, ts])
                    th = jnp.tanh(a * (GELU_K0 + GELU_K1 * (a * a)))
                    p_refs[s][ws, ts] = pltpu.bitcast(gate[v] * (a * (0.5 + 0.5 * th)), U32)

    def project_from(cols_ref, src_ref):
        for tb in tok_blocks:
            acc_ref[:, tb] += _dot(cols_ref, _unpack_rows(src_ref[:, tb]))

    def project(s):
        project_from(vt_ref[:, s * sub:(s + 1) * sub], p_refs[s])

    last = n_sub - 1

    def project_deferred():
        project_from(vp_ref[...], p_refs[last])

    @pl.when(c == 0)
    def _():
        acc_ref[...] = jnp.zeros(acc_ref.shape, F32)
        p_refs[last][...] = jnp.zeros(p_refs[last].shape, U32)
        activations(0)

    for s in range(n_sub):
        gates(s)
        if s == 0:
            project_deferred()
        else:
            project(s - 1)
        if s < last:
            activations(s + 1)
    activations_from(un_ref[...], a_refs[0])

    @pl.when(c == pl.num_programs(1) - 1)
    def _():
        project(last)
        o_ref[...] = x_ref[...] + g_ref[...] * acc_ref[...].T


def _peer_dense(ht, u_bf, vt_bf, r2, e2, c1, e1, x, g2, tok_tile, tok_off, n_tok):
    d = ht.shape[0]
    n_exp = u_bf.shape[0]
    T = tok_tile
    NC = PEER_EXPERT_CHUNK
    nk = PEER_NKEYS
    SUB = PEER_SUB_CHUNK
    n_sub = NC // SUB
    rt = lambda rows: pl.BlockSpec((PEER_HEADS, rows, T), lambda i, c: (0, 0, i + tok_off))
    return pl.pallas_call(
        _peer_kernel,
        out_shape=jax.ShapeDtypeStruct((n_tok, d), F32),
        grid=(n_tok // T, n_exp // NC),
        in_specs=[pl.BlockSpec((d, T), lambda i, c: (0, i + tok_off)),
                  pl.BlockSpec((NC, d), lambda i, c: (c, 0)),
                  pl.BlockSpec((SUB, d), lambda i, c: (jnp.minimum((c + 1) * n_sub, n_exp // SUB - 1), 0)),
                  pl.BlockSpec((d, NC), lambda i, c: (0, c)),
                  pl.BlockSpec((d, SUB), lambda i, c: (0, jnp.maximum(c * n_sub - 1, 0))),
                  rt(nk // 2), rt(nk // 2), rt(nk), rt(nk),
                  pl.BlockSpec((T, d), lambda i, c: (i + tok_off, 0)),
                  pl.BlockSpec((1, d), lambda i, c: (0, 0))],
        out_specs=pl.BlockSpec((T, d), lambda i, c: (i, 0)),
        scratch_shapes=[pltpu.VMEM((d, T), F32)] + [pltpu.VMEM((PEER_SUB_CHUNK // 2, T), U32)] * (2 * n_sub),
        compiler_params=_cparams("arbitrary", "arbitrary"),
        name="peer_dense",
    )(ht, u_bf, u_bf, vt_bf, vt_bf, r2, e2, c1, e1, x, g2)


def _rope_tables(L, n):
    t = jnp.arange(L)
    row = (t // GRID_W).astype(F32)
    col = (t % GRID_W).astype(F32)
    q = HEAD_DIM // 4
    inv = ROPE_THETA ** (-jnp.arange(q, dtype=F32) / q)
    ar = row[:, None] * inv
    ac = col[:, None] * inv
    cos = jnp.concatenate([jnp.cos(ar), jnp.cos(ar), jnp.cos(ac), jnp.cos(ac)], axis=-1)
    sin = jnp.concatenate([-jnp.sin(ar), jnp.sin(ar), -jnp.sin(ac), jnp.sin(ac)], axis=-1)
    cos = jnp.concatenate([cos, jnp.ones((n - L, HEAD_DIM), F32)], axis=0)
    sin = jnp.concatenate([sin, jnp.zeros((n - L, HEAD_DIM), F32)], axis=0)
    return jnp.tile(cos, (1, LANES // HEAD_DIM)), jnp.tile(sin, (1, LANES // HEAD_DIM))


def _block_diag(w):
    nb, di, do = w.shape
    eye = jnp.eye(nb, dtype=w.dtype)
    return (eye[:, None, :, None] * w[:, :, None, :]).reshape(nb * di, nb * do)


def kernel(x, c, ctx, c_ctx, w_mod, b_mod, norm1_g, norm2_g, w_in, w_out, lru_conv_w, lru_conv_b, lru_wa, lru_ba, lru_wx, lru_bx, lru_lam, win_qn_g, win_kn_g, win_sink, na_qn_g, na_kn_g, na_rpb, peer_wq, peer_k1, peer_k2, peer_u, peer_v):
    B, L, D = x.shape
    Lc = ctx.shape[1]
    depth = w_mod.shape[0]
    assert B == 1 and Lc == TOK_TILE and L % PEER_TOK_TILE == 0 and L // GRID_W >= 3 * NA_ROW_BLOCK
    n = L + Lc
    n_lat = L // TOK_TILE
    rows = L // GRID_W

    cvec = jnp.zeros((8, D), F32).at[0].set(c[0]).at[1].set(c_ctx)
    mods = _modulation(cvec, w_mod, b_mod)
    cos_t, sin_t = _rope_tables(L, n)
    tile2 = lambda g: jnp.tile(g, LANES // HEAD_DIM)

    xl, xc = x[0], ctx[0]
    for l in range(depth):
        with_ctx = l < depth - 1
        mod = mods[l]
        hg = jnp.zeros((8, LANES), F32)
        hg = hg.at[0].set(tile2(win_qn_g[l])).at[1].set(tile2(win_kn_g[l]))
        hg = hg.at[2].set(tile2(na_qn_g[l])).at[3].set(tile2(na_kn_g[l]))
        xa, ga, qw, kw, vw, qn, kn, vn = _in_proj(xl, xc, mod, norm1_g[l][None], w_in[l].astype(BF16),
                                                  cos_t, sin_t, hg)

        w_gates = jnp.concatenate([_block_diag(lru_wa[l, 0]), _block_diag(lru_wx[l, 0]),
                                   _block_diag(lru_wa[l, 1]), _block_diag(lru_wx[l, 1])], axis=1).astype(BF16)
        b_gates = jnp.concatenate([lru_ba[l, 0], lru_bx[l, 0], lru_ba[l, 1], lru_bx[l, 1]])[None]
        conv_w = jnp.zeros((8, D_LRU), F32).at[0:lru_conv_w.shape[1]].set(lru_conv_w[l])
        lam = jnp.zeros((8, D_LRU), F32).at[0:2].set(lru_lam[l])
        hf, hb = _lru(xa, conv_w, lru_conv_b[l][None], w_gates, b_gates, lam, n_lat)

        yb = _window_attention(win_sink[l], qw, kw, vw, L, with_ctx)
        yc = _neighborhood_attention(qn, kn, vn, _na_bias_table(na_rpb[l], rows), L, with_ctx)

        xn, ht = _out_proj(xl, xc, hf, hb, ga, yb, yc, w_out[l].astype(BF16), mod, norm2_g[l][None], with_ctx)

        r2, e2, c1, e1 = _peer_prep(ht, peer_wq[l].T.astype(BF16), peer_k1[l].astype(BF16),
                                    peer_k2[l].astype(BF16))
        u_bf = peer_u[l].astype(BF16)
        vt_bf = peer_v[l].T.astype(BF16)
        g2 = mod[:, 5 * D:6 * D]
        xl_new = _peer_dense(ht, u_bf, vt_bf, r2, e2, c1, e1, xn, g2[0:1], PEER_TOK_TILE, 0, L)
        if with_ctx:
            xc = _peer_dense(ht, u_bf, vt_bf, r2, e2, c1, e1, xn, g2[1:2], Lc, L // Lc, Lc)
        xl = xl_new
    return xl[None]
```

```python
import functools

import numpy as np
import jax
import jax.numpy as jnp
from jax import lax
from jax.experimental import pallas as pl
from jax.experimental.pallas import tpu as pltpu

F32 = jnp.float32
BF16 = jnp.bfloat16
U32 = jnp.uint32

HEAD_DIM = 64
GRID_W = 64
EPS = 1e-6
ROPE_THETA = 10000.0
D_LRU = 256
LRU_BLOCKS = 4
LRU_C = 8.0
WIN_Q_HEADS = 6
WIN_KV_HEADS = 2
WIN_BLOCK = 128
NA_HEADS = 6
NA_KH = 8
NA_KW = 16
NA_ROW_BLOCK = 4
NA_KEY_ROWS = NA_ROW_BLOCK + NA_KH - 1
PEER_HEADS = 8
PEER_NKEYS = 128
PEER_TOPK = 16
D_WIN_Q = WIN_Q_HEADS * HEAD_DIM
D_WIN_KV = WIN_KV_HEADS * HEAD_DIM
D_NA = NA_HEADS * HEAD_DIM

LANES = 128
TOK_TILE = 256
PEER_TOK_TILE = 512
PEER_EXPERT_CHUNK = 2048
PEER_SUB_CHUNK = 512
PEER_MXU_TOKENS = 256
PREP_TOK_TILE = 256
VMEM_LIMIT = 56 * 1024 * 1024
NEG = -1e30
GELU_K0 = float(np.sqrt(2.0 / np.pi))
GELU_K1 = 0.044715 * GELU_K0


def _cparams(*sem):
    return pltpu.CompilerParams(dimension_semantics=sem, vmem_limit_bytes=VMEM_LIMIT)


def _dot(a, b):
    return jnp.dot(a, b, preferred_element_type=F32)


def _pack_rows(x):
    return pltpu.bitcast(x.astype(BF16), U32)


def _unpack_rows(w):
    return pltpu.bitcast(w, BF16)


def _dot_nt(a, b):
    return lax.dot_general(a, b, (((1,), (1,)), ((), ())), preferred_element_type=F32)


def _mod_kernel(c_ref, w_ref, b_ref, o_ref):
    c = c_ref[...]
    a = (c * jax.nn.sigmoid(c)).astype(BF16)
    o_ref[0] = _dot(a, w_ref[0].astype(BF16)) + b_ref[0]


def _modulation(cvec, w_mod, b_mod):
    depth, d, d6 = w_mod.shape
    return pl.pallas_call(
        _mod_kernel,
        out_shape=jax.ShapeDtypeStruct((depth, 8, d6), F32),
        grid=(depth, d6 // d),
        in_specs=[pl.BlockSpec((8, d), lambda l, j: (0, 0)),
                  pl.BlockSpec((1, d, d), lambda l, j: (l, 0, j)),
                  pl.BlockSpec((1, 1, d), lambda l, j: (l, 0, j))],
        out_specs=pl.BlockSpec((1, 8, d), lambda l, j: (l, 0, j)),
        compiler_params=_cparams("arbitrary", "arbitrary"),
        name="modulation",
    )(cvec, w_mod, b_mod.reshape(depth, 1, d6))


def _pair_mean_matrix():
    r = lax.broadcasted_iota(jnp.int32, (LANES, LANES), 0) < HEAD_DIM
    c = lax.broadcasted_iota(jnp.int32, (LANES, LANES), 1) < HEAD_DIM
    return jnp.where(r == c, 1.0 / HEAD_DIM, 0.0).astype(BF16)


def _head_norm(z, gain, bd):
    z2 = z * z
    hi = z2.astype(BF16)
    lo = (z2 - hi.astype(F32)).astype(BF16)
    ms = _dot(hi, bd) + _dot(lo, bd)
    return z * lax.rsqrt(ms + EPS) * gain


def _rope(z, cos, sin):
    lane = lax.broadcasted_iota(jnp.int32, z.shape, 1)
    first = (lane & 16) == 0
    partner = jnp.where(first, pltpu.roll(z, LANES - 16, 1), pltpu.roll(z, 16, 1))
    return z * cos + partner * sin


def _in_kernel(xl_ref, xc_ref, mod_ref, g_ref, w_ref, cos_ref, sin_ref, hg_ref,
               xa_ref, ga_ref, qw_ref, kw_ref, vw_ref, qn_ref, kn_ref, vn_ref, *, n_lat):
    d = xl_ref.shape[1]
    i = pl.program_id(0)
    row = (i == n_lat).astype(jnp.int32)
    is_ctx = jnp.full((TOK_TILE, 1), row) == 1
    x = jnp.where(is_ctx, xc_ref[...], xl_ref[...])
    ms = jnp.mean(x * x, axis=-1, keepdims=True)
    y = x * lax.rsqrt(ms + EPS) * g_ref[...]
    sh = mod_ref[pl.ds(row, 1), 0:d]
    sc = mod_ref[pl.ds(row, 1), d:2 * d]
    h = (y * (1.0 + sc) + sh).astype(BF16)
    z = _dot(h, w_ref[...])

    bd = _pair_mean_matrix()
    cos = cos_ref[...]
    sin = sin_ref[...]
    scale = HEAD_DIM ** -0.5
    o = 0
    xa_ref[...] = z[:, o:o + D_LRU]
    o += D_LRU
    ga_ref[...] = z[:, o:o + D_LRU]
    o += D_LRU
    for g in range(D_WIN_Q // LANES):
        zz = _rope(_head_norm(z[:, o:o + LANES], hg_ref[0:1], bd), cos, sin)
        qw_ref[:, g * LANES:(g + 1) * LANES] = (zz * scale).astype(BF16)
        o += LANES
    kw_ref[...] = _rope(_head_norm(z[:, o:o + LANES], hg_ref[1:2], bd), cos, sin).astype(BF16)
    o += LANES
    vw_ref[...] = z[:, o:o + LANES].astype(BF16)
    o += LANES
    for g in range(D_NA // LANES):
        zz = _head_norm(z[:, o:o + LANES], hg_ref[2:3], bd)
        qn_ref[:, g * LANES:(g + 1) * LANES] = (zz * scale).astype(BF16)
        o += LANES
    for g in range(D_NA // LANES):
        zz = _head_norm(z[:, o:o + LANES], hg_ref[3:4], bd)
        kn_ref[:, g * LANES:(g + 1) * LANES] = zz.astype(BF16)
        o += LANES
    vn_ref[...] = z[:, o:o + D_NA].astype(BF16)


def _in_proj(xl, xc, mod, g, w_bf, cos_t, sin_t, hg):
    L, d = xl.shape
    n_lat = L // TOK_TILE
    n = L + xc.shape[0]
    d_in = w_bf.shape[1]
    T = TOK_TILE
    tok = lambda w: pl.BlockSpec((T, w), lambda i: (i, 0))
    full = lambda a: pl.BlockSpec(a.shape, lambda i: (0,) * a.ndim)
    widths = (D_LRU, D_LRU, D_WIN_Q, D_WIN_KV, D_WIN_KV, D_NA, D_NA, D_NA)
    dtypes = (F32, F32, BF16, BF16, BF16, BF16, BF16, BF16)
    return pl.pallas_call(
        functools.partial(_in_kernel, n_lat=n_lat),
        out_shape=[jax.ShapeDtypeStruct((n, w), t) for w, t in zip(widths, dtypes)],
        grid=(n_lat + 1,),
        in_specs=[pl.BlockSpec((T, d), lambda i: (jnp.minimum(i, n_lat - 1), 0)),
                  pl.BlockSpec((T, d), lambda i: (0, 0)),
                  full(mod), full(g), full(w_bf), tok(LANES), tok(LANES), full(hg)],
        out_specs=[tok(w) for w in widths],
        compiler_params=_cparams("arbitrary"),
        name="in_proj",
    )(xl, xc, mod, g, w_bf, cos_t, sin_t, hg)


def _chunk_scan(a, b, reverse):
    T = a.shape[0]
    rows = lax.broadcasted_iota(jnp.int32, a.shape, 0)
    s = 1
    while s < T:
        if reverse:
            edge = rows >= T - s
            shift = T - s
        else:
            edge = rows < s
            shift = s
        a_s = jnp.where(edge, 1.0, pltpu.roll(a, shift, 0))
        b_s = jnp.where(edge, 0.0, pltpu.roll(b, shift, 0))
        b = a * b_s + b
        a = a * a_s
        s *= 2
    return a, b


def _lru_direction(xm, ph, nh, pv, nv, cw, cb, w, bias, spl, carry_ref, first, reverse):
    T, C = xm.shape
    rows = lax.broadcasted_iota(jnp.int32, (T, C), 0)
    p6 = ph[6:7] * pv
    p7 = ph[7:8] * pv
    n0 = nh[0:1] * nv
    x_m1 = jnp.where(rows == 0, p7, pltpu.roll(xm, 1, 0))
    x_m2 = jnp.where(rows == 0, p6, jnp.where(rows == 1, p7, pltpu.roll(xm, 2, 0)))
    x_p1 = jnp.where(rows == T - 1, n0, pltpu.roll(xm, T - 1, 0))
    u = cw[0:1] * x_m2 + cw[1:2] * x_m1 + cw[2:3] * xm + cw[3:4] * x_p1 + cb
    zz = _dot(u.astype(BF16), w) + bias
    r = jax.nn.sigmoid(zz[:, :C])
    ig = jax.nn.sigmoid(zz[:, C:])
    log_a = -LRU_C * r * spl
    a = jnp.exp(log_a)
    b = jnp.sqrt(-jnp.tanh(log_a) * (a * a + 1.0)) * ig * u
    a, b = _chunk_scan(a, b, reverse)

    @pl.when(first)
    def _():
        carry_ref[...] = jnp.zeros(carry_ref.shape, F32)

    h = a * carry_ref[0:1] + b
    edge = h[0:1] if reverse else h[T - 1:T]
    carry_ref[...] = jnp.broadcast_to(edge, carry_ref.shape)
    return h


def _lru_kernel(xf_ref, pf_ref, nf_ref, xb_ref, pb_ref, nb_ref, cw_ref, cb_ref, w_ref, b_ref,
                lam_ref, hf_ref, hb_ref, cf_ref, cbk_ref, *, n_lat):
    j = pl.program_id(0)
    C = D_LRU
    fblk = jnp.where(j == 0, n_lat, j - 1)
    bblk = jnp.where(j == 0, n_lat, n_lat - j)
    lam = lam_ref[...]
    spl = jnp.maximum(-lam, 0.0) + jnp.log1p(jnp.exp(-jnp.abs(lam)))
    cw = cw_ref[...]
    cb = cb_ref[...]

    def halo_valid(blk):
        pv = jnp.logical_and(blk != 0, blk != n_lat).astype(F32)
        nv = jnp.logical_and(blk != n_lat - 1, blk != n_lat).astype(F32)
        return pv, nv

    pv, nv = halo_valid(fblk)
    hf_ref[...] = _lru_direction(xf_ref[...], pf_ref[...], nf_ref[...], pv, nv, cw, cb,
                                 w_ref[:, 0:2 * C], b_ref[:, 0:2 * C], spl[0:1], cf_ref, j == 0, False)
    pv, nv = halo_valid(bblk)
    hb_ref[...] = _lru_direction(xb_ref[...], pb_ref[...], nb_ref[...], pv, nv, cw, cb,
                                 w_ref[:, 2 * C:4 * C], b_ref[:, 2 * C:4 * C], spl[1:2], cbk_ref, j == 0, True)


def _lru(xa, conv_w, conv_b, w_gates, b_gates, lam, n_lat):
    n, C = xa.shape
    T = TOK_TILE
    sub = T // 8
    nblk8 = n // 8
    fblk = lambda j: jnp.where(j == 0, n_lat, j - 1)
    bblk = lambda j: jnp.where(j == 0, n_lat, n_lat - j)
    prev8 = lambda blk: jnp.maximum(blk * sub - 1, 0)
    next8 = lambda blk: jnp.minimum((blk + 1) * sub, nblk8 - 1)
    full = lambda a: pl.BlockSpec(a.shape, lambda j: (0,) * a.ndim)
    return pl.pallas_call(
        functools.partial(_lru_kernel, n_lat=n_lat),
        out_shape=[jax.ShapeDtypeStruct((n, C), F32)] * 2,
        grid=(n_lat + 1,),
        in_specs=[pl.BlockSpec((T, C), lambda j: (fblk(j), 0)),
                  pl.BlockSpec((8, C), lambda j: (prev8(fblk(j)), 0)),
                  pl.BlockSpec((8, C), lambda j: (next8(fblk(j)), 0)),
                  pl.BlockSpec((T, C), lambda j: (bblk(j), 0)),
                  pl.BlockSpec((8, C), lambda j: (prev8(bblk(j)), 0)),
                  pl.BlockSpec((8, C), lambda j: (next8(bblk(j)), 0)),
                  full(conv_w), full(conv_b), full(w_gates), full(b_gates), full(lam)],
        out_specs=[pl.BlockSpec((T, C), lambda j: (fblk(j), 0)),
                   pl.BlockSpec((T, C), lambda j: (bblk(j), 0))],
        scratch_shapes=[pltpu.VMEM((8, C), F32), pltpu.VMEM((8, C), F32)],
        compiler_params=_cparams("arbitrary"),
        name="rglru",
    )(xa, xa, xa, xa, xa, xa, conv_w, conv_b, w_gates, b_gates, lam)


def _win_kernel(sink_ref, q_ref, kp_ref, kc_ref, kn_ref, vp_ref, vc_ref, vn_ref, kx_ref, vx_ref,
                o_ref, *, nb):
    W = WIN_BLOCK
    R = WIN_Q_HEADS // WIN_KV_HEADS
    b = pl.program_id(0)
    blk = jnp.where(b < nb, b, -4)
    rows = lax.broadcasted_iota(jnp.int32, (R * W, 3 * W), 0)
    cols = lax.broadcasted_iota(jnp.int32, (R * W, 3 * W), 1)
    kpos = (blk - 1) * W + cols
    qpos = blk * W + (rows & (W - 1))
    valid = jnp.logical_and(jnp.abs(kpos - qpos) <= W, jnp.logical_and(kpos >= 0, kpos < nb * W))
    row1 = lax.broadcasted_iota(jnp.int32, (R * W, 1), 0)
    q = q_ref[...]
    for g in range(WIN_KV_HEADS):
        ls = slice(g * HEAD_DIM, (g + 1) * HEAD_DIM)
        qs = jnp.concatenate([q[:, (g * R + r) * HEAD_DIM:(g * R + r + 1) * HEAD_DIM] for r in range(R)], axis=0)
        kl = jnp.concatenate([kp_ref[:, ls], kc_ref[:, ls], kn_ref[:, ls]], axis=0)
        vl = jnp.concatenate([vp_ref[:, ls], vc_ref[:, ls], vn_ref[:, ls]], axis=0)
        s_loc = jnp.where(valid, _dot_nt(qs, kl), NEG)
        s_ctx = _dot_nt(qs, kx_ref[:, ls])
        sink = jnp.full((R * W, 1), sink_ref[g * R], F32)
        for r in range(1, R):
            sink = jnp.where(row1 >= r * W, sink_ref[g * R + r], sink)
        m = jnp.maximum(jnp.maximum(jnp.max(s_loc, axis=-1, keepdims=True),
                                    jnp.max(s_ctx, axis=-1, keepdims=True)), sink)
        p_loc = jnp.exp(s_loc - m)
        p_ctx = jnp.exp(s_ctx - m)
        den = (jnp.sum(p_loc, axis=-1, keepdims=True) + jnp.sum(p_ctx, axis=-1, keepdims=True)
               + jnp.exp(sink - m))
        o = (_dot(p_loc.astype(BF16), vl) + _dot(p_ctx.astype(BF16), vx_ref[:, ls])) / den
        for r in range(R):
            hh = g * R + r
            o_ref[:, hh * HEAD_DIM:(hh + 1) * HEAD_DIM] = o[r * W:(r + 1) * W].astype(o_ref.dtype)


def _window_attention(sink, qw, kw, vw, L, with_ctx):
    n = qw.shape[0]
    W = WIN_BLOCK
    nb = L // W
    nq = n // W if with_ctx else nb
    ctx_blk = L // TOK_TILE
    lat = lambda b: jnp.minimum(b, nb - 1)
    kv = lambda f: pl.BlockSpec((W, D_WIN_KV), lambda b: (f(b), 0))
    prv = lambda b: jnp.maximum(lat(b) - 1, 0)
    nxt = lambda b: jnp.minimum(lat(b) + 1, nb - 1)
    ctx = pl.BlockSpec((n - L, D_WIN_KV), lambda b: (ctx_blk, 0))
    return pl.pallas_call(
        functools.partial(_win_kernel, nb=nb),
        out_shape=jax.ShapeDtypeStruct((nq * W, D_WIN_Q), BF16),
        grid=(nq,),
        in_specs=[pl.BlockSpec(memory_space=pltpu.SMEM),
                  pl.BlockSpec((W, D_WIN_Q), lambda b: (b, 0)),
                  kv(prv), kv(lat), kv(nxt), kv(prv), kv(lat), kv(nxt), ctx, ctx],
        out_specs=pl.BlockSpec((W, D_WIN_Q), lambda b: (b, 0)),
        compiler_params=_cparams("arbitrary"),
        name="window_attn",
    )(sink, qw, kw, kw, kw, vw, vw, vw, kw, vw)


def _na_kernel(q_ref, kl_ref, vl_ref, kx_ref, vx_ref, bias_ref, o_ref, *, n_blocks):
    b = pl.program_id(0)
    nq = q_ref.shape[0]
    latent = jnp.full((nq, 1), (b < n_blocks).astype(jnp.int32)) == 1
    q = q_ref[...]
    for h in range(NA_HEADS):
        ls = slice(h * HEAD_DIM, (h + 1) * HEAD_DIM)
        qh = q[:, ls]
        s_loc = jnp.where(latent, _dot_nt(qh, kl_ref[:, ls]) + bias_ref[0, h], NEG)
        s_ctx = _dot_nt(qh, kx_ref[:, ls])
        m = jnp.maximum(jnp.max(s_loc, axis=-1, keepdims=True), jnp.max(s_ctx, axis=-1, keepdims=True))
        p_loc = jnp.exp(s_loc - m)
        p_ctx = jnp.exp(s_ctx - m)
        den = jnp.sum(p_loc, axis=-1, keepdims=True) + jnp.sum(p_ctx, axis=-1, keepdims=True)
        o = (_dot(p_loc.astype(BF16), vl_ref[:, ls]) + _dot(p_ctx.astype(BF16), vx_ref[:, ls])) / den
        o_ref[:, ls] = o.astype(o_ref.dtype)


def _na_bias_table(rpb, rows):
    R, KR = NA_ROW_BLOCK, NA_KEY_ROWS
    qc = np.arange(GRID_W)[:, None]
    kc = np.arange(GRID_W)[None, :]
    qstart = np.clip(qc - NA_KW // 2, 0, GRID_W - NA_KW)
    inside = (kc - qstart >= 0) & (kc - qstart < NA_KW)
    pad = GRID_W - NA_KW
    rp = jnp.pad(rpb.astype(F32), ((0, 0), (0, 0), (pad, pad)))
    shifted = jnp.stack([rp[:, :, NA_KW - 1 - q + pad:NA_KW - 1 - q + pad + GRID_W] for q in range(GRID_W)], axis=2)
    per_dr = jnp.where(inside[None, None], shifted, NEG)
    masked = jnp.full((NA_HEADS, GRID_W, GRID_W), NEG, F32)
    half = NA_KH // 2
    cases = []
    for r0, ks in ((0, 0), (half, 0), (rows - R, rows - KR)):
        row_blocks = []
        for rr in range(R):
            r = r0 + rr
            kr = min(max(r - half, 0), rows - NA_KH)
            blocks = [per_dr[:, ks + kk - r + NA_KH - 1] if kr <= ks + kk < kr + NA_KH else masked
                      for kk in range(KR)]
            row_blocks.append(jnp.concatenate(blocks, axis=-1))
        cases.append(jnp.concatenate(row_blocks, axis=1))
    return jnp.stack(cases)


def _neighborhood_attention(qn, kn, vn, bias, L, with_ctx):
    n = qn.shape[0]
    R, KR = NA_ROW_BLOCK, NA_KEY_ROWS
    rows = L // GRID_W
    nb = rows // R
    nq = R * GRID_W
    steps = n // nq if with_ctx else nb
    ctx_blk = L // TOK_TILE
    half = NA_KH // 2

    def blk_of(b):
        return jnp.minimum(b, nb - 1)

    def key_start(b):
        return jnp.clip(blk_of(b) * R - half, 0, rows - KR) * GRID_W

    def case_of(b):
        return jnp.where(blk_of(b) == 0, 0, jnp.where(blk_of(b) == nb - 1, 2, 1))

    loc = pl.BlockSpec((pl.Element(KR * GRID_W), pl.Element(D_NA)), lambda b: (key_start(b), 0))
    ctx = pl.BlockSpec((n - L, D_NA), lambda b: (ctx_blk, 0))
    return pl.pallas_call(
        functools.partial(_na_kernel, n_blocks=nb),
        out_shape=jax.ShapeDtypeStruct((steps * nq, D_NA), BF16),
        grid=(steps,),
        in_specs=[pl.BlockSpec((nq, D_NA), lambda b: (b, 0)), loc, loc, ctx, ctx,
                  pl.BlockSpec((1, NA_HEADS, nq, KR * GRID_W), lambda b: (case_of(b), 0, 0, 0))],
        out_specs=pl.BlockSpec((nq, D_NA), lambda b: (b, 0)),
        compiler_params=_cparams("arbitrary"),
        name="neighborhood_attn",
    )(qn, kn, vn, kn, vn, bias)


def _out_kernel(xl_ref, xc_ref, hf_ref, hb_ref, ga_ref, yb_ref, yc_ref, w_ref, mod_ref, g_ref,
                xo_ref, ht_ref, *, n_lat):
    d = xl_ref.shape[1]
    i = pl.program_id(0)
    row = (i == n_lat).astype(jnp.int32)
    is_ctx = jnp.full((TOK_TILE, 1), row) == 1
    x = jnp.where(is_ctx, xc_ref[...], xl_ref[...])
    ya = ((hf_ref[...] + hb_ref[...]) * jax.nn.gelu(ga_ref[...])).astype(BF16)
    o1 = D_LRU
    o2 = D_LRU + D_WIN_Q
    mix = (_dot(ya, w_ref[0:o1]) + _dot(yb_ref[...], w_ref[o1:o2]) + _dot(yc_ref[...], w_ref[o2:o2 + D_NA]))
    xn = x + mod_ref[pl.ds(row, 1), 2 * d:3 * d] * mix
    xo_ref[...] = xn
    ms = jnp.mean(xn * xn, axis=-1, keepdims=True)
    y = xn * lax.rsqrt(ms + EPS) * g_ref[...]
    h2 = y * (1.0 + mod_ref[pl.ds(row, 1), 4 * d:5 * d]) + mod_ref[pl.ds(row, 1), 3 * d:4 * d]
    ht_ref[...] = h2.T.astype(BF16)


def _out_proj(xl, xc, hf, hb, ga, yb, yc, w_bf, mod, g, with_ctx):
    L, d = xl.shape
    T = TOK_TILE
    n_lat = L // T
    nt = n_lat + 1 if with_ctx else n_lat
    tok = lambda w: pl.BlockSpec((T, w), lambda i: (i, 0))
    full = lambda a: pl.BlockSpec(a.shape, lambda i: (0,) * a.ndim)
    return pl.pallas_call(
        functools.partial(_out_kernel, n_lat=n_lat),
        out_shape=[jax.ShapeDtypeStruct((nt * T, d), F32), jax.ShapeDtypeStruct((d, nt * T), BF16)],
        grid=(nt,),
        in_specs=[pl.BlockSpec((T, d), lambda i: (jnp.minimum(i, n_lat - 1), 0)),
                  pl.BlockSpec((T, d), lambda i: (0, 0)),
                  tok(D_LRU), tok(D_LRU), tok(D_LRU), tok(D_WIN_Q), tok(D_NA),
                  full(w_bf), full(mod), full(g)],
        out_specs=[tok(d), pl.BlockSpec((d, T), lambda i: (0, i))],
        compiler_params=_cparams("arbitrary"),
        name="out_proj",
    )(xl, xc, hf, hb, ga, yb, yc, w_bf, mod, g)


SUBLANES = 8


def _sorting_pairs(n):
    pairs, p = [], 1
    while p < n:
        k = p
        while k >= 1:
            for j in range(k % p, n - k, 2 * k):
                for i in range(min(k, n - j - k)):
                    if (i + j) // (2 * p) == (i + j + k) // (2 * p):
                        pairs.append((i + j, i + j + k))
            k //= 2
        p *= 2
    return pairs


def _vmax(a, b):
    if a is None:
        return b
    if b is None:
        return a
    return jnp.maximum(a, b)


def _vmin(a, b):
    if a is None or b is None:
        return None
    return jnp.minimum(a, b)


def _top16_sorted(slabs):
    K = PEER_TOPK
    w = list(slabs)
    for a, b in _sorting_pairs(K):
        w[a], w[b] = _vmax(w[a], w[b]), _vmin(w[a], w[b])
    shift = SUBLANES // 2
    while shift >= 1:
        partner = [None if w[K - 1 - i] is None else pltpu.roll(w[K - 1 - i], shift, 0) for i in range(K)]
        w = [_vmax(w[i], partner[i]) for i in range(K)]
        stride = K // 2
        while stride >= 1:
            for i in range(K):
                if i & stride == 0:
                    w[i], w[i + stride] = _vmax(w[i], w[i + stride]), _vmin(w[i], w[i + stride])
            stride //= 2
        shift //= 2
    return w


def _allsum8(x):
    x = x + pltpu.roll(x, 4, 0)
    x = x + pltpu.roll(x, 2, 0)
    return x + pltpu.roll(x, 1, 0)


def _route_tile(s1, s2):
    K = PEER_TOPK
    S = SUBLANES
    n_slab = s1.shape[0] // S
    a1 = [s1[j * S:(j + 1) * S] for j in range(n_slab)]
    a2 = [s2[j * S:(j + 1) * S] for j in range(n_slab)]
    v1 = _top16_sorted(a1)
    v2 = _top16_sorted(a2)
    sub = lax.broadcasted_iota(jnp.int32, a1[0].shape, 0)

    def as_rows(v, lo):
        out = v[lo]
        for b in range(1, S):
            out = jnp.where(sub == b, v[lo + b], out)
        return out

    v2_lo, v2_hi, v1_hi = as_rows(v2, 0), as_rows(v2, S), as_rows(v1, S)
    lens = [K // (a + 1) for a in range(S)]
    cands = [v1[0] + v2_lo, v1[0] + v2_hi]
    for a in range(1, S):
        ca = v1[a] + v2_lo
        cands.append(ca if lens[a] >= S else jnp.where(sub < lens[a], ca, -jnp.inf))
    cands.append(v1_hi + v2[0])
    thr = _top16_sorted(cands + [None] * (K - len(cands)))[K - 1]
    x2_lo, x2_hi = jnp.exp(v2_lo - v2[0]), jnp.exp(v2_hi - v2[0])
    sel_lo, sel_hi = cands[0] >= thr, cands[1] >= thr
    zsum = jnp.where(sel_lo, x2_lo, 0.0) + jnp.where(sel_hi, x2_hi, 0.0)
    cnt = [_allsum8(jnp.where(sel_lo, 1.0, 0.0) + jnp.where(sel_hi, 1.0, 0.0))]
    for a in range(1, S):
        sel = cands[a + 1] >= thr
        zsum = zsum + jnp.where(sel, jnp.exp(v1[a] - v1[0]) * x2_lo, 0.0)
        cnt.append(_allsum8(jnp.where(sel, 1.0, 0.0)))
    zsum = zsum + jnp.where(cands[S + 1] >= thr, jnp.exp(v1_hi - v1[0]), 0.0)
    inv_z = 1.0 / _allsum8(zsum)
    r2, e2, c1, e1 = [], [], [], []
    for j in range(n_slab):
        r = jnp.full(a2[j].shape, float(K), F32)
        for k in reversed(range(K)):
            r = jnp.where(a2[j] >= v2[k], float(k), r)
        r2.append(r)
        e2.append(jnp.exp(a2[j] - v2[0]) * inv_z)
        c = jnp.where(a1[j] + v2[0] >= thr, 1.0, 0.0)
        for a in reversed(range(S)):
            c = jnp.where(a1[j] >= v1[a], cnt[a], c)
        c1.append(c)
        e1.append(jnp.exp(a1[j] - v1[0]))
    cat = lambda xs: jnp.concatenate(xs, axis=0)
    return cat(r2), cat(e2), cat(c1), cat(e1)


def _prep_kernel(ht_ref, wq_ref, k1_ref, k2_ref, r2_ref, e2_ref, c1_ref, e1_ref, q_scr):
    nk = PEER_NKEYS
    q_scr[...] = _dot(wq_ref[...], ht_ref[...])

    def head(h, carry):
        base = pl.multiple_of(h * 2 * nk, 2 * nk)
        s1 = _dot(k1_ref[...], q_scr[pl.ds(base, nk), :].astype(BF16))
        s2 = _dot(k2_ref[...], q_scr[pl.ds(base + nk, nk), :].astype(BF16))
        for t in range(s1.shape[1] // LANES):
            ts = slice(t * LANES, (t + 1) * LANES)
            r2, e2, c1, e1 = _route_tile(s1[:, ts], s2[:, ts])
            r2_ref[h, :, ts] = _pack_rows(r2)
            e2_ref[h, :, ts] = _pack_rows(e2)
            c1_ref[h, :, ts] = c1
            e1_ref[h, :, ts] = e1
        return carry

    lax.fori_loop(0, PEER_HEADS, head, 0)


def _peer_prep(ht, wq_t, k1, k2):
    d, n = ht.shape
    T = PREP_TOK_TILE
    nk = PEER_NKEYS
    full = lambda a: pl.BlockSpec(a.shape, lambda i: (0,) * a.ndim)
    spec = lambda rows: pl.BlockSpec((PEER_HEADS, rows, T), lambda i: (0, 0, i))
    return pl.pallas_call(
        _prep_kernel,
        out_shape=[jax.ShapeDtypeStruct((PEER_HEADS, nk // 2, n), U32)] * 2
        + [jax.ShapeDtypeStruct((PEER_HEADS, nk, n), F32)] * 2,
        grid=(n // T,),
        in_specs=[pl.BlockSpec((d, T), lambda i: (0, i)), full(wq_t), full(k1), full(k2)],
        out_specs=[spec(nk // 2), spec(nk // 2), spec(nk), spec(nk)],
        scratch_shapes=[pltpu.VMEM((wq_t.shape[0], T), F32)],
        compiler_params=_cparams("arbitrary"),
        name="peer_prep",
    )(ht, wq_t, k1, k2)


def _peer_kernel(ht_ref, u_ref, un_ref, vt_ref, vp_ref, r2_ref, e2_ref, c1_ref, e1_ref, x_ref, g_ref,
                 o_ref, acc_ref, *ap_refs):
    nk = PEER_NKEYS
    c = pl.program_id(1)
    n_sub = len(ap_refs) // 2
    a_refs, p_refs = ap_refs[:n_sub], ap_refs[n_sub:]
    sub = 2 * a_refs[0].shape[0]
    T = a_refs[0].shape[1]
    rows_per_sub = sub // nk
    PK = 16
    WK = PK // 2

    tok_blocks = [slice(n, min(n + PEER_MXU_TOKENS, T)) for n in range(0, T, PEER_MXU_TOKENS)]

    def activations_from(rows_ref, dst_ref):
        for tb in tok_blocks:
            dst_ref[:, tb] = _pack_rows(_dot(rows_ref, ht_ref[:, tb]))

    def activations(s):
        activations_from(u_ref[s * sub:(s + 1) * sub, :], a_refs[s])

    def gates(s):
        for t in range(T // LANES):
            ts = slice(t * LANES, (t + 1) * LANES)
            for j in range(rows_per_sub):
                row = s * rows_per_sub + j
                gate = [None] * (nk // PK)
                for h in range(PEER_HEADS):
                    c1 = c1_ref[h, :, ts][row:row + 1]
                    e1 = e1_ref[h, :, ts][row:row + 1]
                    c1 = jnp.broadcast_to(c1, (PK, LANES)).astype(BF16)
                    e1 = jnp.broadcast_to(e1, (PK, LANES)).astype(BF16)
                    for v in range(nk // PK):
                        ws = slice(v * WK, (v + 1) * WK)
                        r2 = _unpack_rows(r2_ref[h, ws, ts])
                        e2 = _unpack_rows(e2_ref[h, ws, ts])
                        g = jnp.where(r2 < c1, e2 * e1, 0.0)
                        gate[v] = g if h == 0 else gate[v] + g
                for v in range(nk // PK):
                    ws = slice((j * nk + v * PK) // 2, (j * nk + (v + 1) * PK) // 2)
                    a = _unpack_rows(a_refs[s][ws, ts])
                    th = jnp.tanh(a * (GELU_K0 + GELU_K1 * (a * a)))
                    p_refs[s][ws, ts] = pltpu.bitcast(gate[v] * (a * (0.5 + 0.5 * th)), U32)

    def project_from(cols_ref, src_ref):
        for tb in tok_blocks:
            acc_ref[:, tb] += _dot(cols_ref, _unpack_rows(src_ref[:, tb]))

    def project(s):
        project_from(vt_ref[:, s * sub:(s + 1) * sub], p_refs[s])

    last = n_sub - 1

    def project_deferred():
        project_from(vp_ref[...], p_refs[last])

    @pl.when(c == 0)
    def _():
        acc_ref[...] = jnp.zeros(acc_ref.shape, F32)
        p_refs[last][...] = jnp.zeros(p_refs[last].shape, U32)
        activations(0)

    for s in range(n_sub):
        gates(s)
        if s == 0:
            project_deferred()
        else:
            project(s - 1)
        if s < last:
            activations(s + 1)
    activations_from(un_ref[...], a_refs[0])

    @pl.when(c == pl.num_programs(1) - 1)
    def _():
        project(last)
        o_ref[...] = x_ref[...] + g_ref[...] * acc_ref[...].T


def _peer_dense(ht, u_bf, vt_bf, r2, e2, c1, e1, x, g2, tok_tile, tok_off, n_tok):
    d = ht.shape[0]
    n_exp = u_bf.shape[0]
    T = tok_tile
    NC = PEER_EXPERT_CHUNK
    nk = PEER_NKEYS
    SUB = PEER_SUB_CHUNK
    n_sub = NC // SUB
    rt = lambda rows: pl.BlockSpec((PEER_HEADS, rows, T), lambda i, c: (0, 0, i + tok_off))
    per_chunk = pl.BlockSpec((PEER_HEADS, NC // nk, T), lambda i, c: (0, c, i + tok_off))
    return pl.pallas_call(
        _peer_kernel,
        out_shape=jax.ShapeDtypeStruct((n_tok, d), F32),
        grid=(n_tok // T, n_exp // NC),
        in_specs=[pl.BlockSpec((d, T), lambda i, c: (0, i + tok_off)),
                  pl.BlockSpec((NC, d), lambda i, c: (c, 0)),
                  pl.BlockSpec((SUB, d), lambda i, c: (jnp.minimum((c + 1) * n_sub, n_exp // SUB - 1), 0)),
                  pl.BlockSpec((d, NC), lambda i, c: (0, c)),
                  pl.BlockSpec((d, SUB), lambda i, c: (0, jnp.maximum(c * n_sub - 1, 0))),
                  rt(nk // 2), rt(nk // 2), per_chunk, per_chunk,
                  pl.BlockSpec((T, d), lambda i, c: (i + tok_off, 0)),
                  pl.BlockSpec((1, d), lambda i, c: (0, 0))],
        out_specs=pl.BlockSpec((T, d), lambda i, c: (i, 0)),
        scratch_shapes=[pltpu.VMEM((d, T), F32)] + [pltpu.VMEM((PEER_SUB_CHUNK // 2, T), U32)] * (2 * n_sub),
        compiler_params=_cparams("arbitrary", "arbitrary"),
        name="peer_dense",
    )(ht, u_bf, u_bf, vt_bf, vt_bf, r2, e2, c1, e1, x, g2)


def _rope_tables(L, n):
    t = jnp.arange(L)
    row = (t // GRID_W).astype(F32)
    col = (t % GRID_W).astype(F32)
    q = HEAD_DIM // 4
    inv = ROPE_THETA ** (-jnp.arange(q, dtype=F32) / q)
    ar = row[:, None] * inv
    ac = col[:, None] * inv
    cos = jnp.concatenate([jnp.cos(ar), jnp.cos(ar), jnp.cos(ac), jnp.cos(ac)], axis=-1)
    sin = jnp.concatenate([-jnp.sin(ar), jnp.sin(ar), -jnp.sin(ac), jnp.sin(ac)], axis=-1)
    cos = jnp.concatenate([cos, jnp.ones((n - L, HEAD_DIM), F32)], axis=0)
    sin = jnp.concatenate([sin, jnp.zeros((n - L, HEAD_DIM), F32)], axis=0)
    return jnp.tile(cos, (1, LANES // HEAD_DIM)), jnp.tile(sin, (1, LANES // HEAD_DIM))


def _block_diag(w):
    nb, di, do = w.shape
    eye = jnp.eye(nb, dtype=w.dtype)
    return (eye[:, None, :, None] * w[:, :, None, :]).reshape(nb * di, nb * do)


def kernel(x, c, ctx, c_ctx, w_mod, b_mod, norm1_g, norm2_g, w_in, w_out, lru_conv_w, lru_conv_b, lru_wa, lru_ba, lru_wx, lru_bx, lru_lam, win_qn_g, win_kn_g, win_sink, na_qn_g, na_kn_g, na_rpb, peer_wq, peer_k1, peer_k2, peer_u, peer_v):
    B, L, D = x.shape
    Lc = ctx.shape[1]
    depth = w_mod.shape[0]
    assert B == 1 and Lc == TOK_TILE and L % PEER_TOK_TILE == 0 and L // GRID_W >= 3 * NA_ROW_BLOCK
    n = L + Lc
    n_lat = L // TOK_TILE
    rows = L // GRID_W

    cvec = jnp.zeros((8, D), F32).at[0].set(c[0]).at[1].set(c_ctx)
    mods = _modulation(cvec, w_mod, b_mod)
    cos_t, sin_t = _rope_tables(L, n)
    tile2 = lambda g: jnp.tile(g, LANES // HEAD_DIM)

    xl, xc = x[0], ctx[0]
    for l in range(depth):
        with_ctx = l < depth - 1
        mod = mods[l]
        hg = jnp.zeros((8, LANES), F32)
        hg = hg.at[0].set(tile2(win_qn_g[l])).at[1].set(tile2(win_kn_g[l]))
        hg = hg.at[2].set(tile2(na_qn_g[l])).at[3].set(tile2(na_kn_g[l]))
        xa, ga, qw, kw, vw, qn, kn, vn = _in_proj(xl, xc, mod, norm1_g[l][None], w_in[l].astype(BF16),
                                                  cos_t, sin_t, hg)

        w_gates = jnp.concatenate([_block_diag(lru_wa[l, 0]), _block_diag(lru_wx[l, 0]),
                                   _block_diag(lru_wa[l, 1]), _block_diag(lru_wx[l, 1])], axis=1).astype(BF16)
        b_gates = jnp.concatenate([lru_ba[l, 0], lru_bx[l, 0], lru_ba[l, 1], lru_bx[l, 1]])[None]
        conv_w = jnp.zeros((8, D_LRU), F32).at[0:lru_conv_w.shape[1]].set(lru_conv_w[l])
        lam = jnp.zeros((8, D_LRU), F32).at[0:2].set(lru_lam[l])
        hf, hb = _lru(xa, conv_w, lru_conv_b[l][None], w_gates, b_gates, lam, n_lat)

        yb = _window_attention(win_sink[l], qw, kw, vw, L, with_ctx)
        yc = _neighborhood_attention(qn, kn, vn, _na_bias_table(na_rpb[l], rows), L, with_ctx)

        xn, ht = _out_proj(xl, xc, hf, hb, ga, yb, yc, w_out[l].astype(BF16), mod, norm2_g[l][None], with_ctx)

        r2, e2, c1, e1 = _peer_prep(ht, peer_wq[l].T.astype(BF16), peer_k1[l].astype(BF16),
                                    peer_k2[l].astype(BF16))
        u_bf = peer_u[l].astype(BF16)
        vt_bf = peer_v[l].T.astype(BF16)
        g2 = mod[:, 5 * D:6 * D]
        xl_new = _peer_dense(ht, u_bf, vt_bf, r2, e2, c1, e1, xn, g2[0:1], PEER_TOK_TILE, 0, L)
        if with_ctx:
            xc = _peer_dense(ht, u_bf, vt_bf, r2, e2, c1, e1, xn, g2[1:2], Lc, L // Lc, Lc)
        xl = xl_new
    return xl[None]
```

```python
import functools

import numpy as np
import jax
import jax.numpy as jnp
from jax import lax
from jax.experimental import pallas as pl
from jax.experimental.pallas import tpu as pltpu

F32 = jnp.float32
BF16 = jnp.bfloat16
U32 = jnp.uint32

HEAD_DIM = 64
GRID_W = 64
EPS = 1e-6
ROPE_THETA = 10000.0
D_LRU = 256
LRU_BLOCKS = 4
LRU_C = 8.0
WIN_Q_HEADS = 6
WIN_KV_HEADS = 2
WIN_BLOCK = 128
NA_HEADS = 6
NA_KH = 8
NA_KW = 16
NA_ROW_BLOCK = 4
NA_KEY_ROWS = NA_ROW_BLOCK + NA_KH - 1
PEER_HEADS = 8
PEER_NKEYS = 128
PEER_TOPK = 16
D_WIN_Q = WIN_Q_HEADS * HEAD_DIM
D_WIN_KV = WIN_KV_HEADS * HEAD_DIM
D_NA = NA_HEADS * HEAD_DIM

LANES = 128
TOK_TILE = 256
PEER_TOK_TILE = 512
PEER_EXPERT_CHUNK = 2048
PEER_SUB_CHUNK = 512
PEER_MXU_TOKENS = 256
PREP_TOK_TILE = 256
VMEM_LIMIT = 56 * 1024 * 1024
NEG = -1e30
GELU_K0 = float(np.sqrt(2.0 / np.pi))
GELU_K1 = 0.044715 * GELU_K0


def _cparams(*sem):
    return pltpu.CompilerParams(dimension_semantics=sem, vmem_limit_bytes=VMEM_LIMIT)


def _dot(a, b):
    return jnp.dot(a, b, preferred_element_type=F32)


def _pack_rows(x):
    return pltpu.bitcast(x.astype(BF16), U32)


def _unpack_rows(w):
    return pltpu.bitcast(w, BF16)


def _dot_nt(a, b):
    return lax.dot_general(a, b, (((1,), (1,)), ((), ())), preferred_element_type=F32)


def _mod_kernel(c_ref, w_ref, b_ref, o_ref):
    c = c_ref[...]
    a = (c * jax.nn.sigmoid(c)).astype(BF16)
    o_ref[0] = _dot(a, w_ref[0].astype(BF16)) + b_ref[0]


def _modulation(cvec, w_mod, b_mod):
    depth, d, d6 = w_mod.shape
    return pl.pallas_call(
        _mod_kernel,
        out_shape=jax.ShapeDtypeStruct((depth, 8, d6), F32),
        grid=(depth, d6 // d),
        in_specs=[pl.BlockSpec((8, d), lambda l, j: (0, 0)),
                  pl.BlockSpec((1, d, d), lambda l, j: (l, 0, j)),
                  pl.BlockSpec((1, 1, d), lambda l, j: (l, 0, j))],
        out_specs=pl.BlockSpec((1, 8, d), lambda l, j: (l, 0, j)),
        compiler_params=_cparams("arbitrary", "arbitrary"),
        name="modulation",
    )(cvec, w_mod, b_mod.reshape(depth, 1, d6))


def _pair_mean_matrix():
    r = lax.broadcasted_iota(jnp.int32, (LANES, LANES), 0) < HEAD_DIM
    c = lax.broadcasted_iota(jnp.int32, (LANES, LANES), 1) < HEAD_DIM
    return jnp.where(r == c, 1.0 / HEAD_DIM, 0.0).astype(BF16)


def _head_norm(z, gain, bd):
    z2 = z * z
    hi = z2.astype(BF16)
    lo = (z2 - hi.astype(F32)).astype(BF16)
    ms = _dot(hi, bd) + _dot(lo, bd)
    return z * lax.rsqrt(ms + EPS) * gain


def _rope(z, cos, sin):
    lane = lax.broadcasted_iota(jnp.int32, z.shape, 1)
    first = (lane & 16) == 0
    partner = jnp.where(first, pltpu.roll(z, LANES - 16, 1), pltpu.roll(z, 16, 1))
    return z * cos + partner * sin


def _in_kernel(xl_ref, xc_ref, mod_ref, g_ref, w_ref, cos_ref, sin_ref, hg_ref,
               xa_ref, ga_ref, qw_ref, kw_ref, vw_ref, qn_ref, kn_ref, vn_ref, *, n_lat):
    d = xl_ref.shape[1]
    i = pl.program_id(0)
    row = (i == n_lat).astype(jnp.int32)
    is_ctx = jnp.full((TOK_TILE, 1), row) == 1
    x = jnp.where(is_ctx, xc_ref[...], xl_ref[...])
    ms = jnp.mean(x * x, axis=-1, keepdims=True)
    y = x * lax.rsqrt(ms + EPS) * g_ref[...]
    sh = mod_ref[pl.ds(row, 1), 0:d]
    sc = mod_ref[pl.ds(row, 1), d:2 * d]
    h = (y * (1.0 + sc) + sh).astype(BF16)
    z = _dot(h, w_ref[...])

    bd = _pair_mean_matrix()
    cos = cos_ref[...]
    sin = sin_ref[...]
    scale = HEAD_DIM ** -0.5
    o = 0
    xa_ref[...] = z[:, o:o + D_LRU]
    o += D_LRU
    ga_ref[...] = z[:, o:o + D_LRU]
    o += D_LRU
    for g in range(D_WIN_Q // LANES):
        zz = _rope(_head_norm(z[:, o:o + LANES], hg_ref[0:1], bd), cos, sin)
        qw_ref[:, g * LANES:(g + 1) * LANES] = (zz * scale).astype(BF16)
        o += LANES
    kw_ref[...] = _rope(_head_norm(z[:, o:o + LANES], hg_ref[1:2], bd), cos, sin).astype(BF16)
    o += LANES
    vw_ref[...] = z[:, o:o + LANES].astype(BF16)
    o += LANES
    for g in range(D_NA // LANES):
        zz = _head_norm(z[:, o:o + LANES], hg_ref[2:3], bd)
        qn_ref[:, g * LANES:(g + 1) * LANES] = (zz * scale).astype(BF16)
        o += LANES
    for g in range(D_NA // LANES):
        zz = _head_norm(z[:, o:o + LANES], hg_ref[3:4], bd)
        kn_ref[:, g * LANES:(g + 1) * LANES] = zz.astype(BF16)
        o += LANES
    vn_ref[...] = z[:, o:o + D_NA].astype(BF16)


def _in_proj(xl, xc, mod, g, w_bf, cos_t, sin_t, hg):
    L, d = xl.shape
    n_lat = L // TOK_TILE
    n = L + xc.shape[0]
    d_in = w_bf.shape[1]
    T = TOK_TILE
    tok = lambda w: pl.BlockSpec((T, w), lambda i: (i, 0))
    full = lambda a: pl.BlockSpec(a.shape, lambda i: (0,) * a.ndim)
    widths = (D_LRU, D_LRU, D_WIN_Q, D_WIN_KV, D_WIN_KV, D_NA, D_NA, D_NA)
    dtypes = (F32, F32, BF16, BF16, BF16, BF16, BF16, BF16)
    return pl.pallas_call(
        functools.partial(_in_kernel, n_lat=n_lat),
        out_shape=[jax.ShapeDtypeStruct((n, w), t) for w, t in zip(widths, dtypes)],
        grid=(n_lat + 1,),
        in_specs=[pl.BlockSpec((T, d), lambda i: (jnp.minimum(i, n_lat - 1), 0)),
                  pl.BlockSpec((T, d), lambda i: (0, 0)),
                  full(mod), full(g), full(w_bf), tok(LANES), tok(LANES), full(hg)],
        out_specs=[tok(w) for w in widths],
        compiler_params=_cparams("arbitrary"),
        name="in_proj",
    )(xl, xc, mod, g, w_bf, cos_t, sin_t, hg)


def _chunk_scan(a, b, reverse):
    T = a.shape[0]
    rows = lax.broadcasted_iota(jnp.int32, a.shape, 0)
    s = 1
    while s < T:
        if reverse:
            edge = rows >= T - s
            shift = T - s
        else:
            edge = rows < s
            shift = s
        a_s = jnp.where(edge, 1.0, pltpu.roll(a, shift, 0))
        b_s = jnp.where(edge, 0.0, pltpu.roll(b, shift, 0))
        b = a * b_s + b
        a = a * a_s
        s *= 2
    return a, b


def _lru_direction(xm, ph, nh, pv, nv, cw, cb, w, bias, spl, carry_ref, first, reverse):
    T, C = xm.shape
    rows = lax.broadcasted_iota(jnp.int32, (T, C), 0)
    p6 = ph[6:7] * pv
    p7 = ph[7:8] * pv
    n0 = nh[0:1] * nv
    x_m1 = jnp.where(rows == 0, p7, pltpu.roll(xm, 1, 0))
    x_m2 = jnp.where(rows == 0, p6, jnp.where(rows == 1, p7, pltpu.roll(xm, 2, 0)))
    x_p1 = jnp.where(rows == T - 1, n0, pltpu.roll(xm, T - 1, 0))
    u = cw[0:1] * x_m2 + cw[1:2] * x_m1 + cw[2:3] * xm + cw[3:4] * x_p1 + cb
    zz = _dot(u.astype(BF16), w) + bias
    r = jax.nn.sigmoid(zz[:, :C])
    ig = jax.nn.sigmoid(zz[:, C:])
    log_a = -LRU_C * r * spl
    a = jnp.exp(log_a)
    b = jnp.sqrt(-jnp.tanh(log_a) * (a * a + 1.0)) * ig * u
    a, b = _chunk_scan(a, b, reverse)

    @pl.when(first)
    def _():
        carry_ref[...] = jnp.zeros(carry_ref.shape, F32)

    h = a * carry_ref[0:1] + b
    edge = h[0:1] if reverse else h[T - 1:T]
    carry_ref[...] = jnp.broadcast_to(edge, carry_ref.shape)
    return h


def _lru_kernel(xf_ref, pf_ref, nf_ref, xb_ref, pb_ref, nb_ref, cw_ref, cb_ref, w_ref, b_ref,
                lam_ref, hf_ref, hb_ref, cf_ref, cbk_ref, *, n_lat):
    j = pl.program_id(0)
    C = D_LRU
    fblk = jnp.where(j == 0, n_lat, j - 1)
    bblk = jnp.where(j == 0, n_lat, n_lat - j)
    lam = lam_ref[...]
    spl = jnp.maximum(-lam, 0.0) + jnp.log1p(jnp.exp(-jnp.abs(lam)))
    cw = cw_ref[...]
    cb = cb_ref[...]

    def halo_valid(blk):
        pv = jnp.logical_and(blk != 0, blk != n_lat).astype(F32)
        nv = jnp.logical_and(blk != n_lat - 1, blk != n_lat).astype(F32)
        return pv, nv

    pv, nv = halo_valid(fblk)
    hf_ref[...] = _lru_direction(xf_ref[...], pf_ref[...], nf_ref[...], pv, nv, cw, cb,
                                 w_ref[:, 0:2 * C], b_ref[:, 0:2 * C], spl[0:1], cf_ref, j == 0, False)
    pv, nv = halo_valid(bblk)
    hb_ref[...] = _lru_direction(xb_ref[...], pb_ref[...], nb_ref[...], pv, nv, cw, cb,
                                 w_ref[:, 2 * C:4 * C], b_ref[:, 2 * C:4 * C], spl[1:2], cbk_ref, j == 0, True)


def _lru(xa, conv_w, conv_b, w_gates, b_gates, lam, n_lat):
    n, C = xa.shape
    T = TOK_TILE
    sub = T // 8
    nblk8 = n // 8
    fblk = lambda j: jnp.where(j == 0, n_lat, j - 1)
    bblk = lambda j: jnp.where(j == 0, n_lat, n_lat - j)
    prev8 = lambda blk: jnp.maximum(blk * sub - 1, 0)
    next8 = lambda blk: jnp.minimum((blk + 1) * sub, nblk8 - 1)
    full = lambda a: pl.BlockSpec(a.shape, lambda j: (0,) * a.ndim)
    return pl.pallas_call(
        functools.partial(_lru_kernel, n_lat=n_lat),
        out_shape=[jax.ShapeDtypeStruct((n, C), F32)] * 2,
        grid=(n_lat + 1,),
        in_specs=[pl.BlockSpec((T, C), lambda j: (fblk(j), 0)),
                  pl.BlockSpec((8, C), lambda j: (prev8(fblk(j)), 0)),
                  pl.BlockSpec((8, C), lambda j: (next8(fblk(j)), 0)),
                  pl.BlockSpec((T, C), lambda j: (bblk(j), 0)),
                  pl.BlockSpec((8, C), lambda j: (prev8(bblk(j)), 0)),
                  pl.BlockSpec((8, C), lambda j: (next8(bblk(j)), 0)),
                  full(conv_w), full(conv_b), full(w_gates), full(b_gates), full(lam)],
        out_specs=[pl.BlockSpec((T, C), lambda j: (fblk(j), 0)),
                   pl.BlockSpec((T, C), lambda j: (bblk(j), 0))],
        scratch_shapes=[pltpu.VMEM((8, C), F32), pltpu.VMEM((8, C), F32)],
        compiler_params=_cparams("arbitrary"),
        name="rglru",
    )(xa, xa, xa, xa, xa, xa, conv_w, conv_b, w_gates, b_gates, lam)


def _win_kernel(sink_ref, q_ref, kp_ref, kc_ref, kn_ref, vp_ref, vc_ref, vn_ref, kx_ref, vx_ref,
                o_ref, *, nb):
    W = WIN_BLOCK
    R = WIN_Q_HEADS // WIN_KV_HEADS
    b = pl.program_id(0)
    blk = jnp.where(b < nb, b, -4)
    rows = lax.broadcasted_iota(jnp.int32, (R * W, 3 * W), 0)
    cols = lax.broadcasted_iota(jnp.int32, (R * W, 3 * W), 1)
    kpos = (blk - 1) * W + cols
    qpos = blk * W + (rows & (W - 1))
    valid = jnp.logical_and(jnp.abs(kpos - qpos) <= W, jnp.logical_and(kpos >= 0, kpos < nb * W))
    row1 = lax.broadcasted_iota(jnp.int32, (R * W, 1), 0)
    q = q_ref[...]
    for g in range(WIN_KV_HEADS):
        ls = slice(g * HEAD_DIM, (g + 1) * HEAD_DIM)
        qs = jnp.concatenate([q[:, (g * R + r) * HEAD_DIM:(g * R + r + 1) * HEAD_DIM] for r in range(R)], axis=0)
        kl = jnp.concatenate([kp_ref[:, ls], kc_ref[:, ls], kn_ref[:, ls]], axis=0)
        vl = jnp.concatenate([vp_ref[:, ls], vc_ref[:, ls], vn_ref[:, ls]], axis=0)
        s_loc = jnp.where(valid, _dot_nt(qs, kl), NEG)
        s_ctx = _dot_nt(qs, kx_ref[:, ls])
        sink = jnp.full((R * W, 1), sink_ref[g * R], F32)
        for r in range(1, R):
            sink = jnp.where(row1 >= r * W, sink_ref[g * R + r], sink)
        m = jnp.maximum(jnp.maximum(jnp.max(s_loc, axis=-1, keepdims=True),
                                    jnp.max(s_ctx, axis=-1, keepdims=True)), sink)
        p_loc = jnp.exp(s_loc - m)
        p_ctx = jnp.exp(s_ctx - m)
        den = (jnp.sum(p_loc, axis=-1, keepdims=True) + jnp.sum(p_ctx, axis=-1, keepdims=True)
               + jnp.exp(sink - m))
        o = (_dot(p_loc.astype(BF16), vl) + _dot(p_ctx.astype(BF16), vx_ref[:, ls])) / den
        for r in range(R):
            hh = g * R + r
            o_ref[:, hh * HEAD_DIM:(hh + 1) * HEAD_DIM] = o[r * W:(r + 1) * W].astype(o_ref.dtype)


def _window_attention(sink, qw, kw, vw, L, with_ctx):
    n = qw.shape[0]
    W = WIN_BLOCK
    nb = L // W
    nq = n // W if with_ctx else nb
    ctx_blk = L // TOK_TILE
    lat = lambda b: jnp.minimum(b, nb - 1)
    kv = lambda f: pl.BlockSpec((W, D_WIN_KV), lambda b: (f(b), 0))
    prv = lambda b: jnp.maximum(lat(b) - 1, 0)
    nxt = lambda b: jnp.minimum(lat(b) + 1, nb - 1)
    ctx = pl.BlockSpec((n - L, D_WIN_KV), lambda b: (ctx_blk, 0))
    return pl.pallas_call(
        functools.partial(_win_kernel, nb=nb),
        out_shape=jax.ShapeDtypeStruct((nq * W, D_WIN_Q), BF16),
        grid=(nq,),
        in_specs=[pl.BlockSpec(memory_space=pltpu.SMEM),
                  pl.BlockSpec((W, D_WIN_Q), lambda b: (b, 0)),
                  kv(prv), kv(lat), kv(nxt), kv(prv), kv(lat), kv(nxt), ctx, ctx],
        out_specs=pl.BlockSpec((W, D_WIN_Q), lambda b: (b, 0)),
        compiler_params=_cparams("arbitrary"),
        name="window_attn",
    )(sink, qw, kw, kw, kw, vw, vw, vw, kw, vw)


def _na_kernel(q_ref, kl_ref, vl_ref, kx_ref, vx_ref, bias_ref, o_ref, *, n_blocks):
    b = pl.program_id(0)
    nq = q_ref.shape[0]
    latent = jnp.full((nq, 1), (b < n_blocks).astype(jnp.int32)) == 1
    q = q_ref[...]
    for h in range(NA_HEADS):
        ls = slice(h * HEAD_DIM, (h + 1) * HEAD_DIM)
        qh = q[:, ls]
        s_loc = jnp.where(latent, _dot_nt(qh, kl_ref[:, ls]) + bias_ref[0, h], NEG)
        s_ctx = _dot_nt(qh, kx_ref[:, ls])
        m = jnp.maximum(jnp.max(s_loc, axis=-1, keepdims=True), jnp.max(s_ctx, axis=-1, keepdims=True))
        p_loc = jnp.exp(s_loc - m)
        p_ctx = jnp.exp(s_ctx - m)
        den = jnp.sum(p_loc, axis=-1, keepdims=True) + jnp.sum(p_ctx, axis=-1, keepdims=True)
        o = (_dot(p_loc.astype(BF16), vl_ref[:, ls]) + _dot(p_ctx.astype(BF16), vx_ref[:, ls])) / den
        o_ref[:, ls] = o.astype(o_ref.dtype)


def _na_bias_table(rpb, rows):
    R, KR = NA_ROW_BLOCK, NA_KEY_ROWS
    qc = np.arange(GRID_W)[:, None]
    kc = np.arange(GRID_W)[None, :]
    qstart = np.clip(qc - NA_KW // 2, 0, GRID_W - NA_KW)
    inside = (kc - qstart >= 0) & (kc - qstart < NA_KW)
    pad = GRID_W - NA_KW
    rp = jnp.pad(rpb.astype(F32), ((0, 0), (0, 0), (pad, pad)))
    shifted = jnp.stack([rp[:, :, NA_KW - 1 - q + pad:NA_KW - 1 - q + pad + GRID_W] for q in range(GRID_W)], axis=2)
    per_dr = jnp.where(inside[None, None], shifted, NEG)
    masked = jnp.full((NA_HEADS, GRID_W, GRID_W), NEG, F32)
    half = NA_KH // 2
    cases = []
    for r0, ks in ((0, 0), (half, 0), (rows - R, rows - KR)):
        row_blocks = []
        for rr in range(R):
            r = r0 + rr
            kr = min(max(r - half, 0), rows - NA_KH)
            blocks = [per_dr[:, ks + kk - r + NA_KH - 1] if kr <= ks + kk < kr + NA_KH else masked
                      for kk in range(KR)]
            row_blocks.append(jnp.concatenate(blocks, axis=-1))
        cases.append(jnp.concatenate(row_blocks, axis=1))
    return jnp.stack(cases)


def _neighborhood_attention(qn, kn, vn, bias, L, with_ctx):
    n = qn.shape[0]
    R, KR = NA_ROW_BLOCK, NA_KEY_ROWS
    rows = L // GRID_W
    nb = rows // R
    nq = R * GRID_W
    steps = n // nq if with_ctx else nb
    ctx_blk = L // TOK_TILE
    half = NA_KH // 2

    def blk_of(b):
        return jnp.minimum(b, nb - 1)

    def key_start(b):
        return jnp.clip(blk_of(b) * R - half, 0, rows - KR) * GRID_W

    def case_of(b):
        return jnp.where(blk_of(b) == 0, 0, jnp.where(blk_of(b) == nb - 1, 2, 1))

    loc = pl.BlockSpec((pl.Element(KR * GRID_W), pl.Element(D_NA)), lambda b: (key_start(b), 0))
    ctx = pl.BlockSpec((n - L, D_NA), lambda b: (ctx_blk, 0))
    return pl.pallas_call(
        functools.partial(_na_kernel, n_blocks=nb),
        out_shape=jax.ShapeDtypeStruct((steps * nq, D_NA), BF16),
        grid=(steps,),
        in_specs=[pl.BlockSpec((nq, D_NA), lambda b: (b, 0)), loc, loc, ctx, ctx,
                  pl.BlockSpec((1, NA_HEADS, nq, KR * GRID_W), lambda b: (case_of(b), 0, 0, 0))],
        out_specs=pl.BlockSpec((nq, D_NA), lambda b: (b, 0)),
        compiler_params=_cparams("arbitrary"),
        name="neighborhood_attn",
    )(qn, kn, vn, kn, vn, bias)


def _out_kernel(xl_ref, xc_ref, hf_ref, hb_ref, ga_ref, yb_ref, yc_ref, w_ref, mod_ref, g_ref,
                xo_ref, ht_ref, *, n_lat):
    d = xl_ref.shape[1]
    i = pl.program_id(0)
    row = (i == n_lat).astype(jnp.int32)
    is_ctx = jnp.full((TOK_TILE, 1), row) == 1
    x = jnp.where(is_ctx, xc_ref[...], xl_ref[...])
    ya = ((hf_ref[...] + hb_ref[...]) * jax.nn.gelu(ga_ref[...])).astype(BF16)
    o1 = D_LRU
    o2 = D_LRU + D_WIN_Q
    mix = (_dot(ya, w_ref[0:o1]) + _dot(yb_ref[...], w_ref[o1:o2]) + _dot(yc_ref[...], w_ref[o2:o2 + D_NA]))
    xn = x + mod_ref[pl.ds(row, 1), 2 * d:3 * d] * mix
    xo_ref[...] = xn
    ms = jnp.mean(xn * xn, axis=-1, keepdims=True)
    y = xn * lax.rsqrt(ms + EPS) * g_ref[...]
    h2 = y * (1.0 + mod_ref[pl.ds(row, 1), 4 * d:5 * d]) + mod_ref[pl.ds(row, 1), 3 * d:4 * d]
    ht_ref[...] = h2.T.astype(BF16)


def _out_proj(xl, xc, hf, hb, ga, yb, yc, w_bf, mod, g, with_ctx):
    L, d = xl.shape
    T = TOK_TILE
    n_lat = L // T
    nt = n_lat + 1 if with_ctx else n_lat
    tok = lambda w: pl.BlockSpec((T, w), lambda i: (i, 0))
    full = lambda a: pl.BlockSpec(a.shape, lambda i: (0,) * a.ndim)
    return pl.pallas_call(
        functools.partial(_out_kernel, n_lat=n_lat),
        out_shape=[jax.ShapeDtypeStruct((nt * T, d), F32), jax.ShapeDtypeStruct((d, nt * T), BF16)],
        grid=(nt,),
        in_specs=[pl.BlockSpec((T, d), lambda i: (jnp.minimum(i, n_lat - 1), 0)),
                  pl.BlockSpec((T, d), lambda i: (0, 0)),
                  tok(D_LRU), tok(D_LRU), tok(D_LRU), tok(D_WIN_Q), tok(D_NA),
                  full(w_bf), full(mod), full(g)],
        out_specs=[tok(d), pl.BlockSpec((d, T), lambda i: (0, i))],
        compiler_params=_cparams("arbitrary"),
        name="out_proj",
    )(xl, xc, hf, hb, ga, yb, yc, w_bf, mod, g)


SUBLANES = 8


def _sorting_pairs(n):
    pairs, p = [], 1
    while p < n:
        k = p
        while k >= 1:
            for j in range(k % p, n - k, 2 * k):
                for i in range(min(k, n - j - k)):
                    if (i + j) // (2 * p) == (i + j + k) // (2 * p):
                        pairs.append((i + j, i + j + k))
            k //= 2
        p *= 2
    return pairs


def _vmax(a, b):
    if a is None:
        return b
    if b is None:
        return a
    return jnp.maximum(a, b)


def _vmin(a, b):
    if a is None or b is None:
        return None
    return jnp.minimum(a, b)


def _top16_sorted(slabs):
    K = PEER_TOPK
    w = list(slabs)
    for a, b in _sorting_pairs(K):
        w[a], w[b] = _vmax(w[a], w[b]), _vmin(w[a], w[b])
    shift = SUBLANES // 2
    while shift >= 1:
        partner = [None if w[K - 1 - i] is None else pltpu.roll(w[K - 1 - i], shift, 0) for i in range(K)]
        w = [_vmax(w[i], partner[i]) for i in range(K)]
        stride = K // 2
        while stride >= 1:
            for i in range(K):
                if i & stride == 0:
                    w[i], w[i + stride] = _vmax(w[i], w[i + stride]), _vmin(w[i], w[i + stride])
            stride //= 2
        shift //= 2
    return w


def _allsum8(x):
    x = x + pltpu.roll(x, 4, 0)
    x = x + pltpu.roll(x, 2, 0)
    return x + pltpu.roll(x, 1, 0)


def _route_tile(s1, s2):
    K = PEER_TOPK
    S = SUBLANES
    n_slab = s1.shape[0] // S
    a1 = [s1[j * S:(j + 1) * S] for j in range(n_slab)]
    a2 = [s2[j * S:(j + 1) * S] for j in range(n_slab)]
    v1 = _top16_sorted(a1)
    v2 = _top16_sorted(a2)
    sub = lax.broadcasted_iota(jnp.int32, a1[0].shape, 0)

    def as_rows(v, lo):
        out = v[lo]
        for b in range(1, S):
            out = jnp.where(sub == b, v[lo + b], out)
        return out

    v2_lo, v2_hi, v1_hi = as_rows(v2, 0), as_rows(v2, S), as_rows(v1, S)
    lens = [K // (a + 1) for a in range(S)]
    cands = [v1[0] + v2_lo, v1[0] + v2_hi]
    for a in range(1, S):
        ca = v1[a] + v2_lo
        cands.append(ca if lens[a] >= S else jnp.where(sub < lens[a], ca, -jnp.inf))
    cands.append(v1_hi + v2[0])
    thr = _top16_sorted(cands + [None] * (K - len(cands)))[K - 1]
    x2_lo, x2_hi = jnp.exp(v2_lo - v2[0]), jnp.exp(v2_hi - v2[0])
    sel_lo, sel_hi = cands[0] >= thr, cands[1] >= thr
    zsum = jnp.where(sel_lo, x2_lo, 0.0) + jnp.where(sel_hi, x2_hi, 0.0)
    cnt = [_allsum8(jnp.where(sel_lo, 1.0, 0.0) + jnp.where(sel_hi, 1.0, 0.0))]
    for a in range(1, S):
        sel = cands[a + 1] >= thr
        zsum = zsum + jnp.where(sel, jnp.exp(v1[a] - v1[0]) * x2_lo, 0.0)
        cnt.append(_allsum8(jnp.where(sel, 1.0, 0.0)))
    zsum = zsum + jnp.where(cands[S + 1] >= thr, jnp.exp(v1_hi - v1[0]), 0.0)
    inv_z = 1.0 / _allsum8(zsum)
    r2, e2, c1, e1 = [], [], [], []
    for j in range(n_slab):
        r = jnp.full(a2[j].shape, float(K), F32)
        for k in reversed(range(K)):
            r = jnp.where(a2[j] >= v2[k], float(k), r)
        r2.append(r)
        e2.append(jnp.exp(a2[j] - v2[0]) * inv_z)
        c = jnp.where(a1[j] + v2[0] >= thr, 1.0, 0.0)
        for a in reversed(range(S)):
            c = jnp.where(a1[j] >= v1[a], cnt[a], c)
        c1.append(c)
        e1.append(jnp.exp(a1[j] - v1[0]))
    cat = lambda xs: jnp.concatenate(xs, axis=0)
    return cat(r2), cat(e2), cat(c1), cat(e1)


def _prep_kernel(ht_ref, wq_ref, k1_ref, k2_ref, r2_ref, e2_ref, c1_ref, e1_ref, q_scr):
    nk = PEER_NKEYS
    q_scr[...] = _dot(wq_ref[...], ht_ref[...])

    def head(h, carry):
        base = pl.multiple_of(h * 2 * nk, 2 * nk)
        s1 = _dot(k1_ref[...], q_scr[pl.ds(base, nk), :].astype(BF16))
        s2 = _dot(k2_ref[...], q_scr[pl.ds(base + nk, nk), :].astype(BF16))
        for t in range(s1.shape[1] // LANES):
            ts = slice(t * LANES, (t + 1) * LANES)
            r2, e2, c1, e1 = _route_tile(s1[:, ts], s2[:, ts])
            r2_ref[h, :, ts] = _pack_rows(r2)
            e2_ref[h, :, ts] = _pack_rows(e2)
            c1_ref[h, :, ts] = c1
            e1_ref[h, :, ts] = e1
        return carry

    lax.fori_loop(0, PEER_HEADS, head, 0)


def _peer_prep(ht, wq_t, k1, k2):
    d, n = ht.shape
    T = PREP_TOK_TILE
    nk = PEER_NKEYS
    full = lambda a: pl.BlockSpec(a.shape, lambda i: (0,) * a.ndim)
    spec = lambda rows: pl.BlockSpec((PEER_HEADS, rows, T), lambda i: (0, 0, i))
    return pl.pallas_call(
        _prep_kernel,
        out_shape=[jax.ShapeDtypeStruct((PEER_HEADS, nk // 2, n), U32)] * 2
        + [jax.ShapeDtypeStruct((PEER_HEADS, nk, n), F32)] * 2,
        grid=(n // T,),
        in_specs=[pl.BlockSpec((d, T), lambda i: (0, i)), full(wq_t), full(k1), full(k2)],
        out_specs=[spec(nk // 2), spec(nk // 2), spec(nk), spec(nk)],
        scratch_shapes=[pltpu.VMEM((wq_t.shape[0], T), F32)],
        compiler_params=_cparams("arbitrary"),
        name="peer_prep",
    )(ht, wq_t, k1, k2)


def _peer_kernel(ht_ref, u_ref, un_ref, vt_ref, vp_ref, r2_ref, e2_ref, c1_ref, e1_ref, x_ref, g_ref,
                 o_ref, acc_ref, *ap_refs):
    nk = PEER_NKEYS
    c = pl.program_id(1)
    n_sub = len(ap_refs) // 2
    a_refs, p_refs = ap_refs[:n_sub], ap_refs[n_sub:]
    sub = 2 * a_refs[0].shape[0]
    T = a_refs[0].shape[1]
    rows_per_sub = sub // nk
    PK = 16
    WK = PK // 2

    tok_blocks = [slice(n, min(n + PEER_MXU_TOKENS, T)) for n in range(0, T, PEER_MXU_TOKENS)]

    def activations_from(rows_ref, dst_ref):
        for tb in tok_blocks:
            dst_ref[:, tb] = _pack_rows(_dot(rows_ref, ht_ref[:, tb]))

    def activations(s):
        activations_from(u_ref[s * sub:(s + 1) * sub, :], a_refs[s])

    def gates(s):
        for t in range(T // LANES):
            ts = slice(t * LANES, (t + 1) * LANES)
            for j in range(rows_per_sub):
                row = s * rows_per_sub + j
                gate = [None] * (nk // PK)
                for h in range(PEER_HEADS):
                    c1 = c1_ref[h, :, ts][row:row + 1]
                    e1 = e1_ref[h, :, ts][row:row + 1]
                    c1 = jnp.broadcast_to(c1, (PK, LANES)).astype(BF16)
                    e1 = jnp.broadcast_to(e1, (PK, LANES)).astype(BF16)
                    for v in range(nk // PK):
                        ws = slice(v * WK, (v + 1) * WK)
                        r2 = _unpack_rows(r2_ref[h, ws, ts])
                        e2 = _unpack_rows(e2_ref[h, ws, ts])
                        g = jnp.where(r2 < c1, e2 * e1, 0.0)
                        gate[v] = g if h == 0 else gate[v] + g
                for v in range(nk // PK):
                    ws = slice((j * nk + v * PK) // 2, (j * nk + (v + 1) * PK) // 2)
                    a = _unpack_rows(a_refs[s][ws, ts])
                    th = jnp.tanh(a * (GELU_K0 + GELU_K1 * (a * a)))
                    p_refs[s][ws, ts] = pltpu.bitcast(gate[v] * (a * (0.5 + 0.5 * th)), U32)

    def project(terms):
        for tb in tok_blocks:
            parts = [_dot(cols, _unpack_rows(src[:, tb])) for cols, src in terms]
            acc_ref[:, tb] += functools.reduce(lambda a, b: a + b, parts)

    def term(s):
        return vt_ref[:, s * sub:(s + 1) * sub], p_refs[s]

    last = n_sub - 1
    assert n_sub % 2 == 0

    @pl.when(c == 0)
    def _():
        acc_ref[...] = jnp.zeros(acc_ref.shape, F32)
        p_refs[last][...] = jnp.zeros(p_refs[last].shape, U32)
        activations(0)

    pending = [(vp_ref[...], p_refs[last])]
    for s in range(n_sub):
        gates(s)
        if s < last:
            pending.append(term(s))
            activations(s + 1)
        if len(pending) == 2:
            project(pending)
            pending = []
    activations_from(un_ref[...], a_refs[0])

    @pl.when(c == pl.num_programs(1) - 1)
    def _():
        project([term(last)])
        o_ref[...] = x_ref[...] + g_ref[...] * acc_ref[...].T


def _peer_dense(ht, u_bf, vt_bf, r2, e2, c1, e1, x, g2, tok_tile, tok_off, n_tok):
    d = ht.shape[0]
    n_exp = u_bf.shape[0]
    T = tok_tile
    NC = PEER_EXPERT_CHUNK
    nk = PEER_NKEYS
    SUB = PEER_SUB_CHUNK
    n_sub = NC // SUB
    rt = lambda rows: pl.BlockSpec((PEER_HEADS, rows, T), lambda i, c: (0, 0, i + tok_off))
    per_chunk = pl.BlockSpec((PEER_HEADS, NC // nk, T), lambda i, c: (0, c, i + tok_off))
    return pl.pallas_call(
        _peer_kernel,
        out_shape=jax.ShapeDtypeStruct((n_tok, d), F32),
        grid=(n_tok // T, n_exp // NC),
        in_specs=[pl.BlockSpec((d, T), lambda i, c: (0, i + tok_off)),
                  pl.BlockSpec((NC, d), lambda i, c: (c, 0)),
                  pl.BlockSpec((SUB, d), lambda i, c: (jnp.minimum((c + 1) * n_sub, n_exp // SUB - 1), 0)),
                  pl.BlockSpec((d, NC), lambda i, c: (0, c)),
                  pl.BlockSpec((d, SUB), lambda i, c: (0, jnp.maximum(c * n_sub - 1, 0))),
                  rt(nk // 2), rt(nk // 2), per_chunk, per_chunk,
                  pl.BlockSpec((T, d), lambda i, c: (i + tok_off, 0)),
                  pl.BlockSpec((1, d), lambda i, c: (0, 0))],
        out_specs=pl.BlockSpec((T, d), lambda i, c: (i, 0)),
        scratch_shapes=[pltpu.VMEM((d, T), F32)] + [pltpu.VMEM((PEER_SUB_CHUNK // 2, T), U32)] * (2 * n_sub),
        compiler_params=_cparams("arbitrary", "arbitrary"),
        name="peer_dense",
    )(ht, u_bf, u_bf, vt_bf, vt_bf, r2, e2, c1, e1, x, g2)


def _rope_tables(L, n):
    t = jnp.arange(L)
    row = (t // GRID_W).astype(F32)
    col = (t % GRID_W).astype(F32)
    q = HEAD_DIM // 4
    inv = ROPE_THETA ** (-jnp.arange(q, dtype=F32) / q)
    ar = row[:, None] * inv
    ac = col[:, None] * inv
    cos = jnp.concatenate([jnp.cos(ar), jnp.cos(ar), jnp.cos(ac), jnp.cos(ac)], axis=-1)
    sin = jnp.concatenate([-jnp.sin(ar), jnp.sin(ar), -jnp.sin(ac), jnp.sin(ac)], axis=-1)
    cos = jnp.concatenate([cos, jnp.ones((n - L, HEAD_DIM), F32)], axis=0)
    sin = jnp.concatenate([sin, jnp.zeros((n - L, HEAD_DIM), F32)], axis=0)
    return jnp.tile(cos, (1, LANES // HEAD_DIM)), jnp.tile(sin, (1, LANES // HEAD_DIM))


def _block_diag(w):
    nb, di, do = w.shape
    eye = jnp.eye(nb, dtype=w.dtype)
    return (eye[:, None, :, None] * w[:, :, None, :]).reshape(nb * di, nb * do)


def kernel(x, c, ctx, c_ctx, w_mod, b_mod, norm1_g, norm2_g, w_in, w_out, lru_conv_w, lru_conv_b, lru_wa, lru_ba, lru_wx, lru_bx, lru_lam, win_qn_g, win_kn_g, win_sink, na_qn_g, na_kn_g, na_rpb, peer_wq, peer_k1, peer_k2, peer_u, peer_v):
    B, L, D = x.shape
    Lc = ctx.shape[1]
    depth = w_mod.shape[0]
    assert B == 1 and Lc == TOK_TILE and L % PEER_TOK_TILE == 0 and L // GRID_W >= 3 * NA_ROW_BLOCK
    n = L + Lc
    n_lat = L // TOK_TILE
    rows = L // GRID_W

    cvec = jnp.zeros((8, D), F32).at[0].set(c[0]).at[1].set(c_ctx)
    mods = _modulation(cvec, w_mod, b_mod)
    cos_t, sin_t = _rope_tables(L, n)
    tile2 = lambda g: jnp.tile(g, LANES // HEAD_DIM)

    xl, xc = x[0], ctx[0]
    for l in range(depth):
        with_ctx = l < depth - 1
        mod = mods[l]
        hg = jnp.zeros((8, LANES), F32)
        hg = hg.at[0].set(tile2(win_qn_g[l])).at[1].set(tile2(win_kn_g[l]))
        hg = hg.at[2].set(tile2(na_qn_g[l])).at[3].set(tile2(na_kn_g[l]))
        xa, ga, qw, kw, vw, qn, kn, vn = _in_proj(xl, xc, mod, norm1_g[l][None], w_in[l].astype(BF16),
                                                  cos_t, sin_t, hg)

        w_gates = jnp.concatenate([_block_diag(lru_wa[l, 0]), _block_diag(lru_wx[l, 0]),
                                   _block_diag(lru_wa[l, 1]), _block_diag(lru_wx[l, 1])], axis=1).astype(BF16)
        b_gates = jnp.concatenate([lru_ba[l, 0], lru_bx[l, 0], lru_ba[l, 1], lru_bx[l, 1]])[None]
        conv_w = jnp.zeros((8, D_LRU), F32).at[0:lru_conv_w.shape[1]].set(lru_conv_w[l])
        lam = jnp.zeros((8, D_LRU), F32).at[0:2].set(lru_lam[l])
        hf, hb = _lru(xa, conv_w, lru_conv_b[l][None], w_gates, b_gates, lam, n_lat)

        yb = _window_attention(win_sink[l], qw, kw, vw, L, with_ctx)
        yc = _neighborhood_attention(qn, kn, vn, _na_bias_table(na_rpb[l], rows), L, with_ctx)

        xn, ht = _out_proj(xl, xc, hf, hb, ga, yb, yc, w_out[l].astype(BF16), mod, norm2_g[l][None], with_ctx)

        r2, e2, c1, e1 = _peer_prep(ht, peer_wq[l].T.astype(BF16), peer_k1[l].astype(BF16),
                                    peer_k2[l].astype(BF16))
        u_bf = peer_u[l].astype(BF16)
        vt_bf = peer_v[l].T.astype(BF16)
        g2 = mod[:, 5 * D:6 * D]
        xl_new = _peer_dense(ht, u_bf, vt_bf, r2, e2, c1, e1, xn, g2[0:1], PEER_TOK_TILE, 0, L)
        if with_ctx:
            xc = _peer_dense(ht, u_bf, vt_bf, r2, e2, c1, e1, xn, g2[1:2], Lc, L // Lc, Lc)
        xl = xl_new
    return xl[None]
```

```python
import functools

import numpy as np
import jax
import jax.numpy as jnp
from jax import lax
from jax.experimental import pallas as pl
from jax.experimental.pallas import tpu as pltpu

F32 = jnp.float32
BF16 = jnp.bfloat16
U32 = jnp.uint32

HEAD_DIM = 64
GRID_W = 64
EPS = 1e-6
ROPE_THETA = 10000.0
D_LRU = 256
LRU_BLOCKS = 4
LRU_C = 8.0
WIN_Q_HEADS = 6
WIN_KV_HEADS = 2
WIN_BLOCK = 128
NA_HEADS = 6
NA_KH = 8
NA_KW = 16
NA_ROW_BLOCK = 4
NA_KEY_ROWS = NA_ROW_BLOCK + NA_KH - 1
PEER_HEADS = 8
PEER_NKEYS = 128
PEER_TOPK = 16
D_WIN_Q = WIN_Q_HEADS * HEAD_DIM
D_WIN_KV = WIN_KV_HEADS * HEAD_DIM
D_NA = NA_HEADS * HEAD_DIM

LANES = 128
TOK_TILE = 256
PEER_TOK_TILE = 512
PEER_EXPERT_CHUNK = 2048
PEER_SUB_CHUNK = 512
PEER_MXU_TOKENS = 256
PREP_TOK_TILE = 256
VMEM_LIMIT = 56 * 1024 * 1024
NEG = -1e30
GELU_K0 = float(np.sqrt(2.0 / np.pi))
GELU_K1 = 0.044715 * GELU_K0


def _cparams(*sem):
    return pltpu.CompilerParams(dimension_semantics=sem, vmem_limit_bytes=VMEM_LIMIT)


def _dot(a, b):
    return jnp.dot(a, b, preferred_element_type=F32)


def _pack_rows(x):
    return pltpu.bitcast(x.astype(BF16), U32)


def _unpack_rows(w):
    return pltpu.bitcast(w, BF16)


def _dot_nt(a, b):
    return lax.dot_general(a, b, (((1,), (1,)), ((), ())), preferred_element_type=F32)


def _mod_kernel(c_ref, w_ref, b_ref, o_ref):
    c = c_ref[...]
    a = (c * jax.nn.sigmoid(c)).astype(BF16)
    o_ref[0] = _dot(a, w_ref[0].astype(BF16)) + b_ref[0]


def _modulation(cvec, w_mod, b_mod):
    depth, d, d6 = w_mod.shape
    return pl.pallas_call(
        _mod_kernel,
        out_shape=jax.ShapeDtypeStruct((depth, 8, d6), F32),
        grid=(depth, d6 // d),
        in_specs=[pl.BlockSpec((8, d), lambda l, j: (0, 0)),
                  pl.BlockSpec((1, d, d), lambda l, j: (l, 0, j)),
                  pl.BlockSpec((1, 1, d), lambda l, j: (l, 0, j))],
        out_specs=pl.BlockSpec((1, 8, d), lambda l, j: (l, 0, j)),
        compiler_params=_cparams("arbitrary", "arbitrary"),
        name="modulation",
    )(cvec, w_mod, b_mod.reshape(depth, 1, d6))


def _pair_mean_matrix():
    r = lax.broadcasted_iota(jnp.int32, (LANES, LANES), 0) < HEAD_DIM
    c = lax.broadcasted_iota(jnp.int32, (LANES, LANES), 1) < HEAD_DIM
    return jnp.where(r == c, 1.0 / HEAD_DIM, 0.0).astype(BF16)


def _head_norm(z, gain, bd):
    z2 = z * z
    hi = z2.astype(BF16)
    lo = (z2 - hi.astype(F32)).astype(BF16)
    ms = _dot(hi, bd) + _dot(lo, bd)
    return z * lax.rsqrt(ms + EPS) * gain


def _rope(z, cos, sin):
    lane = lax.broadcasted_iota(jnp.int32, z.shape, 1)
    first = (lane & 16) == 0
    partner = jnp.where(first, pltpu.roll(z, LANES - 16, 1), pltpu.roll(z, 16, 1))
    return z * cos + partner * sin


def _in_kernel(xl_ref, xc_ref, mod_ref, g_ref, w_ref, cos_ref, sin_ref, hg_ref,
               xa_ref, ga_ref, qw_ref, kw_ref, vw_ref, qn_ref, kn_ref, vn_ref, *, n_lat):
    d = xl_ref.shape[1]
    i = pl.program_id(0)
    row = (i == n_lat).astype(jnp.int32)
    is_ctx = jnp.full((TOK_TILE, 1), row) == 1
    x = jnp.where(is_ctx, xc_ref[...], xl_ref[...])
    ms = jnp.mean(x * x, axis=-1, keepdims=True)
    y = x * lax.rsqrt(ms + EPS) * g_ref[...]
    sh = mod_ref[pl.ds(row, 1), 0:d]
    sc = mod_ref[pl.ds(row, 1), d:2 * d]
    h = (y * (1.0 + sc) + sh).astype(BF16)
    z = _dot(h, w_ref[...])

    bd = _pair_mean_matrix()
    cos = cos_ref[...]
    sin = sin_ref[...]
    scale = HEAD_DIM ** -0.5
    o = 0
    xa_ref[...] = z[:, o:o + D_LRU]
    o += D_LRU
    ga_ref[...] = z[:, o:o + D_LRU]
    o += D_LRU
    for g in range(D_WIN_Q // LANES):
        zz = _rope(_head_norm(z[:, o:o + LANES], hg_ref[0:1], bd), cos, sin)
        qw_ref[:, g * LANES:(g + 1) * LANES] = (zz * scale).astype(BF16)
        o += LANES
    kw_ref[...] = _rope(_head_norm(z[:, o:o + LANES], hg_ref[1:2], bd), cos, sin).astype(BF16)
    o += LANES
    vw_ref[...] = z[:, o:o + LANES].astype(BF16)
    o += LANES
    for g in range(D_NA // LANES):
        zz = _head_norm(z[:, o:o + LANES], hg_ref[2:3], bd)
        qn_ref[:, g * LANES:(g + 1) * LANES] = (zz * scale).astype(BF16)
        o += LANES
    for g in range(D_NA // LANES):
        zz = _head_norm(z[:, o:o + LANES], hg_ref[3:4], bd)
        kn_ref[:, g * LANES:(g + 1) * LANES] = zz.astype(BF16)
        o += LANES
    vn_ref[...] = z[:, o:o + D_NA].astype(BF16)


def _in_proj(xl, xc, mod, g, w_bf, cos_t, sin_t, hg):
    L, d = xl.shape
    n_lat = L // TOK_TILE
    n = L + xc.shape[0]
    d_in = w_bf.shape[1]
    T = TOK_TILE
    tok = lambda w: pl.BlockSpec((T, w), lambda i: (i, 0))
    full = lambda a: pl.BlockSpec(a.shape, lambda i: (0,) * a.ndim)
    widths = (D_LRU, D_LRU, D_WIN_Q, D_WIN_KV, D_WIN_KV, D_NA, D_NA, D_NA)
    dtypes = (F32, F32, BF16, BF16, BF16, BF16, BF16, BF16)
    return pl.pallas_call(
        functools.partial(_in_kernel, n_lat=n_lat),
        out_shape=[jax.ShapeDtypeStruct((n, w), t) for w, t in zip(widths, dtypes)],
        grid=(n_lat + 1,),
        in_specs=[pl.BlockSpec((T, d), lambda i: (jnp.minimum(i, n_lat - 1), 0)),
                  pl.BlockSpec((T, d), lambda i: (0, 0)),
                  full(mod), full(g), full(w_bf), tok(LANES), tok(LANES), full(hg)],
        out_specs=[tok(w) for w in widths],
        compiler_params=_cparams("arbitrary"),
        name="in_proj",
    )(xl, xc, mod, g, w_bf, cos_t, sin_t, hg)


def _chunk_scan(a, b, reverse):
    T = a.shape[0]
    rows = lax.broadcasted_iota(jnp.int32, a.shape, 0)
    s = 1
    while s < T:
        if reverse:
            edge = rows >= T - s
            shift = T - s
        else:
            edge = rows < s
            shift = s
        a_s = jnp.where(edge, 1.0, pltpu.roll(a, shift, 0))
        b_s = jnp.where(edge, 0.0, pltpu.roll(b, shift, 0))
        b = a * b_s + b
        a = a * a_s
        s *= 2
    return a, b


def _lru_direction(xm, ph, nh, pv, nv, cw, cb, w, bias, spl, carry_ref, first, reverse):
    T, C = xm.shape
    rows = lax.broadcasted_iota(jnp.int32, (T, C), 0)
    p6 = ph[6:7] * pv
    p7 = ph[7:8] * pv
    n0 = nh[0:1] * nv
    x_m1 = jnp.where(rows == 0, p7, pltpu.roll(xm, 1, 0))
    x_m2 = jnp.where(rows == 0, p6, jnp.where(rows == 1, p7, pltpu.roll(xm, 2, 0)))
    x_p1 = jnp.where(rows == T - 1, n0, pltpu.roll(xm, T - 1, 0))
    u = cw[0:1] * x_m2 + cw[1:2] * x_m1 + cw[2:3] * xm + cw[3:4] * x_p1 + cb
    zz = _dot(u.astype(BF16), w) + bias
    r = jax.nn.sigmoid(zz[:, :C])
    ig = jax.nn.sigmoid(zz[:, C:])
    log_a = -LRU_C * r * spl
    a = jnp.exp(log_a)
    b = jnp.sqrt(-jnp.tanh(log_a) * (a * a + 1.0)) * ig * u
    a, b = _chunk_scan(a, b, reverse)

    @pl.when(first)
    def _():
        carry_ref[...] = jnp.zeros(carry_ref.shape, F32)

    h = a * carry_ref[0:1] + b
    edge = h[0:1] if reverse else h[T - 1:T]
    carry_ref[...] = jnp.broadcast_to(edge, carry_ref.shape)
    return h


def _lru_kernel(xf_ref, pf_ref, nf_ref, xb_ref, pb_ref, nb_ref, cw_ref, cb_ref, w_ref, b_ref,
                lam_ref, hf_ref, hb_ref, cf_ref, cbk_ref, *, n_lat):
    j = pl.program_id(0)
    C = D_LRU
    fblk = jnp.where(j == 0, n_lat, j - 1)
    bblk = jnp.where(j == 0, n_lat, n_lat - j)
    lam = lam_ref[...]
    spl = jnp.maximum(-lam, 0.0) + jnp.log1p(jnp.exp(-jnp.abs(lam)))
    cw = cw_ref[...]
    cb = cb_ref[...]

    def halo_valid(blk):
        pv = jnp.logical_and(blk != 0, blk != n_lat).astype(F32)
        nv = jnp.logical_and(blk != n_lat - 1, blk != n_lat).astype(F32)
        return pv, nv

    pv, nv = halo_valid(fblk)
    hf_ref[...] = _lru_direction(xf_ref[...], pf_ref[...], nf_ref[...], pv, nv, cw, cb,
                                 w_ref[:, 0:2 * C], b_ref[:, 0:2 * C], spl[0:1], cf_ref, j == 0, False)
    pv, nv = halo_valid(bblk)
    hb_ref[...] = _lru_direction(xb_ref[...], pb_ref[...], nb_ref[...], pv, nv, cw, cb,
                                 w_ref[:, 2 * C:4 * C], b_ref[:, 2 * C:4 * C], spl[1:2], cbk_ref, j == 0, True)


def _lru(xa, conv_w, conv_b, w_gates, b_gates, lam, n_lat):
    n, C = xa.shape
    T = TOK_TILE
    sub = T // 8
    nblk8 = n // 8
    fblk = lambda j: jnp.where(j == 0, n_lat, j - 1)
    bblk = lambda j: jnp.where(j == 0, n_lat, n_lat - j)
    prev8 = lambda blk: jnp.maximum(blk * sub - 1, 0)
    next8 = lambda blk: jnp.minimum((blk + 1) * sub, nblk8 - 1)
    full = lambda a: pl.BlockSpec(a.shape, lambda j: (0,) * a.ndim)
    return pl.pallas_call(
        functools.partial(_lru_kernel, n_lat=n_lat),
        out_shape=[jax.ShapeDtypeStruct((n, C), F32)] * 2,
        grid=(n_lat + 1,),
        in_specs=[pl.BlockSpec((T, C), lambda j: (fblk(j), 0)),
                  pl.BlockSpec((8, C), lambda j: (prev8(fblk(j)), 0)),
                  pl.BlockSpec((8, C), lambda j: (next8(fblk(j)), 0)),
                  pl.BlockSpec((T, C), lambda j: (bblk(j), 0)),
                  pl.BlockSpec((8, C), lambda j: (prev8(bblk(j)), 0)),
                  pl.BlockSpec((8, C), lambda j: (next8(bblk(j)), 0)),
                  full(conv_w), full(conv_b), full(w_gates), full(b_gates), full(lam)],
        out_specs=[pl.BlockSpec((T, C), lambda j: (fblk(j), 0)),
                   pl.BlockSpec((T, C), lambda j: (bblk(j), 0))],
        scratch_shapes=[pltpu.VMEM((8, C), F32), pltpu.VMEM((8, C), F32)],
        compiler_params=_cparams("arbitrary"),
        name="rglru",
    )(xa, xa, xa, xa, xa, xa, conv_w, conv_b, w_gates, b_gates, lam)


def _win_kernel(sink_ref, q_ref, kp_ref, kc_ref, kn_ref, vp_ref, vc_ref, vn_ref, kx_ref, vx_ref,
                o_ref, *, nb):
    W = WIN_BLOCK
    R = WIN_Q_HEADS // WIN_KV_HEADS
    b = pl.program_id(0)
    blk = jnp.where(b < nb, b, -4)
    rows = lax.broadcasted_iota(jnp.int32, (R * W, 3 * W), 0)
    cols = lax.broadcasted_iota(jnp.int32, (R * W, 3 * W), 1)
    kpos = (blk - 1) * W + cols
    qpos = blk * W + (rows & (W - 1))
    valid = jnp.logical_and(jnp.abs(kpos - qpos) <= W, jnp.logical_and(kpos >= 0, kpos < nb * W))
    row1 = lax.broadcasted_iota(jnp.int32, (R * W, 1), 0)
    q = q_ref[...]
    for g in range(WIN_KV_HEADS):
        ls = slice(g * HEAD_DIM, (g + 1) * HEAD_DIM)
        qs = jnp.concatenate([q[:, (g * R + r) * HEAD_DIM:(g * R + r + 1) * HEAD_DIM] for r in range(R)], axis=0)
        kl = jnp.concatenate([kp_ref[:, ls], kc_ref[:, ls], kn_ref[:, ls]], axis=0)
        vl = jnp.concatenate([vp_ref[:, ls], vc_ref[:, ls], vn_ref[:, ls]], axis=0)
        s_loc = jnp.where(valid, _dot_nt(qs, kl), NEG)
        s_ctx = _dot_nt(qs, kx_ref[:, ls])
        sink = jnp.full((R * W, 1), sink_ref[g * R], F32)
        for r in range(1, R):
            sink = jnp.where(row1 >= r * W, sink_ref[g * R + r], sink)
        m = jnp.maximum(jnp.maximum(jnp.max(s_loc, axis=-1, keepdims=True),
                                    jnp.max(s_ctx, axis=-1, keepdims=True)), sink)
        p_loc = jnp.exp(s_loc - m)
        p_ctx = jnp.exp(s_ctx - m)
        den = (jnp.sum(p_loc, axis=-1, keepdims=True) + jnp.sum(p_ctx, axis=-1, keepdims=True)
               + jnp.exp(sink - m))
        o = (_dot(p_loc.astype(BF16), vl) + _dot(p_ctx.astype(BF16), vx_ref[:, ls])) / den
        for r in range(R):
            hh = g * R + r
            o_ref[:, hh * HEAD_DIM:(hh + 1) * HEAD_DIM] = o[r * W:(r + 1) * W].astype(o_ref.dtype)


def _window_attention(sink, qw, kw, vw, L, with_ctx):
    n = qw.shape[0]
    W = WIN_BLOCK
    nb = L // W
    nq = n // W if with_ctx else nb
    ctx_blk = L // TOK_TILE
    lat = lambda b: jnp.minimum(b, nb - 1)
    kv = lambda f: pl.BlockSpec((W, D_WIN_KV), lambda b: (f(b), 0))
    prv = lambda b: jnp.maximum(lat(b) - 1, 0)
    nxt = lambda b: jnp.minimum(lat(b) + 1, nb - 1)
    ctx = pl.BlockSpec((n - L, D_WIN_KV), lambda b: (ctx_blk, 0))
    return pl.pallas_call(
        functools.partial(_win_kernel, nb=nb),
        out_shape=jax.ShapeDtypeStruct((nq * W, D_WIN_Q), BF16),
        grid=(nq,),
        in_specs=[pl.BlockSpec(memory_space=pltpu.SMEM),
                  pl.BlockSpec((W, D_WIN_Q), lambda b: (b, 0)),
                  kv(prv), kv(lat), kv(nxt), kv(prv), kv(lat), kv(nxt), ctx, ctx],
        out_specs=pl.BlockSpec((W, D_WIN_Q), lambda b: (b, 0)),
        compiler_params=_cparams("arbitrary"),
        name="window_attn",
    )(sink, qw, kw, kw, kw, vw, vw, vw, kw, vw)


def _na_kernel(q_ref, kl_ref, vl_ref, kx_ref, vx_ref, bias_ref, o_ref, *, n_blocks):
    b = pl.program_id(0)
    nq = q_ref.shape[0]
    latent = jnp.full((nq, 1), (b < n_blocks).astype(jnp.int32)) == 1
    q = q_ref[...]
    for h in range(NA_HEADS):
        ls = slice(h * HEAD_DIM, (h + 1) * HEAD_DIM)
        qh = q[:, ls]
        s_loc = jnp.where(latent, _dot_nt(qh, kl_ref[:, ls]) + bias_ref[0, h], NEG)
        s_ctx = _dot_nt(qh, kx_ref[:, ls])
        m = jnp.maximum(jnp.max(s_loc, axis=-1, keepdims=True), jnp.max(s_ctx, axis=-1, keepdims=True))
        p_loc = jnp.exp(s_loc - m)
        p_ctx = jnp.exp(s_ctx - m)
        den = jnp.sum(p_loc, axis=-1, keepdims=True) + jnp.sum(p_ctx, axis=-1, keepdims=True)
        o = (_dot(p_loc.astype(BF16), vl_ref[:, ls]) + _dot(p_ctx.astype(BF16), vx_ref[:, ls])) / den
        o_ref[:, ls] = o.astype(o_ref.dtype)


def _na_bias_table(rpb, rows):
    R, KR = NA_ROW_BLOCK, NA_KEY_ROWS
    qc = np.arange(GRID_W)[:, None]
    kc = np.arange(GRID_W)[None, :]
    qstart = np.clip(qc - NA_KW // 2, 0, GRID_W - NA_KW)
    inside = (kc - qstart >= 0) & (kc - qstart < NA_KW)
    pad = GRID_W - NA_KW
    rp = jnp.pad(rpb.astype(F32), ((0, 0), (0, 0), (pad, pad)))
    shifted = jnp.stack([rp[:, :, NA_KW - 1 - q + pad:NA_KW - 1 - q + pad + GRID_W] for q in range(GRID_W)], axis=2)
    per_dr = jnp.where(inside[None, None], shifted, NEG)
    masked = jnp.full((NA_HEADS, GRID_W, GRID_W), NEG, F32)
    half = NA_KH // 2
    cases = []
    for r0, ks in ((0, 0), (half, 0), (rows - R, rows - KR)):
        row_blocks = []
        for rr in range(R):
            r = r0 + rr
            kr = min(max(r - half, 0), rows - NA_KH)
            blocks = [per_dr[:, ks + kk - r + NA_KH - 1] if kr <= ks + kk < kr + NA_KH else masked
                      for kk in range(KR)]
            row_blocks.append(jnp.concatenate(blocks, axis=-1))
        cases.append(jnp.concatenate(row_blocks, axis=1))
    return jnp.stack(cases)


def _neighborhood_attention(qn, kn, vn, bias, L, with_ctx):
    n = qn.shape[0]
    R, KR = NA_ROW_BLOCK, NA_KEY_ROWS
    rows = L // GRID_W
    nb = rows // R
    nq = R * GRID_W
    steps = n // nq if with_ctx else nb
    ctx_blk = L // TOK_TILE
    half = NA_KH // 2

    def blk_of(b):
        return jnp.minimum(b, nb - 1)

    def key_start(b):
        return jnp.clip(blk_of(b) * R - half, 0, rows - KR) * GRID_W

    def case_of(b):
        return jnp.where(blk_of(b) == 0, 0, jnp.where(blk_of(b) == nb - 1, 2, 1))

    loc = pl.BlockSpec((pl.Element(KR * GRID_W), pl.Element(D_NA)), lambda b: (key_start(b), 0))
    ctx = pl.BlockSpec((n - L, D_NA), lambda b: (ctx_blk, 0))
    return pl.pallas_call(
        functools.partial(_na_kernel, n_blocks=nb),
        out_shape=jax.ShapeDtypeStruct((steps * nq, D_NA), BF16),
        grid=(steps,),
        in_specs=[pl.BlockSpec((nq, D_NA), lambda b: (b, 0)), loc, loc, ctx, ctx,
                  pl.BlockSpec((1, NA_HEADS, nq, KR * GRID_W), lambda b: (case_of(b), 0, 0, 0))],
        out_specs=pl.BlockSpec((nq, D_NA), lambda b: (b, 0)),
        compiler_params=_cparams("arbitrary"),
        name="neighborhood_attn",
    )(qn, kn, vn, kn, vn, bias)


def _out_kernel(xl_ref, xc_ref, hf_ref, hb_ref, ga_ref, yb_ref, yc_ref, w_ref, mod_ref, g_ref,
                xo_ref, ht_ref, *, n_lat):
    d = xl_ref.shape[1]
    i = pl.program_id(0)
    row = (i == n_lat).astype(jnp.int32)
    is_ctx = jnp.full((TOK_TILE, 1), row) == 1
    x = jnp.where(is_ctx, xc_ref[...], xl_ref[...])
    ya = ((hf_ref[...] + hb_ref[...]) * jax.nn.gelu(ga_ref[...])).astype(BF16)
    o1 = D_LRU
    o2 = D_LRU + D_WIN_Q
    mix = (_dot(ya, w_ref[0:o1]) + _dot(yb_ref[...], w_ref[o1:o2]) + _dot(yc_ref[...], w_ref[o2:o2 + D_NA]))
    xn = x + mod_ref[pl.ds(row, 1), 2 * d:3 * d] * mix
    xo_ref[...] = xn
    ms = jnp.mean(xn * xn, axis=-1, keepdims=True)
    y = xn * lax.rsqrt(ms + EPS) * g_ref[...]
    h2 = y * (1.0 + mod_ref[pl.ds(row, 1), 4 * d:5 * d]) + mod_ref[pl.ds(row, 1), 3 * d:4 * d]
    ht_ref[...] = h2.T.astype(BF16)


def _out_proj(xl, xc, hf, hb, ga, yb, yc, w_bf, mod, g, with_ctx):
    L, d = xl.shape
    T = TOK_TILE
    n_lat = L // T
    nt = n_lat + 1 if with_ctx else n_lat
    tok = lambda w: pl.BlockSpec((T, w), lambda i: (i, 0))
    full = lambda a: pl.BlockSpec(a.shape, lambda i: (0,) * a.ndim)
    return pl.pallas_call(
        functools.partial(_out_kernel, n_lat=n_lat),
        out_shape=[jax.ShapeDtypeStruct((nt * T, d), F32), jax.ShapeDtypeStruct((d, nt * T), BF16)],
        grid=(nt,),
        in_specs=[pl.BlockSpec((T, d), lambda i: (jnp.minimum(i, n_lat - 1), 0)),
                  pl.BlockSpec((T, d), lambda i: (0, 0)),
                  tok(D_LRU), tok(D_LRU), tok(D_LRU), tok(D_WIN_Q), tok(D_NA),
                  full(w_bf), full(mod), full(g)],
        out_specs=[tok(d), pl.BlockSpec((d, T), lambda i: (0, i))],
        compiler_params=_cparams("arbitrary"),
        name="out_proj",
    )(xl, xc, hf, hb, ga, yb, yc, w_bf, mod, g)


SUBLANES = 8


def _sorting_pairs(n):
    pairs, p = [], 1
    while p < n:
        k = p
        while k >= 1:
            for j in range(k % p, n - k, 2 * k):
                for i in range(min(k, n - j - k)):
                    if (i + j) // (2 * p) == (i + j + k) // (2 * p):
                        pairs.append((i + j, i + j + k))
            k //= 2
        p *= 2
    return pairs


def _vmax(a, b):
    if a is None:
        return b
    if b is None:
        return a
    return jnp.maximum(a, b)


def _vmin(a, b):
    if a is None or b is None:
        return None
    return jnp.minimum(a, b)


def _top16_sorted(slabs):
    K = PEER_TOPK
    w = list(slabs)
    for a, b in _sorting_pairs(K):
        w[a], w[b] = _vmax(w[a], w[b]), _vmin(w[a], w[b])
    shift = SUBLANES // 2
    while shift >= 1:
        partner = [None if w[K - 1 - i] is None else pltpu.roll(w[K - 1 - i], shift, 0) for i in range(K)]
        w = [_vmax(w[i], partner[i]) for i in range(K)]
        stride = K // 2
        while stride >= 1:
            for i in range(K):
                if i & stride == 0:
                    w[i], w[i + stride] = _vmax(w[i], w[i + stride]), _vmin(w[i], w[i + stride])
            stride //= 2
        shift //= 2
    return w


def _allsum8(x):
    x = x + pltpu.roll(x, 4, 0)
    x = x + pltpu.roll(x, 2, 0)
    return x + pltpu.roll(x, 1, 0)


def _route_tile(s1, s2):
    K = PEER_TOPK
    S = SUBLANES
    n_slab = s1.shape[0] // S
    a1 = [s1[j * S:(j + 1) * S] for j in range(n_slab)]
    a2 = [s2[j * S:(j + 1) * S] for j in range(n_slab)]
    v1 = _top16_sorted(a1)
    v2 = _top16_sorted(a2)
    sub = lax.broadcasted_iota(jnp.int32, a1[0].shape, 0)

    def as_rows(v, lo):
        out = v[lo]
        for b in range(1, S):
            out = jnp.where(sub == b, v[lo + b], out)
        return out

    v2_lo, v2_hi, v1_hi = as_rows(v2, 0), as_rows(v2, S), as_rows(v1, S)
    lens = [K // (a + 1) for a in range(S)]
    cands = [v1[0] + v2_lo, v1[0] + v2_hi]
    for a in range(1, S):
        ca = v1[a] + v2_lo
        cands.append(ca if lens[a] >= S else jnp.where(sub < lens[a], ca, -jnp.inf))
    cands.append(v1_hi + v2[0])
    thr = _top16_sorted(cands + [None] * (K - len(cands)))[K - 1]
    x2_lo, x2_hi = jnp.exp(v2_lo - v2[0]), jnp.exp(v2_hi - v2[0])
    sel_lo, sel_hi = cands[0] >= thr, cands[1] >= thr
    zsum = jnp.where(sel_lo, x2_lo, 0.0) + jnp.where(sel_hi, x2_hi, 0.0)
    cnt = [_allsum8(jnp.where(sel_lo, 1.0, 0.0) + jnp.where(sel_hi, 1.0, 0.0))]
    for a in range(1, S):
        sel = cands[a + 1] >= thr
        zsum = zsum + jnp.where(sel, jnp.exp(v1[a] - v1[0]) * x2_lo, 0.0)
        cnt.append(_allsum8(jnp.where(sel, 1.0, 0.0)))
    zsum = zsum + jnp.where(cands[S + 1] >= thr, jnp.exp(v1_hi - v1[0]), 0.0)
    inv_z = 1.0 / _allsum8(zsum)
    r2, e2, c1, e1, pair0 = [], [], [], [], []
    for j in range(n_slab):
        r = jnp.full(a2[j].shape, float(K), F32)
        for k in reversed(range(K)):
            r = jnp.where(a2[j] >= v2[k], float(k), r)
        r2.append(r)
        e2.append(jnp.exp(a2[j] - v2[0]) * inv_z)
        c = jnp.where(a1[j] + v2[0] >= thr, 1.0, 0.0)
        pair0.append(c)
        for a in reversed(range(S)):
            c = jnp.where(a1[j] >= v1[a], cnt[a], c)
        c1.append(c)
        e1.append(jnp.exp(a1[j] - v1[0]))
    lor, land = jnp.logical_or, jnp.logical_and
    n_sel = functools.reduce(lambda a, b: a + b, cnt) + _allsum8(jnp.where(cands[S + 1] >= thr, 1.0, 0.0))
    f1 = land(v1[S - 1] == v1[S], cnt[S - 1] >= 2.0)
    for k in range(S - 1):
        f1 = lor(f1, land(v1[k] == v1[k + 1], cnt[k] != cnt[k + 1]))
    f1 = lor(f1, _allsum8(functools.reduce(lambda a, b: a + b, pair0)) > float(K))
    f2 = v2[0] == v2[1]
    for k in range(1, K - 1):
        f2 = lor(f2, land(v2[k] == v2[k + 1], cnt[0] > float(k)))
    r_sum = _allsum8(functools.reduce(lambda a, b: a + b, r2))
    any_tie2 = r_sum != float(K * (K - 1) // 2 + K * (n_slab * S - K))
    f2 = lor(f2, land(cnt[0] == float(K), any_tie2))
    flag = lor(lor(f1, f2), n_sel != float(K))
    cat = lambda xs: jnp.concatenate(xs, axis=0)
    return cat(r2), cat(e2), cat(c1), cat(e1), jnp.where(flag, 1.0, 0.0)


def _route_tile_exact(s1, s2):
    K = PEER_TOPK

    def take16(x):
        idx = lax.broadcasted_iota(jnp.int32, x.shape, 0).astype(F32)
        rank = jnp.full(x.shape, float(K), F32)
        vals, cur = [], x
        for k in range(K):
            m = jnp.max(cur, axis=0, keepdims=True)
            first = jnp.min(jnp.where(cur == m, idx, float(x.shape[0])), axis=0, keepdims=True)
            hit = idx == first
            rank = jnp.where(hit, float(k), rank)
            vals.append(m)
            cur = jnp.where(hit, -jnp.inf, cur)
        return rank, vals

    r1, v1 = take16(s1)
    r2, v2 = take16(s2)
    v2a = jnp.concatenate(v2, axis=0)
    x1 = [jnp.exp(v1[a] - v1[0]) for a in range(K)]
    x2a = jnp.exp(v2a - v2[0])
    cand = jnp.concatenate([v1[a] + v2a for a in range(K)], axis=0)
    crank, _ = take16(cand)
    sel = crank < float(K)
    z = jnp.zeros_like(v1[0])
    c1 = jnp.zeros(s1.shape, F32)
    for a in range(K):
        sel_a = sel[a * K:(a + 1) * K]
        z = z + jnp.sum(jnp.where(sel_a, x1[a] * x2a, 0.0), axis=0, keepdims=True)
        cnt_a = jnp.sum(jnp.where(sel_a, 1.0, 0.0), axis=0, keepdims=True)
        c1 = jnp.where(r1 == float(a), cnt_a, c1)
    return r2, jnp.exp(s2 - v2[0]) / z, c1, jnp.exp(s1 - v1[0])


def _prep_kernel(ht_ref, wq_ref, k1_ref, k2_ref, r2_ref, e2_ref, c1_ref, e1_ref, q_scr):
    nk = PEER_NKEYS
    q_scr[...] = _dot(wq_ref[...], ht_ref[...])

    def scores(h):
        base = pl.multiple_of(h * 2 * nk, 2 * nk)
        s1 = _dot(k1_ref[...], q_scr[pl.ds(base, nk), :].astype(BF16))
        s2 = _dot(k2_ref[...], q_scr[pl.ds(base + nk, nk), :].astype(BF16))
        return s1, s2

    def store(h, ts, r2, e2, c1, e1):
        r2_ref[h, :, ts] = _pack_rows(r2)
        e2_ref[h, :, ts] = _pack_rows(e2)
        c1_ref[h, :, ts] = c1
        e1_ref[h, :, ts] = e1

    def head(h, flags):
        s1, s2 = scores(h)
        for t in range(s1.shape[1] // LANES):
            ts = slice(t * LANES, (t + 1) * LANES)
            r2, e2, c1, e1, flag = _route_tile(s1[:, ts], s2[:, ts])
            store(h, ts, r2, e2, c1, e1)
            flags = jnp.maximum(flags, flag)
        return flags

    flags = lax.fori_loop(0, PEER_HEADS, head, jnp.zeros((SUBLANES, LANES), F32))

    @pl.when(jnp.max(flags) > 0.0)
    def _():
        def head_exact(h, carry):
            s1, s2 = scores(h)
            for t in range(s1.shape[1] // LANES):
                ts = slice(t * LANES, (t + 1) * LANES)
                store(h, ts, *_route_tile_exact(s1[:, ts], s2[:, ts]))
            return carry

        lax.fori_loop(0, PEER_HEADS, head_exact, 0)


def _peer_prep(ht, wq_t, k1, k2):
    d, n = ht.shape
    T = PREP_TOK_TILE
    nk = PEER_NKEYS
    full = lambda a: pl.BlockSpec(a.shape, lambda i: (0,) * a.ndim)
    spec = lambda rows: pl.BlockSpec((PEER_HEADS, rows, T), lambda i: (0, 0, i))
    return pl.pallas_call(
        _prep_kernel,
        out_shape=[jax.ShapeDtypeStruct((PEER_HEADS, nk // 2, n), U32)] * 2
        + [jax.ShapeDtypeStruct((PEER_HEADS, nk, n), F32)] * 2,
        grid=(n // T,),
        in_specs=[pl.BlockSpec((d, T), lambda i: (0, i)), full(wq_t), full(k1), full(k2)],
        out_specs=[spec(nk // 2), spec(nk // 2), spec(nk), spec(nk)],
        scratch_shapes=[pltpu.VMEM((wq_t.shape[0], T), F32)],
        compiler_params=_cparams("arbitrary"),
        name="peer_prep",
    )(ht, wq_t, k1, k2)


def _peer_kernel(ht_ref, u_ref, un_ref, vt_ref, vp_ref, r2_ref, e2_ref, c1_ref, e1_ref, x_ref, g_ref,
                 o_ref, acc_ref, *ap_refs):
    nk = PEER_NKEYS
    c = pl.program_id(1)
    n_sub = len(ap_refs) // 2
    a_refs, p_refs = ap_refs[:n_sub], ap_refs[n_sub:]
    sub = 2 * a_refs[0].shape[0]
    T = a_refs[0].shape[1]
    rows_per_sub = sub // nk
    PK = 16
    WK = PK // 2

    tok_blocks = [slice(n, min(n + PEER_MXU_TOKENS, T)) for n in range(0, T, PEER_MXU_TOKENS)]

    def activations_from(rows_ref, dst_ref):
        for tb in tok_blocks:
            dst_ref[:, tb] = _pack_rows(_dot(rows_ref, ht_ref[:, tb]))

    def activations(s):
        activations_from(u_ref[s * sub:(s + 1) * sub, :], a_refs[s])

    def gates(s):
        for t in range(T // LANES):
            ts = slice(t * LANES, (t + 1) * LANES)
            for j in range(rows_per_sub):
                row = s * rows_per_sub + j
                gate = [None] * (nk // PK)
                for h in range(PEER_HEADS):
                    c1 = c1_ref[h, :, ts][row:row + 1]
                    e1 = e1_ref[h, :, ts][row:row + 1]
                    c1 = jnp.broadcast_to(c1, (PK, LANES)).astype(BF16)
                    e1 = jnp.broadcast_to(e1, (PK, LANES)).astype(BF16)
                    for v in range(nk // PK):
                        ws = slice(v * WK, (v + 1) * WK)
                        r2 = _unpack_rows(r2_ref[h, ws, ts])
                        e2 = _unpack_rows(e2_ref[h, ws, ts])
                        g = jnp.where(r2 < c1, e2 * e1, 0.0)
                        gate[v] = g if h == 0 else gate[v] + g
                for v in range(nk // PK):
                    ws = slice((j * nk + v * PK) // 2, (j * nk + (v + 1) * PK) // 2)
                    a = _unpack_rows(a_refs[s][ws, ts])
                    th = jnp.tanh(a * (GELU_K0 + GELU_K1 * (a * a)))
                    p_refs[s][ws, ts] = pltpu.bitcast(gate[v] * (a * (0.5 + 0.5 * th)), U32)

    def project(terms):
        for tb in tok_blocks:
            parts = [_dot(cols, _unpack_rows(src[:, tb])) for cols, src in terms]
            acc_ref[:, tb] += functools.reduce(lambda a, b: a + b, parts)

    def term(s):
        return vt_ref[:, s * sub:(s + 1) * sub], p_refs[s]

    last = n_sub - 1
    assert n_sub % 2 == 0

    @pl.when(c == 0)
    def _():
        acc_ref[...] = jnp.zeros(acc_ref.shape, F32)
        p_refs[last][...] = jnp.zeros(p_refs[last].shape, U32)
        activations(0)

    pending = [(vp_ref[...], p_refs[last])]
    for s in range(n_sub):
        gates(s)
        if s < last:
            pending.append(term(s))
            activations(s + 1)
        if len(pending) == 2:
            project(pending)
            pending = []
    activations_from(un_ref[...], a_refs[0])

    @pl.when(c == pl.num_programs(1) - 1)
    def _():
        project([term(last)])
        o_ref[...] = x_ref[...] + g_ref[...] * acc_ref[...].T


def _peer_dense(ht, u_bf, vt_bf, r2, e2, c1, e1, x, g2, tok_tile, tok_off, n_tok):
    d = ht.shape[0]
    n_exp = u_bf.shape[0]
    T = tok_tile
    NC = PEER_EXPERT_CHUNK
    nk = PEER_NKEYS
    SUB = PEER_SUB_CHUNK
    n_sub = NC // SUB
    rt = lambda rows: pl.BlockSpec((PEER_HEADS, rows, T), lambda i, c: (0, 0, i + tok_off))
    per_chunk = pl.BlockSpec((PEER_HEADS, NC // nk, T), lambda i, c: (0, c, i + tok_off))
    return pl.pallas_call(
        _peer_kernel,
        out_shape=jax.ShapeDtypeStruct((n_tok, d), F32),
        grid=(n_tok // T, n_exp // NC),
        in_specs=[pl.BlockSpec((d, T), lambda i, c: (0, i + tok_off)),
                  pl.BlockSpec((NC, d), lambda i, c: (c, 0)),
                  pl.BlockSpec((SUB, d), lambda i, c: (jnp.minimum((c + 1) * n_sub, n_exp // SUB - 1), 0)),
                  pl.BlockSpec((d, NC), lambda i, c: (0, c)),
                  pl.BlockSpec((d, SUB), lambda i, c: (0, jnp.maximum(c * n_sub - 1, 0))),
                  rt(nk // 2), rt(nk // 2), per_chunk, per_chunk,
                  pl.BlockSpec((T, d), lambda i, c: (i + tok_off, 0)),
                  pl.BlockSpec((1, d), lambda i, c: (0, 0))],
        out_specs=pl.BlockSpec((T, d), lambda i, c: (i, 0)),
        scratch_shapes=[pltpu.VMEM((d, T), F32)] + [pltpu.VMEM((PEER_SUB_CHUNK // 2, T), U32)] * (2 * n_sub),
        compiler_params=_cparams("arbitrary", "arbitrary"),
        name="peer_dense",
    )(ht, u_bf, u_bf, vt_bf, vt_bf, r2, e2, c1, e1, x, g2)


def _rope_tables(L, n):
    t = jnp.arange(L)
    row = (t // GRID_W).astype(F32)
    col = (t % GRID_W).astype(F32)
    q = HEAD_DIM // 4
    inv = ROPE_THETA ** (-jnp.arange(q, dtype=F32) / q)
    ar = row[:, None] * inv
    ac = col[:, None] * inv
    cos = jnp.concatenate([jnp.cos(ar), jnp.cos(ar), jnp.cos(ac), jnp.cos(ac)], axis=-1)
    sin = jnp.concatenate([-jnp.sin(ar), jnp.sin(ar), -jnp.sin(ac), jnp.sin(ac)], axis=-1)
    cos = jnp.concatenate([cos, jnp.ones((n - L, HEAD_DIM), F32)], axis=0)
    sin = jnp.concatenate([sin, jnp.zeros((n - L, HEAD_DIM), F32)], axis=0)
    return jnp.tile(cos, (1, LANES // HEAD_DIM)), jnp.tile(sin, (1, LANES // HEAD_DIM))


def _block_diag(w):
    nb, di, do = w.shape
    eye = jnp.eye(nb, dtype=w.dtype)
    return (eye[:, None, :, None] * w[:, :, None, :]).reshape(nb * di, nb * do)


def kernel(x, c, ctx, c_ctx, w_mod, b_mod, norm1_g, norm2_g, w_in, w_out, lru_conv_w, lru_conv_b, lru_wa, lru_ba, lru_wx, lru_bx, lru_lam, win_qn_g, win_kn_g, win_sink, na_qn_g, na_kn_g, na_rpb, peer_wq, peer_k1, peer_k2, peer_u, peer_v):
    B, L, D = x.shape
    Lc = ctx.shape[1]
    depth = w_mod.shape[0]
    assert B == 1 and Lc == TOK_TILE and L % PEER_TOK_TILE == 0 and L // GRID_W >= 3 * NA_ROW_BLOCK
    n = L + Lc
    n_lat = L // TOK_TILE
    rows = L // GRID_W

    cvec = jnp.zeros((8, D), F32).at[0].set(c[0]).at[1].set(c_ctx)
    mods = _modulation(cvec, w_mod, b_mod)
    cos_t, sin_t = _rope_tables(L, n)
    tile2 = lambda g: jnp.tile(g, LANES // HEAD_DIM)

    xl, xc = x[0], ctx[0]
    for l in range(depth):
        with_ctx = l < depth - 1
        mod = mods[l]
        hg = jnp.zeros((8, LANES), F32)
        hg = hg.at[0].set(tile2(win_qn_g[l])).at[1].set(tile2(win_kn_g[l]))
        hg = hg.at[2].set(tile2(na_qn_g[l])).at[3].set(tile2(na_kn_g[l]))
        xa, ga, qw, kw, vw, qn, kn, vn = _in_proj(xl, xc, mod, norm1_g[l][None], w_in[l].astype(BF16),
                                                  cos_t, sin_t, hg)

        w_gates = jnp.concatenate([_block_diag(lru_wa[l, 0]), _block_diag(lru_wx[l, 0]),
                                   _block_diag(lru_wa[l, 1]), _block_diag(lru_wx[l, 1])], axis=1).astype(BF16)
        b_gates = jnp.concatenate([lru_ba[l, 0], lru_bx[l, 0], lru_ba[l, 1], lru_bx[l, 1]])[None]
        conv_w = jnp.zeros((8, D_LRU), F32).at[0:lru_conv_w.shape[1]].set(lru_conv_w[l])
        lam = jnp.zeros((8, D_LRU), F32).at[0:2].set(lru_lam[l])
        hf, hb = _lru(xa, conv_w, lru_conv_b[l][None], w_gates, b_gates, lam, n_lat)

        yb = _window_attention(win_sink[l], qw, kw, vw, L, with_ctx)
        yc = _neighborhood_attention(qn, kn, vn, _na_bias_table(na_rpb[l], rows), L, with_ctx)

        xn, ht = _out_proj(xl, xc, hf, hb, ga, yb, yc, w_out[l].astype(BF16), mod, norm2_g[l][None], with_ctx)

        r2, e2, c1, e1 = _peer_prep(ht, peer_wq[l].T.astype(BF16), peer_k1[l].astype(BF16),
                                    peer_k2[l].astype(BF16))
        u_bf = peer_u[l].astype(BF16)
        vt_bf = peer_v[l].T.astype(BF16)
        g2 = mod[:, 5 * D:6 * D]
        xl_new = _peer_dense(ht, u_bf, vt_bf, r2, e2, c1, e1, xn, g2[0:1], PEER_TOK_TILE, 0, L)
        if with_ctx:
            xc = _peer_dense(ht, u_bf, vt_bf, r2, e2, c1, e1, xn, g2[1:2], Lc, L // Lc, Lc)
        xl = xl_new
    return xl[None]
```

```python
import functools

import numpy as np
import jax
import jax.numpy as jnp
from jax import lax
from jax.experimental import pallas as pl
from jax.experimental.pallas import tpu as pltpu

F32 = jnp.float32
BF16 = jnp.bfloat16
U32 = jnp.uint32

HEAD_DIM = 64
GRID_W = 64
EPS = 1e-6
ROPE_THETA = 10000.0
D_LRU = 256
LRU_BLOCKS = 4
LRU_C = 8.0
WIN_Q_HEADS = 6
WIN_KV_HEADS = 2
WIN_BLOCK = 128
NA_HEADS = 6
NA_KH = 8
NA_KW = 16
NA_ROW_BLOCK = 4
NA_KEY_ROWS = NA_ROW_BLOCK + NA_KH - 1
PEER_HEADS = 8
PEER_NKEYS = 128
PEER_TOPK = 16
D_WIN_Q = WIN_Q_HEADS * HEAD_DIM
D_WIN_KV = WIN_KV_HEADS * HEAD_DIM
D_NA = NA_HEADS * HEAD_DIM

LANES = 128
TOK_TILE = 256
PEER_TOK_TILE = 512
PEER_EXPERT_CHUNK = 2048
PEER_SUB_CHUNK = 512
PEER_MXU_TOKENS = 256
PREP_TOK_TILE = 1024
VMEM_LIMIT = 56 * 1024 * 1024
NEG = -1e30
GELU_K0 = float(np.sqrt(2.0 / np.pi))
GELU_K1 = 0.044715 * GELU_K0


def _cparams(*sem):
    return pltpu.CompilerParams(dimension_semantics=sem, vmem_limit_bytes=VMEM_LIMIT)


def _dot(a, b):
    return jnp.dot(a, b, preferred_element_type=F32)


def _pack_rows(x):
    return pltpu.bitcast(x.astype(BF16), U32)


def _unpack_rows(w):
    return pltpu.bitcast(w, BF16)


def _dot_nt(a, b):
    return lax.dot_general(a, b, (((1,), (1,)), ((), ())), preferred_element_type=F32)


def _mod_kernel(c_ref, w_ref, b_ref, o_ref):
    c = c_ref[...]
    a = (c * jax.nn.sigmoid(c)).astype(BF16)
    o_ref[0] = _dot(a, w_ref[0].astype(BF16)) + b_ref[0]


def _modulation(cvec, w_mod, b_mod):
    depth, d, d6 = w_mod.shape
    return pl.pallas_call(
        _mod_kernel,
        out_shape=jax.ShapeDtypeStruct((depth, 8, d6), F32),
        grid=(depth, d6 // d),
        in_specs=[pl.BlockSpec((8, d), lambda l, j: (0, 0)),
                  pl.BlockSpec((1, d, d), lambda l, j: (l, 0, j)),
                  pl.BlockSpec((1, 1, d), lambda l, j: (l, 0, j))],
        out_specs=pl.BlockSpec((1, 8, d), lambda l, j: (l, 0, j)),
        compiler_params=_cparams("arbitrary", "arbitrary"),
        name="modulation",
    )(cvec, w_mod, b_mod.reshape(depth, 1, d6))


def _pair_mean_matrix():
    r = lax.broadcasted_iota(jnp.int32, (LANES, LANES), 0) < HEAD_DIM
    c = lax.broadcasted_iota(jnp.int32, (LANES, LANES), 1) < HEAD_DIM
    return jnp.where(r == c, 1.0 / HEAD_DIM, 0.0).astype(BF16)


def _head_norm(z, gain, bd):
    z2 = z * z
    hi = z2.astype(BF16)
    lo = (z2 - hi.astype(F32)).astype(BF16)
    ms = _dot(hi, bd) + _dot(lo, bd)
    return z * lax.rsqrt(ms + EPS) * gain


def _rope(z, cos, sin):
    lane = lax.broadcasted_iota(jnp.int32, z.shape, 1)
    first = (lane & 16) == 0
    partner = jnp.where(first, pltpu.roll(z, LANES - 16, 1), pltpu.roll(z, 16, 1))
    return z * cos + partner * sin


def _in_kernel(xl_ref, xc_ref, mod_ref, g_ref, w_ref, cos_ref, sin_ref, hg_ref,
               xa_ref, ga_ref, qw_ref, kw_ref, vw_ref, qn_ref, kn_ref, vn_ref, *, n_lat):
    d = xl_ref.shape[1]
    i = pl.program_id(0)
    row = (i == n_lat).astype(jnp.int32)
    is_ctx = jnp.full((TOK_TILE, 1), row) == 1
    x = jnp.where(is_ctx, xc_ref[...], xl_ref[...])
    ms = jnp.mean(x * x, axis=-1, keepdims=True)
    y = x * lax.rsqrt(ms + EPS) * g_ref[...]
    sh = mod_ref[pl.ds(row, 1), 0:d]
    sc = mod_ref[pl.ds(row, 1), d:2 * d]
    h = (y * (1.0 + sc) + sh).astype(BF16)
    z = _dot(h, w_ref[...])

    bd = _pair_mean_matrix()
    cos = cos_ref[...]
    sin = sin_ref[...]
    scale = HEAD_DIM ** -0.5
    o = 0
    xa_ref[...] = z[:, o:o + D_LRU]
    o += D_LRU
    ga_ref[...] = z[:, o:o + D_LRU]
    o += D_LRU
    for g in range(D_WIN_Q // LANES):
        zz = _rope(_head_norm(z[:, o:o + LANES], hg_ref[0:1], bd), cos, sin)
        qw_ref[:, g * LANES:(g + 1) * LANES] = (zz * scale).astype(BF16)
        o += LANES
    kw_ref[...] = _rope(_head_norm(z[:, o:o + LANES], hg_ref[1:2], bd), cos, sin).astype(BF16)
    o += LANES
    vw_ref[...] = z[:, o:o + LANES].astype(BF16)
    o += LANES
    for g in range(D_NA // LANES):
        zz = _head_norm(z[:, o:o + LANES], hg_ref[2:3], bd)
        qn_ref[:, g * LANES:(g + 1) * LANES] = (zz * scale).astype(BF16)
        o += LANES
    for g in range(D_NA // LANES):
        zz = _head_norm(z[:, o:o + LANES], hg_ref[3:4], bd)
        kn_ref[:, g * LANES:(g + 1) * LANES] = zz.astype(BF16)
        o += LANES
    vn_ref[...] = z[:, o:o + D_NA].astype(BF16)


def _in_proj(xl, xc, mod, g, w_bf, cos_t, sin_t, hg):
    L, d = xl.shape
    n_lat = L // TOK_TILE
    n = L + xc.shape[0]
    d_in = w_bf.shape[1]
    T = TOK_TILE
    tok = lambda w: pl.BlockSpec((T, w), lambda i: (i, 0))
    full = lambda a: pl.BlockSpec(a.shape, lambda i: (0,) * a.ndim)
    widths = (D_LRU, D_LRU, D_WIN_Q, D_WIN_KV, D_WIN_KV, D_NA, D_NA, D_NA)
    dtypes = (F32, F32, BF16, BF16, BF16, BF16, BF16, BF16)
    return pl.pallas_call(
        functools.partial(_in_kernel, n_lat=n_lat),
        out_shape=[jax.ShapeDtypeStruct((n, w), t) for w, t in zip(widths, dtypes)],
        grid=(n_lat + 1,),
        in_specs=[pl.BlockSpec((T, d), lambda i: (jnp.minimum(i, n_lat - 1), 0)),
                  pl.BlockSpec((T, d), lambda i: (0, 0)),
                  full(mod), full(g), full(w_bf), tok(LANES), tok(LANES), full(hg)],
        out_specs=[tok(w) for w in widths],
        compiler_params=_cparams("arbitrary"),
        name="in_proj",
    )(xl, xc, mod, g, w_bf, cos_t, sin_t, hg)


def _chunk_scan(a, b, reverse):
    T = a.shape[0]
    rows = lax.broadcasted_iota(jnp.int32, a.shape, 0)
    s = 1
    while s < T:
        if reverse:
            edge = rows >= T - s
            shift = T - s
        else:
            edge = rows < s
            shift = s
        a_s = jnp.where(edge, 1.0, pltpu.roll(a, shift, 0))
        b_s = jnp.where(edge, 0.0, pltpu.roll(b, shift, 0))
        b = a * b_s + b
        a = a * a_s
        s *= 2
    return a, b


def _lru_direction(xm, ph, nh, pv, nv, cw, cb, w, bias, spl, carry_ref, first, reverse):
    T, C = xm.shape
    rows = lax.broadcasted_iota(jnp.int32, (T, C), 0)
    p6 = ph[6:7] * pv
    p7 = ph[7:8] * pv
    n0 = nh[0:1] * nv
    x_m1 = jnp.where(rows == 0, p7, pltpu.roll(xm, 1, 0))
    x_m2 = jnp.where(rows == 0, p6, jnp.where(rows == 1, p7, pltpu.roll(xm, 2, 0)))
    x_p1 = jnp.where(rows == T - 1, n0, pltpu.roll(xm, T - 1, 0))
    u = cw[0:1] * x_m2 + cw[1:2] * x_m1 + cw[2:3] * xm + cw[3:4] * x_p1 + cb
    zz = _dot(u.astype(BF16), w) + bias
    r = jax.nn.sigmoid(zz[:, :C])
    ig = jax.nn.sigmoid(zz[:, C:])
    log_a = -LRU_C * r * spl
    a = jnp.exp(log_a)
    b = jnp.sqrt(-jnp.tanh(log_a) * (a * a + 1.0)) * ig * u
    a, b = _chunk_scan(a, b, reverse)

    @pl.when(first)
    def _():
        carry_ref[...] = jnp.zeros(carry_ref.shape, F32)

    h = a * carry_ref[0:1] + b
    edge = h[0:1] if reverse else h[T - 1:T]
    carry_ref[...] = jnp.broadcast_to(edge, carry_ref.shape)
    return h


def _lru_kernel(xf_ref, pf_ref, nf_ref, xb_ref, pb_ref, nb_ref, cw_ref, cb_ref, w_ref, b_ref,
                lam_ref, hf_ref, hb_ref, cf_ref, cbk_ref, *, n_lat):
    j = pl.program_id(0)
    C = D_LRU
    fblk = jnp.where(j == 0, n_lat, j - 1)
    bblk = jnp.where(j == 0, n_lat, n_lat - j)
    lam = lam_ref[...]
    spl = jnp.maximum(-lam, 0.0) + jnp.log1p(jnp.exp(-jnp.abs(lam)))
    cw = cw_ref[...]
    cb = cb_ref[...]

    def halo_valid(blk):
        pv = jnp.logical_and(blk != 0, blk != n_lat).astype(F32)
        nv = jnp.logical_and(blk != n_lat - 1, blk != n_lat).astype(F32)
        return pv, nv

    pv, nv = halo_valid(fblk)
    hf_ref[...] = _lru_direction(xf_ref[...], pf_ref[...], nf_ref[...], pv, nv, cw, cb,
                                 w_ref[:, 0:2 * C], b_ref[:, 0:2 * C], spl[0:1], cf_ref, j == 0, False)
    pv, nv = halo_valid(bblk)
    hb_ref[...] = _lru_direction(xb_ref[...], pb_ref[...], nb_ref[...], pv, nv, cw, cb,
                                 w_ref[:, 2 * C:4 * C], b_ref[:, 2 * C:4 * C], spl[1:2], cbk_ref, j == 0, True)


def _lru(xa, conv_w, conv_b, w_gates, b_gates, lam, n_lat):
    n, C = xa.shape
    T = TOK_TILE
    sub = T // 8
    nblk8 = n // 8
    fblk = lambda j: jnp.where(j == 0, n_lat, j - 1)
    bblk = lambda j: jnp.where(j == 0, n_lat, n_lat - j)
    prev8 = lambda blk: jnp.maximum(blk * sub - 1, 0)
    next8 = lambda blk: jnp.minimum((blk + 1) * sub, nblk8 - 1)
    full = lambda a: pl.BlockSpec(a.shape, lambda j: (0,) * a.ndim)
    return pl.pallas_call(
        functools.partial(_lru_kernel, n_lat=n_lat),
        out_shape=[jax.ShapeDtypeStruct((n, C), F32)] * 2,
        grid=(n_lat + 1,),
        in_specs=[pl.BlockSpec((T, C), lambda j: (fblk(j), 0)),
                  pl.BlockSpec((8, C), lambda j: (prev8(fblk(j)), 0)),
                  pl.BlockSpec((8, C), lambda j: (next8(fblk(j)), 0)),
                  pl.BlockSpec((T, C), lambda j: (bblk(j), 0)),
                  pl.BlockSpec((8, C), lambda j: (prev8(bblk(j)), 0)),
                  pl.BlockSpec((8, C), lambda j: (next8(bblk(j)), 0)),
                  full(conv_w), full(conv_b), full(w_gates), full(b_gates), full(lam)],
        out_specs=[pl.BlockSpec((T, C), lambda j: (fblk(j), 0)),
                   pl.BlockSpec((T, C), lambda j: (bblk(j), 0))],
        scratch_shapes=[pltpu.VMEM((8, C), F32), pltpu.VMEM((8, C), F32)],
        compiler_params=_cparams("arbitrary"),
        name="rglru",
    )(xa, xa, xa, xa, xa, xa, conv_w, conv_b, w_gates, b_gates, lam)


def _win_kernel(sink_ref, q_ref, kp_ref, kc_ref, kn_ref, vp_ref, vc_ref, vn_ref, kx_ref, vx_ref,
                o_ref, *, nb):
    W = WIN_BLOCK
    R = WIN_Q_HEADS // WIN_KV_HEADS
    b = pl.program_id(0)
    blk = jnp.where(b < nb, b, -4)
    rows = lax.broadcasted_iota(jnp.int32, (R * W, 3 * W), 0)
    cols = lax.broadcasted_iota(jnp.int32, (R * W, 3 * W), 1)
    kpos = (blk - 1) * W + cols
    qpos = blk * W + (rows & (W - 1))
    valid = jnp.logical_and(jnp.abs(kpos - qpos) <= W, jnp.logical_and(kpos >= 0, kpos < nb * W))
    row1 = lax.broadcasted_iota(jnp.int32, (R * W, 1), 0)
    q = q_ref[...]
    for g in range(WIN_KV_HEADS):
        ls = slice(g * HEAD_DIM, (g + 1) * HEAD_DIM)
        qs = jnp.concatenate([q[:, (g * R + r) * HEAD_DIM:(g * R + r + 1) * HEAD_DIM] for r in range(R)], axis=0)
        kl = jnp.concatenate([kp_ref[:, ls], kc_ref[:, ls], kn_ref[:, ls]], axis=0)
        vl = jnp.concatenate([vp_ref[:, ls], vc_ref[:, ls], vn_ref[:, ls]], axis=0)
        s_loc = jnp.where(valid, _dot_nt(qs, kl), NEG)
        s_ctx = _dot_nt(qs, kx_ref[:, ls])
        sink = jnp.full((R * W, 1), sink_ref[g * R], F32)
        for r in range(1, R):
            sink = jnp.where(row1 >= r * W, sink_ref[g * R + r], sink)
        m = jnp.maximum(jnp.maximum(jnp.max(s_loc, axis=-1, keepdims=True),
                                    jnp.max(s_ctx, axis=-1, keepdims=True)), sink)
        p_loc = jnp.exp(s_loc - m)
        p_ctx = jnp.exp(s_ctx - m)
        den = (jnp.sum(p_loc, axis=-1, keepdims=True) + jnp.sum(p_ctx, axis=-1, keepdims=True)
               + jnp.exp(sink - m))
        o = (_dot(p_loc.astype(BF16), vl) + _dot(p_ctx.astype(BF16), vx_ref[:, ls])) / den
        for r in range(R):
            hh = g * R + r
            o_ref[:, hh * HEAD_DIM:(hh + 1) * HEAD_DIM] = o[r * W:(r + 1) * W].astype(o_ref.dtype)


def _window_attention(sink, qw, kw, vw, L, with_ctx):
    n = qw.shape[0]
    W = WIN_BLOCK
    nb = L // W
    nq = n // W if with_ctx else nb
    ctx_blk = L // TOK_TILE
    lat = lambda b: jnp.minimum(b, nb - 1)
    kv = lambda f: pl.BlockSpec((W, D_WIN_KV), lambda b: (f(b), 0))
    prv = lambda b: jnp.maximum(lat(b) - 1, 0)
    nxt = lambda b: jnp.minimum(lat(b) + 1, nb - 1)
    ctx = pl.BlockSpec((n - L, D_WIN_KV), lambda b: (ctx_blk, 0))
    return pl.pallas_call(
        functools.partial(_win_kernel, nb=nb),
        out_shape=jax.ShapeDtypeStruct((nq * W, D_WIN_Q), BF16),
        grid=(nq,),
        in_specs=[pl.BlockSpec(memory_space=pltpu.SMEM),
                  pl.BlockSpec((W, D_WIN_Q), lambda b: (b, 0)),
                  kv(prv), kv(lat), kv(nxt), kv(prv), kv(lat), kv(nxt), ctx, ctx],
        out_specs=pl.BlockSpec((W, D_WIN_Q), lambda b: (b, 0)),
        compiler_params=_cparams("arbitrary"),
        name="window_attn",
    )(sink, qw, kw, kw, kw, vw, vw, vw, kw, vw)


def _na_kernel(q_ref, kl_ref, vl_ref, kx_ref, vx_ref, bias_ref, o_ref, *, n_blocks):
    b = pl.program_id(0)
    nq = q_ref.shape[0]
    latent = jnp.full((nq, 1), (b < n_blocks).astype(jnp.int32)) == 1
    q = q_ref[...]
    for h in range(NA_HEADS):
        ls = slice(h * HEAD_DIM, (h + 1) * HEAD_DIM)
        qh = q[:, ls]
        s_loc = jnp.where(latent, _dot_nt(qh, kl_ref[:, ls]) + bias_ref[0, h], NEG)
        s_ctx = _dot_nt(qh, kx_ref[:, ls])
        m = jnp.maximum(jnp.max(s_loc, axis=-1, keepdims=True), jnp.max(s_ctx, axis=-1, keepdims=True))
        p_loc = jnp.exp(s_loc - m)
        p_ctx = jnp.exp(s_ctx - m)
        den = jnp.sum(p_loc, axis=-1, keepdims=True) + jnp.sum(p_ctx, axis=-1, keepdims=True)
        o = (_dot(p_loc.astype(BF16), vl_ref[:, ls]) + _dot(p_ctx.astype(BF16), vx_ref[:, ls])) / den
        o_ref[:, ls] = o.astype(o_ref.dtype)


def _na_bias_table(rpb, rows):
    R, KR = NA_ROW_BLOCK, NA_KEY_ROWS
    qc = np.arange(GRID_W)[:, None]
    kc = np.arange(GRID_W)[None, :]
    qstart = np.clip(qc - NA_KW // 2, 0, GRID_W - NA_KW)
    inside = (kc - qstart >= 0) & (kc - qstart < NA_KW)
    pad = GRID_W - NA_KW
    rp = jnp.pad(rpb.astype(F32), ((0, 0), (0, 0), (pad, pad)))
    shifted = jnp.stack([rp[:, :, NA_KW - 1 - q + pad:NA_KW - 1 - q + pad + GRID_W] for q in range(GRID_W)], axis=2)
    per_dr = jnp.where(inside[None, None], shifted, NEG)
    masked = jnp.full((NA_HEADS, GRID_W, GRID_W), NEG, F32)
    half = NA_KH // 2
    cases = []
    for r0, ks in ((0, 0), (half, 0), (rows - R, rows - KR)):
        row_blocks = []
        for rr in range(R):
            r = r0 + rr
            kr = min(max(r - half, 0), rows - NA_KH)
            blocks = [per_dr[:, ks + kk - r + NA_KH - 1] if kr <= ks + kk < kr + NA_KH else masked
                      for kk in range(KR)]
            row_blocks.append(jnp.concatenate(blocks, axis=-1))
        cases.append(jnp.concatenate(row_blocks, axis=1))
    return jnp.stack(cases)


def _neighborhood_attention(qn, kn, vn, bias, L, with_ctx):
    n = qn.shape[0]
    R, KR = NA_ROW_BLOCK, NA_KEY_ROWS
    rows = L // GRID_W
    nb = rows // R
    nq = R * GRID_W
    steps = n // nq if with_ctx else nb
    ctx_blk = L // TOK_TILE
    half = NA_KH // 2

    def blk_of(b):
        return jnp.minimum(b, nb - 1)

    def key_start(b):
        return jnp.clip(blk_of(b) * R - half, 0, rows - KR) * GRID_W

    def case_of(b):
        return jnp.where(blk_of(b) == 0, 0, jnp.where(blk_of(b) == nb - 1, 2, 1))

    loc = pl.BlockSpec((pl.Element(KR * GRID_W), pl.Element(D_NA)), lambda b: (key_start(b), 0))
    ctx = pl.BlockSpec((n - L, D_NA), lambda b: (ctx_blk, 0))
    return pl.pallas_call(
        functools.partial(_na_kernel, n_blocks=nb),
        out_shape=jax.ShapeDtypeStruct((steps * nq, D_NA), BF16),
        grid=(steps,),
        in_specs=[pl.BlockSpec((nq, D_NA), lambda b: (b, 0)), loc, loc, ctx, ctx,
                  pl.BlockSpec((1, NA_HEADS, nq, KR * GRID_W), lambda b: (case_of(b), 0, 0, 0))],
        out_specs=pl.BlockSpec((nq, D_NA), lambda b: (b, 0)),
        compiler_params=_cparams("arbitrary"),
        name="neighborhood_attn",
    )(qn, kn, vn, kn, vn, bias)


def _out_kernel(xl_ref, xc_ref, hf_ref, hb_ref, ga_ref, yb_ref, yc_ref, w_ref, mod_ref, g_ref,
                xo_ref, ht_ref, *, n_lat):
    d = xl_ref.shape[1]
    i = pl.program_id(0)
    row = (i == n_lat).astype(jnp.int32)
    is_ctx = jnp.full((TOK_TILE, 1), row) == 1
    x = jnp.where(is_ctx, xc_ref[...], xl_ref[...])
    ya = ((hf_ref[...] + hb_ref[...]) * jax.nn.gelu(ga_ref[...])).astype(BF16)
    o1 = D_LRU
    o2 = D_LRU + D_WIN_Q
    mix = (_dot(ya, w_ref[0:o1]) + _dot(yb_ref[...], w_ref[o1:o2]) + _dot(yc_ref[...], w_ref[o2:o2 + D_NA]))
    xn = x + mod_ref[pl.ds(row, 1), 2 * d:3 * d] * mix
    xo_ref[...] = xn
    ms = jnp.mean(xn * xn, axis=-1, keepdims=True)
    y = xn * lax.rsqrt(ms + EPS) * g_ref[...]
    h2 = y * (1.0 + mod_ref[pl.ds(row, 1), 4 * d:5 * d]) + mod_ref[pl.ds(row, 1), 3 * d:4 * d]
    ht_ref[...] = h2.T.astype(BF16)


def _out_proj(xl, xc, hf, hb, ga, yb, yc, w_bf, mod, g, with_ctx):
    L, d = xl.shape
    T = TOK_TILE
    n_lat = L // T
    nt = n_lat + 1 if with_ctx else n_lat
    tok = lambda w: pl.BlockSpec((T, w), lambda i: (i, 0))
    full = lambda a: pl.BlockSpec(a.shape, lambda i: (0,) * a.ndim)
    return pl.pallas_call(
        functools.partial(_out_kernel, n_lat=n_lat),
        out_shape=[jax.ShapeDtypeStruct((nt * T, d), F32), jax.ShapeDtypeStruct((d, nt * T), BF16)],
        grid=(nt,),
        in_specs=[pl.BlockSpec((T, d), lambda i: (jnp.minimum(i, n_lat - 1), 0)),
                  pl.BlockSpec((T, d), lambda i: (0, 0)),
                  tok(D_LRU), tok(D_LRU), tok(D_LRU), tok(D_WIN_Q), tok(D_NA),
                  full(w_bf), full(mod), full(g)],
        out_specs=[tok(d), pl.BlockSpec((d, T), lambda i: (0, i))],
        compiler_params=_cparams("arbitrary"),
        name="out_proj",
    )(xl, xc, hf, hb, ga, yb, yc, w_bf, mod, g)


SUBLANES = 8


def _sorting_pairs(n):
    pairs, p = [], 1
    while p < n:
        k = p
        while k >= 1:
            for j in range(k % p, n - k, 2 * k):
                for i in range(min(k, n - j - k)):
                    if (i + j) // (2 * p) == (i + j + k) // (2 * p):
                        pairs.append((i + j, i + j + k))
            k //= 2
        p *= 2
    return pairs


def _vmax(a, b):
    if a is None:
        return b
    if b is None:
        return a
    return jnp.maximum(a, b)


def _vmin(a, b):
    if a is None or b is None:
        return None
    return jnp.minimum(a, b)


def _top16_sorted(slabs):
    K = PEER_TOPK
    w = list(slabs)
    for a, b in _sorting_pairs(K):
        w[a], w[b] = _vmax(w[a], w[b]), _vmin(w[a], w[b])
    shift = SUBLANES // 2
    while shift >= 1:
        partner = [None if w[K - 1 - i] is None else pltpu.roll(w[K - 1 - i], shift, 0) for i in range(K)]
        w = [_vmax(w[i], partner[i]) for i in range(K)]
        stride = K // 2
        while stride >= 1:
            for i in range(K):
                if i & stride == 0:
                    w[i], w[i + stride] = _vmax(w[i], w[i + stride]), _vmin(w[i], w[i + stride])
            stride //= 2
        shift //= 2
    return w


def _allsum8(x):
    x = x + pltpu.roll(x, 4, 0)
    x = x + pltpu.roll(x, 2, 0)
    return x + pltpu.roll(x, 1, 0)


def _route_tile(s1, s2):
    K = PEER_TOPK
    S = SUBLANES
    n_slab = s1.shape[0] // S
    a1 = [s1[j * S:(j + 1) * S] for j in range(n_slab)]
    a2 = [s2[j * S:(j + 1) * S] for j in range(n_slab)]
    v1 = _top16_sorted(a1)
    v2 = _top16_sorted(a2)
    sub = lax.broadcasted_iota(jnp.int32, a1[0].shape, 0)

    def as_rows(v, lo):
        out = v[lo]
        for b in range(1, S):
            out = jnp.where(sub == b, v[lo + b], out)
        return out

    v2_lo, v2_hi, v1_hi = as_rows(v2, 0), as_rows(v2, S), as_rows(v1, S)
    lens = [K // (a + 1) for a in range(S)]
    cands = [v1[0] + v2_lo, v1[0] + v2_hi]
    for a in range(1, S):
        ca = v1[a] + v2_lo
        cands.append(ca if lens[a] >= S else jnp.where(sub < lens[a], ca, -jnp.inf))
    cands.append(v1_hi + v2[0])
    thr = _top16_sorted(cands + [None] * (K - len(cands)))[K - 1]
    x2_lo, x2_hi = jnp.exp(v2_lo - v2[0]), jnp.exp(v2_hi - v2[0])
    sel_lo, sel_hi = cands[0] >= thr, cands[1] >= thr
    zsum = jnp.where(sel_lo, x2_lo, 0.0) + jnp.where(sel_hi, x2_hi, 0.0)
    cnt = [_allsum8(jnp.where(sel_lo, 1.0, 0.0) + jnp.where(sel_hi, 1.0, 0.0))]
    for a in range(1, S):
        sel = cands[a + 1] >= thr
        zsum = zsum + jnp.where(sel, jnp.exp(v1[a] - v1[0]) * x2_lo, 0.0)
        cnt.append(_allsum8(jnp.where(sel, 1.0, 0.0)))
    zsum = zsum + jnp.where(cands[S + 1] >= thr, jnp.exp(v1_hi - v1[0]), 0.0)
    inv_z = 1.0 / _allsum8(zsum)
    r2, e2, c1, e1, pair0 = [], [], [], [], []
    for j in range(n_slab):
        r = jnp.full(a2[j].shape, float(K), F32)
        for k in reversed(range(K)):
            r = jnp.where(a2[j] >= v2[k], float(k), r)
        r2.append(r)
        e2.append(jnp.exp(a2[j] - v2[0]) * inv_z)
        c = jnp.where(a1[j] + v2[0] >= thr, 1.0, 0.0)
        pair0.append(c)
        for a in reversed(range(S)):
            c = jnp.where(a1[j] >= v1[a], cnt[a], c)
        c1.append(c)
        e1.append(jnp.exp(a1[j] - v1[0]))
    lor, land = jnp.logical_or, jnp.logical_and
    n_sel = functools.reduce(lambda a, b: a + b, cnt) + _allsum8(jnp.where(cands[S + 1] >= thr, 1.0, 0.0))
    f1 = land(v1[S - 1] == v1[S], cnt[S - 1] >= 2.0)
    for k in range(S - 1):
        f1 = lor(f1, land(v1[k] == v1[k + 1], cnt[k] != cnt[k + 1]))
    f1 = lor(f1, _allsum8(functools.reduce(lambda a, b: a + b, pair0)) > float(K))
    f2 = v2[0] == v2[1]
    for k in range(1, K - 1):
        f2 = lor(f2, land(v2[k] == v2[k + 1], cnt[0] > float(k)))
    r_sum = _allsum8(functools.reduce(lambda a, b: a + b, r2))
    any_tie2 = r_sum != float(K * (K - 1) // 2 + K * (n_slab * S - K))
    f2 = lor(f2, land(cnt[0] == float(K), any_tie2))
    flag = lor(lor(f1, f2), n_sel != float(K))
    cat = lambda xs: jnp.concatenate(xs, axis=0)
    return cat(r2), cat(e2), cat(c1), cat(e1), jnp.where(flag, 1.0, 0.0)


def _route_tile_exact(s1, s2):
    K = PEER_TOPK

    def take16(x):
        idx = lax.broadcasted_iota(jnp.int32, x.shape, 0).astype(F32)
        rank = jnp.full(x.shape, float(K), F32)
        vals, cur = [], x
        for k in range(K):
            m = jnp.max(cur, axis=0, keepdims=True)
            first = jnp.min(jnp.where(cur == m, idx, float(x.shape[0])), axis=0, keepdims=True)
            hit = idx == first
            rank = jnp.where(hit, float(k), rank)
            vals.append(m)
            cur = jnp.where(hit, -jnp.inf, cur)
        return rank, vals

    r1, v1 = take16(s1)
    r2, v2 = take16(s2)
    v2a = jnp.concatenate(v2, axis=0)
    x1 = [jnp.exp(v1[a] - v1[0]) for a in range(K)]
    x2a = jnp.exp(v2a - v2[0])
    cand = jnp.concatenate([v1[a] + v2a for a in range(K)], axis=0)
    crank, _ = take16(cand)
    sel = crank < float(K)
    z = jnp.zeros_like(v1[0])
    c1 = jnp.zeros(s1.shape, F32)
    for a in range(K):
        sel_a = sel[a * K:(a + 1) * K]
        z = z + jnp.sum(jnp.where(sel_a, x1[a] * x2a, 0.0), axis=0, keepdims=True)
        cnt_a = jnp.sum(jnp.where(sel_a, 1.0, 0.0), axis=0, keepdims=True)
        c1 = jnp.where(r1 == float(a), cnt_a, c1)
    return r2, jnp.exp(s2 - v2[0]) / z, c1, jnp.exp(s1 - v1[0])


def _prep_kernel(ht_ref, wq_ref, k1_ref, k2_ref, r2_ref, e2_ref, c1_ref, e1_ref, q_scr):
    nk = PEER_NKEYS
    q_scr[...] = _dot(wq_ref[...], ht_ref[...])

    def scores(h):
        base = pl.multiple_of(h * 2 * nk, 2 * nk)
        s1 = _dot(k1_ref[...], q_scr[pl.ds(base, nk), :].astype(BF16))
        s2 = _dot(k2_ref[...], q_scr[pl.ds(base + nk, nk), :].astype(BF16))
        return s1, s2

    def store(h, ts, r2, e2, c1, e1):
        r2_ref[h, :, ts] = _pack_rows(r2)
        e2_ref[h, :, ts] = _pack_rows(e2)
        c1_ref[h, :, ts] = c1
        e1_ref[h, :, ts] = e1

    def head(h, flags):
        s1, s2 = scores(h)
        for t in range(s1.shape[1] // LANES):
            ts = slice(t * LANES, (t + 1) * LANES)
            r2, e2, c1, e1, flag = _route_tile(s1[:, ts], s2[:, ts])
            store(h, ts, r2, e2, c1, e1)
            flags = jnp.maximum(flags, flag)
        return flags

    flags = lax.fori_loop(0, PEER_HEADS, head, jnp.zeros((SUBLANES, LANES), F32))

    @pl.when(jnp.max(flags) > 0.0)
    def _():
        def head_exact(h, carry):
            s1, s2 = scores(h)
            for t in range(s1.shape[1] // LANES):
                ts = slice(t * LANES, (t + 1) * LANES)
                store(h, ts, *_route_tile_exact(s1[:, ts], s2[:, ts]))
            return carry

        lax.fori_loop(0, PEER_HEADS, head_exact, 0)


def _peer_prep(ht, wq_t, k1, k2, tok_tile, tok_off, n):
    d = ht.shape[0]
    T = tok_tile
    nk = PEER_NKEYS
    full = lambda a: pl.BlockSpec(a.shape, lambda i: (0,) * a.ndim)
    spec = lambda rows: pl.BlockSpec((PEER_HEADS, rows, T), lambda i: (0, 0, i))
    return pl.pallas_call(
        _prep_kernel,
        out_shape=[jax.ShapeDtypeStruct((PEER_HEADS, nk // 2, n), U32)] * 2
        + [jax.ShapeDtypeStruct((PEER_HEADS, nk, n), F32)] * 2,
        grid=(n // T,),
        in_specs=[pl.BlockSpec((d, T), lambda i: (0, i + tok_off)), full(wq_t), full(k1), full(k2)],
        out_specs=[spec(nk // 2), spec(nk // 2), spec(nk), spec(nk)],
        scratch_shapes=[pltpu.VMEM((wq_t.shape[0], T), F32)],
        compiler_params=_cparams("arbitrary"),
        name="peer_prep",
    )(ht, wq_t, k1, k2)


def _peer_kernel(ht_ref, u_ref, un_ref, vt_ref, vp_ref, r2_ref, e2_ref, c1_ref, e1_ref, x_ref, g_ref,
                 o_ref, acc_ref, *ap_refs):
    nk = PEER_NKEYS
    c = pl.program_id(1)
    n_sub = len(ap_refs) // 2
    a_refs, p_refs = ap_refs[:n_sub], ap_refs[n_sub:]
    sub = 2 * a_refs[0].shape[0]
    T = a_refs[0].shape[1]
    rows_per_sub = sub // nk
    PK = 16
    WK = PK // 2

    tok_blocks = [slice(n, min(n + PEER_MXU_TOKENS, T)) for n in range(0, T, PEER_MXU_TOKENS)]

    def activations_from(rows_ref, dst_ref):
        for tb in tok_blocks:
            dst_ref[:, tb] = _pack_rows(_dot(rows_ref, ht_ref[:, tb]))

    def activations(s):
        activations_from(u_ref[s * sub:(s + 1) * sub, :], a_refs[s])

    def gates(s):
        for t in range(T // LANES):
            ts = slice(t * LANES, (t + 1) * LANES)
            for j in range(rows_per_sub):
                row = s * rows_per_sub + j
                gate = [None] * (nk // PK)
                for h in range(PEER_HEADS):
                    c1 = c1_ref[h, :, ts][row:row + 1]
                    e1 = e1_ref[h, :, ts][row:row + 1]
                    c1 = jnp.broadcast_to(c1, (PK, LANES)).astype(BF16)
                    e1 = jnp.broadcast_to(e1, (PK, LANES)).astype(BF16)
                    for v in range(nk // PK):
                        ws = slice(v * WK, (v + 1) * WK)
                        r2 = _unpack_rows(r2_ref[h, ws, ts])
                        e2 = _unpack_rows(e2_ref[h, ws, ts])
                        g = jnp.where(r2 < c1, e2 * e1, 0.0)
                        gate[v] = g if h == 0 else gate[v] + g
                for v in range(nk // PK):
                    ws = slice((j * nk + v * PK) // 2, (j * nk + (v + 1) * PK) // 2)
                    a = _unpack_rows(a_refs[s][ws, ts])
                    th = jnp.tanh(a * (GELU_K0 + GELU_K1 * (a * a)))
                    p_refs[s][ws, ts] = pltpu.bitcast(gate[v] * (a * (0.5 + 0.5 * th)), U32)

    def project(terms):
        for tb in tok_blocks:
            parts = [_dot(cols, _unpack_rows(src[:, tb])) for cols, src in terms]
            acc_ref[:, tb] += functools.reduce(lambda a, b: a + b, parts)

    def term(s):
        return vt_ref[:, s * sub:(s + 1) * sub], p_refs[s]

    last = n_sub - 1
    assert n_sub % 2 == 0

    @pl.when(c == 0)
    def _():
        acc_ref[...] = jnp.zeros(acc_ref.shape, F32)
        p_refs[last][...] = jnp.zeros(p_refs[last].shape, U32)
        activations(0)

    pending = [(vp_ref[...], p_refs[last])]
    for s in range(n_sub):
        gates(s)
        if s < last:
            pending.append(term(s))
            activations(s + 1)
        if len(pending) == 2:
            project(pending)
            pending = []
    activations_from(un_ref[...], a_refs[0])

    @pl.when(c == pl.num_programs(1) - 1)
    def _():
        project([term(last)])
        o_ref[...] = x_ref[...] + g_ref[...] * acc_ref[...].T


def _peer_dense(ht, u_bf, vt_bf, routing, x, g2, tok_tile, tok_off, n_tok):
    r2, e2, c1, e1 = routing
    d = ht.shape[0]
    n_exp = u_bf.shape[0]
    T = tok_tile
    NC = PEER_EXPERT_CHUNK
    nk = PEER_NKEYS
    SUB = PEER_SUB_CHUNK
    n_sub = NC // SUB
    rt = lambda rows: pl.BlockSpec((PEER_HEADS, rows, T), lambda i, c: (0, 0, i))
    per_chunk = pl.BlockSpec((PEER_HEADS, NC // nk, T), lambda i, c: (0, c, i))
    return pl.pallas_call(
        _peer_kernel,
        out_shape=jax.ShapeDtypeStruct((n_tok, d), F32),
        grid=(n_tok // T, n_exp // NC),
        in_specs=[pl.BlockSpec((d, T), lambda i, c: (0, i + tok_off)),
                  pl.BlockSpec((NC, d), lambda i, c: (c, 0)),
                  pl.BlockSpec((SUB, d), lambda i, c: (jnp.minimum((c + 1) * n_sub, n_exp // SUB - 1), 0)),
                  pl.BlockSpec((d, NC), lambda i, c: (0, c)),
                  pl.BlockSpec((d, SUB), lambda i, c: (0, jnp.maximum(c * n_sub - 1, 0))),
                  rt(nk // 2), rt(nk // 2), per_chunk, per_chunk,
                  pl.BlockSpec((T, d), lambda i, c: (i + tok_off, 0)),
                  pl.BlockSpec((1, d), lambda i, c: (0, 0))],
        out_specs=pl.BlockSpec((T, d), lambda i, c: (i, 0)),
        scratch_shapes=[pltpu.VMEM((d, T), F32)] + [pltpu.VMEM((PEER_SUB_CHUNK // 2, T), U32)] * (2 * n_sub),
        compiler_params=_cparams("arbitrary", "arbitrary"),
        name="peer_dense",
    )(ht, u_bf, u_bf, vt_bf, vt_bf, r2, e2, c1, e1, x, g2)


def _rope_tables(L, n):
    t = jnp.arange(L)
    row = (t // GRID_W).astype(F32)
    col = (t % GRID_W).astype(F32)
    q = HEAD_DIM // 4
    inv = ROPE_THETA ** (-jnp.arange(q, dtype=F32) / q)
    ar = row[:, None] * inv
    ac = col[:, None] * inv
    cos = jnp.concatenate([jnp.cos(ar), jnp.cos(ar), jnp.cos(ac), jnp.cos(ac)], axis=-1)
    sin = jnp.concatenate([-jnp.sin(ar), jnp.sin(ar), -jnp.sin(ac), jnp.sin(ac)], axis=-1)
    cos = jnp.concatenate([cos, jnp.ones((n - L, HEAD_DIM), F32)], axis=0)
    sin = jnp.concatenate([sin, jnp.zeros((n - L, HEAD_DIM), F32)], axis=0)
    return jnp.tile(cos, (1, LANES // HEAD_DIM)), jnp.tile(sin, (1, LANES // HEAD_DIM))


def _block_diag(w):
    nb, di, do = w.shape
    eye = jnp.eye(nb, dtype=w.dtype)
    return (eye[:, None, :, None] * w[:, :, None, :]).reshape(nb * di, nb * do)


def kernel(x, c, ctx, c_ctx, w_mod, b_mod, norm1_g, norm2_g, w_in, w_out, lru_conv_w, lru_conv_b, lru_wa, lru_ba, lru_wx, lru_bx, lru_lam, win_qn_g, win_kn_g, win_sink, na_qn_g, na_kn_g, na_rpb, peer_wq, peer_k1, peer_k2, peer_u, peer_v):
    B, L, D = x.shape
    Lc = ctx.shape[1]
    depth = w_mod.shape[0]
    assert B == 1 and Lc == TOK_TILE and L % PREP_TOK_TILE == 0 and L // GRID_W >= 3 * NA_ROW_BLOCK
    n = L + Lc
    n_lat = L // TOK_TILE
    rows = L // GRID_W

    cvec = jnp.zeros((8, D), F32).at[0].set(c[0]).at[1].set(c_ctx)
    mods = _modulation(cvec, w_mod, b_mod)
    cos_t, sin_t = _rope_tables(L, n)
    tile2 = lambda g: jnp.tile(g, LANES // HEAD_DIM)

    xl, xc = x[0], ctx[0]
    for l in range(depth):
        with_ctx = l < depth - 1
        mod = mods[l]
        hg = jnp.zeros((8, LANES), F32)
        hg = hg.at[0].set(tile2(win_qn_g[l])).at[1].set(tile2(win_kn_g[l]))
        hg = hg.at[2].set(tile2(na_qn_g[l])).at[3].set(tile2(na_kn_g[l]))
        xa, ga, qw, kw, vw, qn, kn, vn = _in_proj(xl, xc, mod, norm1_g[l][None], w_in[l].astype(BF16),
                                                  cos_t, sin_t, hg)

        w_gates = jnp.concatenate([_block_diag(lru_wa[l, 0]), _block_diag(lru_wx[l, 0]),
                                   _block_diag(lru_wa[l, 1]), _block_diag(lru_wx[l, 1])], axis=1).astype(BF16)
        b_gates = jnp.concatenate([lru_ba[l, 0], lru_bx[l, 0], lru_ba[l, 1], lru_bx[l, 1]])[None]
        conv_w = jnp.zeros((8, D_LRU), F32).at[0:lru_conv_w.shape[1]].set(lru_conv_w[l])
        lam = jnp.zeros((8, D_LRU), F32).at[0:2].set(lru_lam[l])
        hf, hb = _lru(xa, conv_w, lru_conv_b[l][None], w_gates, b_gates, lam, n_lat)

        yb = _window_attention(win_sink[l], qw, kw, vw, L, with_ctx)
        yc = _neighborhood_attention(qn, kn, vn, _na_bias_table(na_rpb[l], rows), L, with_ctx)

        xn, ht = _out_proj(xl, xc, hf, hb, ga, yb, yc, w_out[l].astype(BF16), mod, norm2_g[l][None], with_ctx)

        route_w = (peer_wq[l].T.astype(BF16), peer_k1[l].astype(BF16), peer_k2[l].astype(BF16))
        u_bf = peer_u[l].astype(BF16)
        vt_bf = peer_v[l].T.astype(BF16)
        g2 = mod[:, 5 * D:6 * D]
        routing = _peer_prep(ht, *route_w, PREP_TOK_TILE, 0, L)
        xl_new = _peer_dense(ht, u_bf, vt_bf, routing, xn, g2[0:1], PEER_TOK_TILE, 0, L)
        if with_ctx:
            routing = _peer_prep(ht, *route_w, Lc, L // Lc, Lc)
            xc = _peer_dense(ht, u_bf, vt_bf, routing, xn, g2[1:2], Lc, L // Lc, Lc)
        xl = xl_new
    return xl[None]
```

```python
import functools

import numpy as np
import jax
import jax.numpy as jnp
from jax import lax
from jax.experimental import pallas as pl
from jax.experimental.pallas import tpu as pltpu

F32 = jnp.float32
BF16 = jnp.bfloat16
U32 = jnp.uint32

HEAD_DIM = 64
GRID_W = 64
EPS = 1e-6
ROPE_THETA = 10000.0
D_LRU = 256
LRU_BLOCKS = 4
LRU_C = 8.0
WIN_Q_HEADS = 6
WIN_KV_HEADS = 2
WIN_BLOCK = 128
NA_HEADS = 6
NA_KH = 8
NA_KW = 16
NA_ROW_BLOCK = 4
NA_KEY_ROWS = NA_ROW_BLOCK + NA_KH - 1
PEER_HEADS = 8
PEER_NKEYS = 128
PEER_TOPK = 16
D_WIN_Q = WIN_Q_HEADS * HEAD_DIM
D_WIN_KV = WIN_KV_HEADS * HEAD_DIM
D_NA = NA_HEADS * HEAD_DIM

LANES = 128
TOK_TILE = 256
PEER_TOK_TILE = 512
PEER_EXPERT_CHUNK = 2048
PEER_SUB_CHUNK = 512
PEER_MXU_TOKENS = 256
PREP_TOK_TILE = 1024
VMEM_LIMIT = 56 * 1024 * 1024
NEG = -1e30
GELU_K0 = float(np.sqrt(2.0 / np.pi))
GELU_K1 = 0.044715 * GELU_K0


def _cparams(*sem):
    return pltpu.CompilerParams(dimension_semantics=sem, vmem_limit_bytes=VMEM_LIMIT)


def _dot(a, b):
    return jnp.dot(a, b, preferred_element_type=F32)


def _pack_rows(x):
    return pltpu.bitcast(x.astype(BF16), U32)


def _unpack_rows(w):
    return pltpu.bitcast(w, BF16)


def _dot_nt(a, b):
    return lax.dot_general(a, b, (((1,), (1,)), ((), ())), preferred_element_type=F32)


def _mod_kernel(c_ref, w_ref, b_ref, o_ref):
    c = c_ref[...]
    a = (c * jax.nn.sigmoid(c)).astype(BF16)
    o_ref[0] = _dot(a, w_ref[0].astype(BF16)) + b_ref[0]


def _modulation(cvec, w_mod, b_mod):
    depth, d, d6 = w_mod.shape
    return pl.pallas_call(
        _mod_kernel,
        out_shape=jax.ShapeDtypeStruct((depth, 8, d6), F32),
        grid=(depth, d6 // d),
        in_specs=[pl.BlockSpec((8, d), lambda l, j: (0, 0)),
                  pl.BlockSpec((1, d, d), lambda l, j: (l, 0, j)),
                  pl.BlockSpec((1, 1, d), lambda l, j: (l, 0, j))],
        out_specs=pl.BlockSpec((1, 8, d), lambda l, j: (l, 0, j)),
        compiler_params=_cparams("arbitrary", "arbitrary"),
        name="modulation",
    )(cvec, w_mod, b_mod.reshape(depth, 1, d6))


def _pair_mean_matrix():
    r = lax.broadcasted_iota(jnp.int32, (LANES, LANES), 0) < HEAD_DIM
    c = lax.broadcasted_iota(jnp.int32, (LANES, LANES), 1) < HEAD_DIM
    return jnp.where(r == c, 1.0 / HEAD_DIM, 0.0).astype(BF16)


def _head_norm(z, gain, bd):
    z2 = z * z
    hi = z2.astype(BF16)
    lo = (z2 - hi.astype(F32)).astype(BF16)
    ms = _dot(hi, bd) + _dot(lo, bd)
    return z * lax.rsqrt(ms + EPS) * gain


def _rope(z, cos, sin):
    lane = lax.broadcasted_iota(jnp.int32, z.shape, 1)
    first = (lane & 16) == 0
    partner = jnp.where(first, pltpu.roll(z, LANES - 16, 1), pltpu.roll(z, 16, 1))
    return z * cos + partner * sin


def _in_kernel(xl_ref, xc_ref, mod_ref, g_ref, w_ref, cos_ref, sin_ref, hg_ref,
               xa_ref, ga_ref, qw_ref, kw_ref, vw_ref, qn_ref, kn_ref, vn_ref, *, n_lat):
    d = xl_ref.shape[1]
    i = pl.program_id(0)
    row = (i == n_lat).astype(jnp.int32)
    is_ctx = jnp.full((TOK_TILE, 1), row) == 1
    x = jnp.where(is_ctx, xc_ref[...], xl_ref[...])
    ms = jnp.mean(x * x, axis=-1, keepdims=True)
    y = x * lax.rsqrt(ms + EPS) * g_ref[...]
    sh = mod_ref[pl.ds(row, 1), 0:d]
    sc = mod_ref[pl.ds(row, 1), d:2 * d]
    h = (y * (1.0 + sc) + sh).astype(BF16)
    z = _dot(h, w_ref[...])

    bd = _pair_mean_matrix()
    cos = cos_ref[...]
    sin = sin_ref[...]
    scale = HEAD_DIM ** -0.5
    o = 0
    xa_ref[...] = z[:, o:o + D_LRU]
    o += D_LRU
    ga_ref[...] = z[:, o:o + D_LRU]
    o += D_LRU
    for g in range(D_WIN_Q // LANES):
        zz = _rope(_head_norm(z[:, o:o + LANES], hg_ref[0:1], bd), cos, sin)
        qw_ref[:, g * LANES:(g + 1) * LANES] = (zz * scale).astype(BF16)
        o += LANES
    kw_ref[...] = _rope(_head_norm(z[:, o:o + LANES], hg_ref[1:2], bd), cos, sin).astype(BF16)
    o += LANES
    vw_ref[...] = z[:, o:o + LANES].astype(BF16)
    o += LANES
    for g in range(D_NA // LANES):
        zz = _head_norm(z[:, o:o + LANES], hg_ref[2:3], bd)
        qn_ref[:, g * LANES:(g + 1) * LANES] = (zz * scale).astype(BF16)
        o += LANES
    for g in range(D_NA // LANES):
        zz = _head_norm(z[:, o:o + LANES], hg_ref[3:4], bd)
        kn_ref[:, g * LANES:(g + 1) * LANES] = zz.astype(BF16)
        o += LANES
    vn_ref[...] = z[:, o:o + D_NA].astype(BF16)


def _in_proj(xl, xc, mod, g, w_bf, cos_t, sin_t, hg):
    L, d = xl.shape
    n_lat = L // TOK_TILE
    n = L + xc.shape[0]
    d_in = w_bf.shape[1]
    T = TOK_TILE
    tok = lambda w: pl.BlockSpec((T, w), lambda i: (i, 0))
    full = lambda a: pl.BlockSpec(a.shape, lambda i: (0,) * a.ndim)
    widths = (D_LRU, D_LRU, D_WIN_Q, D_WIN_KV, D_WIN_KV, D_NA, D_NA, D_NA)
    dtypes = (F32, F32, BF16, BF16, BF16, BF16, BF16, BF16)
    return pl.pallas_call(
        functools.partial(_in_kernel, n_lat=n_lat),
        out_shape=[jax.ShapeDtypeStruct((n, w), t) for w, t in zip(widths, dtypes)],
        grid=(n_lat + 1,),
        in_specs=[pl.BlockSpec((T, d), lambda i: (jnp.minimum(i, n_lat - 1), 0)),
                  pl.BlockSpec((T, d), lambda i: (0, 0)),
                  full(mod), full(g), full(w_bf), tok(LANES), tok(LANES), full(hg)],
        out_specs=[tok(w) for w in widths],
        compiler_params=_cparams("arbitrary"),
        name="in_proj",
    )(xl, xc, mod, g, w_bf, cos_t, sin_t, hg)


def _chunk_scan(a, b, reverse):
    T = a.shape[0]
    rows = lax.broadcasted_iota(jnp.int32, a.shape, 0)
    s = 1
    while s < T:
        if reverse:
            edge = rows >= T - s
            shift = T - s
        else:
            edge = rows < s
            shift = s
        a_s = jnp.where(edge, 1.0, pltpu.roll(a, shift, 0))
        b_s = jnp.where(edge, 0.0, pltpu.roll(b, shift, 0))
        b = a * b_s + b
        a = a * a_s
        s *= 2
    return a, b


def _lru_direction(xm, ph, nh, pv, nv, cw, cb, w, bias, spl, carry_ref, first, reverse):
    T, C = xm.shape
    rows = lax.broadcasted_iota(jnp.int32, (T, C), 0)
    p6 = ph[6:7] * pv
    p7 = ph[7:8] * pv
    n0 = nh[0:1] * nv
    x_m1 = jnp.where(rows == 0, p7, pltpu.roll(xm, 1, 0))
    x_m2 = jnp.where(rows == 0, p6, jnp.where(rows == 1, p7, pltpu.roll(xm, 2, 0)))
    x_p1 = jnp.where(rows == T - 1, n0, pltpu.roll(xm, T - 1, 0))
    u = cw[0:1] * x_m2 + cw[1:2] * x_m1 + cw[2:3] * xm + cw[3:4] * x_p1 + cb
    zz = _dot(u.astype(BF16), w) + bias
    r = jax.nn.sigmoid(zz[:, :C])
    ig = jax.nn.sigmoid(zz[:, C:])
    log_a = -LRU_C * r * spl
    a = jnp.exp(log_a)
    b = jnp.sqrt(-jnp.tanh(log_a) * (a * a + 1.0)) * ig * u
    a, b = _chunk_scan(a, b, reverse)

    @pl.when(first)
    def _():
        carry_ref[...] = jnp.zeros(carry_ref.shape, F32)

    h = a * carry_ref[0:1] + b
    edge = h[0:1] if reverse else h[T - 1:T]
    carry_ref[...] = jnp.broadcast_to(edge, carry_ref.shape)
    return h


def _lru_kernel(xf_ref, pf_ref, nf_ref, xb_ref, pb_ref, nb_ref, cw_ref, cb_ref, w_ref, b_ref,
                lam_ref, hf_ref, hb_ref, cf_ref, cbk_ref, *, n_lat):
    j = pl.program_id(0)
    C = D_LRU
    fblk = jnp.where(j == 0, n_lat, j - 1)
    bblk = jnp.where(j == 0, n_lat, n_lat - j)
    lam = lam_ref[...]
    spl = jnp.maximum(-lam, 0.0) + jnp.log1p(jnp.exp(-jnp.abs(lam)))
    cw = cw_ref[...]
    cb = cb_ref[...]

    def halo_valid(blk):
        pv = jnp.logical_and(blk != 0, blk != n_lat).astype(F32)
        nv = jnp.logical_and(blk != n_lat - 1, blk != n_lat).astype(F32)
        return pv, nv

    pv, nv = halo_valid(fblk)
    hf_ref[...] = _lru_direction(xf_ref[...], pf_ref[...], nf_ref[...], pv, nv, cw, cb,
                                 w_ref[:, 0:2 * C], b_ref[:, 0:2 * C], spl[0:1], cf_ref, j == 0, False)
    pv, nv = halo_valid(bblk)
    hb_ref[...] = _lru_direction(xb_ref[...], pb_ref[...], nb_ref[...], pv, nv, cw, cb,
                                 w_ref[:, 2 * C:4 * C], b_ref[:, 2 * C:4 * C], spl[1:2], cbk_ref, j == 0, True)


def _lru(xa, conv_w, conv_b, w_gates, b_gates, lam, n_lat):
    n, C = xa.shape
    T = TOK_TILE
    sub = T // 8
    nblk8 = n // 8
    fblk = lambda j: jnp.where(j == 0, n_lat, j - 1)
    bblk = lambda j: jnp.where(j == 0, n_lat, n_lat - j)
    prev8 = lambda blk: jnp.maximum(blk * sub - 1, 0)
    next8 = lambda blk: jnp.minimum((blk + 1) * sub, nblk8 - 1)
    full = lambda a: pl.BlockSpec(a.shape, lambda j: (0,) * a.ndim)
    return pl.pallas_call(
        functools.partial(_lru_kernel, n_lat=n_lat),
        out_shape=[jax.ShapeDtypeStruct((n, C), F32)] * 2,
        grid=(n_lat + 1,),
        in_specs=[pl.BlockSpec((T, C), lambda j: (fblk(j), 0)),
                  pl.BlockSpec((8, C), lambda j: (prev8(fblk(j)), 0)),
                  pl.BlockSpec((8, C), lambda j: (next8(fblk(j)), 0)),
                  pl.BlockSpec((T, C), lambda j: (bblk(j), 0)),
                  pl.BlockSpec((8, C), lambda j: (prev8(bblk(j)), 0)),
                  pl.BlockSpec((8, C), lambda j: (next8(bblk(j)), 0)),
                  full(conv_w), full(conv_b), full(w_gates), full(b_gates), full(lam)],
        out_specs=[pl.BlockSpec((T, C), lambda j: (fblk(j), 0)),
                   pl.BlockSpec((T, C), lambda j: (bblk(j), 0))],
        scratch_shapes=[pltpu.VMEM((8, C), F32), pltpu.VMEM((8, C), F32)],
        compiler_params=_cparams("arbitrary"),
        name="rglru",
    )(xa, xa, xa, xa, xa, xa, conv_w, conv_b, w_gates, b_gates, lam)


def _win_kernel(sink_ref, q_ref, kp_ref, kc_ref, kn_ref, vp_ref, vc_ref, vn_ref, kx_ref, vx_ref,
                o_ref, *, nb):
    W = WIN_BLOCK
    R = WIN_Q_HEADS // WIN_KV_HEADS
    b = pl.program_id(0)
    blk = jnp.where(b < nb, b, -4)
    rows = lax.broadcasted_iota(jnp.int32, (R * W, 3 * W), 0)
    cols = lax.broadcasted_iota(jnp.int32, (R * W, 3 * W), 1)
    kpos = (blk - 1) * W + cols
    qpos = blk * W + (rows & (W - 1))
    valid = jnp.logical_and(jnp.abs(kpos - qpos) <= W, jnp.logical_and(kpos >= 0, kpos < nb * W))
    row1 = lax.broadcasted_iota(jnp.int32, (R * W, 1), 0)
    q = q_ref[...]
    for g in range(WIN_KV_HEADS):
        ls = slice(g * HEAD_DIM, (g + 1) * HEAD_DIM)
        qs = jnp.concatenate([q[:, (g * R + r) * HEAD_DIM:(g * R + r + 1) * HEAD_DIM] for r in range(R)], axis=0)
        kl = jnp.concatenate([kp_ref[:, ls], kc_ref[:, ls], kn_ref[:, ls]], axis=0)
        vl = jnp.concatenate([vp_ref[:, ls], vc_ref[:, ls], vn_ref[:, ls]], axis=0)
        s_loc = jnp.where(valid, _dot_nt(qs, kl), NEG)
        s_ctx = _dot_nt(qs, kx_ref[:, ls])
        sink = jnp.full((R * W, 1), sink_ref[g * R], F32)
        for r in range(1, R):
            sink = jnp.where(row1 >= r * W, sink_ref[g * R + r], sink)
        m = jnp.maximum(jnp.maximum(jnp.max(s_loc, axis=-1, keepdims=True),
                                    jnp.max(s_ctx, axis=-1, keepdims=True)), sink)
        p_loc = jnp.exp(s_loc - m)
        p_ctx = jnp.exp(s_ctx - m)
        den = (jnp.sum(p_loc, axis=-1, keepdims=True) + jnp.sum(p_ctx, axis=-1, keepdims=True)
               + jnp.exp(sink - m))
        o = (_dot(p_loc.astype(BF16), vl) + _dot(p_ctx.astype(BF16), vx_ref[:, ls])) / den
        for r in range(R):
            hh = g * R + r
            o_ref[:, hh * HEAD_DIM:(hh + 1) * HEAD_DIM] = o[r * W:(r + 1) * W].astype(o_ref.dtype)


def _window_attention(sink, qw, kw, vw, L, with_ctx):
    n = qw.shape[0]
    W = WIN_BLOCK
    nb = L // W
    nq = n // W if with_ctx else nb
    ctx_blk = L // TOK_TILE
    lat = lambda b: jnp.minimum(b, nb - 1)
    kv = lambda f: pl.BlockSpec((W, D_WIN_KV), lambda b: (f(b), 0))
    prv = lambda b: jnp.maximum(lat(b) - 1, 0)
    nxt = lambda b: jnp.minimum(lat(b) + 1, nb - 1)
    ctx = pl.BlockSpec((n - L, D_WIN_KV), lambda b: (ctx_blk, 0))
    return pl.pallas_call(
        functools.partial(_win_kernel, nb=nb),
        out_shape=jax.ShapeDtypeStruct((nq * W, D_WIN_Q), BF16),
        grid=(nq,),
        in_specs=[pl.BlockSpec(memory_space=pltpu.SMEM),
                  pl.BlockSpec((W, D_WIN_Q), lambda b: (b, 0)),
                  kv(prv), kv(lat), kv(nxt), kv(prv), kv(lat), kv(nxt), ctx, ctx],
        out_specs=pl.BlockSpec((W, D_WIN_Q), lambda b: (b, 0)),
        compiler_params=_cparams("arbitrary"),
        name="window_attn",
    )(sink, qw, kw, kw, kw, vw, vw, vw, kw, vw)


def _na_kernel(q_ref, kl_ref, vl_ref, kx_ref, vx_ref, bias_ref, o_ref, *, n_blocks):
    b = pl.program_id(0)
    nq = q_ref.shape[0]
    latent = jnp.full((nq, 1), (b < n_blocks).astype(jnp.int32)) == 1
    q = q_ref[...]
    for h in range(NA_HEADS):
        ls = slice(h * HEAD_DIM, (h + 1) * HEAD_DIM)
        qh = q[:, ls]
        s_loc = jnp.where(latent, _dot_nt(qh, kl_ref[:, ls]) + bias_ref[0, h], NEG)
        s_ctx = _dot_nt(qh, kx_ref[:, ls])
        m = jnp.maximum(jnp.max(s_loc, axis=-1, keepdims=True), jnp.max(s_ctx, axis=-1, keepdims=True))
        p_loc = jnp.exp(s_loc - m)
        p_ctx = jnp.exp(s_ctx - m)
        den = jnp.sum(p_loc, axis=-1, keepdims=True) + jnp.sum(p_ctx, axis=-1, keepdims=True)
        o = (_dot(p_loc.astype(BF16), vl_ref[:, ls]) + _dot(p_ctx.astype(BF16), vx_ref[:, ls])) / den
        o_ref[:, ls] = o.astype(o_ref.dtype)


def _na_bias_table(rpb, rows):
    R, KR = NA_ROW_BLOCK, NA_KEY_ROWS
    qc = np.arange(GRID_W)[:, None]
    kc = np.arange(GRID_W)[None, :]
    qstart = np.clip(qc - NA_KW // 2, 0, GRID_W - NA_KW)
    inside = (kc - qstart >= 0) & (kc - qstart < NA_KW)
    pad = GRID_W - NA_KW
    rp = jnp.pad(rpb.astype(F32), ((0, 0), (0, 0), (pad, pad)))
    shifted = jnp.stack([rp[:, :, NA_KW - 1 - q + pad:NA_KW - 1 - q + pad + GRID_W] for q in range(GRID_W)], axis=2)
    per_dr = jnp.where(inside[None, None], shifted, NEG)
    masked = jnp.full((NA_HEADS, GRID_W, GRID_W), NEG, F32)
    half = NA_KH // 2
    cases = []
    for r0, ks in ((0, 0), (half, 0), (rows - R, rows - KR)):
        row_blocks = []
        for rr in range(R):
            r = r0 + rr
            kr = min(max(r - half, 0), rows - NA_KH)
            blocks = [per_dr[:, ks + kk - r + NA_KH - 1] if kr <= ks + kk < kr + NA_KH else masked
                      for kk in range(KR)]
            row_blocks.append(jnp.concatenate(blocks, axis=-1))
        cases.append(jnp.concatenate(row_blocks, axis=1))
    return jnp.stack(cases)


def _neighborhood_attention(qn, kn, vn, bias, L, with_ctx):
    n = qn.shape[0]
    R, KR = NA_ROW_BLOCK, NA_KEY_ROWS
    rows = L // GRID_W
    nb = rows // R
    nq = R * GRID_W
    steps = n // nq if with_ctx else nb
    ctx_blk = L // TOK_TILE
    half = NA_KH // 2

    def blk_of(b):
        return jnp.minimum(b, nb - 1)

    def key_start(b):
        return jnp.clip(blk_of(b) * R - half, 0, rows - KR) * GRID_W

    def case_of(b):
        return jnp.where(blk_of(b) == 0, 0, jnp.where(blk_of(b) == nb - 1, 2, 1))

    loc = pl.BlockSpec((pl.Element(KR * GRID_W), pl.Element(D_NA)), lambda b: (key_start(b), 0))
    ctx = pl.BlockSpec((n - L, D_NA), lambda b: (ctx_blk, 0))
    return pl.pallas_call(
        functools.partial(_na_kernel, n_blocks=nb),
        out_shape=jax.ShapeDtypeStruct((steps * nq, D_NA), BF16),
        grid=(steps,),
        in_specs=[pl.BlockSpec((nq, D_NA), lambda b: (b, 0)), loc, loc, ctx, ctx,
                  pl.BlockSpec((1, NA_HEADS, nq, KR * GRID_W), lambda b: (case_of(b), 0, 0, 0))],
        out_specs=pl.BlockSpec((nq, D_NA), lambda b: (b, 0)),
        compiler_params=_cparams("arbitrary"),
        name="neighborhood_attn",
    )(qn, kn, vn, kn, vn, bias)


def _out_kernel(xl_ref, xc_ref, hf_ref, hb_ref, ga_ref, yb_ref, yc_ref, w_ref, mod_ref, g_ref,
                xo_ref, ht_ref, *, n_lat):
    d = xl_ref.shape[1]
    i = pl.program_id(0)
    row = (i == n_lat).astype(jnp.int32)
    is_ctx = jnp.full((TOK_TILE, 1), row) == 1
    x = jnp.where(is_ctx, xc_ref[...], xl_ref[...])
    ya = ((hf_ref[...] + hb_ref[...]) * jax.nn.gelu(ga_ref[...])).astype(BF16)
    o1 = D_LRU
    o2 = D_LRU + D_WIN_Q
    mix = (_dot(ya, w_ref[0:o1]) + _dot(yb_ref[...], w_ref[o1:o2]) + _dot(yc_ref[...], w_ref[o2:o2 + D_NA]))
    xn = x + mod_ref[pl.ds(row, 1), 2 * d:3 * d] * mix
    xo_ref[...] = xn
    ms = jnp.mean(xn * xn, axis=-1, keepdims=True)
    y = xn * lax.rsqrt(ms + EPS) * g_ref[...]
    h2 = y * (1.0 + mod_ref[pl.ds(row, 1), 4 * d:5 * d]) + mod_ref[pl.ds(row, 1), 3 * d:4 * d]
    ht_ref[...] = h2.T.astype(BF16)


def _out_proj(xl, xc, hf, hb, ga, yb, yc, w_bf, mod, g, with_ctx):
    L, d = xl.shape
    T = TOK_TILE
    n_lat = L // T
    nt = n_lat + 1 if with_ctx else n_lat
    tok = lambda w: pl.BlockSpec((T, w), lambda i: (i, 0))
    full = lambda a: pl.BlockSpec(a.shape, lambda i: (0,) * a.ndim)
    return pl.pallas_call(
        functools.partial(_out_kernel, n_lat=n_lat),
        out_shape=[jax.ShapeDtypeStruct((nt * T, d), F32), jax.ShapeDtypeStruct((d, nt * T), BF16)],
        grid=(nt,),
        in_specs=[pl.BlockSpec((T, d), lambda i: (jnp.minimum(i, n_lat - 1), 0)),
                  pl.BlockSpec((T, d), lambda i: (0, 0)),
                  tok(D_LRU), tok(D_LRU), tok(D_LRU), tok(D_WIN_Q), tok(D_NA),
                  full(w_bf), full(mod), full(g)],
        out_specs=[tok(d), pl.BlockSpec((d, T), lambda i: (0, i))],
        compiler_params=_cparams("arbitrary"),
        name="out_proj",
    )(xl, xc, hf, hb, ga, yb, yc, w_bf, mod, g)


SUBLANES = 8


def _sorting_pairs(n):
    pairs, p = [], 1
    while p < n:
        k = p
        while k >= 1:
            for j in range(k % p, n - k, 2 * k):
                for i in range(min(k, n - j - k)):
                    if (i + j) // (2 * p) == (i + j + k) // (2 * p):
                        pairs.append((i + j, i + j + k))
            k //= 2
        p *= 2
    return pairs


def _vmax(a, b):
    if a is None:
        return b
    if b is None:
        return a
    return jnp.maximum(a, b)


def _vmin(a, b):
    if a is None or b is None:
        return None
    return jnp.minimum(a, b)


def _top16_sorted(slabs):
    K = PEER_TOPK
    w = list(slabs)
    for a, b in _sorting_pairs(K):
        w[a], w[b] = _vmax(w[a], w[b]), _vmin(w[a], w[b])
    shift = SUBLANES // 2
    while shift >= 1:
        partner = [None if w[K - 1 - i] is None else pltpu.roll(w[K - 1 - i], shift, 0) for i in range(K)]
        w = [_vmax(w[i], partner[i]) for i in range(K)]
        stride = K // 2
        while stride >= 1:
            for i in range(K):
                if i & stride == 0:
                    w[i], w[i + stride] = _vmax(w[i], w[i + stride]), _vmin(w[i], w[i + stride])
            stride //= 2
        shift //= 2
    return w


def _allsum8(x):
    x = x + pltpu.roll(x, 4, 0)
    x = x + pltpu.roll(x, 2, 0)
    return x + pltpu.roll(x, 1, 0)


def _route_tile(s1, s2):
    K = PEER_TOPK
    S = SUBLANES
    n_slab = s1.shape[0] // S
    a1 = [s1[j * S:(j + 1) * S] for j in range(n_slab)]
    a2 = [s2[j * S:(j + 1) * S] for j in range(n_slab)]
    v1 = _top16_sorted(a1)
    v2 = _top16_sorted(a2)
    sub = lax.broadcasted_iota(jnp.int32, a1[0].shape, 0)

    def as_rows(v, lo):
        out = v[lo]
        for b in range(1, S):
            out = jnp.where(sub == b, v[lo + b], out)
        return out

    v2_lo, v2_hi, v1_hi = as_rows(v2, 0), as_rows(v2, S), as_rows(v1, S)
    lens = [K // (a + 1) for a in range(S)]
    cands = [v1[0] + v2_lo, v1[0] + v2_hi]
    for a in range(1, S):
        ca = v1[a] + v2_lo
        cands.append(ca if lens[a] >= S else jnp.where(sub < lens[a], ca, -jnp.inf))
    cands.append(v1_hi + v2[0])
    thr = _top16_sorted(cands + [None] * (K - len(cands)))[K - 1]
    x2_lo, x2_hi = jnp.exp(v2_lo - v2[0]), jnp.exp(v2_hi - v2[0])
    sel_lo, sel_hi = cands[0] >= thr, cands[1] >= thr
    zsum = jnp.where(sel_lo, x2_lo, 0.0) + jnp.where(sel_hi, x2_hi, 0.0)
    cnt = [_allsum8(jnp.where(sel_lo, 1.0, 0.0) + jnp.where(sel_hi, 1.0, 0.0))]
    for a in range(1, S):
        sel = cands[a + 1] >= thr
        zsum = zsum + jnp.where(sel, jnp.exp(v1[a] - v1[0]) * x2_lo, 0.0)
        cnt.append(_allsum8(jnp.where(sel, 1.0, 0.0)))
    zsum = zsum + jnp.where(cands[S + 1] >= thr, jnp.exp(v1_hi - v1[0]), 0.0)
    inv_z = 1.0 / _allsum8(zsum)
    r2, e2, c1, e1, pair0 = [], [], [], [], []
    for j in range(n_slab):
        r = jnp.full(a2[j].shape, float(K), F32)
        for k in reversed(range(K)):
            r = jnp.where(a2[j] >= v2[k], float(k), r)
        r2.append(r)
        e2.append(jnp.exp(a2[j] - v2[0]) * inv_z)
        c = jnp.where(a1[j] + v2[0] >= thr, 1.0, 0.0)
        pair0.append(c)
        for a in reversed(range(S)):
            c = jnp.where(a1[j] >= v1[a], cnt[a], c)
        c1.append(c)
        e1.append(jnp.exp(a1[j] - v1[0]))
    lor, land = jnp.logical_or, jnp.logical_and
    n_sel = functools.reduce(lambda a, b: a + b, cnt) + _allsum8(jnp.where(cands[S + 1] >= thr, 1.0, 0.0))
    f1 = land(v1[S - 1] == v1[S], cnt[S - 1] >= 2.0)
    for k in range(S - 1):
        f1 = lor(f1, land(v1[k] == v1[k + 1], cnt[k] != cnt[k + 1]))
    f1 = lor(f1, _allsum8(functools.reduce(lambda a, b: a + b, pair0)) > float(K))
    f2 = v2[0] == v2[1]
    for k in range(1, K - 1):
        f2 = lor(f2, land(v2[k] == v2[k + 1], cnt[0] > float(k)))
    r_sum = _allsum8(functools.reduce(lambda a, b: a + b, r2))
    any_tie2 = r_sum != float(K * (K - 1) // 2 + K * (n_slab * S - K))
    f2 = lor(f2, land(cnt[0] == float(K), any_tie2))
    flag = lor(lor(f1, f2), n_sel != float(K))
    cat = lambda xs: jnp.concatenate(xs, axis=0)
    return cat(r2), cat(e2), cat(c1), cat(e1), jnp.where(flag, 1.0, 0.0)


def _route_tile_exact(s1, s2):
    K = PEER_TOPK

    def take16(x):
        idx = lax.broadcasted_iota(jnp.int32, x.shape, 0).astype(F32)
        rank = jnp.full(x.shape, float(K), F32)
        vals, cur = [], x
        for k in range(K):
            m = jnp.max(cur, axis=0, keepdims=True)
            first = jnp.min(jnp.where(cur == m, idx, float(x.shape[0])), axis=0, keepdims=True)
            hit = idx == first
            rank = jnp.where(hit, float(k), rank)
            vals.append(m)
            cur = jnp.where(hit, -jnp.inf, cur)
        return rank, vals

    r1, v1 = take16(s1)
    r2, v2 = take16(s2)
    v2a = jnp.concatenate(v2, axis=0)
    x1 = [jnp.exp(v1[a] - v1[0]) for a in range(K)]
    x2a = jnp.exp(v2a - v2[0])
    cand = jnp.concatenate([v1[a] + v2a for a in range(K)], axis=0)
    crank, _ = take16(cand)
    sel = crank < float(K)
    z = jnp.zeros_like(v1[0])
    c1 = jnp.zeros(s1.shape, F32)
    for a in range(K):
        sel_a = sel[a * K:(a + 1) * K]
        z = z + jnp.sum(jnp.where(sel_a, x1[a] * x2a, 0.0), axis=0, keepdims=True)
        cnt_a = jnp.sum(jnp.where(sel_a, 1.0, 0.0), axis=0, keepdims=True)
        c1 = jnp.where(r1 == float(a), cnt_a, c1)
    return r2, jnp.exp(s2 - v2[0]) / z, c1, jnp.exp(s1 - v1[0])


def _prep_kernel(ht_ref, wq_ref, k1_ref, k2_ref, r2_ref, e2_ref, c1_ref, e1_ref, q_scr, flag_scr):
    nk = PEER_NKEYS
    q_scr[...] = _dot(wq_ref[...], ht_ref[...])

    def scores(h):
        base = pl.multiple_of(h * 2 * nk, 2 * nk)
        s1 = _dot(k1_ref[...], q_scr[pl.ds(base, nk), :].astype(BF16))
        s2 = _dot(k2_ref[...], q_scr[pl.ds(base + nk, nk), :].astype(BF16))
        return s1, s2

    def store(h, ts, r2, e2, c1, e1):
        r2_ref[h, :, ts] = _pack_rows(r2)
        e2_ref[h, :, ts] = _pack_rows(e2)
        c1_ref[h, :, ts] = c1
        e1_ref[h, :, ts] = e1

    n_tiles = ht_ref.shape[1] // LANES

    def head(h, flags):
        s1, s2 = scores(h)
        for t in range(n_tiles):
            ts = slice(t * LANES, (t + 1) * LANES)
            r2, e2, c1, e1, flag = _route_tile(s1[:, ts], s2[:, ts])
            store(h, ts, r2, e2, c1, e1)
            flag_scr[h, t * SUBLANES:(t + 1) * SUBLANES, :] = flag
            flags = jnp.maximum(flags, flag)
        return flags

    flags = lax.fori_loop(0, PEER_HEADS, head, jnp.zeros((SUBLANES, LANES), F32))

    @pl.when(jnp.max(flags) > 0.0)
    def _():
        def head_exact(h, carry):
            s1, s2 = scores(h)
            for t in range(n_tiles):
                ts = slice(t * LANES, (t + 1) * LANES)

                @pl.when(jnp.max(flag_scr[h, t * SUBLANES:(t + 1) * SUBLANES, :]) > 0.0)
                def _():
                    store(h, ts, *_route_tile_exact(s1[:, ts], s2[:, ts]))
            return carry

        lax.fori_loop(0, PEER_HEADS, head_exact, 0)


def _peer_prep(ht, wq_t, k1, k2, tok_tile, tok_off, n):
    d = ht.shape[0]
    T = tok_tile
    nk = PEER_NKEYS
    full = lambda a: pl.BlockSpec(a.shape, lambda i: (0,) * a.ndim)
    spec = lambda rows: pl.BlockSpec((PEER_HEADS, rows, T), lambda i: (0, 0, i))
    return pl.pallas_call(
        _prep_kernel,
        out_shape=[jax.ShapeDtypeStruct((PEER_HEADS, nk // 2, n), U32)] * 2
        + [jax.ShapeDtypeStruct((PEER_HEADS, nk, n), F32)] * 2,
        grid=(n // T,),
        in_specs=[pl.BlockSpec((d, T), lambda i: (0, i + tok_off)), full(wq_t), full(k1), full(k2)],
        out_specs=[spec(nk // 2), spec(nk // 2), spec(nk), spec(nk)],
        scratch_shapes=[pltpu.VMEM((wq_t.shape[0], T), F32),
                        pltpu.VMEM((PEER_HEADS, T // LANES * SUBLANES, LANES), F32)],
        compiler_params=_cparams("arbitrary"),
        name="peer_prep",
    )(ht, wq_t, k1, k2)


def _peer_kernel(ht_ref, u_ref, un_ref, vt_ref, vp_ref, r2_ref, e2_ref, c1_ref, e1_ref, x_ref, g_ref,
                 o_ref, acc_ref, *ap_refs):
    nk = PEER_NKEYS
    c = pl.program_id(1)
    n_sub = len(ap_refs) // 2
    a_refs, p_refs = ap_refs[:n_sub], ap_refs[n_sub:]
    sub = 2 * a_refs[0].shape[0]
    T = a_refs[0].shape[1]
    rows_per_sub = sub // nk
    PK = 16
    WK = PK // 2

    tok_blocks = [slice(n, min(n + PEER_MXU_TOKENS, T)) for n in range(0, T, PEER_MXU_TOKENS)]

    def activations_from(rows_ref, dst_ref):
        for tb in tok_blocks:
            dst_ref[:, tb] = _pack_rows(_dot(rows_ref, ht_ref[:, tb]))

    def activations(s):
        activations_from(u_ref[s * sub:(s + 1) * sub, :], a_refs[s])

    def gates(s):
        for t in range(T // LANES):
            ts = slice(t * LANES, (t + 1) * LANES)
            for j in range(rows_per_sub):
                row = s * rows_per_sub + j
                gate = [None] * (nk // PK)
                for h in range(PEER_HEADS):
                    c1 = c1_ref[h, :, ts][row:row + 1]
                    e1 = e1_ref[h, :, ts][row:row + 1]
                    c1 = jnp.broadcast_to(c1, (PK, LANES)).astype(BF16)
                    e1 = jnp.broadcast_to(e1, (PK, LANES)).astype(BF16)
                    for v in range(nk // PK):
                        ws = slice(v * WK, (v + 1) * WK)
                        r2 = _unpack_rows(r2_ref[h, ws, ts])
                        e2 = _unpack_rows(e2_ref[h, ws, ts])
                        g = jnp.where(r2 < c1, e2 * e1, 0.0)
                        gate[v] = g if h == 0 else gate[v] + g
                for v in range(nk // PK):
                    ws = slice((j * nk + v * PK) // 2, (j * nk + (v + 1) * PK) // 2)
                    a = _unpack_rows(a_refs[s][ws, ts])
                    th = jnp.tanh(a * (GELU_K0 + GELU_K1 * (a * a)))
                    p_refs[s][ws, ts] = pltpu.bitcast(gate[v] * (a * (0.5 + 0.5 * th)), U32)

    def project(terms):
        for tb in tok_blocks:
            parts = [_dot(cols, _unpack_rows(src[:, tb])) for cols, src in terms]
            acc_ref[:, tb] += functools.reduce(lambda a, b: a + b, parts)

    def term(s):
        return vt_ref[:, s * sub:(s + 1) * sub], p_refs[s]

    last = n_sub - 1
    assert n_sub % 2 == 0

    @pl.when(c == 0)
    def _():
        acc_ref[...] = jnp.zeros(acc_ref.shape, F32)
        p_refs[last][...] = jnp.zeros(p_refs[last].shape, U32)
        activations(0)

    pending = [(vp_ref[...], p_refs[last])]
    for s in range(n_sub):
        gates(s)
        if s < last:
            pending.append(term(s))
            activations(s + 1)
        if len(pending) == 2:
            project(pending)
            pending = []
    activations_from(un_ref[...], a_refs[0])

    @pl.when(c == pl.num_programs(1) - 1)
    def _():
        project([term(last)])
        o_ref[...] = x_ref[...] + g_ref[...] * acc_ref[...].T


def _peer_dense(ht, u_bf, vt_bf, routing, x, g2, tok_tile, tok_off, n_tok):
    r2, e2, c1, e1 = routing
    d = ht.shape[0]
    n_exp = u_bf.shape[0]
    T = tok_tile
    NC = PEER_EXPERT_CHUNK
    nk = PEER_NKEYS
    SUB = PEER_SUB_CHUNK
    n_sub = NC // SUB
    rt = lambda rows: pl.BlockSpec((PEER_HEADS, rows, T), lambda i, c: (0, 0, i))
    per_chunk = pl.BlockSpec((PEER_HEADS, NC // nk, T), lambda i, c: (0, c, i))
    return pl.pallas_call(
        _peer_kernel,
        out_shape=jax.ShapeDtypeStruct((n_tok, d), F32),
        grid=(n_tok // T, n_exp // NC),
        in_specs=[pl.BlockSpec((d, T), lambda i, c: (0, i + tok_off)),
                  pl.BlockSpec((NC, d), lambda i, c: (c, 0)),
                  pl.BlockSpec((SUB, d), lambda i, c: (jnp.minimum((c + 1) * n_sub, n_exp // SUB - 1), 0)),
                  pl.BlockSpec((d, NC), lambda i, c: (0, c)),
                  pl.BlockSpec((d, SUB), lambda i, c: (0, jnp.maximum(c * n_sub - 1, 0))),
                  rt(nk // 2), rt(nk // 2), per_chunk, per_chunk,
                  pl.BlockSpec((T, d), lambda i, c: (i + tok_off, 0)),
                  pl.BlockSpec((1, d), lambda i, c: (0, 0))],
        out_specs=pl.BlockSpec((T, d), lambda i, c: (i, 0)),
        scratch_shapes=[pltpu.VMEM((d, T), F32)] + [pltpu.VMEM((PEER_SUB_CHUNK // 2, T), U32)] * (2 * n_sub),
        compiler_params=_cparams("arbitrary", "arbitrary"),
        name="peer_dense",
    )(ht, u_bf, u_bf, vt_bf, vt_bf, r2, e2, c1, e1, x, g2)


def _rope_tables(L, n):
    t = jnp.arange(L)
    row = (t // GRID_W).astype(F32)
    col = (t % GRID_W).astype(F32)
    q = HEAD_DIM // 4
    inv = ROPE_THETA ** (-jnp.arange(q, dtype=F32) / q)
    ar = row[:, None] * inv
    ac = col[:, None] * inv
    cos = jnp.concatenate([jnp.cos(ar), jnp.cos(ar), jnp.cos(ac), jnp.cos(ac)], axis=-1)
    sin = jnp.concatenate([-jnp.sin(ar), jnp.sin(ar), -jnp.sin(ac), jnp.sin(ac)], axis=-1)
    cos = jnp.concatenate([cos, jnp.ones((n - L, HEAD_DIM), F32)], axis=0)
    sin = jnp.concatenate([sin, jnp.zeros((n - L, HEAD_DIM), F32)], axis=0)
    return jnp.tile(cos, (1, LANES // HEAD_DIM)), jnp.tile(sin, (1, LANES // HEAD_DIM))


def _block_diag(w):
    nb, di, do = w.shape
    eye = jnp.eye(nb, dtype=w.dtype)
    return (eye[:, None, :, None] * w[:, :, None, :]).reshape(nb * di, nb * do)


def kernel(x, c, ctx, c_ctx, w_mod, b_mod, norm1_g, norm2_g, w_in, w_out, lru_conv_w, lru_conv_b, lru_wa, lru_ba, lru_wx, lru_bx, lru_lam, win_qn_g, win_kn_g, win_sink, na_qn_g, na_kn_g, na_rpb, peer_wq, peer_k1, peer_k2, peer_u, peer_v):
    B, L, D = x.shape
    Lc = ctx.shape[1]
    depth = w_mod.shape[0]
    assert B == 1 and Lc == TOK_TILE and L % PREP_TOK_TILE == 0 and L // GRID_W >= 3 * NA_ROW_BLOCK
    n = L + Lc
    n_lat = L // TOK_TILE
    rows = L // GRID_W

    cvec = jnp.zeros((8, D), F32).at[0].set(c[0]).at[1].set(c_ctx)
    mods = _modulation(cvec, w_mod, b_mod)
    cos_t, sin_t = _rope_tables(L, n)
    tile2 = lambda g: jnp.tile(g, LANES // HEAD_DIM)

    xl, xc = x[0], ctx[0]
    for l in range(depth):
        with_ctx = l < depth - 1
        mod = mods[l]
        hg = jnp.zeros((8, LANES), F32)
        hg = hg.at[0].set(tile2(win_qn_g[l])).at[1].set(tile2(win_kn_g[l]))
        hg = hg.at[2].set(tile2(na_qn_g[l])).at[3].set(tile2(na_kn_g[l]))
        xa, ga, qw, kw, vw, qn, kn, vn = _in_proj(xl, xc, mod, norm1_g[l][None], w_in[l].astype(BF16),
                                                  cos_t, sin_t, hg)

        w_gates = jnp.concatenate([_block_diag(lru_wa[l, 0]), _block_diag(lru_wx[l, 0]),
                                   _block_diag(lru_wa[l, 1]), _block_diag(lru_wx[l, 1])], axis=1).astype(BF16)
        b_gates = jnp.concatenate([lru_ba[l, 0], lru_bx[l, 0], lru_ba[l, 1], lru_bx[l, 1]])[None]
        conv_w = jnp.zeros((8, D_LRU), F32).at[0:lru_conv_w.shape[1]].set(lru_conv_w[l])
        lam = jnp.zeros((8, D_LRU), F32).at[0:2].set(lru_lam[l])
        hf, hb = _lru(xa, conv_w, lru_conv_b[l][None], w_gates, b_gates, lam, n_lat)

        yb = _window_attention(win_sink[l], qw, kw, vw, L, with_ctx)
        yc = _neighborhood_attention(qn, kn, vn, _na_bias_table(na_rpb[l], rows), L, with_ctx)

        xn, ht = _out_proj(xl, xc, hf, hb, ga, yb, yc, w_out[l].astype(BF16), mod, norm2_g[l][None], with_ctx)

        route_w = (peer_wq[l].T.astype(BF16), peer_k1[l].astype(BF16), peer_k2[l].astype(BF16))
        u_bf = peer_u[l].astype(BF16)
        vt_bf = peer_v[l].T.astype(BF16)
        g2 = mod[:, 5 * D:6 * D]
        routing = _peer_prep(ht, *route_w, PREP_TOK_TILE, 0, L)
        xl_new = _peer_dense(ht, u_bf, vt_bf, routing, xn, g2[0:1], PEER_TOK_TILE, 0, L)
        if with_ctx:
            routing = _peer_prep(ht, *route_w, Lc, L // Lc, Lc)
            xc = _peer_dense(ht, u_bf, vt_bf, routing, xn, g2[1:2], Lc, L // Lc, Lc)
        xl = xl_new
    return xl[None]
```

```python
import functools

import numpy as np
import jax
import jax.numpy as jnp
from jax import lax
from jax.experimental import pallas as pl
from jax.experimental.pallas import tpu as pltpu

F32 = jnp.float32
BF16 = jnp.bfloat16
U32 = jnp.uint32

HEAD_DIM = 64
GRID_W = 64
EPS = 1e-6
ROPE_THETA = 10000.0
D_LRU = 256
LRU_BLOCKS = 4
LRU_C = 8.0
WIN_Q_HEADS = 6
WIN_KV_HEADS = 2
WIN_BLOCK = 128
NA_HEADS = 6
NA_KH = 8
NA_KW = 16
NA_ROW_BLOCK = 4
NA_KEY_ROWS = NA_ROW_BLOCK + NA_KH - 1
PEER_HEADS = 8
PEER_NKEYS = 128
PEER_TOPK = 16
D_WIN_Q = WIN_Q_HEADS * HEAD_DIM
D_WIN_KV = WIN_KV_HEADS * HEAD_DIM
D_NA = NA_HEADS * HEAD_DIM

LANES = 128
TOK_TILE = 256
PEER_TOK_TILE = 512
PEER_EXPERT_CHUNK = 2048
PEER_SUB_CHUNK = 1024
PEER_MXU_TOKENS = 256
PREP_TOK_TILE = 1024
VMEM_LIMIT = 56 * 1024 * 1024
NEG = -1e30
GELU_K0 = float(np.sqrt(2.0 / np.pi))
GELU_K1 = 0.044715 * GELU_K0


def _cparams(*sem):
    return pltpu.CompilerParams(dimension_semantics=sem, vmem_limit_bytes=VMEM_LIMIT)


def _dot(a, b):
    return jnp.dot(a, b, preferred_element_type=F32)


def _pack_rows(x):
    return pltpu.bitcast(x.astype(BF16), U32)


def _unpack_rows(w):
    return pltpu.bitcast(w, BF16)


def _dot_nt(a, b):
    return lax.dot_general(a, b, (((1,), (1,)), ((), ())), preferred_element_type=F32)


def _mod_kernel(c_ref, w_ref, b_ref, o_ref):
    c = c_ref[...]
    a = (c * jax.nn.sigmoid(c)).astype(BF16)
    o_ref[0] = _dot(a, w_ref[0].astype(BF16)) + b_ref[0]


def _modulation(cvec, w_mod, b_mod):
    depth, d, d6 = w_mod.shape
    return pl.pallas_call(
        _mod_kernel,
        out_shape=jax.ShapeDtypeStruct((depth, 8, d6), F32),
        grid=(depth, d6 // d),
        in_specs=[pl.BlockSpec((8, d), lambda l, j: (0, 0)),
                  pl.BlockSpec((1, d, d), lambda l, j: (l, 0, j)),
                  pl.BlockSpec((1, 1, d), lambda l, j: (l, 0, j))],
        out_specs=pl.BlockSpec((1, 8, d), lambda l, j: (l, 0, j)),
        compiler_params=_cparams("arbitrary", "arbitrary"),
        name="modulation",
    )(cvec, w_mod, b_mod.reshape(depth, 1, d6))


def _pair_mean_matrix():
    r = lax.broadcasted_iota(jnp.int32, (LANES, LANES), 0) < HEAD_DIM
    c = lax.broadcasted_iota(jnp.int32, (LANES, LANES), 1) < HEAD_DIM
    return jnp.where(r == c, 1.0 / HEAD_DIM, 0.0).astype(BF16)


def _head_norm(z, gain, bd):
    z2 = z * z
    hi = z2.astype(BF16)
    lo = (z2 - hi.astype(F32)).astype(BF16)
    ms = _dot(hi, bd) + _dot(lo, bd)
    return z * lax.rsqrt(ms + EPS) * gain


def _rope(z, cos, sin):
    lane = lax.broadcasted_iota(jnp.int32, z.shape, 1)
    first = (lane & 16) == 0
    partner = jnp.where(first, pltpu.roll(z, LANES - 16, 1), pltpu.roll(z, 16, 1))
    return z * cos + partner * sin


def _in_kernel(xl_ref, xc_ref, mod_ref, g_ref, w_ref, cos_ref, sin_ref, hg_ref,
               xa_ref, ga_ref, qw_ref, kw_ref, vw_ref, qn_ref, kn_ref, vn_ref, *, n_lat):
    d = xl_ref.shape[1]
    i = pl.program_id(0)
    row = (i == n_lat).astype(jnp.int32)
    is_ctx = jnp.full((TOK_TILE, 1), row) == 1
    x = jnp.where(is_ctx, xc_ref[...], xl_ref[...])
    ms = jnp.mean(x * x, axis=-1, keepdims=True)
    y = x * lax.rsqrt(ms + EPS) * g_ref[...]
    sh = mod_ref[pl.ds(row, 1), 0:d]
    sc = mod_ref[pl.ds(row, 1), d:2 * d]
    h = (y * (1.0 + sc) + sh).astype(BF16)
    z = _dot(h, w_ref[...])

    bd = _pair_mean_matrix()
    cos = cos_ref[...]
    sin = sin_ref[...]
    scale = HEAD_DIM ** -0.5
    o = 0
    xa_ref[...] = z[:, o:o + D_LRU]
    o += D_LRU
    ga_ref[...] = z[:, o:o + D_LRU]
    o += D_LRU
    for g in range(D_WIN_Q // LANES):
        zz = _rope(_head_norm(z[:, o:o + LANES], hg_ref[0:1], bd), cos, sin)
        qw_ref[:, g * LANES:(g + 1) * LANES] = (zz * scale).astype(BF16)
        o += LANES
    kw_ref[...] = _rope(_head_norm(z[:, o:o + LANES], hg_ref[1:2], bd), cos, sin).astype(BF16)
    o += LANES
    vw_ref[...] = z[:, o:o + LANES].astype(BF16)
    o += LANES
    for g in range(D_NA // LANES):
        zz = _head_norm(z[:, o:o + LANES], hg_ref[2:3], bd)
        qn_ref[:, g * LANES:(g + 1) * LANES] = (zz * scale).astype(BF16)
        o += LANES
    for g in range(D_NA // LANES):
        zz = _head_norm(z[:, o:o + LANES], hg_ref[3:4], bd)
        kn_ref[:, g * LANES:(g + 1) * LANES] = zz.astype(BF16)
        o += LANES
    vn_ref[...] = z[:, o:o + D_NA].astype(BF16)


def _in_proj(xl, xc, mod, g, w_bf, cos_t, sin_t, hg):
    L, d = xl.shape
    n_lat = L // TOK_TILE
    n = L + xc.shape[0]
    d_in = w_bf.shape[1]
    T = TOK_TILE
    tok = lambda w: pl.BlockSpec((T, w), lambda i: (i, 0))
    full = lambda a: pl.BlockSpec(a.shape, lambda i: (0,) * a.ndim)
    widths = (D_LRU, D_LRU, D_WIN_Q, D_WIN_KV, D_WIN_KV, D_NA, D_NA, D_NA)
    dtypes = (F32, F32, BF16, BF16, BF16, BF16, BF16, BF16)
    return pl.pallas_call(
        functools.partial(_in_kernel, n_lat=n_lat),
        out_shape=[jax.ShapeDtypeStruct((n, w), t) for w, t in zip(widths, dtypes)],
        grid=(n_lat + 1,),
        in_specs=[pl.BlockSpec((T, d), lambda i: (jnp.minimum(i, n_lat - 1), 0)),
                  pl.BlockSpec((T, d), lambda i: (0, 0)),
                  full(mod), full(g), full(w_bf), tok(LANES), tok(LANES), full(hg)],
        out_specs=[tok(w) for w in widths],
        compiler_params=_cparams("arbitrary"),
        name="in_proj",
    )(xl, xc, mod, g, w_bf, cos_t, sin_t, hg)


def _chunk_scan(a, b, reverse):
    T = a.shape[0]
    rows = lax.broadcasted_iota(jnp.int32, a.shape, 0)
    s = 1
    while s < T:
        if reverse:
            edge = rows >= T - s
            shift = T - s
        else:
            edge = rows < s
            shift = s
        a_s = jnp.where(edge, 1.0, pltpu.roll(a, shift, 0))
        b_s = jnp.where(edge, 0.0, pltpu.roll(b, shift, 0))
        b = a * b_s + b
        a = a * a_s
        s *= 2
    return a, b


def _lru_direction(xm, ph, nh, pv, nv, cw, cb, w, bias, spl, carry_ref, first, reverse):
    T, C = xm.shape
    rows = lax.broadcasted_iota(jnp.int32, (T, C), 0)
    p6 = ph[6:7] * pv
    p7 = ph[7:8] * pv
    n0 = nh[0:1] * nv
    x_m1 = jnp.where(rows == 0, p7, pltpu.roll(xm, 1, 0))
    x_m2 = jnp.where(rows == 0, p6, jnp.where(rows == 1, p7, pltpu.roll(xm, 2, 0)))
    x_p1 = jnp.where(rows == T - 1, n0, pltpu.roll(xm, T - 1, 0))
    u = cw[0:1] * x_m2 + cw[1:2] * x_m1 + cw[2:3] * xm + cw[3:4] * x_p1 + cb
    zz = _dot(u.astype(BF16), w) + bias
    r = jax.nn.sigmoid(zz[:, :C])
    ig = jax.nn.sigmoid(zz[:, C:])
    log_a = -LRU_C * r * spl
    a = jnp.exp(log_a)
    b = jnp.sqrt(-jnp.tanh(log_a) * (a * a + 1.0)) * ig * u
    a, b = _chunk_scan(a, b, reverse)

    @pl.when(first)
    def _():
        carry_ref[...] = jnp.zeros(carry_ref.shape, F32)

    h = a * carry_ref[0:1] + b
    edge = h[0:1] if reverse else h[T - 1:T]
    carry_ref[...] = jnp.broadcast_to(edge, carry_ref.shape)
    return h


def _lru_kernel(xf_ref, pf_ref, nf_ref, xb_ref, pb_ref, nb_ref, cw_ref, cb_ref, w_ref, b_ref,
                lam_ref, hf_ref, hb_ref, cf_ref, cbk_ref, *, n_lat):
    j = pl.program_id(0)
    C = D_LRU
    fblk = jnp.where(j == 0, n_lat, j - 1)
    bblk = jnp.where(j == 0, n_lat, n_lat - j)
    lam = lam_ref[...]
    spl = jnp.maximum(-lam, 0.0) + jnp.log1p(jnp.exp(-jnp.abs(lam)))
    cw = cw_ref[...]
    cb = cb_ref[...]

    def halo_valid(blk):
        pv = jnp.logical_and(blk != 0, blk != n_lat).astype(F32)
        nv = jnp.logical_and(blk != n_lat - 1, blk != n_lat).astype(F32)
        return pv, nv

    pv, nv = halo_valid(fblk)
    hf_ref[...] = _lru_direction(xf_ref[...], pf_ref[...], nf_ref[...], pv, nv, cw, cb,
                                 w_ref[:, 0:2 * C], b_ref[:, 0:2 * C], spl[0:1], cf_ref, j == 0, False)
    pv, nv = halo_valid(bblk)
    hb_ref[...] = _lru_direction(xb_ref[...], pb_ref[...], nb_ref[...], pv, nv, cw, cb,
                                 w_ref[:, 2 * C:4 * C], b_ref[:, 2 * C:4 * C], spl[1:2], cbk_ref, j == 0, True)


def _lru(xa, conv_w, conv_b, w_gates, b_gates, lam, n_lat):
    n, C = xa.shape
    T = TOK_TILE
    sub = T // 8
    nblk8 = n // 8
    fblk = lambda j: jnp.where(j == 0, n_lat, j - 1)
    bblk = lambda j: jnp.where(j == 0, n_lat, n_lat - j)
    prev8 = lambda blk: jnp.maximum(blk * sub - 1, 0)
    next8 = lambda blk: jnp.minimum((blk + 1) * sub, nblk8 - 1)
    full = lambda a: pl.BlockSpec(a.shape, lambda j: (0,) * a.ndim)
    return pl.pallas_call(
        functools.partial(_lru_kernel, n_lat=n_lat),
        out_shape=[jax.ShapeDtypeStruct((n, C), F32)] * 2,
        grid=(n_lat + 1,),
        in_specs=[pl.BlockSpec((T, C), lambda j: (fblk(j), 0)),
                  pl.BlockSpec((8, C), lambda j: (prev8(fblk(j)), 0)),
                  pl.BlockSpec((8, C), lambda j: (next8(fblk(j)), 0)),
                  pl.BlockSpec((T, C), lambda j: (bblk(j), 0)),
                  pl.BlockSpec((8, C), lambda j: (prev8(bblk(j)), 0)),
                  pl.BlockSpec((8, C), lambda j: (next8(bblk(j)), 0)),
                  full(conv_w), full(conv_b), full(w_gates), full(b_gates), full(lam)],
        out_specs=[pl.BlockSpec((T, C), lambda j: (fblk(j), 0)),
                   pl.BlockSpec((T, C), lambda j: (bblk(j), 0))],
        scratch_shapes=[pltpu.VMEM((8, C), F32), pltpu.VMEM((8, C), F32)],
        compiler_params=_cparams("arbitrary"),
        name="rglru",
    )(xa, xa, xa, xa, xa, xa, conv_w, conv_b, w_gates, b_gates, lam)


def _win_kernel(sink_ref, q_ref, kp_ref, kc_ref, kn_ref, vp_ref, vc_ref, vn_ref, kx_ref, vx_ref,
                o_ref, *, nb):
    W = WIN_BLOCK
    R = WIN_Q_HEADS // WIN_KV_HEADS
    b = pl.program_id(0)
    blk = jnp.where(b < nb, b, -4)
    rows = lax.broadcasted_iota(jnp.int32, (R * W, 3 * W), 0)
    cols = lax.broadcasted_iota(jnp.int32, (R * W, 3 * W), 1)
    kpos = (blk - 1) * W + cols
    qpos = blk * W + (rows & (W - 1))
    valid = jnp.logical_and(jnp.abs(kpos - qpos) <= W, jnp.logical_and(kpos >= 0, kpos < nb * W))
    row1 = lax.broadcasted_iota(jnp.int32, (R * W, 1), 0)
    q = q_ref[...]
    for g in range(WIN_KV_HEADS):
        ls = slice(g * HEAD_DIM, (g + 1) * HEAD_DIM)
        qs = jnp.concatenate([q[:, (g * R + r) * HEAD_DIM:(g * R + r + 1) * HEAD_DIM] for r in range(R)], axis=0)
        kl = jnp.concatenate([kp_ref[:, ls], kc_ref[:, ls], kn_ref[:, ls]], axis=0)
        vl = jnp.concatenate([vp_ref[:, ls], vc_ref[:, ls], vn_ref[:, ls]], axis=0)
        s_loc = jnp.where(valid, _dot_nt(qs, kl), NEG)
        s_ctx = _dot_nt(qs, kx_ref[:, ls])
        sink = jnp.full((R * W, 1), sink_ref[g * R], F32)
        for r in range(1, R):
            sink = jnp.where(row1 >= r * W, sink_ref[g * R + r], sink)
        m = jnp.maximum(jnp.maximum(jnp.max(s_loc, axis=-1, keepdims=True),
                                    jnp.max(s_ctx, axis=-1, keepdims=True)), sink)
        p_loc = jnp.exp(s_loc - m)
        p_ctx = jnp.exp(s_ctx - m)
        den = (jnp.sum(p_loc, axis=-1, keepdims=True) + jnp.sum(p_ctx, axis=-1, keepdims=True)
               + jnp.exp(sink - m))
        o = (_dot(p_loc.astype(BF16), vl) + _dot(p_ctx.astype(BF16), vx_ref[:, ls])) / den
        for r in range(R):
            hh = g * R + r
            o_ref[:, hh * HEAD_DIM:(hh + 1) * HEAD_DIM] = o[r * W:(r + 1) * W].astype(o_ref.dtype)


def _window_attention(sink, qw, kw, vw, L, with_ctx):
    n = qw.shape[0]
    W = WIN_BLOCK
    nb = L // W
    nq = n // W if with_ctx else nb
    ctx_blk = L // TOK_TILE
    lat = lambda b: jnp.minimum(b, nb - 1)
    kv = lambda f: pl.BlockSpec((W, D_WIN_KV), lambda b: (f(b), 0))
    prv = lambda b: jnp.maximum(lat(b) - 1, 0)
    nxt = lambda b: jnp.minimum(lat(b) + 1, nb - 1)
    ctx = pl.BlockSpec((n - L, D_WIN_KV), lambda b: (ctx_blk, 0))
    return pl.pallas_call(
        functools.partial(_win_kernel, nb=nb),
        out_shape=jax.ShapeDtypeStruct((nq * W, D_WIN_Q), BF16),
        grid=(nq,),
        in_specs=[pl.BlockSpec(memory_space=pltpu.SMEM),
                  pl.BlockSpec((W, D_WIN_Q), lambda b: (b, 0)),
                  kv(prv), kv(lat), kv(nxt), kv(prv), kv(lat), kv(nxt), ctx, ctx],
        out_specs=pl.BlockSpec((W, D_WIN_Q), lambda b: (b, 0)),
        compiler_params=_cparams("arbitrary"),
        name="window_attn",
    )(sink, qw, kw, kw, kw, vw, vw, vw, kw, vw)


def _na_kernel(q_ref, kl_ref, vl_ref, kx_ref, vx_ref, bias_ref, o_ref, *, n_blocks):
    b = pl.program_id(0)
    nq = q_ref.shape[0]
    latent = jnp.full((nq, 1), (b < n_blocks).astype(jnp.int32)) == 1
    q = q_ref[...]
    for h in range(NA_HEADS):
        ls = slice(h * HEAD_DIM, (h + 1) * HEAD_DIM)
        qh = q[:, ls]
        s_loc = jnp.where(latent, _dot_nt(qh, kl_ref[:, ls]) + bias_ref[0, h], NEG)
        s_ctx = _dot_nt(qh, kx_ref[:, ls])
        m = jnp.maximum(jnp.max(s_loc, axis=-1, keepdims=True), jnp.max(s_ctx, axis=-1, keepdims=True))
        p_loc = jnp.exp(s_loc - m)
        p_ctx = jnp.exp(s_ctx - m)
        den = jnp.sum(p_loc, axis=-1, keepdims=True) + jnp.sum(p_ctx, axis=-1, keepdims=True)
        o = (_dot(p_loc.astype(BF16), vl_ref[:, ls]) + _dot(p_ctx.astype(BF16), vx_ref[:, ls])) / den
        o_ref[:, ls] = o.astype(o_ref.dtype)


def _na_bias_table(rpb, rows):
    R, KR = NA_ROW_BLOCK, NA_KEY_ROWS
    qc = np.arange(GRID_W)[:, None]
    kc = np.arange(GRID_W)[None, :]
    qstart = np.clip(qc - NA_KW // 2, 0, GRID_W - NA_KW)
    inside = (kc - qstart >= 0) & (kc - qstart < NA_KW)
    pad = GRID_W - NA_KW
    rp = jnp.pad(rpb.astype(F32), ((0, 0), (0, 0), (pad, pad)))
    shifted = jnp.stack([rp[:, :, NA_KW - 1 - q + pad:NA_KW - 1 - q + pad + GRID_W] for q in range(GRID_W)], axis=2)
    per_dr = jnp.where(inside[None, None], shifted, NEG)
    masked = jnp.full((NA_HEADS, GRID_W, GRID_W), NEG, F32)
    half = NA_KH // 2
    cases = []
    for r0, ks in ((0, 0), (half, 0), (rows - R, rows - KR)):
        row_blocks = []
        for rr in range(R):
            r = r0 + rr
            kr = min(max(r - half, 0), rows - NA_KH)
            blocks = [per_dr[:, ks + kk - r + NA_KH - 1] if kr <= ks + kk < kr + NA_KH else masked
                      for kk in range(KR)]
            row_blocks.append(jnp.concatenate(blocks, axis=-1))
        cases.append(jnp.concatenate(row_blocks, axis=1))
    return jnp.stack(cases)


def _neighborhood_attention(qn, kn, vn, bias, L, with_ctx):
    n = qn.shape[0]
    R, KR = NA_ROW_BLOCK, NA_KEY_ROWS
    rows = L // GRID_W
    nb = rows // R
    nq = R * GRID_W
    steps = n // nq if with_ctx else nb
    ctx_blk = L // TOK_TILE
    half = NA_KH // 2

    def blk_of(b):
        return jnp.minimum(b, nb - 1)

    def key_start(b):
        return jnp.clip(blk_of(b) * R - half, 0, rows - KR) * GRID_W

    def case_of(b):
        return jnp.where(blk_of(b) == 0, 0, jnp.where(blk_of(b) == nb - 1, 2, 1))

    loc = pl.BlockSpec((pl.Element(KR * GRID_W), pl.Element(D_NA)), lambda b: (key_start(b), 0))
    ctx = pl.BlockSpec((n - L, D_NA), lambda b: (ctx_blk, 0))
    return pl.pallas_call(
        functools.partial(_na_kernel, n_blocks=nb),
        out_shape=jax.ShapeDtypeStruct((steps * nq, D_NA), BF16),
        grid=(steps,),
        in_specs=[pl.BlockSpec((nq, D_NA), lambda b: (b, 0)), loc, loc, ctx, ctx,
                  pl.BlockSpec((1, NA_HEADS, nq, KR * GRID_W), lambda b: (case_of(b), 0, 0, 0))],
        out_specs=pl.BlockSpec((nq, D_NA), lambda b: (b, 0)),
        compiler_params=_cparams("arbitrary"),
        name="neighborhood_attn",
    )(qn, kn, vn, kn, vn, bias)


def _out_kernel(xl_ref, xc_ref, hf_ref, hb_ref, ga_ref, yb_ref, yc_ref, w_ref, mod_ref, g_ref,
                xo_ref, ht_ref, *, n_lat):
    d = xl_ref.shape[1]
    i = pl.program_id(0)
    row = (i == n_lat).astype(jnp.int32)
    is_ctx = jnp.full((TOK_TILE, 1), row) == 1
    x = jnp.where(is_ctx, xc_ref[...], xl_ref[...])
    ya = ((hf_ref[...] + hb_ref[...]) * jax.nn.gelu(ga_ref[...])).astype(BF16)
    o1 = D_LRU
    o2 = D_LRU + D_WIN_Q
    mix = (_dot(ya, w_ref[0:o1]) + _dot(yb_ref[...], w_ref[o1:o2]) + _dot(yc_ref[...], w_ref[o2:o2 + D_NA]))
    xn = x + mod_ref[pl.ds(row, 1), 2 * d:3 * d] * mix
    xo_ref[...] = xn
    ms = jnp.mean(xn * xn, axis=-1, keepdims=True)
    y = xn * lax.rsqrt(ms + EPS) * g_ref[...]
    h2 = y * (1.0 + mod_ref[pl.ds(row, 1), 4 * d:5 * d]) + mod_ref[pl.ds(row, 1), 3 * d:4 * d]
    ht_ref[...] = h2.T.astype(BF16)


def _out_proj(xl, xc, hf, hb, ga, yb, yc, w_bf, mod, g, with_ctx):
    L, d = xl.shape
    T = TOK_TILE
    n_lat = L // T
    nt = n_lat + 1 if with_ctx else n_lat
    tok = lambda w: pl.BlockSpec((T, w), lambda i: (i, 0))
    full = lambda a: pl.BlockSpec(a.shape, lambda i: (0,) * a.ndim)
    return pl.pallas_call(
        functools.partial(_out_kernel, n_lat=n_lat),
        out_shape=[jax.ShapeDtypeStruct((nt * T, d), F32), jax.ShapeDtypeStruct((d, nt * T), BF16)],
        grid=(nt,),
        in_specs=[pl.BlockSpec((T, d), lambda i: (jnp.minimum(i, n_lat - 1), 0)),
                  pl.BlockSpec((T, d), lambda i: (0, 0)),
                  tok(D_LRU), tok(D_LRU), tok(D_LRU), tok(D_WIN_Q), tok(D_NA),
                  full(w_bf), full(mod), full(g)],
        out_specs=[tok(d), pl.BlockSpec((d, T), lambda i: (0, i))],
        compiler_params=_cparams("arbitrary"),
        name="out_proj",
    )(xl, xc, hf, hb, ga, yb, yc, w_bf, mod, g)


SUBLANES = 8


def _sorting_pairs(n):
    pairs, p = [], 1
    while p < n:
        k = p
        while k >= 1:
            for j in range(k % p, n - k, 2 * k):
                for i in range(min(k, n - j - k)):
                    if (i + j) // (2 * p) == (i + j + k) // (2 * p):
                        pairs.append((i + j, i + j + k))
            k //= 2
        p *= 2
    return pairs


def _vmax(a, b):
    if a is None:
        return b
    if b is None:
        return a
    return jnp.maximum(a, b)


def _vmin(a, b):
    if a is None or b is None:
        return None
    return jnp.minimum(a, b)


def _top16_sorted(slabs):
    K = PEER_TOPK
    w = list(slabs)
    for a, b in _sorting_pairs(K):
        w[a], w[b] = _vmax(w[a], w[b]), _vmin(w[a], w[b])
    shift = SUBLANES // 2
    while shift >= 1:
        partner = [None if w[K - 1 - i] is None else pltpu.roll(w[K - 1 - i], shift, 0) for i in range(K)]
        w = [_vmax(w[i], partner[i]) for i in range(K)]
        stride = K // 2
        while stride >= 1:
            for i in range(K):
                if i & stride == 0:
                    w[i], w[i + stride] = _vmax(w[i], w[i + stride]), _vmin(w[i], w[i + stride])
            stride //= 2
        shift //= 2
    return w


def _allsum8(x):
    x = x + pltpu.roll(x, 4, 0)
    x = x + pltpu.roll(x, 2, 0)
    return x + pltpu.roll(x, 1, 0)


def _route_tile(s1, s2):
    K = PEER_TOPK
    S = SUBLANES
    n_slab = s1.shape[0] // S
    a1 = [s1[j * S:(j + 1) * S] for j in range(n_slab)]
    a2 = [s2[j * S:(j + 1) * S] for j in range(n_slab)]
    v1 = _top16_sorted(a1)
    v2 = _top16_sorted(a2)
    sub = lax.broadcasted_iota(jnp.int32, a1[0].shape, 0)

    def as_rows(v, lo):
        out = v[lo]
        for b in range(1, S):
            out = jnp.where(sub == b, v[lo + b], out)
        return out

    v2_lo, v2_hi, v1_hi = as_rows(v2, 0), as_rows(v2, S), as_rows(v1, S)
    lens = [K // (a + 1) for a in range(S)]
    cands = [v1[0] + v2_lo, v1[0] + v2_hi]
    for a in range(1, S):
        ca = v1[a] + v2_lo
        cands.append(ca if lens[a] >= S else jnp.where(sub < lens[a], ca, -jnp.inf))
    cands.append(v1_hi + v2[0])
    thr = _top16_sorted(cands + [None] * (K - len(cands)))[K - 1]
    x2_lo, x2_hi = jnp.exp(v2_lo - v2[0]), jnp.exp(v2_hi - v2[0])
    sel_lo, sel_hi = cands[0] >= thr, cands[1] >= thr
    zsum = jnp.where(sel_lo, x2_lo, 0.0) + jnp.where(sel_hi, x2_hi, 0.0)
    cnt = [_allsum8(jnp.where(sel_lo, 1.0, 0.0) + jnp.where(sel_hi, 1.0, 0.0))]
    for a in range(1, S):
        sel = cands[a + 1] >= thr
        zsum = zsum + jnp.where(sel, jnp.exp(v1[a] - v1[0]) * x2_lo, 0.0)
        cnt.append(_allsum8(jnp.where(sel, 1.0, 0.0)))
    zsum = zsum + jnp.where(cands[S + 1] >= thr, jnp.exp(v1_hi - v1[0]), 0.0)
    inv_z = 1.0 / _allsum8(zsum)
    r2, e2, c1, e1, pair0 = [], [], [], [], []
    for j in range(n_slab):
        r = jnp.full(a2[j].shape, float(K), F32)
        for k in reversed(range(K)):
            r = jnp.where(a2[j] >= v2[k], float(k), r)
        r2.append(r)
        e2.append(jnp.exp(a2[j] - v2[0]) * inv_z)
        c = jnp.where(a1[j] + v2[0] >= thr, 1.0, 0.0)
        pair0.append(c)
        for a in reversed(range(S)):
            c = jnp.where(a1[j] >= v1[a], cnt[a], c)
        c1.append(c)
        e1.append(jnp.exp(a1[j] - v1[0]))
    lor, land = jnp.logical_or, jnp.logical_and
    n_sel = functools.reduce(lambda a, b: a + b, cnt) + _allsum8(jnp.where(cands[S + 1] >= thr, 1.0, 0.0))
    f1 = land(v1[S - 1] == v1[S], cnt[S - 1] >= 2.0)
    for k in range(S - 1):
        f1 = lor(f1, land(v1[k] == v1[k + 1], cnt[k] != cnt[k + 1]))
    f1 = lor(f1, _allsum8(functools.reduce(lambda a, b: a + b, pair0)) > float(K))
    f2 = v2[0] == v2[1]
    for k in range(1, K - 1):
        f2 = lor(f2, land(v2[k] == v2[k + 1], cnt[0] > float(k)))
    r_sum = _allsum8(functools.reduce(lambda a, b: a + b, r2))
    any_tie2 = r_sum != float(K * (K - 1) // 2 + K * (n_slab * S - K))
    f2 = lor(f2, land(cnt[0] == float(K), any_tie2))
    flag = lor(lor(f1, f2), n_sel != float(K))
    cat = lambda xs: jnp.concatenate(xs, axis=0)
    return cat(r2), cat(e2), cat(c1), cat(e1), jnp.where(flag, 1.0, 0.0)


def _route_tile_exact(s1, s2):
    K = PEER_TOPK

    def take16(x):
        idx = lax.broadcasted_iota(jnp.int32, x.shape, 0).astype(F32)
        rank = jnp.full(x.shape, float(K), F32)
        vals, cur = [], x
        for k in range(K):
            m = jnp.max(cur, axis=0, keepdims=True)
            first = jnp.min(jnp.where(cur == m, idx, float(x.shape[0])), axis=0, keepdims=True)
            hit = idx == first
            rank = jnp.where(hit, float(k), rank)
            vals.append(m)
            cur = jnp.where(hit, -jnp.inf, cur)
        return rank, vals

    r1, v1 = take16(s1)
    r2, v2 = take16(s2)
    v2a = jnp.concatenate(v2, axis=0)
    x1 = [jnp.exp(v1[a] - v1[0]) for a in range(K)]
    x2a = jnp.exp(v2a - v2[0])
    cand = jnp.concatenate([v1[a] + v2a for a in range(K)], axis=0)
    crank, _ = take16(cand)
    sel = crank < float(K)
    z = jnp.zeros_like(v1[0])
    c1 = jnp.zeros(s1.shape, F32)
    for a in range(K):
        sel_a = sel[a * K:(a + 1) * K]
        z = z + jnp.sum(jnp.where(sel_a, x1[a] * x2a, 0.0), axis=0, keepdims=True)
        cnt_a = jnp.sum(jnp.where(sel_a, 1.0, 0.0), axis=0, keepdims=True)
        c1 = jnp.where(r1 == float(a), cnt_a, c1)
    return r2, jnp.exp(s2 - v2[0]) / z, c1, jnp.exp(s1 - v1[0])


def _prep_kernel(ht_ref, wq_ref, k1_ref, k2_ref, r2_ref, e2_ref, c1_ref, e1_ref, q_scr, flag_scr):
    nk = PEER_NKEYS
    q_scr[...] = _dot(wq_ref[...], ht_ref[...])

    def scores(h):
        base = pl.multiple_of(h * 2 * nk, 2 * nk)
        s1 = _dot(k1_ref[...], q_scr[pl.ds(base, nk), :].astype(BF16))
        s2 = _dot(k2_ref[...], q_scr[pl.ds(base + nk, nk), :].astype(BF16))
        return s1, s2

    def store(h, ts, r2, e2, c1, e1):
        r2_ref[h, :, ts] = _pack_rows(r2)
        e2_ref[h, :, ts] = _pack_rows(e2)
        c1_ref[h, :, ts] = c1
        e1_ref[h, :, ts] = e1

    n_tiles = ht_ref.shape[1] // LANES

    def head(h, flags):
        s1, s2 = scores(h)
        for t in range(n_tiles):
            ts = slice(t * LANES, (t + 1) * LANES)
            r2, e2, c1, e1, flag = _route_tile(s1[:, ts], s2[:, ts])
            store(h, ts, r2, e2, c1, e1)
            flag_scr[h, t * SUBLANES:(t + 1) * SUBLANES, :] = flag
            flags = jnp.maximum(flags, flag)
        return flags

    flags = lax.fori_loop(0, PEER_HEADS, head, jnp.zeros((SUBLANES, LANES), F32))

    @pl.when(jnp.max(flags) > 0.0)
    def _():
        def head_exact(h, carry):
            s1, s2 = scores(h)
            for t in range(n_tiles):
                ts = slice(t * LANES, (t + 1) * LANES)

                @pl.when(jnp.max(flag_scr[h, t * SUBLANES:(t + 1) * SUBLANES, :]) > 0.0)
                def _():
                    store(h, ts, *_route_tile_exact(s1[:, ts], s2[:, ts]))
            return carry

        lax.fori_loop(0, PEER_HEADS, head_exact, 0)


def _peer_prep(ht, wq_t, k1, k2, tok_tile, tok_off, n):
    d = ht.shape[0]
    T = tok_tile
    nk = PEER_NKEYS
    full = lambda a: pl.BlockSpec(a.shape, lambda i: (0,) * a.ndim)
    spec = lambda rows: pl.BlockSpec((PEER_HEADS, rows, T), lambda i: (0, 0, i))
    return pl.pallas_call(
        _prep_kernel,
        out_shape=[jax.ShapeDtypeStruct((PEER_HEADS, nk // 2, n), U32)] * 2
        + [jax.ShapeDtypeStruct((PEER_HEADS, nk, n), F32)] * 2,
        grid=(n // T,),
        in_specs=[pl.BlockSpec((d, T), lambda i: (0, i + tok_off)), full(wq_t), full(k1), full(k2)],
        out_specs=[spec(nk // 2), spec(nk // 2), spec(nk), spec(nk)],
        scratch_shapes=[pltpu.VMEM((wq_t.shape[0], T), F32),
                        pltpu.VMEM((PEER_HEADS, T // LANES * SUBLANES, LANES), F32)],
        compiler_params=_cparams("arbitrary"),
        name="peer_prep",
    )(ht, wq_t, k1, k2)


def _peer_kernel(ht_ref, u_ref, un_ref, vt_ref, vp_ref, r2_ref, e2_ref, c1_ref, e1_ref, x_ref, g_ref,
                 o_ref, acc_ref, *ap_refs):
    nk = PEER_NKEYS
    c = pl.program_id(1)
    n_sub = len(ap_refs) // 2
    a_refs, p_refs = ap_refs[:n_sub], ap_refs[n_sub:]
    sub = 2 * a_refs[0].shape[0]
    T = a_refs[0].shape[1]
    rows_per_sub = sub // nk
    PK = 16
    WK = PK // 2

    tok_blocks = [slice(n, min(n + PEER_MXU_TOKENS, T)) for n in range(0, T, PEER_MXU_TOKENS)]

    def activations_from(rows_ref, dst_ref):
        for tb in tok_blocks:
            dst_ref[:, tb] = _pack_rows(_dot(rows_ref, ht_ref[:, tb]))

    def activations(s):
        activations_from(u_ref[s * sub:(s + 1) * sub, :], a_refs[s])

    def gates(s):
        for t in range(T // LANES):
            ts = slice(t * LANES, (t + 1) * LANES)
            for j in range(rows_per_sub):
                row = s * rows_per_sub + j
                gate = [None] * (nk // PK)
                for h in range(PEER_HEADS):
                    c1 = c1_ref[h, :, ts][row:row + 1]
                    e1 = e1_ref[h, :, ts][row:row + 1]
                    c1 = jnp.broadcast_to(c1, (PK, LANES)).astype(BF16)
                    e1 = jnp.broadcast_to(e1, (PK, LANES)).astype(BF16)
                    for v in range(nk // PK):
                        ws = slice(v * WK, (v + 1) * WK)
                        r2 = _unpack_rows(r2_ref[h, ws, ts])
                        e2 = _unpack_rows(e2_ref[h, ws, ts])
                        g = jnp.where(r2 < c1, e2 * e1, 0.0)
                        gate[v] = g if h == 0 else gate[v] + g
                for v in range(nk // PK):
                    ws = slice((j * nk + v * PK) // 2, (j * nk + (v + 1) * PK) // 2)
                    a = _unpack_rows(a_refs[s][ws, ts])
                    th = jnp.tanh(a * (GELU_K0 + GELU_K1 * (a * a)))
                    p_refs[s][ws, ts] = pltpu.bitcast(gate[v] * (a * (0.5 + 0.5 * th)), U32)

    def project(terms):
        for tb in tok_blocks:
            parts = [_dot(cols, _unpack_rows(src[:, tb])) for cols, src in terms]
            acc_ref[:, tb] += functools.reduce(lambda a, b: a + b, parts)

    def term(s):
        return vt_ref[:, s * sub:(s + 1) * sub], p_refs[s]

    last = n_sub - 1
    assert n_sub % 2 == 0

    @pl.when(c == 0)
    def _():
        acc_ref[...] = jnp.zeros(acc_ref.shape, F32)
        p_refs[last][...] = jnp.zeros(p_refs[last].shape, U32)
        activations(0)

    pending = [(vp_ref[...], p_refs[last])]
    for s in range(n_sub):
        gates(s)
        if s < last:
            pending.append(term(s))
            activations(s + 1)
        if len(pending) == 2:
            project(pending)
            pending = []
    activations_from(un_ref[...], a_refs[0])

    @pl.when(c == pl.num_programs(1) - 1)
    def _():
        project([term(last)])
        o_ref[...] = x_ref[...] + g_ref[...] * acc_ref[...].T


def _peer_dense(ht, u_bf, vt_bf, routing, x, g2, tok_tile, tok_off, n_tok):
    r2, e2, c1, e1 = routing
    d = ht.shape[0]
    n_exp = u_bf.shape[0]
    T = tok_tile
    NC = PEER_EXPERT_CHUNK
    nk = PEER_NKEYS
    SUB = PEER_SUB_CHUNK
    n_sub = NC // SUB
    rt = lambda rows: pl.BlockSpec((PEER_HEADS, rows, T), lambda i, c: (0, 0, i))
    per_chunk = pl.BlockSpec((PEER_HEADS, NC // nk, T), lambda i, c: (0, c, i))
    return pl.pallas_call(
        _peer_kernel,
        out_shape=jax.ShapeDtypeStruct((n_tok, d), F32),
        grid=(n_tok // T, n_exp // NC),
        in_specs=[pl.BlockSpec((d, T), lambda i, c: (0, i + tok_off)),
                  pl.BlockSpec((NC, d), lambda i, c: (c, 0)),
                  pl.BlockSpec((SUB, d), lambda i, c: (jnp.minimum((c + 1) * n_sub, n_exp // SUB - 1), 0)),
                  pl.BlockSpec((d, NC), lambda i, c: (0, c)),
                  pl.BlockSpec((d, SUB), lambda i, c: (0, jnp.maximum(c * n_sub - 1, 0))),
                  rt(nk // 2), rt(nk // 2), per_chunk, per_chunk,
                  pl.BlockSpec((T, d), lambda i, c: (i + tok_off, 0)),
                  pl.BlockSpec((1, d), lambda i, c: (0, 0))],
        out_specs=pl.BlockSpec((T, d), lambda i, c: (i, 0)),
        scratch_shapes=[pltpu.VMEM((d, T), F32)] + [pltpu.VMEM((PEER_SUB_CHUNK // 2, T), U32)] * (2 * n_sub),
        compiler_params=_cparams("arbitrary", "arbitrary"),
        name="peer_dense",
    )(ht, u_bf, u_bf, vt_bf, vt_bf, r2, e2, c1, e1, x, g2)


def _rope_tables(L, n):
    t = jnp.arange(L)
    row = (t // GRID_W).astype(F32)
    col = (t % GRID_W).astype(F32)
    q = HEAD_DIM // 4
    inv = ROPE_THETA ** (-jnp.arange(q, dtype=F32) / q)
    ar = row[:, None] * inv
    ac = col[:, None] * inv
    cos = jnp.concatenate([jnp.cos(ar), jnp.cos(ar), jnp.cos(ac), jnp.cos(ac)], axis=-1)
    sin = jnp.concatenate([-jnp.sin(ar), jnp.sin(ar), -jnp.sin(ac), jnp.sin(ac)], axis=-1)
    cos = jnp.concatenate([cos, jnp.ones((n - L, HEAD_DIM), F32)], axis=0)
    sin = jnp.concatenate([sin, jnp.zeros((n - L, HEAD_DIM), F32)], axis=0)
    return jnp.tile(cos, (1, LANES // HEAD_DIM)), jnp.tile(sin, (1, LANES // HEAD_DIM))


def _block_diag(w):
    nb, di, do = w.shape
    eye = jnp.eye(nb, dtype=w.dtype)
    return (eye[:, None, :, None] * w[:, :, None, :]).reshape(nb * di, nb * do)


def kernel(x, c, ctx, c_ctx, w_mod, b_mod, norm1_g, norm2_g, w_in, w_out, lru_conv_w, lru_conv_b, lru_wa, lru_ba, lru_wx, lru_bx, lru_lam, win_qn_g, win_kn_g, win_sink, na_qn_g, na_kn_g, na_rpb, peer_wq, peer_k1, peer_k2, peer_u, peer_v):
    B, L, D = x.shape
    Lc = ctx.shape[1]
    depth = w_mod.shape[0]
    assert B == 1 and Lc == TOK_TILE and L % PREP_TOK_TILE == 0 and L // GRID_W >= 3 * NA_ROW_BLOCK
    n = L + Lc
    n_lat = L // TOK_TILE
    rows = L // GRID_W

    cvec = jnp.zeros((8, D), F32).at[0].set(c[0]).at[1].set(c_ctx)
    mods = _modulation(cvec, w_mod, b_mod)
    cos_t, sin_t = _rope_tables(L, n)
    tile2 = lambda g: jnp.tile(g, LANES // HEAD_DIM)

    xl, xc = x[0], ctx[0]
    for l in range(depth):
        with_ctx = l < depth - 1
        mod = mods[l]
        hg = jnp.zeros((8, LANES), F32)
        hg = hg.at[0].set(tile2(win_qn_g[l])).at[1].set(tile2(win_kn_g[l]))
        hg = hg.at[2].set(tile2(na_qn_g[l])).at[3].set(tile2(na_kn_g[l]))
        xa, ga, qw, kw, vw, qn, kn, vn = _in_proj(xl, xc, mod, norm1_g[l][None], w_in[l].astype(BF16),
                                                  cos_t, sin_t, hg)

        w_gates = jnp.concatenate([_block_diag(lru_wa[l, 0]), _block_diag(lru_wx[l, 0]),
                                   _block_diag(lru_wa[l, 1]), _block_diag(lru_wx[l, 1])], axis=1).astype(BF16)
        b_gates = jnp.concatenate([lru_ba[l, 0], lru_bx[l, 0], lru_ba[l, 1], lru_bx[l, 1]])[None]
        conv_w = jnp.zeros((8, D_LRU), F32).at[0:lru_conv_w.shape[1]].set(lru_conv_w[l])
        lam = jnp.zeros((8, D_LRU), F32).at[0:2].set(lru_lam[l])
        hf, hb = _lru(xa, conv_w, lru_conv_b[l][None], w_gates, b_gates, lam, n_lat)

        yb = _window_attention(win_sink[l], qw, kw, vw, L, with_ctx)
        yc = _neighborhood_attention(qn, kn, vn, _na_bias_table(na_rpb[l], rows), L, with_ctx)

        xn, ht = _out_proj(xl, xc, hf, hb, ga, yb, yc, w_out[l].astype(BF16), mod, norm2_g[l][None], with_ctx)

        route_w = (peer_wq[l].T.astype(BF16), peer_k1[l].astype(BF16), peer_k2[l].astype(BF16))
        u_bf = peer_u[l].astype(BF16)
        vt_bf = peer_v[l].T.astype(BF16)
        g2 = mod[:, 5 * D:6 * D]
        routing = _peer_prep(ht, *route_w, PREP_TOK_TILE, 0, L)
        xl_new = _peer_dense(ht, u_bf, vt_bf, routing, xn, g2[0:1], PEER_TOK_TILE, 0, L)
        if with_ctx:
            routing = _peer_prep(ht, *route_w, Lc, L // Lc, Lc)
            xc = _peer_dense(ht, u_bf, vt_bf, routing, xn, g2[1:2], Lc, L // Lc, Lc)
        xl = xl_new
    return xl[None]
```

```python
import functools

import numpy as np
import jax
import jax.numpy as jnp
from jax import lax
from jax.experimental import pallas as pl
from jax.experimental.pallas import tpu as pltpu

F32 = jnp.float32
BF16 = jnp.bfloat16
U32 = jnp.uint32

HEAD_DIM = 64
GRID_W = 64
EPS = 1e-6
ROPE_THETA = 10000.0
D_LRU = 256
LRU_BLOCKS = 4
LRU_C = 8.0
WIN_Q_HEADS = 6
WIN_KV_HEADS = 2
WIN_BLOCK = 128
NA_HEADS = 6
NA_KH = 8
NA_KW = 16
NA_ROW_BLOCK = 4
NA_KEY_ROWS = NA_ROW_BLOCK + NA_KH - 1
PEER_HEADS = 8
PEER_NKEYS = 128
PEER_TOPK = 16
D_WIN_Q = WIN_Q_HEADS * HEAD_DIM
D_WIN_KV = WIN_KV_HEADS * HEAD_DIM
D_NA = NA_HEADS * HEAD_DIM

LANES = 128
TOK_TILE = 256
PEER_TOK_TILE = 512
PEER_EXPERT_CHUNK = 2048
PEER_SUB_CHUNK = 256
PEER_MXU_TOKENS = 256
PREP_TOK_TILE = 1024
VMEM_LIMIT = 56 * 1024 * 1024
NEG = -1e30
GELU_K0 = float(np.sqrt(2.0 / np.pi))
GELU_K1 = 0.044715 * GELU_K0


def _cparams(*sem):
    return pltpu.CompilerParams(dimension_semantics=sem, vmem_limit_bytes=VMEM_LIMIT)


def _dot(a, b):
    return jnp.dot(a, b, preferred_element_type=F32)


def _pack_rows(x):
    return pltpu.bitcast(x.astype(BF16), U32)


def _unpack_rows(w):
    return pltpu.bitcast(w, BF16)


def _dot_nt(a, b):
    return lax.dot_general(a, b, (((1,), (1,)), ((), ())), preferred_element_type=F32)


def _mod_kernel(c_ref, w_ref, b_ref, o_ref):
    c = c_ref[...]
    a = (c * jax.nn.sigmoid(c)).astype(BF16)
    o_ref[0] = _dot(a, w_ref[0].astype(BF16)) + b_ref[0]


def _modulation(cvec, w_mod, b_mod):
    depth, d, d6 = w_mod.shape
    return pl.pallas_call(
        _mod_kernel,
        out_shape=jax.ShapeDtypeStruct((depth, 8, d6), F32),
        grid=(depth, d6 // d),
        in_specs=[pl.BlockSpec((8, d), lambda l, j: (0, 0)),
                  pl.BlockSpec((1, d, d), lambda l, j: (l, 0, j)),
                  pl.BlockSpec((1, 1, d), lambda l, j: (l, 0, j))],
        out_specs=pl.BlockSpec((1, 8, d), lambda l, j: (l, 0, j)),
        compiler_params=_cparams("arbitrary", "arbitrary"),
        name="modulation",
    )(cvec, w_mod, b_mod.reshape(depth, 1, d6))


def _pair_mean_matrix():
    r = lax.broadcasted_iota(jnp.int32, (LANES, LANES), 0) < HEAD_DIM
    c = lax.broadcasted_iota(jnp.int32, (LANES, LANES), 1) < HEAD_DIM
    return jnp.where(r == c, 1.0 / HEAD_DIM, 0.0).astype(BF16)


def _head_norm(z, gain, bd):
    z2 = z * z
    hi = z2.astype(BF16)
    lo = (z2 - hi.astype(F32)).astype(BF16)
    ms = _dot(hi, bd) + _dot(lo, bd)
    return z * lax.rsqrt(ms + EPS) * gain


def _rope(z, cos, sin):
    lane = lax.broadcasted_iota(jnp.int32, z.shape, 1)
    first = (lane & 16) == 0
    partner = jnp.where(first, pltpu.roll(z, LANES - 16, 1), pltpu.roll(z, 16, 1))
    return z * cos + partner * sin


def _in_kernel(xl_ref, xc_ref, mod_ref, g_ref, w_ref, cos_ref, sin_ref, hg_ref,
               xa_ref, ga_ref, qw_ref, kw_ref, vw_ref, qn_ref, kn_ref, vn_ref, *, n_lat):
    d = xl_ref.shape[1]
    i = pl.program_id(0)
    row = (i == n_lat).astype(jnp.int32)
    is_ctx = jnp.full((TOK_TILE, 1), row) == 1
    x = jnp.where(is_ctx, xc_ref[...], xl_ref[...])
    ms = jnp.mean(x * x, axis=-1, keepdims=True)
    y = x * lax.rsqrt(ms + EPS) * g_ref[...]
    sh = mod_ref[pl.ds(row, 1), 0:d]
    sc = mod_ref[pl.ds(row, 1), d:2 * d]
    h = (y * (1.0 + sc) + sh).astype(BF16)
    z = _dot(h, w_ref[...])

    bd = _pair_mean_matrix()
    cos = cos_ref[...]
    sin = sin_ref[...]
    scale = HEAD_DIM ** -0.5
    o = 0
    xa_ref[...] = z[:, o:o + D_LRU]
    o += D_LRU
    ga_ref[...] = z[:, o:o + D_LRU]
    o += D_LRU
    for g in range(D_WIN_Q // LANES):
        zz = _rope(_head_norm(z[:, o:o + LANES], hg_ref[0:1], bd), cos, sin)
        qw_ref[:, g * LANES:(g + 1) * LANES] = (zz * scale).astype(BF16)
        o += LANES
    kw_ref[...] = _rope(_head_norm(z[:, o:o + LANES], hg_ref[1:2], bd), cos, sin).astype(BF16)
    o += LANES
    vw_ref[...] = z[:, o:o + LANES].astype(BF16)
    o += LANES
    for g in range(D_NA // LANES):
        zz = _head_norm(z[:, o:o + LANES], hg_ref[2:3], bd)
        qn_ref[:, g * LANES:(g + 1) * LANES] = (zz * scale).astype(BF16)
        o += LANES
    for g in range(D_NA // LANES):
        zz = _head_norm(z[:, o:o + LANES], hg_ref[3:4], bd)
        kn_ref[:, g * LANES:(g + 1) * LANES] = zz.astype(BF16)
        o += LANES
    vn_ref[...] = z[:, o:o + D_NA].astype(BF16)


def _in_proj(xl, xc, mod, g, w_bf, cos_t, sin_t, hg):
    L, d = xl.shape
    n_lat = L // TOK_TILE
    n = L + xc.shape[0]
    d_in = w_bf.shape[1]
    T = TOK_TILE
    tok = lambda w: pl.BlockSpec((T, w), lambda i: (i, 0))
    full = lambda a: pl.BlockSpec(a.shape, lambda i: (0,) * a.ndim)
    widths = (D_LRU, D_LRU, D_WIN_Q, D_WIN_KV, D_WIN_KV, D_NA, D_NA, D_NA)
    dtypes = (F32, F32, BF16, BF16, BF16, BF16, BF16, BF16)
    return pl.pallas_call(
        functools.partial(_in_kernel, n_lat=n_lat),
        out_shape=[jax.ShapeDtypeStruct((n, w), t) for w, t in zip(widths, dtypes)],
        grid=(n_lat + 1,),
        in_specs=[pl.BlockSpec((T, d), lambda i: (jnp.minimum(i, n_lat - 1), 0)),
                  pl.BlockSpec((T, d), lambda i: (0, 0)),
                  full(mod), full(g), full(w_bf), tok(LANES), tok(LANES), full(hg)],
        out_specs=[tok(w) for w in widths],
        compiler_params=_cparams("arbitrary"),
        name="in_proj",
    )(xl, xc, mod, g, w_bf, cos_t, sin_t, hg)


def _chunk_scan(a, b, reverse):
    T = a.shape[0]
    rows = lax.broadcasted_iota(jnp.int32, a.shape, 0)
    s = 1
    while s < T:
        if reverse:
            edge = rows >= T - s
            shift = T - s
        else:
            edge = rows < s
            shift = s
        a_s = jnp.where(edge, 1.0, pltpu.roll(a, shift, 0))
        b_s = jnp.where(edge, 0.0, pltpu.roll(b, shift, 0))
        b = a * b_s + b
        a = a * a_s
        s *= 2
    return a, b


def _lru_direction(xm, ph, nh, pv, nv, cw, cb, w, bias, spl, carry_ref, first, reverse):
    T, C = xm.shape
    rows = lax.broadcasted_iota(jnp.int32, (T, C), 0)
    p6 = ph[6:7] * pv
    p7 = ph[7:8] * pv
    n0 = nh[0:1] * nv
    x_m1 = jnp.where(rows == 0, p7, pltpu.roll(xm, 1, 0))
    x_m2 = jnp.where(rows == 0, p6, jnp.where(rows == 1, p7, pltpu.roll(xm, 2, 0)))
    x_p1 = jnp.where(rows == T - 1, n0, pltpu.roll(xm, T - 1, 0))
    u = cw[0:1] * x_m2 + cw[1:2] * x_m1 + cw[2:3] * xm + cw[3:4] * x_p1 + cb
    zz = _dot(u.astype(BF16), w) + bias
    r = jax.nn.sigmoid(zz[:, :C])
    ig = jax.nn.sigmoid(zz[:, C:])
    log_a = -LRU_C * r * spl
    a = jnp.exp(log_a)
    b = jnp.sqrt(-jnp.tanh(log_a) * (a * a + 1.0)) * ig * u
    a, b = _chunk_scan(a, b, reverse)

    @pl.when(first)
    def _():
        carry_ref[...] = jnp.zeros(carry_ref.shape, F32)

    h = a * carry_ref[0:1] + b
    edge = h[0:1] if reverse else h[T - 1:T]
    carry_ref[...] = jnp.broadcast_to(edge, carry_ref.shape)
    return h


def _lru_kernel(xf_ref, pf_ref, nf_ref, xb_ref, pb_ref, nb_ref, cw_ref, cb_ref, w_ref, b_ref,
                lam_ref, hf_ref, hb_ref, cf_ref, cbk_ref, *, n_lat):
    j = pl.program_id(0)
    C = D_LRU
    fblk = jnp.where(j == 0, n_lat, j - 1)
    bblk = jnp.where(j == 0, n_lat, n_lat - j)
    lam = lam_ref[...]
    spl = jnp.maximum(-lam, 0.0) + jnp.log1p(jnp.exp(-jnp.abs(lam)))
    cw = cw_ref[...]
    cb = cb_ref[...]

    def halo_valid(blk):
        pv = jnp.logical_and(blk != 0, blk != n_lat).astype(F32)
        nv = jnp.logical_and(blk != n_lat - 1, blk != n_lat).astype(F32)
        return pv, nv

    pv, nv = halo_valid(fblk)
    hf_ref[...] = _lru_direction(xf_ref[...], pf_ref[...], nf_ref[...], pv, nv, cw, cb,
                                 w_ref[:, 0:2 * C], b_ref[:, 0:2 * C], spl[0:1], cf_ref, j == 0, False)
    pv, nv = halo_valid(bblk)
    hb_ref[...] = _lru_direction(xb_ref[...], pb_ref[...], nb_ref[...], pv, nv, cw, cb,
                                 w_ref[:, 2 * C:4 * C], b_ref[:, 2 * C:4 * C], spl[1:2], cbk_ref, j == 0, True)


def _lru(xa, conv_w, conv_b, w_gates, b_gates, lam, n_lat):
    n, C = xa.shape
    T = TOK_TILE
    sub = T // 8
    nblk8 = n // 8
    fblk = lambda j: jnp.where(j == 0, n_lat, j - 1)
    bblk = lambda j: jnp.where(j == 0, n_lat, n_lat - j)
    prev8 = lambda blk: jnp.maximum(blk * sub - 1, 0)
    next8 = lambda blk: jnp.minimum((blk + 1) * sub, nblk8 - 1)
    full = lambda a: pl.BlockSpec(a.shape, lambda j: (0,) * a.ndim)
    return pl.pallas_call(
        functools.partial(_lru_kernel, n_lat=n_lat),
        out_shape=[jax.ShapeDtypeStruct((n, C), F32)] * 2,
        grid=(n_lat + 1,),
        in_specs=[pl.BlockSpec((T, C), lambda j: (fblk(j), 0)),
                  pl.BlockSpec((8, C), lambda j: (prev8(fblk(j)), 0)),
                  pl.BlockSpec((8, C), lambda j: (next8(fblk(j)), 0)),
                  pl.BlockSpec((T, C), lambda j: (bblk(j), 0)),
                  pl.BlockSpec((8, C), lambda j: (prev8(bblk(j)), 0)),
                  pl.BlockSpec((8, C), lambda j: (next8(bblk(j)), 0)),
                  full(conv_w), full(conv_b), full(w_gates), full(b_gates), full(lam)],
        out_specs=[pl.BlockSpec((T, C), lambda j: (fblk(j), 0)),
                   pl.BlockSpec((T, C), lambda j: (bblk(j), 0))],
        scratch_shapes=[pltpu.VMEM((8, C), F32), pltpu.VMEM((8, C), F32)],
        compiler_params=_cparams("arbitrary"),
        name="rglru",
    )(xa, xa, xa, xa, xa, xa, conv_w, conv_b, w_gates, b_gates, lam)


def _win_kernel(sink_ref, q_ref, kp_ref, kc_ref, kn_ref, vp_ref, vc_ref, vn_ref, kx_ref, vx_ref,
                o_ref, *, nb):
    W = WIN_BLOCK
    R = WIN_Q_HEADS // WIN_KV_HEADS
    b = pl.program_id(0)
    blk = jnp.where(b < nb, b, -4)
    rows = lax.broadcasted_iota(jnp.int32, (R * W, 3 * W), 0)
    cols = lax.broadcasted_iota(jnp.int32, (R * W, 3 * W), 1)
    kpos = (blk - 1) * W + cols
    qpos = blk * W + (rows & (W - 1))
    valid = jnp.logical_and(jnp.abs(kpos - qpos) <= W, jnp.logical_and(kpos >= 0, kpos < nb * W))
    row1 = lax.broadcasted_iota(jnp.int32, (R * W, 1), 0)
    q = q_ref[...]
    for g in range(WIN_KV_HEADS):
        ls = slice(g * HEAD_DIM, (g + 1) * HEAD_DIM)
        qs = jnp.concatenate([q[:, (g * R + r) * HEAD_DIM:(g * R + r + 1) * HEAD_DIM] for r in range(R)], axis=0)
        kl = jnp.concatenate([kp_ref[:, ls], kc_ref[:, ls], kn_ref[:, ls]], axis=0)
        vl = jnp.concatenate([vp_ref[:, ls], vc_ref[:, ls], vn_ref[:, ls]], axis=0)
        s_loc = jnp.where(valid, _dot_nt(qs, kl), NEG)
        s_ctx = _dot_nt(qs, kx_ref[:, ls])
        sink = jnp.full((R * W, 1), sink_ref[g * R], F32)
        for r in range(1, R):
            sink = jnp.where(row1 >= r * W, sink_ref[g * R + r], sink)
        m = jnp.maximum(jnp.maximum(jnp.max(s_loc, axis=-1, keepdims=True),
                                    jnp.max(s_ctx, axis=-1, keepdims=True)), sink)
        p_loc = jnp.exp(s_loc - m)
        p_ctx = jnp.exp(s_ctx - m)
        den = (jnp.sum(p_loc, axis=-1, keepdims=True) + jnp.sum(p_ctx, axis=-1, keepdims=True)
               + jnp.exp(sink - m))
        o = (_dot(p_loc.astype(BF16), vl) + _dot(p_ctx.astype(BF16), vx_ref[:, ls])) / den
        for r in range(R):
            hh = g * R + r
            o_ref[:, hh * HEAD_DIM:(hh + 1) * HEAD_DIM] = o[r * W:(r + 1) * W].astype(o_ref.dtype)


def _window_attention(sink, qw, kw, vw, L, with_ctx):
    n = qw.shape[0]
    W = WIN_BLOCK
    nb = L // W
    nq = n // W if with_ctx else nb
    ctx_blk = L // TOK_TILE
    lat = lambda b: jnp.minimum(b, nb - 1)
    kv = lambda f: pl.BlockSpec((W, D_WIN_KV), lambda b: (f(b), 0))
    prv = lambda b: jnp.maximum(lat(b) - 1, 0)
    nxt = lambda b: jnp.minimum(lat(b) + 1, nb - 1)
    ctx = pl.BlockSpec((n - L, D_WIN_KV), lambda b: (ctx_blk, 0))
    return pl.pallas_call(
        functools.partial(_win_kernel, nb=nb),
        out_shape=jax.ShapeDtypeStruct((nq * W, D_WIN_Q), BF16),
        grid=(nq,),
        in_specs=[pl.BlockSpec(memory_space=pltpu.SMEM),
                  pl.BlockSpec((W, D_WIN_Q), lambda b: (b, 0)),
                  kv(prv), kv(lat), kv(nxt), kv(prv), kv(lat), kv(nxt), ctx, ctx],
        out_specs=pl.BlockSpec((W, D_WIN_Q), lambda b: (b, 0)),
        compiler_params=_cparams("arbitrary"),
        name="window_attn",
    )(sink, qw, kw, kw, kw, vw, vw, vw, kw, vw)


def _na_kernel(q_ref, kl_ref, vl_ref, kx_ref, vx_ref, bias_ref, o_ref, *, n_blocks):
    b = pl.program_id(0)
    nq = q_ref.shape[0]
    latent = jnp.full((nq, 1), (b < n_blocks).astype(jnp.int32)) == 1
    q = q_ref[...]
    for h in range(NA_HEADS):
        ls = slice(h * HEAD_DIM, (h + 1) * HEAD_DIM)
        qh = q[:, ls]
        s_loc = jnp.where(latent, _dot_nt(qh, kl_ref[:, ls]) + bias_ref[0, h], NEG)
        s_ctx = _dot_nt(qh, kx_ref[:, ls])
        m = jnp.maximum(jnp.max(s_loc, axis=-1, keepdims=True), jnp.max(s_ctx, axis=-1, keepdims=True))
        p_loc = jnp.exp(s_loc - m)
        p_ctx = jnp.exp(s_ctx - m)
        den = jnp.sum(p_loc, axis=-1, keepdims=True) + jnp.sum(p_ctx, axis=-1, keepdims=True)
        o = (_dot(p_loc.astype(BF16), vl_ref[:, ls]) + _dot(p_ctx.astype(BF16), vx_ref[:, ls])) / den
        o_ref[:, ls] = o.astype(o_ref.dtype)


def _na_bias_table(rpb, rows):
    R, KR = NA_ROW_BLOCK, NA_KEY_ROWS
    qc = np.arange(GRID_W)[:, None]
    kc = np.arange(GRID_W)[None, :]
    qstart = np.clip(qc - NA_KW // 2, 0, GRID_W - NA_KW)
    inside = (kc - qstart >= 0) & (kc - qstart < NA_KW)
    pad = GRID_W - NA_KW
    rp = jnp.pad(rpb.astype(F32), ((0, 0), (0, 0), (pad, pad)))
    shifted = jnp.stack([rp[:, :, NA_KW - 1 - q + pad:NA_KW - 1 - q + pad + GRID_W] for q in range(GRID_W)], axis=2)
    per_dr = jnp.where(inside[None, None], shifted, NEG)
    masked = jnp.full((NA_HEADS, GRID_W, GRID_W), NEG, F32)
    half = NA_KH // 2
    cases = []
    for r0, ks in ((0, 0), (half, 0), (rows - R, rows - KR)):
        row_blocks = []
        for rr in range(R):
            r = r0 + rr
            kr = min(max(r - half, 0), rows - NA_KH)
            blocks = [per_dr[:, ks + kk - r + NA_KH - 1] if kr <= ks + kk < kr + NA_KH else masked
                      for kk in range(KR)]
            row_blocks.append(jnp.concatenate(blocks, axis=-1))
        cases.append(jnp.concatenate(row_blocks, axis=1))
    return jnp.stack(cases)


def _neighborhood_attention(qn, kn, vn, bias, L, with_ctx):
    n = qn.shape[0]
    R, KR = NA_ROW_BLOCK, NA_KEY_ROWS
    rows = L // GRID_W
    nb = rows // R
    nq = R * GRID_W
    steps = n // nq if with_ctx else nb
    ctx_blk = L // TOK_TILE
    half = NA_KH // 2

    def blk_of(b):
        return jnp.minimum(b, nb - 1)

    def key_start(b):
        return jnp.clip(blk_of(b) * R - half, 0, rows - KR) * GRID_W

    def case_of(b):
        return jnp.where(blk_of(b) == 0, 0, jnp.where(blk_of(b) == nb - 1, 2, 1))

    loc = pl.BlockSpec((pl.Element(KR * GRID_W), pl.Element(D_NA)), lambda b: (key_start(b), 0))
    ctx = pl.BlockSpec((n - L, D_NA), lambda b: (ctx_blk, 0))
    return pl.pallas_call(
        functools.partial(_na_kernel, n_blocks=nb),
        out_shape=jax.ShapeDtypeStruct((steps * nq, D_NA), BF16),
        grid=(steps,),
        in_specs=[pl.BlockSpec((nq, D_NA), lambda b: (b, 0)), loc, loc, ctx, ctx,
                  pl.BlockSpec((1, NA_HEADS, nq, KR * GRID_W), lambda b: (case_of(b), 0, 0, 0))],
        out_specs=pl.BlockSpec((nq, D_NA), lambda b: (b, 0)),
        compiler_params=_cparams("arbitrary"),
        name="neighborhood_attn",
    )(qn, kn, vn, kn, vn, bias)


def _out_kernel(xl_ref, xc_ref, hf_ref, hb_ref, ga_ref, yb_ref, yc_ref, w_ref, mod_ref, g_ref,
                xo_ref, ht_ref, *, n_lat):
    d = xl_ref.shape[1]
    i = pl.program_id(0)
    row = (i == n_lat).astype(jnp.int32)
    is_ctx = jnp.full((TOK_TILE, 1), row) == 1
    x = jnp.where(is_ctx, xc_ref[...], xl_ref[...])
    ya = ((hf_ref[...] + hb_ref[...]) * jax.nn.gelu(ga_ref[...])).astype(BF16)
    o1 = D_LRU
    o2 = D_LRU + D_WIN_Q
    mix = (_dot(ya, w_ref[0:o1]) + _dot(yb_ref[...], w_ref[o1:o2]) + _dot(yc_ref[...], w_ref[o2:o2 + D_NA]))
    xn = x + mod_ref[pl.ds(row, 1), 2 * d:3 * d] * mix
    xo_ref[...] = xn
    ms = jnp.mean(xn * xn, axis=-1, keepdims=True)
    y = xn * lax.rsqrt(ms + EPS) * g_ref[...]
    h2 = y * (1.0 + mod_ref[pl.ds(row, 1), 4 * d:5 * d]) + mod_ref[pl.ds(row, 1), 3 * d:4 * d]
    ht_ref[...] = h2.T.astype(BF16)


def _out_proj(xl, xc, hf, hb, ga, yb, yc, w_bf, mod, g, with_ctx):
    L, d = xl.shape
    T = TOK_TILE
    n_lat = L // T
    nt = n_lat + 1 if with_ctx else n_lat
    tok = lambda w: pl.BlockSpec((T, w), lambda i: (i, 0))
    full = lambda a: pl.BlockSpec(a.shape, lambda i: (0,) * a.ndim)
    return pl.pallas_call(
        functools.partial(_out_kernel, n_lat=n_lat),
        out_shape=[jax.ShapeDtypeStruct((nt * T, d), F32), jax.ShapeDtypeStruct((d, nt * T), BF16)],
        grid=(nt,),
        in_specs=[pl.BlockSpec((T, d), lambda i: (jnp.minimum(i, n_lat - 1), 0)),
                  pl.BlockSpec((T, d), lambda i: (0, 0)),
                  tok(D_LRU), tok(D_LRU), tok(D_LRU), tok(D_WIN_Q), tok(D_NA),
                  full(w_bf), full(mod), full(g)],
        out_specs=[tok(d), pl.BlockSpec((d, T), lambda i: (0, i))],
        compiler_params=_cparams("arbitrary"),
        name="out_proj",
    )(xl, xc, hf, hb, ga, yb, yc, w_bf, mod, g)


SUBLANES = 8


def _sorting_pairs(n):
    pairs, p = [], 1
    while p < n:
        k = p
        while k >= 1:
            for j in range(k % p, n - k, 2 * k):
                for i in range(min(k, n - j - k)):
                    if (i + j) // (2 * p) == (i + j + k) // (2 * p):
                        pairs.append((i + j, i + j + k))
            k //= 2
        p *= 2
    return pairs


def _vmax(a, b):
    if a is None:
        return b
    if b is None:
        return a
    return jnp.maximum(a, b)


def _vmin(a, b):
    if a is None or b is None:
        return None
    return jnp.minimum(a, b)


def _top16_sorted(slabs):
    K = PEER_TOPK
    w = list(slabs)
    for a, b in _sorting_pairs(K):
        w[a], w[b] = _vmax(w[a], w[b]), _vmin(w[a], w[b])
    shift = SUBLANES // 2
    while shift >= 1:
        partner = [None if w[K - 1 - i] is None else pltpu.roll(w[K - 1 - i], shift, 0) for i in range(K)]
        w = [_vmax(w[i], partner[i]) for i in range(K)]
        stride = K // 2
        while stride >= 1:
            for i in range(K):
                if i & stride == 0:
                    w[i], w[i + stride] = _vmax(w[i], w[i + stride]), _vmin(w[i], w[i + stride])
            stride //= 2
        shift //= 2
    return w


def _allsum8(x):
    x = x + pltpu.roll(x, 4, 0)
    x = x + pltpu.roll(x, 2, 0)
    return x + pltpu.roll(x, 1, 0)


def _route_tile(s1, s2):
    K = PEER_TOPK
    S = SUBLANES
    n_slab = s1.shape[0] // S
    a1 = [s1[j * S:(j + 1) * S] for j in range(n_slab)]
    a2 = [s2[j * S:(j + 1) * S] for j in range(n_slab)]
    v1 = _top16_sorted(a1)
    v2 = _top16_sorted(a2)
    sub = lax.broadcasted_iota(jnp.int32, a1[0].shape, 0)

    def as_rows(v, lo):
        out = v[lo]
        for b in range(1, S):
            out = jnp.where(sub == b, v[lo + b], out)
        return out

    v2_lo, v2_hi, v1_hi = as_rows(v2, 0), as_rows(v2, S), as_rows(v1, S)
    lens = [K // (a + 1) for a in range(S)]
    cands = [v1[0] + v2_lo, v1[0] + v2_hi]
    for a in range(1, S):
        ca = v1[a] + v2_lo
        cands.append(ca if lens[a] >= S else jnp.where(sub < lens[a], ca, -jnp.inf))
    cands.append(v1_hi + v2[0])
    thr = _top16_sorted(cands + [None] * (K - len(cands)))[K - 1]
    x2_lo, x2_hi = jnp.exp(v2_lo - v2[0]), jnp.exp(v2_hi - v2[0])
    sel_lo, sel_hi = cands[0] >= thr, cands[1] >= thr
    zsum = jnp.where(sel_lo, x2_lo, 0.0) + jnp.where(sel_hi, x2_hi, 0.0)
    cnt = [_allsum8(jnp.where(sel_lo, 1.0, 0.0) + jnp.where(sel_hi, 1.0, 0.0))]
    for a in range(1, S):
        sel = cands[a + 1] >= thr
        zsum = zsum + jnp.where(sel, jnp.exp(v1[a] - v1[0]) * x2_lo, 0.0)
        cnt.append(_allsum8(jnp.where(sel, 1.0, 0.0)))
    zsum = zsum + jnp.where(cands[S + 1] >= thr, jnp.exp(v1_hi - v1[0]), 0.0)
    inv_z = 1.0 / _allsum8(zsum)
    r2, e2, c1, e1, pair0 = [], [], [], [], []
    for j in range(n_slab):
        r = jnp.full(a2[j].shape, float(K), F32)
        for k in reversed(range(K)):
            r = jnp.where(a2[j] >= v2[k], float(k), r)
        r2.append(r)
        e2.append(jnp.exp(a2[j] - v2[0]) * inv_z)
        c = jnp.where(a1[j] + v2[0] >= thr, 1.0, 0.0)
        pair0.append(c)
        for a in reversed(range(S)):
            c = jnp.where(a1[j] >= v1[a], cnt[a], c)
        c1.append(c)
        e1.append(jnp.exp(a1[j] - v1[0]))
    lor, land = jnp.logical_or, jnp.logical_and
    n_sel = functools.reduce(lambda a, b: a + b, cnt) + _allsum8(jnp.where(cands[S + 1] >= thr, 1.0, 0.0))
    f1 = land(v1[S - 1] == v1[S], cnt[S - 1] >= 2.0)
    for k in range(S - 1):
        f1 = lor(f1, land(v1[k] == v1[k + 1], cnt[k] != cnt[k + 1]))
    f1 = lor(f1, _allsum8(functools.reduce(lambda a, b: a + b, pair0)) > float(K))
    f2 = v2[0] == v2[1]
    for k in range(1, K - 1):
        f2 = lor(f2, land(v2[k] == v2[k + 1], cnt[0] > float(k)))
    r_sum = _allsum8(functools.reduce(lambda a, b: a + b, r2))
    any_tie2 = r_sum != float(K * (K - 1) // 2 + K * (n_slab * S - K))
    f2 = lor(f2, land(cnt[0] == float(K), any_tie2))
    flag = lor(lor(f1, f2), n_sel != float(K))
    cat = lambda xs: jnp.concatenate(xs, axis=0)
    return cat(r2), cat(e2), cat(c1), cat(e1), jnp.where(flag, 1.0, 0.0)


def _route_tile_exact(s1, s2):
    K = PEER_TOPK

    def take16(x):
        idx = lax.broadcasted_iota(jnp.int32, x.shape, 0).astype(F32)
        rank = jnp.full(x.shape, float(K), F32)
        vals, cur = [], x
        for k in range(K):
            m = jnp.max(cur, axis=0, keepdims=True)
            first = jnp.min(jnp.where(cur == m, idx, float(x.shape[0])), axis=0, keepdims=True)
            hit = idx == first
            rank = jnp.where(hit, float(k), rank)
            vals.append(m)
            cur = jnp.where(hit, -jnp.inf, cur)
        return rank, vals

    r1, v1 = take16(s1)
    r2, v2 = take16(s2)
    v2a = jnp.concatenate(v2, axis=0)
    x1 = [jnp.exp(v1[a] - v1[0]) for a in range(K)]
    x2a = jnp.exp(v2a - v2[0])
    cand = jnp.concatenate([v1[a] + v2a for a in range(K)], axis=0)
    crank, _ = take16(cand)
    sel = crank < float(K)
    z = jnp.zeros_like(v1[0])
    c1 = jnp.zeros(s1.shape, F32)
    for a in range(K):
        sel_a = sel[a * K:(a + 1) * K]
        z = z + jnp.sum(jnp.where(sel_a, x1[a] * x2a, 0.0), axis=0, keepdims=True)
        cnt_a = jnp.sum(jnp.where(sel_a, 1.0, 0.0), axis=0, keepdims=True)
        c1 = jnp.where(r1 == float(a), cnt_a, c1)
    return r2, jnp.exp(s2 - v2[0]) / z, c1, jnp.exp(s1 - v1[0])


def _prep_kernel(ht_ref, wq_ref, k1_ref, k2_ref, r2_ref, e2_ref, c1_ref, e1_ref, q_scr, flag_scr):
    nk = PEER_NKEYS
    q_scr[...] = _dot(wq_ref[...], ht_ref[...])

    def scores(h):
        base = pl.multiple_of(h * 2 * nk, 2 * nk)
        s1 = _dot(k1_ref[...], q_scr[pl.ds(base, nk), :].astype(BF16))
        s2 = _dot(k2_ref[...], q_scr[pl.ds(base + nk, nk), :].astype(BF16))
        return s1, s2

    def store(h, ts, r2, e2, c1, e1):
        r2_ref[h, :, ts] = _pack_rows(r2)
        e2_ref[h, :, ts] = _pack_rows(e2)
        c1_ref[h, :, ts] = c1
        e1_ref[h, :, ts] = e1

    n_tiles = ht_ref.shape[1] // LANES

    def head(h, flags):
        s1, s2 = scores(h)
        for t in range(n_tiles):
            ts = slice(t * LANES, (t + 1) * LANES)
            r2, e2, c1, e1, flag = _route_tile(s1[:, ts], s2[:, ts])
            store(h, ts, r2, e2, c1, e1)
            flag_scr[h, t * SUBLANES:(t + 1) * SUBLANES, :] = flag
            flags = jnp.maximum(flags, flag)
        return flags

    flags = lax.fori_loop(0, PEER_HEADS, head, jnp.zeros((SUBLANES, LANES), F32))

    @pl.when(jnp.max(flags) > 0.0)
    def _():
        def head_exact(h, carry):
            s1, s2 = scores(h)
            for t in range(n_tiles):
                ts = slice(t * LANES, (t + 1) * LANES)

                @pl.when(jnp.max(flag_scr[h, t * SUBLANES:(t + 1) * SUBLANES, :]) > 0.0)
                def _():
                    store(h, ts, *_route_tile_exact(s1[:, ts], s2[:, ts]))
            return carry

        lax.fori_loop(0, PEER_HEADS, head_exact, 0)


def _peer_prep(ht, wq_t, k1, k2, tok_tile, tok_off, n):
    d = ht.shape[0]
    T = tok_tile
    nk = PEER_NKEYS
    full = lambda a: pl.BlockSpec(a.shape, lambda i: (0,) * a.ndim)
    spec = lambda rows: pl.BlockSpec((PEER_HEADS, rows, T), lambda i: (0, 0, i))
    return pl.pallas_call(
        _prep_kernel,
        out_shape=[jax.ShapeDtypeStruct((PEER_HEADS, nk // 2, n), U32)] * 2
        + [jax.ShapeDtypeStruct((PEER_HEADS, nk, n), F32)] * 2,
        grid=(n // T,),
        in_specs=[pl.BlockSpec((d, T), lambda i: (0, i + tok_off)), full(wq_t), full(k1), full(k2)],
        out_specs=[spec(nk // 2), spec(nk // 2), spec(nk), spec(nk)],
        scratch_shapes=[pltpu.VMEM((wq_t.shape[0], T), F32),
                        pltpu.VMEM((PEER_HEADS, T // LANES * SUBLANES, LANES), F32)],
        compiler_params=_cparams("arbitrary"),
        name="peer_prep",
    )(ht, wq_t, k1, k2)


def _peer_kernel(ht_ref, u_ref, un_ref, vt_ref, vp_ref, r2_ref, e2_ref, c1_ref, e1_ref, x_ref, g_ref,
                 o_ref, acc_ref, *ap_refs):
    nk = PEER_NKEYS
    c = pl.program_id(1)
    n_sub = len(ap_refs) // 2
    a_refs, p_refs = ap_refs[:n_sub], ap_refs[n_sub:]
    sub = 2 * a_refs[0].shape[0]
    T = a_refs[0].shape[1]
    rows_per_sub = sub // nk
    PK = 16
    WK = PK // 2

    tok_blocks = [slice(n, min(n + PEER_MXU_TOKENS, T)) for n in range(0, T, PEER_MXU_TOKENS)]

    def activations_from(rows_ref, dst_ref):
        for tb in tok_blocks:
            dst_ref[:, tb] = _pack_rows(_dot(rows_ref, ht_ref[:, tb]))

    def activations(s):
        activations_from(u_ref[s * sub:(s + 1) * sub, :], a_refs[s])

    def gates(s):
        for t in range(T // LANES):
            ts = slice(t * LANES, (t + 1) * LANES)
            for j in range(rows_per_sub):
                row = s * rows_per_sub + j
                gate = [None] * (nk // PK)
                for h in range(PEER_HEADS):
                    c1 = c1_ref[h, :, ts][row:row + 1]
                    e1 = e1_ref[h, :, ts][row:row + 1]
                    c1 = jnp.broadcast_to(c1, (PK, LANES)).astype(BF16)
                    e1 = jnp.broadcast_to(e1, (PK, LANES)).astype(BF16)
                    for v in range(nk // PK):
                        ws = slice(v * WK, (v + 1) * WK)
                        r2 = _unpack_rows(r2_ref[h, ws, ts])
                        e2 = _unpack_rows(e2_ref[h, ws, ts])
                        g = jnp.where(r2 < c1, e2 * e1, 0.0)
                        gate[v] = g if h == 0 else gate[v] + g
                for v in range(nk // PK):
                    ws = slice((j * nk + v * PK) // 2, (j * nk + (v + 1) * PK) // 2)
                    a = _unpack_rows(a_refs[s][ws, ts])
                    th = jnp.tanh(a * (GELU_K0 + GELU_K1 * (a * a)))
                    p_refs[s][ws, ts] = pltpu.bitcast(gate[v] * (a * (0.5 + 0.5 * th)), U32)

    def project(terms):
        for tb in tok_blocks:
            parts = [_dot(cols, _unpack_rows(src[:, tb])) for cols, src in terms]
            acc_ref[:, tb] += functools.reduce(lambda a, b: a + b, parts)

    def term(s):
        return vt_ref[:, s * sub:(s + 1) * sub], p_refs[s]

    last = n_sub - 1
    assert n_sub % 2 == 0

    @pl.when(c == 0)
    def _():
        acc_ref[...] = jnp.zeros(acc_ref.shape, F32)
        p_refs[last][...] = jnp.zeros(p_refs[last].shape, U32)
        activations(0)

    pending = [(vp_ref[...], p_refs[last])]
    for s in range(n_sub):
        gates(s)
        if s < last:
            pending.append(term(s))
            activations(s + 1)
        if len(pending) == n_sub // 2:
            project(pending)
            pending = []
    activations_from(un_ref[...], a_refs[0])

    @pl.when(c == pl.num_programs(1) - 1)
    def _():
        project([term(last)])
        o_ref[...] = x_ref[...] + g_ref[...] * acc_ref[...].T


def _peer_dense(ht, u_bf, vt_bf, routing, x, g2, tok_tile, tok_off, n_tok):
    r2, e2, c1, e1 = routing
    d = ht.shape[0]
    n_exp = u_bf.shape[0]
    T = tok_tile
    NC = PEER_EXPERT_CHUNK
    nk = PEER_NKEYS
    SUB = PEER_SUB_CHUNK
    n_sub = NC // SUB
    rt = lambda rows: pl.BlockSpec((PEER_HEADS, rows, T), lambda i, c: (0, 0, i))
    per_chunk = pl.BlockSpec((PEER_HEADS, NC // nk, T), lambda i, c: (0, c, i))
    return pl.pallas_call(
        _peer_kernel,
        out_shape=jax.ShapeDtypeStruct((n_tok, d), F32),
        grid=(n_tok // T, n_exp // NC),
        in_specs=[pl.BlockSpec((d, T), lambda i, c: (0, i + tok_off)),
                  pl.BlockSpec((NC, d), lambda i, c: (c, 0)),
                  pl.BlockSpec((SUB, d), lambda i, c: (jnp.minimum((c + 1) * n_sub, n_exp // SUB - 1), 0)),
                  pl.BlockSpec((d, NC), lambda i, c: (0, c)),
                  pl.BlockSpec((d, SUB), lambda i, c: (0, jnp.maximum(c * n_sub - 1, 0))),
                  rt(nk // 2), rt(nk // 2), per_chunk, per_chunk,
                  pl.BlockSpec((T, d), lambda i, c: (i + tok_off, 0)),
                  pl.BlockSpec((1, d), lambda i, c: (0, 0))],
        out_specs=pl.BlockSpec((T, d), lambda i, c: (i, 0)),
        scratch_shapes=[pltpu.VMEM((d, T), F32)] + [pltpu.VMEM((PEER_SUB_CHUNK // 2, T), U32)] * (2 * n_sub),
        compiler_params=_cparams("arbitrary", "arbitrary"),
        name="peer_dense",
    )(ht, u_bf, u_bf, vt_bf, vt_bf, r2, e2, c1, e1, x, g2)


def _rope_tables(L, n):
    t = jnp.arange(L)
    row = (t // GRID_W).astype(F32)
    col = (t % GRID_W).astype(F32)
    q = HEAD_DIM // 4
    inv = ROPE_THETA ** (-jnp.arange(q, dtype=F32) / q)
    ar = row[:, None] * inv
    ac = col[:, None] * inv
    cos = jnp.concatenate([jnp.cos(ar), jnp.cos(ar), jnp.cos(ac), jnp.cos(ac)], axis=-1)
    sin = jnp.concatenate([-jnp.sin(ar), jnp.sin(ar), -jnp.sin(ac), jnp.sin(ac)], axis=-1)
    cos = jnp.concatenate([cos, jnp.ones((n - L, HEAD_DIM), F32)], axis=0)
    sin = jnp.concatenate([sin, jnp.zeros((n - L, HEAD_DIM), F32)], axis=0)
    return jnp.tile(cos, (1, LANES // HEAD_DIM)), jnp.tile(sin, (1, LANES // HEAD_DIM))


def _block_diag(w):
    nb, di, do = w.shape
    eye = jnp.eye(nb, dtype=w.dtype)
    return (eye[:, None, :, None] * w[:, :, None, :]).reshape(nb * di, nb * do)


def kernel(x, c, ctx, c_ctx, w_mod, b_mod, norm1_g, norm2_g, w_in, w_out, lru_conv_w, lru_conv_b, lru_wa, lru_ba, lru_wx, lru_bx, lru_lam, win_qn_g, win_kn_g, win_sink, na_qn_g, na_kn_g, na_rpb, peer_wq, peer_k1, peer_k2, peer_u, peer_v):
    B, L, D = x.shape
    Lc = ctx.shape[1]
    depth = w_mod.shape[0]
    assert B == 1 and Lc == TOK_TILE and L % PREP_TOK_TILE == 0 and L // GRID_W >= 3 * NA_ROW_BLOCK
    n = L + Lc
    n_lat = L // TOK_TILE
    rows = L // GRID_W

    cvec = jnp.zeros((8, D), F32).at[0].set(c[0]).at[1].set(c_ctx)
    mods = _modulation(cvec, w_mod, b_mod)
    cos_t, sin_t = _rope_tables(L, n)
    tile2 = lambda g: jnp.tile(g, LANES // HEAD_DIM)

    xl, xc = x[0], ctx[0]
    for l in range(depth):
        with_ctx = l < depth - 1
        mod = mods[l]
        hg = jnp.zeros((8, LANES), F32)
        hg = hg.at[0].set(tile2(win_qn_g[l])).at[1].set(tile2(win_kn_g[l]))
        hg = hg.at[2].set(tile2(na_qn_g[l])).at[3].set(tile2(na_kn_g[l]))
        xa, ga, qw, kw, vw, qn, kn, vn = _in_proj(xl, xc, mod, norm1_g[l][None], w_in[l].astype(BF16),
                                                  cos_t, sin_t, hg)

        w_gates = jnp.concatenate([_block_diag(lru_wa[l, 0]), _block_diag(lru_wx[l, 0]),
                                   _block_diag(lru_wa[l, 1]), _block_diag(lru_wx[l, 1])], axis=1).astype(BF16)
        b_gates = jnp.concatenate([lru_ba[l, 0], lru_bx[l, 0], lru_ba[l, 1], lru_bx[l, 1]])[None]
        conv_w = jnp.zeros((8, D_LRU), F32).at[0:lru_conv_w.shape[1]].set(lru_conv_w[l])
        lam = jnp.zeros((8, D_LRU), F32).at[0:2].set(lru_lam[l])
        hf, hb = _lru(xa, conv_w, lru_conv_b[l][None], w_gates, b_gates, lam, n_lat)

        yb = _window_attention(win_sink[l], qw, kw, vw, L, with_ctx)
        yc = _neighborhood_attention(qn, kn, vn, _na_bias_table(na_rpb[l], rows), L, with_ctx)

        xn, ht = _out_proj(xl, xc, hf, hb, ga, yb, yc, w_out[l].astype(BF16), mod, norm2_g[l][None], with_ctx)

        route_w = (peer_wq[l].T.astype(BF16), peer_k1[l].astype(BF16), peer_k2[l].astype(BF16))
        u_bf = peer_u[l].astype(BF16)
        vt_bf = peer_v[l].T.astype(BF16)
        g2 = mod[:, 5 * D:6 * D]
        routing = _peer_prep(ht, *route_w, PREP_TOK_TILE, 0, L)
        xl_new = _peer_dense(ht, u_bf, vt_bf, routing, xn, g2[0:1], PEER_TOK_TILE, 0, L)
        if with_ctx:
            routing = _peer_prep(ht, *route_w, Lc, L // Lc, Lc)
            xc = _peer_dense(ht, u_bf, vt_bf, routing, xn, g2[1:2], Lc, L // Lc, Lc)
        xl = xl_new
    return xl[None]
```

```python
import functools

import numpy as np
import jax
import jax.numpy as jnp
from jax import lax
from jax.experimental import pallas as pl
from jax.experimental.pallas import tpu as pltpu

F32 = jnp.float32
BF16 = jnp.bfloat16
U32 = jnp.uint32

HEAD_DIM = 64
GRID_W = 64
EPS = 1e-6
ROPE_THETA = 10000.0
D_LRU = 256
LRU_BLOCKS = 4
LRU_C = 8.0
WIN_Q_HEADS = 6
WIN_KV_HEADS = 2
WIN_BLOCK = 128
NA_HEADS = 6
NA_KH = 8
NA_KW = 16
NA_ROW_BLOCK = 4
NA_KEY_ROWS = NA_ROW_BLOCK + NA_KH - 1
PEER_HEADS = 8
PEER_NKEYS = 128
PEER_TOPK = 16
D_WIN_Q = WIN_Q_HEADS * HEAD_DIM
D_WIN_KV = WIN_KV_HEADS * HEAD_DIM
D_NA = NA_HEADS * HEAD_DIM

LANES = 128
TOK_TILE = 256
PEER_TOK_TILE = 1024
PEER_EXPERT_CHUNK = 2048
PEER_SUB_CHUNK = 512
PEER_MXU_TOKENS = 256
PREP_TOK_TILE = 1024
VMEM_LIMIT = 56 * 1024 * 1024
NEG = -1e30
GELU_K0 = float(np.sqrt(2.0 / np.pi))
GELU_K1 = 0.044715 * GELU_K0


def _cparams(*sem):
    return pltpu.CompilerParams(dimension_semantics=sem, vmem_limit_bytes=VMEM_LIMIT)


def _dot(a, b):
    return jnp.dot(a, b, preferred_element_type=F32)


def _pack_rows(x):
    return pltpu.bitcast(x.astype(BF16), U32)


def _unpack_rows(w):
    return pltpu.bitcast(w, BF16)


def _dot_nt(a, b):
    return lax.dot_general(a, b, (((1,), (1,)), ((), ())), preferred_element_type=F32)


def _mod_kernel(c_ref, w_ref, b_ref, o_ref):
    c = c_ref[...]
    a = (c * jax.nn.sigmoid(c)).astype(BF16)
    o_ref[0] = _dot(a, w_ref[0].astype(BF16)) + b_ref[0]


def _modulation(cvec, w_mod, b_mod):
    depth, d, d6 = w_mod.shape
    return pl.pallas_call(
        _mod_kernel,
        out_shape=jax.ShapeDtypeStruct((depth, 8, d6), F32),
        grid=(depth, d6 // d),
        in_specs=[pl.BlockSpec((8, d), lambda l, j: (0, 0)),
                  pl.BlockSpec((1, d, d), lambda l, j: (l, 0, j)),
                  pl.BlockSpec((1, 1, d), lambda l, j: (l, 0, j))],
        out_specs=pl.BlockSpec((1, 8, d), lambda l, j: (l, 0, j)),
        compiler_params=_cparams("arbitrary", "arbitrary"),
        name="modulation",
    )(cvec, w_mod, b_mod.reshape(depth, 1, d6))


def _pair_mean_matrix():
    r = lax.broadcasted_iota(jnp.int32, (LANES, LANES), 0) < HEAD_DIM
    c = lax.broadcasted_iota(jnp.int32, (LANES, LANES), 1) < HEAD_DIM
    return jnp.where(r == c, 1.0 / HEAD_DIM, 0.0).astype(BF16)


def _head_norm(z, gain, bd):
    z2 = z * z
    hi = z2.astype(BF16)
    lo = (z2 - hi.astype(F32)).astype(BF16)
    ms = _dot(hi, bd) + _dot(lo, bd)
    return z * lax.rsqrt(ms + EPS) * gain


def _rope(z, cos, sin):
    lane = lax.broadcasted_iota(jnp.int32, z.shape, 1)
    first = (lane & 16) == 0
    partner = jnp.where(first, pltpu.roll(z, LANES - 16, 1), pltpu.roll(z, 16, 1))
    return z * cos + partner * sin


def _in_kernel(xl_ref, xc_ref, mod_ref, g_ref, w_ref, cos_ref, sin_ref, hg_ref,
               xa_ref, ga_ref, qw_ref, kw_ref, vw_ref, qn_ref, kn_ref, vn_ref, *, n_lat):
    d = xl_ref.shape[1]
    i = pl.program_id(0)
    row = (i == n_lat).astype(jnp.int32)
    is_ctx = jnp.full((TOK_TILE, 1), row) == 1
    x = jnp.where(is_ctx, xc_ref[...], xl_ref[...])
    ms = jnp.mean(x * x, axis=-1, keepdims=True)
    y = x * lax.rsqrt(ms + EPS) * g_ref[...]
    sh = mod_ref[pl.ds(row, 1), 0:d]
    sc = mod_ref[pl.ds(row, 1), d:2 * d]
    h = (y * (1.0 + sc) + sh).astype(BF16)
    z = _dot(h, w_ref[...])

    bd = _pair_mean_matrix()
    cos = cos_ref[...]
    sin = sin_ref[...]
    scale = HEAD_DIM ** -0.5
    o = 0
    xa_ref[...] = z[:, o:o + D_LRU]
    o += D_LRU
    ga_ref[...] = z[:, o:o + D_LRU]
    o += D_LRU
    for g in range(D_WIN_Q // LANES):
        zz = _rope(_head_norm(z[:, o:o + LANES], hg_ref[0:1], bd), cos, sin)
        qw_ref[:, g * LANES:(g + 1) * LANES] = (zz * scale).astype(BF16)
        o += LANES
    kw_ref[...] = _rope(_head_norm(z[:, o:o + LANES], hg_ref[1:2], bd), cos, sin).astype(BF16)
    o += LANES
    vw_ref[...] = z[:, o:o + LANES].astype(BF16)
    o += LANES
    for g in range(D_NA // LANES):
        zz = _head_norm(z[:, o:o + LANES], hg_ref[2:3], bd)
        qn_ref[:, g * LANES:(g + 1) * LANES] = (zz * scale).astype(BF16)
        o += LANES
    for g in range(D_NA // LANES):
        zz = _head_norm(z[:, o:o + LANES], hg_ref[3:4], bd)
        kn_ref[:, g * LANES:(g + 1) * LANES] = zz.astype(BF16)
        o += LANES
    vn_ref[...] = z[:, o:o + D_NA].astype(BF16)


def _in_proj(xl, xc, mod, g, w_bf, cos_t, sin_t, hg):
    L, d = xl.shape
    n_lat = L // TOK_TILE
    n = L + xc.shape[0]
    d_in = w_bf.shape[1]
    T = TOK_TILE
    tok = lambda w: pl.BlockSpec((T, w), lambda i: (i, 0))
    full = lambda a: pl.BlockSpec(a.shape, lambda i: (0,) * a.ndim)
    widths = (D_LRU, D_LRU, D_WIN_Q, D_WIN_KV, D_WIN_KV, D_NA, D_NA, D_NA)
    dtypes = (F32, F32, BF16, BF16, BF16, BF16, BF16, BF16)
    return pl.pallas_call(
        functools.partial(_in_kernel, n_lat=n_lat),
        out_shape=[jax.ShapeDtypeStruct((n, w), t) for w, t in zip(widths, dtypes)],
        grid=(n_lat + 1,),
        in_specs=[pl.BlockSpec((T, d), lambda i: (jnp.minimum(i, n_lat - 1), 0)),
                  pl.BlockSpec((T, d), lambda i: (0, 0)),
                  full(mod), full(g), full(w_bf), tok(LANES), tok(LANES), full(hg)],
        out_specs=[tok(w) for w in widths],
        compiler_params=_cparams("arbitrary"),
        name="in_proj",
    )(xl, xc, mod, g, w_bf, cos_t, sin_t, hg)


def _chunk_scan(a, b, reverse):
    T = a.shape[0]
    rows = lax.broadcasted_iota(jnp.int32, a.shape, 0)
    s = 1
    while s < T:
        if reverse:
            edge = rows >= T - s
            shift = T - s
        else:
            edge = rows < s
            shift = s
        a_s = jnp.where(edge, 1.0, pltpu.roll(a, shift, 0))
        b_s = jnp.where(edge, 0.0, pltpu.roll(b, shift, 0))
        b = a * b_s + b
        a = a * a_s
        s *= 2
    return a, b


def _lru_direction(xm, ph, nh, pv, nv, cw, cb, w, bias, spl, carry_ref, first, reverse):
    T, C = xm.shape
    rows = lax.broadcasted_iota(jnp.int32, (T, C), 0)
    p6 = ph[6:7] * pv
    p7 = ph[7:8] * pv
    n0 = nh[0:1] * nv
    x_m1 = jnp.where(rows == 0, p7, pltpu.roll(xm, 1, 0))
    x_m2 = jnp.where(rows == 0, p6, jnp.where(rows == 1, p7, pltpu.roll(xm, 2, 0)))
    x_p1 = jnp.where(rows == T - 1, n0, pltpu.roll(xm, T - 1, 0))
    u = cw[0:1] * x_m2 + cw[1:2] * x_m1 + cw[2:3] * xm + cw[3:4] * x_p1 + cb
    zz = _dot(u.astype(BF16), w) + bias
    r = jax.nn.sigmoid(zz[:, :C])
    ig = jax.nn.sigmoid(zz[:, C:])
    log_a = -LRU_C * r * spl
    a = jnp.exp(log_a)
    b = jnp.sqrt(-jnp.tanh(log_a) * (a * a + 1.0)) * ig * u
    a, b = _chunk_scan(a, b, reverse)

    @pl.when(first)
    def _():
        carry_ref[...] = jnp.zeros(carry_ref.shape, F32)

    h = a * carry_ref[0:1] + b
    edge = h[0:1] if reverse else h[T - 1:T]
    carry_ref[...] = jnp.broadcast_to(edge, carry_ref.shape)
    return h


def _lru_kernel(xf_ref, pf_ref, nf_ref, xb_ref, pb_ref, nb_ref, cw_ref, cb_ref, w_ref, b_ref,
                lam_ref, hf_ref, hb_ref, cf_ref, cbk_ref, *, n_lat):
    j = pl.program_id(0)
    C = D_LRU
    fblk = jnp.where(j == 0, n_lat, j - 1)
    bblk = jnp.where(j == 0, n_lat, n_lat - j)
    lam = lam_ref[...]
    spl = jnp.maximum(-lam, 0.0) + jnp.log1p(jnp.exp(-jnp.abs(lam)))
    cw = cw_ref[...]
    cb = cb_ref[...]

    def halo_valid(blk):
        pv = jnp.logical_and(blk != 0, blk != n_lat).astype(F32)
        nv = jnp.logical_and(blk != n_lat - 1, blk != n_lat).astype(F32)
        return pv, nv

    pv, nv = halo_valid(fblk)
    hf_ref[...] = _lru_direction(xf_ref[...], pf_ref[...], nf_ref[...], pv, nv, cw, cb,
                                 w_ref[:, 0:2 * C], b_ref[:, 0:2 * C], spl[0:1], cf_ref, j == 0, False)
    pv, nv = halo_valid(bblk)
    hb_ref[...] = _lru_direction(xb_ref[...], pb_ref[...], nb_ref[...], pv, nv, cw, cb,
                                 w_ref[:, 2 * C:4 * C], b_ref[:, 2 * C:4 * C], spl[1:2], cbk_ref, j == 0, True)


def _lru(xa, conv_w, conv_b, w_gates, b_gates, lam, n_lat):
    n, C = xa.shape
    T = TOK_TILE
    sub = T // 8
    nblk8 = n // 8
    fblk = lambda j: jnp.where(j == 0, n_lat, j - 1)
    bblk = lambda j: jnp.where(j == 0, n_lat, n_lat - j)
    prev8 = lambda blk: jnp.maximum(blk * sub - 1, 0)
    next8 = lambda blk: jnp.minimum((blk + 1) * sub, nblk8 - 1)
    full = lambda a: pl.BlockSpec(a.shape, lambda j: (0,) * a.ndim)
    return pl.pallas_call(
        functools.partial(_lru_kernel, n_lat=n_lat),
        out_shape=[jax.ShapeDtypeStruct((n, C), F32)] * 2,
        grid=(n_lat + 1,),
        in_specs=[pl.BlockSpec((T, C), lambda j: (fblk(j), 0)),
                  pl.BlockSpec((8, C), lambda j: (prev8(fblk(j)), 0)),
                  pl.BlockSpec((8, C), lambda j: (next8(fblk(j)), 0)),
                  pl.BlockSpec((T, C), lambda j: (bblk(j), 0)),
                  pl.BlockSpec((8, C), lambda j: (prev8(bblk(j)), 0)),
                  pl.BlockSpec((8, C), lambda j: (next8(bblk(j)), 0)),
                  full(conv_w), full(conv_b), full(w_gates), full(b_gates), full(lam)],
        out_specs=[pl.BlockSpec((T, C), lambda j: (fblk(j), 0)),
                   pl.BlockSpec((T, C), lambda j: (bblk(j), 0))],
        scratch_shapes=[pltpu.VMEM((8, C), F32), pltpu.VMEM((8, C), F32)],
        compiler_params=_cparams("arbitrary"),
        name="rglru",
    )(xa, xa, xa, xa, xa, xa, conv_w, conv_b, w_gates, b_gates, lam)


def _win_kernel(sink_ref, q_ref, kp_ref, kc_ref, kn_ref, vp_ref, vc_ref, vn_ref, kx_ref, vx_ref,
                o_ref, *, nb):
    W = WIN_BLOCK
    R = WIN_Q_HEADS // WIN_KV_HEADS
    b = pl.program_id(0)
    blk = jnp.where(b < nb, b, -4)
    rows = lax.broadcasted_iota(jnp.int32, (R * W, 3 * W), 0)
    cols = lax.broadcasted_iota(jnp.int32, (R * W, 3 * W), 1)
    kpos = (blk - 1) * W + cols
    qpos = blk * W + (rows & (W - 1))
    valid = jnp.logical_and(jnp.abs(kpos - qpos) <= W, jnp.logical_and(kpos >= 0, kpos < nb * W))
    row1 = lax.broadcasted_iota(jnp.int32, (R * W, 1), 0)
    q = q_ref[...]
    for g in range(WIN_KV_HEADS):
        ls = slice(g * HEAD_DIM, (g + 1) * HEAD_DIM)
        qs = jnp.concatenate([q[:, (g * R + r) * HEAD_DIM:(g * R + r + 1) * HEAD_DIM] for r in range(R)], axis=0)
        kl = jnp.concatenate([kp_ref[:, ls], kc_ref[:, ls], kn_ref[:, ls]], axis=0)
        vl = jnp.concatenate([vp_ref[:, ls], vc_ref[:, ls], vn_ref[:, ls]], axis=0)
        s_loc = jnp.where(valid, _dot_nt(qs, kl), NEG)
        s_ctx = _dot_nt(qs, kx_ref[:, ls])
        sink = jnp.full((R * W, 1), sink_ref[g * R], F32)
        for r in range(1, R):
            sink = jnp.where(row1 >= r * W, sink_ref[g * R + r], sink)
        m = jnp.maximum(jnp.maximum(jnp.max(s_loc, axis=-1, keepdims=True),
                                    jnp.max(s_ctx, axis=-1, keepdims=True)), sink)
        p_loc = jnp.exp(s_loc - m)
        p_ctx = jnp.exp(s_ctx - m)
        den = (jnp.sum(p_loc, axis=-1, keepdims=True) + jnp.sum(p_ctx, axis=-1, keepdims=True)
               + jnp.exp(sink - m))
        o = (_dot(p_loc.astype(BF16), vl) + _dot(p_ctx.astype(BF16), vx_ref[:, ls])) / den
        for r in range(R):
            hh = g * R + r
            o_ref[:, hh * HEAD_DIM:(hh + 1) * HEAD_DIM] = o[r * W:(r + 1) * W].astype(o_ref.dtype)


def _window_attention(sink, qw, kw, vw, L, with_ctx):
    n = qw.shape[0]
    W = WIN_BLOCK
    nb = L // W
    nq = n // W if with_ctx else nb
    ctx_blk = L // TOK_TILE
    lat = lambda b: jnp.minimum(b, nb - 1)
    kv = lambda f: pl.BlockSpec((W, D_WIN_KV), lambda b: (f(b), 0))
    prv = lambda b: jnp.maximum(lat(b) - 1, 0)
    nxt = lambda b: jnp.minimum(lat(b) + 1, nb - 1)
    ctx = pl.BlockSpec((n - L, D_WIN_KV), lambda b: (ctx_blk, 0))
    return pl.pallas_call(
        functools.partial(_win_kernel, nb=nb),
        out_shape=jax.ShapeDtypeStruct((nq * W, D_WIN_Q), BF16),
        grid=(nq,),
        in_specs=[pl.BlockSpec(memory_space=pltpu.SMEM),
                  pl.BlockSpec((W, D_WIN_Q), lambda b: (b, 0)),
                  kv(prv), kv(lat), kv(nxt), kv(prv), kv(lat), kv(nxt), ctx, ctx],
        out_specs=pl.BlockSpec((W, D_WIN_Q), lambda b: (b, 0)),
        compiler_params=_cparams("arbitrary"),
        name="window_attn",
    )(sink, qw, kw, kw, kw, vw, vw, vw, kw, vw)


def _na_kernel(q_ref, kl_ref, vl_ref, kx_ref, vx_ref, bias_ref, o_ref, *, n_blocks):
    b = pl.program_id(0)
    nq = q_ref.shape[0]
    latent = jnp.full((nq, 1), (b < n_blocks).astype(jnp.int32)) == 1
    q = q_ref[...]
    for h in range(NA_HEADS):
        ls = slice(h * HEAD_DIM, (h + 1) * HEAD_DIM)
        qh = q[:, ls]
        s_loc = jnp.where(latent, _dot_nt(qh, kl_ref[:, ls]) + bias_ref[0, h], NEG)
        s_ctx = _dot_nt(qh, kx_ref[:, ls])
        m = jnp.maximum(jnp.max(s_loc, axis=-1, keepdims=True), jnp.max(s_ctx, axis=-1, keepdims=True))
        p_loc = jnp.exp(s_loc - m)
        p_ctx = jnp.exp(s_ctx - m)
        den = jnp.sum(p_loc, axis=-1, keepdims=True) + jnp.sum(p_ctx, axis=-1, keepdims=True)
        o = (_dot(p_loc.astype(BF16), vl_ref[:, ls]) + _dot(p_ctx.astype(BF16), vx_ref[:, ls])) / den
        o_ref[:, ls] = o.astype(o_ref.dtype)


def _na_bias_table(rpb, rows):
    R, KR = NA_ROW_BLOCK, NA_KEY_ROWS
    qc = np.arange(GRID_W)[:, None]
    kc = np.arange(GRID_W)[None, :]
    qstart = np.clip(qc - NA_KW // 2, 0, GRID_W - NA_KW)
    inside = (kc - qstart >= 0) & (kc - qstart < NA_KW)
    pad = GRID_W - NA_KW
    rp = jnp.pad(rpb.astype(F32), ((0, 0), (0, 0), (pad, pad)))
    shifted = jnp.stack([rp[:, :, NA_KW - 1 - q + pad:NA_KW - 1 - q + pad + GRID_W] for q in range(GRID_W)], axis=2)
    per_dr = jnp.where(inside[None, None], shifted, NEG)
    masked = jnp.full((NA_HEADS, GRID_W, GRID_W), NEG, F32)
    half = NA_KH // 2
    cases = []
    for r0, ks in ((0, 0), (half, 0), (rows - R, rows - KR)):
        row_blocks = []
        for rr in range(R):
            r = r0 + rr
            kr = min(max(r - half, 0), rows - NA_KH)
            blocks = [per_dr[:, ks + kk - r + NA_KH - 1] if kr <= ks + kk < kr + NA_KH else masked
                      for kk in range(KR)]
            row_blocks.append(jnp.concatenate(blocks, axis=-1))
        cases.append(jnp.concatenate(row_blocks, axis=1))
    return jnp.stack(cases)


def _neighborhood_attention(qn, kn, vn, bias, L, with_ctx):
    n = qn.shape[0]
    R, KR = NA_ROW_BLOCK, NA_KEY_ROWS
    rows = L // GRID_W
    nb = rows // R
    nq = R * GRID_W
    steps = n // nq if with_ctx else nb
    ctx_blk = L // TOK_TILE
    half = NA_KH // 2

    def blk_of(b):
        return jnp.minimum(b, nb - 1)

    def key_start(b):
        return jnp.clip(blk_of(b) * R - half, 0, rows - KR) * GRID_W

    def case_of(b):
        return jnp.where(blk_of(b) == 0, 0, jnp.where(blk_of(b) == nb - 1, 2, 1))

    loc = pl.BlockSpec((pl.Element(KR * GRID_W), pl.Element(D_NA)), lambda b: (key_start(b), 0))
    ctx = pl.BlockSpec((n - L, D_NA), lambda b: (ctx_blk, 0))
    return pl.pallas_call(
        functools.partial(_na_kernel, n_blocks=nb),
        out_shape=jax.ShapeDtypeStruct((steps * nq, D_NA), BF16),
        grid=(steps,),
        in_specs=[pl.BlockSpec((nq, D_NA), lambda b: (b, 0)), loc, loc, ctx, ctx,
                  pl.BlockSpec((1, NA_HEADS, nq, KR * GRID_W), lambda b: (case_of(b), 0, 0, 0))],
        out_specs=pl.BlockSpec((nq, D_NA), lambda b: (b, 0)),
        compiler_params=_cparams("arbitrary"),
        name="neighborhood_attn",
    )(qn, kn, vn, kn, vn, bias)


def _out_kernel(xl_ref, xc_ref, hf_ref, hb_ref, ga_ref, yb_ref, yc_ref, w_ref, mod_ref, g_ref,
                xo_ref, ht_ref, *, n_lat):
    d = xl_ref.shape[1]
    i = pl.program_id(0)
    row = (i == n_lat).astype(jnp.int32)
    is_ctx = jnp.full((TOK_TILE, 1), row) == 1
    x = jnp.where(is_ctx, xc_ref[...], xl_ref[...])
    ya = ((hf_ref[...] + hb_ref[...]) * jax.nn.gelu(ga_ref[...])).astype(BF16)
    o1 = D_LRU
    o2 = D_LRU + D_WIN_Q
    mix = (_dot(ya, w_ref[0:o1]) + _dot(yb_ref[...], w_ref[o1:o2]) + _dot(yc_ref[...], w_ref[o2:o2 + D_NA]))
    xn = x + mod_ref[pl.ds(row, 1), 2 * d:3 * d] * mix
    xo_ref[...] = xn
    ms = jnp.mean(xn * xn, axis=-1, keepdims=True)
    y = xn * lax.rsqrt(ms + EPS) * g_ref[...]
    h2 = y * (1.0 + mod_ref[pl.ds(row, 1), 4 * d:5 * d]) + mod_ref[pl.ds(row, 1), 3 * d:4 * d]
    ht_ref[...] = h2.T.astype(BF16)


def _out_proj(xl, xc, hf, hb, ga, yb, yc, w_bf, mod, g, with_ctx):
    L, d = xl.shape
    T = TOK_TILE
    n_lat = L // T
    nt = n_lat + 1 if with_ctx else n_lat
    tok = lambda w: pl.BlockSpec((T, w), lambda i: (i, 0))
    full = lambda a: pl.BlockSpec(a.shape, lambda i: (0,) * a.ndim)
    return pl.pallas_call(
        functools.partial(_out_kernel, n_lat=n_lat),
        out_shape=[jax.ShapeDtypeStruct((nt * T, d), F32), jax.ShapeDtypeStruct((d, nt * T), BF16)],
        grid=(nt,),
        in_specs=[pl.BlockSpec((T, d), lambda i: (jnp.minimum(i, n_lat - 1), 0)),
                  pl.BlockSpec((T, d), lambda i: (0, 0)),
                  tok(D_LRU), tok(D_LRU), tok(D_LRU), tok(D_WIN_Q), tok(D_NA),
                  full(w_bf), full(mod), full(g)],
        out_specs=[tok(d), pl.BlockSpec((d, T), lambda i: (0, i))],
        compiler_params=_cparams("arbitrary"),
        name="out_proj",
    )(xl, xc, hf, hb, ga, yb, yc, w_bf, mod, g)


SUBLANES = 8


def _sorting_pairs(n):
    pairs, p = [], 1
    while p < n:
        k = p
        while k >= 1:
            for j in range(k % p, n - k, 2 * k):
                for i in range(min(k, n - j - k)):
                    if (i + j) // (2 * p) == (i + j + k) // (2 * p):
                        pairs.append((i + j, i + j + k))
            k //= 2
        p *= 2
    return pairs


def _vmax(a, b):
    if a is None:
        return b
    if b is None:
        return a
    return jnp.maximum(a, b)


def _vmin(a, b):
    if a is None or b is None:
        return None
    return jnp.minimum(a, b)


def _top16_sorted(slabs):
    K = PEER_TOPK
    w = list(slabs)
    for a, b in _sorting_pairs(K):
        w[a], w[b] = _vmax(w[a], w[b]), _vmin(w[a], w[b])
    shift = SUBLANES // 2
    while shift >= 1:
        partner = [None if w[K - 1 - i] is None else pltpu.roll(w[K - 1 - i], shift, 0) for i in range(K)]
        w = [_vmax(w[i], partner[i]) for i in range(K)]
        stride = K // 2
        while stride >= 1:
            for i in range(K):
                if i & stride == 0:
                    w[i], w[i + stride] = _vmax(w[i], w[i + stride]), _vmin(w[i], w[i + stride])
            stride //= 2
        shift //= 2
    return w


def _allsum8(x):
    x = x + pltpu.roll(x, 4, 0)
    x = x + pltpu.roll(x, 2, 0)
    return x + pltpu.roll(x, 1, 0)


def _route_tile(s1, s2):
    K = PEER_TOPK
    S = SUBLANES
    n_slab = s1.shape[0] // S
    a1 = [s1[j * S:(j + 1) * S] for j in range(n_slab)]
    a2 = [s2[j * S:(j + 1) * S] for j in range(n_slab)]
    v1 = _top16_sorted(a1)
    v2 = _top16_sorted(a2)
    sub = lax.broadcasted_iota(jnp.int32, a1[0].shape, 0)

    def as_rows(v, lo):
        out = v[lo]
        for b in range(1, S):
            out = jnp.where(sub == b, v[lo + b], out)
        return out

    v2_lo, v2_hi, v1_hi = as_rows(v2, 0), as_rows(v2, S), as_rows(v1, S)
    lens = [K // (a + 1) for a in range(S)]
    cands = [v1[0] + v2_lo, v1[0] + v2_hi]
    for a in range(1, S):
        ca = v1[a] + v2_lo
        cands.append(ca if lens[a] >= S else jnp.where(sub < lens[a], ca, -jnp.inf))
    cands.append(v1_hi + v2[0])
    thr = _top16_sorted(cands + [None] * (K - len(cands)))[K - 1]
    x2_lo, x2_hi = jnp.exp(v2_lo - v2[0]), jnp.exp(v2_hi - v2[0])
    sel_lo, sel_hi = cands[0] >= thr, cands[1] >= thr
    zsum = jnp.where(sel_lo, x2_lo, 0.0) + jnp.where(sel_hi, x2_hi, 0.0)
    cnt = [_allsum8(jnp.where(sel_lo, 1.0, 0.0) + jnp.where(sel_hi, 1.0, 0.0))]
    for a in range(1, S):
        sel = cands[a + 1] >= thr
        zsum = zsum + jnp.where(sel, jnp.exp(v1[a] - v1[0]) * x2_lo, 0.0)
        cnt.append(_allsum8(jnp.where(sel, 1.0, 0.0)))
    zsum = zsum + jnp.where(cands[S + 1] >= thr, jnp.exp(v1_hi - v1[0]), 0.0)
    inv_z = 1.0 / _allsum8(zsum)
    r2, e2, c1, e1, pair0 = [], [], [], [], []
    for j in range(n_slab):
        r = jnp.full(a2[j].shape, float(K), F32)
        for k in reversed(range(K)):
            r = jnp.where(a2[j] >= v2[k], float(k), r)
        r2.append(r)
        e2.append(jnp.exp(a2[j] - v2[0]) * inv_z)
        c = jnp.where(a1[j] + v2[0] >= thr, 1.0, 0.0)
        pair0.append(c)
        for a in reversed(range(S)):
            c = jnp.where(a1[j] >= v1[a], cnt[a], c)
        c1.append(c)
        e1.append(jnp.exp(a1[j] - v1[0]))
    lor, land = jnp.logical_or, jnp.logical_and
    n_sel = functools.reduce(lambda a, b: a + b, cnt) + _allsum8(jnp.where(cands[S + 1] >= thr, 1.0, 0.0))
    f1 = land(v1[S - 1] == v1[S], cnt[S - 1] >= 2.0)
    for k in range(S - 1):
        f1 = lor(f1, land(v1[k] == v1[k + 1], cnt[k] != cnt[k + 1]))
    f1 = lor(f1, _allsum8(functools.reduce(lambda a, b: a + b, pair0)) > float(K))
    f2 = v2[0] == v2[1]
    for k in range(1, K - 1):
        f2 = lor(f2, land(v2[k] == v2[k + 1], cnt[0] > float(k)))
    r_sum = _allsum8(functools.reduce(lambda a, b: a + b, r2))
    any_tie2 = r_sum != float(K * (K - 1) // 2 + K * (n_slab * S - K))
    f2 = lor(f2, land(cnt[0] == float(K), any_tie2))
    flag = lor(lor(f1, f2), n_sel != float(K))
    cat = lambda xs: jnp.concatenate(xs, axis=0)
    return cat(r2), cat(e2), cat(c1), cat(e1), jnp.where(flag, 1.0, 0.0)


def _route_tile_exact(s1, s2):
    K = PEER_TOPK

    def take16(x):
        idx = lax.broadcasted_iota(jnp.int32, x.shape, 0).astype(F32)
        rank = jnp.full(x.shape, float(K), F32)
        vals, cur = [], x
        for k in range(K):
            m = jnp.max(cur, axis=0, keepdims=True)
            first = jnp.min(jnp.where(cur == m, idx, float(x.shape[0])), axis=0, keepdims=True)
            hit = idx == first
            rank = jnp.where(hit, float(k), rank)
            vals.append(m)
            cur = jnp.where(hit, -jnp.inf, cur)
        return rank, vals

    r1, v1 = take16(s1)
    r2, v2 = take16(s2)
    v2a = jnp.concatenate(v2, axis=0)
    x1 = [jnp.exp(v1[a] - v1[0]) for a in range(K)]
    x2a = jnp.exp(v2a - v2[0])
    cand = jnp.concatenate([v1[a] + v2a for a in range(K)], axis=0)
    crank, _ = take16(cand)
    sel = crank < float(K)
    z = jnp.zeros_like(v1[0])
    c1 = jnp.zeros(s1.shape, F32)
    for a in range(K):
        sel_a = sel[a * K:(a + 1) * K]
        z = z + jnp.sum(jnp.where(sel_a, x1[a] * x2a, 0.0), axis=0, keepdims=True)
        cnt_a = jnp.sum(jnp.where(sel_a, 1.0, 0.0), axis=0, keepdims=True)
        c1 = jnp.where(r1 == float(a), cnt_a, c1)
    return r2, jnp.exp(s2 - v2[0]) / z, c1, jnp.exp(s1 - v1[0])


def _prep_kernel(ht_ref, wq_ref, k1_ref, k2_ref, r2_ref, e2_ref, c1_ref, e1_ref, q_scr, flag_scr):
    nk = PEER_NKEYS
    q_scr[...] = _dot(wq_ref[...], ht_ref[...])

    def scores(h):
        base = pl.multiple_of(h * 2 * nk, 2 * nk)
        s1 = _dot(k1_ref[...], q_scr[pl.ds(base, nk), :].astype(BF16))
        s2 = _dot(k2_ref[...], q_scr[pl.ds(base + nk, nk), :].astype(BF16))
        return s1, s2

    def store(h, ts, r2, e2, c1, e1):
        r2_ref[h, :, ts] = _pack_rows(r2)
        e2_ref[h, :, ts] = _pack_rows(e2)
        c1_ref[h, :, ts] = c1
        e1_ref[h, :, ts] = e1

    n_tiles = ht_ref.shape[1] // LANES

    def head(h, flags):
        s1, s2 = scores(h)
        for t in range(n_tiles):
            ts = slice(t * LANES, (t + 1) * LANES)
            r2, e2, c1, e1, flag = _route_tile(s1[:, ts], s2[:, ts])
            store(h, ts, r2, e2, c1, e1)
            flag_scr[h, t * SUBLANES:(t + 1) * SUBLANES, :] = flag
            flags = jnp.maximum(flags, flag)
        return flags

    flags = lax.fori_loop(0, PEER_HEADS, head, jnp.zeros((SUBLANES, LANES), F32))

    @pl.when(jnp.max(flags) > 0.0)
    def _():
        def head_exact(h, carry):
            s1, s2 = scores(h)
            for t in range(n_tiles):
                ts = slice(t * LANES, (t + 1) * LANES)

                @pl.when(jnp.max(flag_scr[h, t * SUBLANES:(t + 1) * SUBLANES, :]) > 0.0)
                def _():
                    store(h, ts, *_route_tile_exact(s1[:, ts], s2[:, ts]))
            return carry

        lax.fori_loop(0, PEER_HEADS, head_exact, 0)


def _peer_prep(ht, wq_t, k1, k2, tok_tile, tok_off, n):
    d = ht.shape[0]
    T = tok_tile
    nk = PEER_NKEYS
    full = lambda a: pl.BlockSpec(a.shape, lambda i: (0,) * a.ndim)
    spec = lambda rows: pl.BlockSpec((PEER_HEADS, rows, T), lambda i: (0, 0, i))
    return pl.pallas_call(
        _prep_kernel,
        out_shape=[jax.ShapeDtypeStruct((PEER_HEADS, nk // 2, n), U32)] * 2
        + [jax.ShapeDtypeStruct((PEER_HEADS, nk, n), F32)] * 2,
        grid=(n // T,),
        in_specs=[pl.BlockSpec((d, T), lambda i: (0, i + tok_off)), full(wq_t), full(k1), full(k2)],
        out_specs=[spec(nk // 2), spec(nk // 2), spec(nk), spec(nk)],
        scratch_shapes=[pltpu.VMEM((wq_t.shape[0], T), F32),
                        pltpu.VMEM((PEER_HEADS, T // LANES * SUBLANES, LANES), F32)],
        compiler_params=_cparams("arbitrary"),
        name="peer_prep",
    )(ht, wq_t, k1, k2)


def _peer_kernel(ht_ref, u_ref, un_ref, vt_ref, vp_ref, r2_ref, e2_ref, c1_ref, e1_ref, x_ref, g_ref,
                 o_ref, acc_ref, *ap_refs):
    nk = PEER_NKEYS
    c = pl.program_id(1)
    n_sub = len(ap_refs) // 2
    a_refs, p_refs = ap_refs[:n_sub], ap_refs[n_sub:]
    sub = 2 * a_refs[0].shape[0]
    T = a_refs[0].shape[1]
    rows_per_sub = sub // nk
    PK = 16
    WK = PK // 2

    tok_blocks = [slice(n, min(n + PEER_MXU_TOKENS, T)) for n in range(0, T, PEER_MXU_TOKENS)]

    def activations_from(rows_ref, dst_ref):
        for tb in tok_blocks:
            dst_ref[:, tb] = _pack_rows(_dot(rows_ref, ht_ref[:, tb]))

    def activations(s):
        activations_from(u_ref[s * sub:(s + 1) * sub, :], a_refs[s])

    def gates(s):
        for t in range(T // LANES):
            ts = slice(t * LANES, (t + 1) * LANES)
            for j in range(rows_per_sub):
                row = s * rows_per_sub + j
                gate = [None] * (nk // PK)
                for h in range(PEER_HEADS):
                    c1 = c1_ref[h, :, ts][row:row + 1]
                    e1 = e1_ref[h, :, ts][row:row + 1]
                    c1 = jnp.broadcast_to(c1, (PK, LANES)).astype(BF16)
                    e1 = jnp.broadcast_to(e1, (PK, LANES)).astype(BF16)
                    for v in range(nk // PK):
                        ws = slice(v * WK, (v + 1) * WK)
                        r2 = _unpack_rows(r2_ref[h, ws, ts])
                        e2 = _unpack_rows(e2_ref[h, ws, ts])
                        g = jnp.where(r2 < c1, e2 * e1, 0.0)
                        gate[v] = g if h == 0 else gate[v] + g
                for v in range(nk // PK):
                    ws = slice((j * nk + v * PK) // 2, (j * nk + (v + 1) * PK) // 2)
                    a = _unpack_rows(a_refs[s][ws, ts])
                    th = jnp.tanh(a * (GELU_K0 + GELU_K1 * (a * a)))
                    p_refs[s][ws, ts] = pltpu.bitcast(gate[v] * (a * (0.5 + 0.5 * th)), U32)

    def project(terms):
        for tb in tok_blocks:
            parts = [_dot(cols, _unpack_rows(src[:, tb])) for cols, src in terms]
            acc_ref[:, tb] += functools.reduce(lambda a, b: a + b, parts)

    def term(s):
        return vt_ref[:, s * sub:(s + 1) * sub], p_refs[s]

    last = n_sub - 1
    assert n_sub % 2 == 0

    @pl.when(c == 0)
    def _():
        acc_ref[...] = jnp.zeros(acc_ref.shape, F32)
        p_refs[last][...] = jnp.zeros(p_refs[last].shape, U32)
        activations(0)

    pending = [(vp_ref[...], p_refs[last])]
    for s in range(n_sub):
        gates(s)
        if s < last:
            pending.append(term(s))
            activations(s + 1)
        if len(pending) == 2:
            project(pending)
            pending = []
    activations_from(un_ref[...], a_refs[0])

    @pl.when(c == pl.num_programs(1) - 1)
    def _():
        project([term(last)])
        o_ref[...] = x_ref[...] + g_ref[...] * acc_ref[...].T


def _peer_dense(ht, u_bf, vt_bf, routing, x, g2, tok_tile, tok_off, n_tok):
    r2, e2, c1, e1 = routing
    d = ht.shape[0]
    n_exp = u_bf.shape[0]
    T = tok_tile
    NC = PEER_EXPERT_CHUNK
    nk = PEER_NKEYS
    SUB = PEER_SUB_CHUNK
    n_sub = NC // SUB
    once = dict(pipeline_mode=pl.Buffered(1))
    rt = lambda rows: pl.BlockSpec((PEER_HEADS, rows, T), lambda i, c: (0, 0, i), **once)
    per_chunk = pl.BlockSpec((PEER_HEADS, NC // nk, T), lambda i, c: (0, c, i))
    return pl.pallas_call(
        _peer_kernel,
        out_shape=jax.ShapeDtypeStruct((n_tok, d), F32),
        grid=(n_tok // T, n_exp // NC),
        in_specs=[pl.BlockSpec((d, T), lambda i, c: (0, i + tok_off), **once),
                  pl.BlockSpec((NC, d), lambda i, c: (c, 0)),
                  pl.BlockSpec((SUB, d), lambda i, c: (jnp.minimum((c + 1) * n_sub, n_exp // SUB - 1), 0)),
                  pl.BlockSpec((d, NC), lambda i, c: (0, c)),
                  pl.BlockSpec((d, SUB), lambda i, c: (0, jnp.maximum(c * n_sub - 1, 0))),
                  rt(nk // 2), rt(nk // 2), per_chunk, per_chunk,
                  pl.BlockSpec((T, d), lambda i, c: (i + tok_off, 0), **once),
                  pl.BlockSpec((1, d), lambda i, c: (0, 0))],
        out_specs=pl.BlockSpec((T, d), lambda i, c: (i, 0)),
        scratch_shapes=[pltpu.VMEM((d, T), F32)] + [pltpu.VMEM((PEER_SUB_CHUNK // 2, T), U32)] * (2 * n_sub),
        compiler_params=_cparams("arbitrary", "arbitrary"),
        name="peer_dense",
    )(ht, u_bf, u_bf, vt_bf, vt_bf, r2, e2, c1, e1, x, g2)


def _rope_tables(L, n):
    t = jnp.arange(L)
    row = (t // GRID_W).astype(F32)
    col = (t % GRID_W).astype(F32)
    q = HEAD_DIM // 4
    inv = ROPE_THETA ** (-jnp.arange(q, dtype=F32) / q)
    ar = row[:, None] * inv
    ac = col[:, None] * inv
    cos = jnp.concatenate([jnp.cos(ar), jnp.cos(ar), jnp.cos(ac), jnp.cos(ac)], axis=-1)
    sin = jnp.concatenate([-jnp.sin(ar), jnp.sin(ar), -jnp.sin(ac), jnp.sin(ac)], axis=-1)
    cos = jnp.concatenate([cos, jnp.ones((n - L, HEAD_DIM), F32)], axis=0)
    sin = jnp.concatenate([sin, jnp.zeros((n - L, HEAD_DIM), F32)], axis=0)
    return jnp.tile(cos, (1, LANES // HEAD_DIM)), jnp.tile(sin, (1, LANES // HEAD_DIM))


def _block_diag(w):
    nb, di, do = w.shape
    eye = jnp.eye(nb, dtype=w.dtype)
    return (eye[:, None, :, None] * w[:, :, None, :]).reshape(nb * di, nb * do)


def kernel(x, c, ctx, c_ctx, w_mod, b_mod, norm1_g, norm2_g, w_in, w_out, lru_conv_w, lru_conv_b, lru_wa, lru_ba, lru_wx, lru_bx, lru_lam, win_qn_g, win_kn_g, win_sink, na_qn_g, na_kn_g, na_rpb, peer_wq, peer_k1, peer_k2, peer_u, peer_v):
    B, L, D = x.shape
    Lc = ctx.shape[1]
    depth = w_mod.shape[0]
    assert B == 1 and Lc == TOK_TILE and L % PREP_TOK_TILE == 0 and L // GRID_W >= 3 * NA_ROW_BLOCK
    n = L + Lc
    n_lat = L // TOK_TILE
    rows = L // GRID_W

    cvec = jnp.zeros((8, D), F32).at[0].set(c[0]).at[1].set(c_ctx)
    mods = _modulation(cvec, w_mod, b_mod)
    cos_t, sin_t = _rope_tables(L, n)
    tile2 = lambda g: jnp.tile(g, LANES // HEAD_DIM)

    xl, xc = x[0], ctx[0]
    for l in range(depth):
        with_ctx = l < depth - 1
        mod = mods[l]
        hg = jnp.zeros((8, LANES), F32)
        hg = hg.at[0].set(tile2(win_qn_g[l])).at[1].set(tile2(win_kn_g[l]))
        hg = hg.at[2].set(tile2(na_qn_g[l])).at[3].set(tile2(na_kn_g[l]))
        xa, ga, qw, kw, vw, qn, kn, vn = _in_proj(xl, xc, mod, norm1_g[l][None], w_in[l].astype(BF16),
                                                  cos_t, sin_t, hg)

        w_gates = jnp.concatenate([_block_diag(lru_wa[l, 0]), _block_diag(lru_wx[l, 0]),
                                   _block_diag(lru_wa[l, 1]), _block_diag(lru_wx[l, 1])], axis=1).astype(BF16)
        b_gates = jnp.concatenate([lru_ba[l, 0], lru_bx[l, 0], lru_ba[l, 1], lru_bx[l, 1]])[None]
        conv_w = jnp.zeros((8, D_LRU), F32).at[0:lru_conv_w.shape[1]].set(lru_conv_w[l])
        lam = jnp.zeros((8, D_LRU), F32).at[0:2].set(lru_lam[l])
        hf, hb = _lru(xa, conv_w, lru_conv_b[l][None], w_gates, b_gates, lam, n_lat)

        yb = _window_attention(win_sink[l], qw, kw, vw, L, with_ctx)
        yc = _neighborhood_attention(qn, kn, vn, _na_bias_table(na_rpb[l], rows), L, with_ctx)

        xn, ht = _out_proj(xl, xc, hf, hb, ga, yb, yc, w_out[l].astype(BF16), mod, norm2_g[l][None], with_ctx)

        route_w = (peer_wq[l].T.astype(BF16), peer_k1[l].astype(BF16), peer_k2[l].astype(BF16))
        u_bf = peer_u[l].astype(BF16)
        vt_bf = peer_v[l].T.astype(BF16)
        g2 = mod[:, 5 * D:6 * D]
        routing = _peer_prep(ht, *route_w, PREP_TOK_TILE, 0, L)
        xl_new = _peer_dense(ht, u_bf, vt_bf, routing, xn, g2[0:1], PEER_TOK_TILE, 0, L)
        if with_ctx:
            routing = _peer_prep(ht, *route_w, Lc, L // Lc, Lc)
            xc = _peer_dense(ht, u_bf, vt_bf, routing, xn, g2[1:2], Lc, L // Lc, Lc)
        xl = xl_new
    return xl[None]
```

```python
import functools

import numpy as np
import jax
import jax.numpy as jnp
from jax import lax
from jax.experimental import pallas as pl
from jax.experimental.pallas import tpu as pltpu

F32 = jnp.float32
BF16 = jnp.bfloat16
U32 = jnp.uint32

HEAD_DIM = 64
GRID_W = 64
EPS = 1e-6
ROPE_THETA = 10000.0
D_LRU = 256
LRU_BLOCKS = 4
LRU_C = 8.0
WIN_Q_HEADS = 6
WIN_KV_HEADS = 2
WIN_BLOCK = 128
NA_HEADS = 6
NA_KH = 8
NA_KW = 16
NA_ROW_BLOCK = 4
NA_KEY_ROWS = NA_ROW_BLOCK + NA_KH - 1
PEER_HEADS = 8
PEER_NKEYS = 128
PEER_TOPK = 16
D_WIN_Q = WIN_Q_HEADS * HEAD_DIM
D_WIN_KV = WIN_KV_HEADS * HEAD_DIM
D_NA = NA_HEADS * HEAD_DIM

LANES = 128
TOK_TILE = 256
PEER_TOK_TILE = 1024
PEER_EXPERT_CHUNK = 2048
PEER_SUB_CHUNK = 512
PEER_MXU_TOKENS = 256
PREP_TOK_TILE = 1024
VMEM_LIMIT = 60 * 1024 * 1024
NEG = -1e30
GELU_K0 = float(np.sqrt(2.0 / np.pi))
GELU_K1 = 0.044715 * GELU_K0


def _cparams(*sem):
    return pltpu.CompilerParams(dimension_semantics=sem, vmem_limit_bytes=VMEM_LIMIT)


def _dot(a, b):
    return jnp.dot(a, b, preferred_element_type=F32)


def _pack_rows(x):
    return pltpu.bitcast(x.astype(BF16), U32)


def _unpack_rows(w):
    return pltpu.bitcast(w, BF16)


def _dot_nt(a, b):
    return lax.dot_general(a, b, (((1,), (1,)), ((), ())), preferred_element_type=F32)


def _mod_kernel(c_ref, w_ref, b_ref, o_ref):
    c = c_ref[...]
    a = (c * jax.nn.sigmoid(c)).astype(BF16)
    o_ref[0] = _dot(a, w_ref[0].astype(BF16)) + b_ref[0]


def _modulation(cvec, w_mod, b_mod):
    depth, d, d6 = w_mod.shape
    return pl.pallas_call(
        _mod_kernel,
        out_shape=jax.ShapeDtypeStruct((depth, 8, d6), F32),
        grid=(depth, d6 // d),
        in_specs=[pl.BlockSpec((8, d), lambda l, j: (0, 0)),
                  pl.BlockSpec((1, d, d), lambda l, j: (l, 0, j)),
                  pl.BlockSpec((1, 1, d), lambda l, j: (l, 0, j))],
        out_specs=pl.BlockSpec((1, 8, d), lambda l, j: (l, 0, j)),
        compiler_params=_cparams("arbitrary", "arbitrary"),
        name="modulation",
    )(cvec, w_mod, b_mod.reshape(depth, 1, d6))


def _pair_mean_matrix():
    r = lax.broadcasted_iota(jnp.int32, (LANES, LANES), 0) < HEAD_DIM
    c = lax.broadcasted_iota(jnp.int32, (LANES, LANES), 1) < HEAD_DIM
    return jnp.where(r == c, 1.0 / HEAD_DIM, 0.0).astype(BF16)


def _head_norm(z, gain, bd):
    z2 = z * z
    hi = z2.astype(BF16)
    lo = (z2 - hi.astype(F32)).astype(BF16)
    ms = _dot(hi, bd) + _dot(lo, bd)
    return z * lax.rsqrt(ms + EPS) * gain


def _rope(z, cos, sin):
    lane = lax.broadcasted_iota(jnp.int32, z.shape, 1)
    first = (lane & 16) == 0
    partner = jnp.where(first, pltpu.roll(z, LANES - 16, 1), pltpu.roll(z, 16, 1))
    return z * cos + partner * sin


def _in_kernel(xl_ref, xc_ref, mod_ref, g_ref, w_ref, cos_ref, sin_ref, hg_ref,
               xa_ref, ga_ref, qw_ref, kw_ref, vw_ref, qn_ref, kn_ref, vn_ref, *, n_lat):
    d = xl_ref.shape[1]
    i = pl.program_id(0)
    row = (i == n_lat).astype(jnp.int32)
    is_ctx = jnp.full((TOK_TILE, 1), row) == 1
    x = jnp.where(is_ctx, xc_ref[...], xl_ref[...])
    ms = jnp.mean(x * x, axis=-1, keepdims=True)
    y = x * lax.rsqrt(ms + EPS) * g_ref[...]
    sh = mod_ref[pl.ds(row, 1), 0:d]
    sc = mod_ref[pl.ds(row, 1), d:2 * d]
    h = (y * (1.0 + sc) + sh).astype(BF16)
    z = _dot(h, w_ref[...])

    bd = _pair_mean_matrix()
    cos = cos_ref[...]
    sin = sin_ref[...]
    scale = HEAD_DIM ** -0.5
    o = 0
    xa_ref[...] = z[:, o:o + D_LRU]
    o += D_LRU
    ga_ref[...] = z[:, o:o + D_LRU]
    o += D_LRU
    for g in range(D_WIN_Q // LANES):
        zz = _rope(_head_norm(z[:, o:o + LANES], hg_ref[0:1], bd), cos, sin)
        qw_ref[:, g * LANES:(g + 1) * LANES] = (zz * scale).astype(BF16)
        o += LANES
    kw_ref[...] = _rope(_head_norm(z[:, o:o + LANES], hg_ref[1:2], bd), cos, sin).astype(BF16)
    o += LANES
    vw_ref[...] = z[:, o:o + LANES].astype(BF16)
    o += LANES
    for g in range(D_NA // LANES):
        zz = _head_norm(z[:, o:o + LANES], hg_ref[2:3], bd)
        qn_ref[:, g * LANES:(g + 1) * LANES] = (zz * scale).astype(BF16)
        o += LANES
    for g in range(D_NA // LANES):
        zz = _head_norm(z[:, o:o + LANES], hg_ref[3:4], bd)
        kn_ref[:, g * LANES:(g + 1) * LANES] = zz.astype(BF16)
        o += LANES
    vn_ref[...] = z[:, o:o + D_NA].astype(BF16)


def _in_proj(xl, xc, mod, g, w_bf, cos_t, sin_t, hg):
    L, d = xl.shape
    n_lat = L // TOK_TILE
    n = L + xc.shape[0]
    d_in = w_bf.shape[1]
    T = TOK_TILE
    tok = lambda w: pl.BlockSpec((T, w), lambda i: (i, 0))
    full = lambda a: pl.BlockSpec(a.shape, lambda i: (0,) * a.ndim)
    widths = (D_LRU, D_LRU, D_WIN_Q, D_WIN_KV, D_WIN_KV, D_NA, D_NA, D_NA)
    dtypes = (F32, F32, BF16, BF16, BF16, BF16, BF16, BF16)
    return pl.pallas_call(
        functools.partial(_in_kernel, n_lat=n_lat),
        out_shape=[jax.ShapeDtypeStruct((n, w), t) for w, t in zip(widths, dtypes)],
        grid=(n_lat + 1,),
        in_specs=[pl.BlockSpec((T, d), lambda i: (jnp.minimum(i, n_lat - 1), 0)),
                  pl.BlockSpec((T, d), lambda i: (0, 0)),
                  full(mod), full(g), full(w_bf), tok(LANES), tok(LANES), full(hg)],
        out_specs=[tok(w) for w in widths],
        compiler_params=_cparams("arbitrary"),
        name="in_proj",
    )(xl, xc, mod, g, w_bf, cos_t, sin_t, hg)


def _chunk_scan(a, b, reverse):
    T = a.shape[0]
    rows = lax.broadcasted_iota(jnp.int32, a.shape, 0)
    s = 1
    while s < T:
        if reverse:
            edge = rows >= T - s
            shift = T - s
        else:
            edge = rows < s
            shift = s
        a_s = jnp.where(edge, 1.0, pltpu.roll(a, shift, 0))
        b_s = jnp.where(edge, 0.0, pltpu.roll(b, shift, 0))
        b = a * b_s + b
        a = a * a_s
        s *= 2
    return a, b


def _lru_direction(xm, ph, nh, pv, nv, cw, cb, w, bias, spl, carry_ref, first, reverse):
    T, C = xm.shape
    rows = lax.broadcasted_iota(jnp.int32, (T, C), 0)
    p6 = ph[6:7] * pv
    p7 = ph[7:8] * pv
    n0 = nh[0:1] * nv
    x_m1 = jnp.where(rows == 0, p7, pltpu.roll(xm, 1, 0))
    x_m2 = jnp.where(rows == 0, p6, jnp.where(rows == 1, p7, pltpu.roll(xm, 2, 0)))
    x_p1 = jnp.where(rows == T - 1, n0, pltpu.roll(xm, T - 1, 0))
    u = cw[0:1] * x_m2 + cw[1:2] * x_m1 + cw[2:3] * xm + cw[3:4] * x_p1 + cb
    zz = _dot(u.astype(BF16), w) + bias
    r = jax.nn.sigmoid(zz[:, :C])
    ig = jax.nn.sigmoid(zz[:, C:])
    log_a = -LRU_C * r * spl
    a = jnp.exp(log_a)
    b = jnp.sqrt(-jnp.tanh(log_a) * (a * a + 1.0)) * ig * u
    a, b = _chunk_scan(a, b, reverse)

    @pl.when(first)
    def _():
        carry_ref[...] = jnp.zeros(carry_ref.shape, F32)

    h = a * carry_ref[0:1] + b
    edge = h[0:1] if reverse else h[T - 1:T]
    carry_ref[...] = jnp.broadcast_to(edge, carry_ref.shape)
    return h


def _lru_kernel(xf_ref, pf_ref, nf_ref, xb_ref, pb_ref, nb_ref, cw_ref, cb_ref, w_ref, b_ref,
                lam_ref, hf_ref, hb_ref, cf_ref, cbk_ref, *, n_lat):
    j = pl.program_id(0)
    C = D_LRU
    fblk = jnp.where(j == 0, n_lat, j - 1)
    bblk = jnp.where(j == 0, n_lat, n_lat - j)
    lam = lam_ref[...]
    spl = jnp.maximum(-lam, 0.0) + jnp.log1p(jnp.exp(-jnp.abs(lam)))
    cw = cw_ref[...]
    cb = cb_ref[...]

    def halo_valid(blk):
        pv = jnp.logical_and(blk != 0, blk != n_lat).astype(F32)
        nv = jnp.logical_and(blk != n_lat - 1, blk != n_lat).astype(F32)
        return pv, nv

    pv, nv = halo_valid(fblk)
    hf_ref[...] = _lru_direction(xf_ref[...], pf_ref[...], nf_ref[...], pv, nv, cw, cb,
                                 w_ref[:, 0:2 * C], b_ref[:, 0:2 * C], spl[0:1], cf_ref, j == 0, False)
    pv, nv = halo_valid(bblk)
    hb_ref[...] = _lru_direction(xb_ref[...], pb_ref[...], nb_ref[...], pv, nv, cw, cb,
                                 w_ref[:, 2 * C:4 * C], b_ref[:, 2 * C:4 * C], spl[1:2], cbk_ref, j == 0, True)


def _lru(xa, conv_w, conv_b, w_gates, b_gates, lam, n_lat):
    n, C = xa.shape
    T = TOK_TILE
    sub = T // 8
    nblk8 = n // 8
    fblk = lambda j: jnp.where(j == 0, n_lat, j - 1)
    bblk = lambda j: jnp.where(j == 0, n_lat, n_lat - j)
    prev8 = lambda blk: jnp.maximum(blk * sub - 1, 0)
    next8 = lambda blk: jnp.minimum((blk + 1) * sub, nblk8 - 1)
    full = lambda a: pl.BlockSpec(a.shape, lambda j: (0,) * a.ndim)
    return pl.pallas_call(
        functools.partial(_lru_kernel, n_lat=n_lat),
        out_shape=[jax.ShapeDtypeStruct((n, C), F32)] * 2,
        grid=(n_lat + 1,),
        in_specs=[pl.BlockSpec((T, C), lambda j: (fblk(j), 0)),
                  pl.BlockSpec((8, C), lambda j: (prev8(fblk(j)), 0)),
                  pl.BlockSpec((8, C), lambda j: (next8(fblk(j)), 0)),
                  pl.BlockSpec((T, C), lambda j: (bblk(j), 0)),
                  pl.BlockSpec((8, C), lambda j: (prev8(bblk(j)), 0)),
                  pl.BlockSpec((8, C), lambda j: (next8(bblk(j)), 0)),
                  full(conv_w), full(conv_b), full(w_gates), full(b_gates), full(lam)],
        out_specs=[pl.BlockSpec((T, C), lambda j: (fblk(j), 0)),
                   pl.BlockSpec((T, C), lambda j: (bblk(j), 0))],
        scratch_shapes=[pltpu.VMEM((8, C), F32), pltpu.VMEM((8, C), F32)],
        compiler_params=_cparams("arbitrary"),
        name="rglru",
    )(xa, xa, xa, xa, xa, xa, conv_w, conv_b, w_gates, b_gates, lam)


def _win_kernel(sink_ref, q_ref, kp_ref, kc_ref, kn_ref, vp_ref, vc_ref, vn_ref, kx_ref, vx_ref,
                o_ref, *, nb):
    W = WIN_BLOCK
    R = WIN_Q_HEADS // WIN_KV_HEADS
    b = pl.program_id(0)
    blk = jnp.where(b < nb, b, -4)
    rows = lax.broadcasted_iota(jnp.int32, (R * W, 3 * W), 0)
    cols = lax.broadcasted_iota(jnp.int32, (R * W, 3 * W), 1)
    kpos = (blk - 1) * W + cols
    qpos = blk * W + (rows & (W - 1))
    valid = jnp.logical_and(jnp.abs(kpos - qpos) <= W, jnp.logical_and(kpos >= 0, kpos < nb * W))
    row1 = lax.broadcasted_iota(jnp.int32, (R * W, 1), 0)
    q = q_ref[...]
    for g in range(WIN_KV_HEADS):
        ls = slice(g * HEAD_DIM, (g + 1) * HEAD_DIM)
        qs = jnp.concatenate([q[:, (g * R + r) * HEAD_DIM:(g * R + r + 1) * HEAD_DIM] for r in range(R)], axis=0)
        kl = jnp.concatenate([kp_ref[:, ls], kc_ref[:, ls], kn_ref[:, ls]], axis=0)
        vl = jnp.concatenate([vp_ref[:, ls], vc_ref[:, ls], vn_ref[:, ls]], axis=0)
        s_loc = jnp.where(valid, _dot_nt(qs, kl), NEG)
        s_ctx = _dot_nt(qs, kx_ref[:, ls])
        sink = jnp.full((R * W, 1), sink_ref[g * R], F32)
        for r in range(1, R):
            sink = jnp.where(row1 >= r * W, sink_ref[g * R + r], sink)
        m = jnp.maximum(jnp.maximum(jnp.max(s_loc, axis=-1, keepdims=True),
                                    jnp.max(s_ctx, axis=-1, keepdims=True)), sink)
        p_loc = jnp.exp(s_loc - m)
        p_ctx = jnp.exp(s_ctx - m)
        den = (jnp.sum(p_loc, axis=-1, keepdims=True) + jnp.sum(p_ctx, axis=-1, keepdims=True)
               + jnp.exp(sink - m))
        o = (_dot(p_loc.astype(BF16), vl) + _dot(p_ctx.astype(BF16), vx_ref[:, ls])) / den
        for r in range(R):
            hh = g * R + r
            o_ref[:, hh * HEAD_DIM:(hh + 1) * HEAD_DIM] = o[r * W:(r + 1) * W].astype(o_ref.dtype)


def _window_attention(sink, qw, kw, vw, L, with_ctx):
    n = qw.shape[0]
    W = WIN_BLOCK
    nb = L // W
    nq = n // W if with_ctx else nb
    ctx_blk = L // TOK_TILE
    lat = lambda b: jnp.minimum(b, nb - 1)
    kv = lambda f: pl.BlockSpec((W, D_WIN_KV), lambda b: (f(b), 0))
    prv = lambda b: jnp.maximum(lat(b) - 1, 0)
    nxt = lambda b: jnp.minimum(lat(b) + 1, nb - 1)
    ctx = pl.BlockSpec((n - L, D_WIN_KV), lambda b: (ctx_blk, 0))
    return pl.pallas_call(
        functools.partial(_win_kernel, nb=nb),
        out_shape=jax.ShapeDtypeStruct((nq * W, D_WIN_Q), BF16),
        grid=(nq,),
        in_specs=[pl.BlockSpec(memory_space=pltpu.SMEM),
                  pl.BlockSpec((W, D_WIN_Q), lambda b: (b, 0)),
                  kv(prv), kv(lat), kv(nxt), kv(prv), kv(lat), kv(nxt), ctx, ctx],
        out_specs=pl.BlockSpec((W, D_WIN_Q), lambda b: (b, 0)),
        compiler_params=_cparams("arbitrary"),
        name="window_attn",
    )(sink, qw, kw, kw, kw, vw, vw, vw, kw, vw)


def _na_kernel(q_ref, kl_ref, vl_ref, kx_ref, vx_ref, bias_ref, o_ref, *, n_blocks):
    b = pl.program_id(0)
    nq = q_ref.shape[0]
    latent = jnp.full((nq, 1), (b < n_blocks).astype(jnp.int32)) == 1
    q = q_ref[...]
    for h in range(NA_HEADS):
        ls = slice(h * HEAD_DIM, (h + 1) * HEAD_DIM)
        qh = q[:, ls]
        s_loc = jnp.where(latent, _dot_nt(qh, kl_ref[:, ls]) + bias_ref[0, h], NEG)
        s_ctx = _dot_nt(qh, kx_ref[:, ls])
        m = jnp.maximum(jnp.max(s_loc, axis=-1, keepdims=True), jnp.max(s_ctx, axis=-1, keepdims=True))
        p_loc = jnp.exp(s_loc - m)
        p_ctx = jnp.exp(s_ctx - m)
        den = jnp.sum(p_loc, axis=-1, keepdims=True) + jnp.sum(p_ctx, axis=-1, keepdims=True)
        o = (_dot(p_loc.astype(BF16), vl_ref[:, ls]) + _dot(p_ctx.astype(BF16), vx_ref[:, ls])) / den
        o_ref[:, ls] = o.astype(o_ref.dtype)


def _na_bias_table(rpb, rows):
    R, KR = NA_ROW_BLOCK, NA_KEY_ROWS
    qc = np.arange(GRID_W)[:, None]
    kc = np.arange(GRID_W)[None, :]
    qstart = np.clip(qc - NA_KW // 2, 0, GRID_W - NA_KW)
    inside = (kc - qstart >= 0) & (kc - qstart < NA_KW)
    pad = GRID_W - NA_KW
    rp = jnp.pad(rpb.astype(F32), ((0, 0), (0, 0), (pad, pad)))
    shifted = jnp.stack([rp[:, :, NA_KW - 1 - q + pad:NA_KW - 1 - q + pad + GRID_W] for q in range(GRID_W)], axis=2)
    per_dr = jnp.where(inside[None, None], shifted, NEG)
    masked = jnp.full((NA_HEADS, GRID_W, GRID_W), NEG, F32)
    half = NA_KH // 2
    cases = []
    for r0, ks in ((0, 0), (half, 0), (rows - R, rows - KR)):
        row_blocks = []
        for rr in range(R):
            r = r0 + rr
            kr = min(max(r - half, 0), rows - NA_KH)
            blocks = [per_dr[:, ks + kk - r + NA_KH - 1] if kr <= ks + kk < kr + NA_KH else masked
                      for kk in range(KR)]
            row_blocks.append(jnp.concatenate(blocks, axis=-1))
        cases.append(jnp.concatenate(row_blocks, axis=1))
    return jnp.stack(cases)


def _neighborhood_attention(qn, kn, vn, bias, L, with_ctx):
    n = qn.shape[0]
    R, KR = NA_ROW_BLOCK, NA_KEY_ROWS
    rows = L // GRID_W
    nb = rows // R
    nq = R * GRID_W
    steps = n // nq if with_ctx else nb
    ctx_blk = L // TOK_TILE
    half = NA_KH // 2

    def blk_of(b):
        return jnp.minimum(b, nb - 1)

    def key_start(b):
        return jnp.clip(blk_of(b) * R - half, 0, rows - KR) * GRID_W

    def case_of(b):
        return jnp.where(blk_of(b) == 0, 0, jnp.where(blk_of(b) == nb - 1, 2, 1))

    loc = pl.BlockSpec((pl.Element(KR * GRID_W), pl.Element(D_NA)), lambda b: (key_start(b), 0))
    ctx = pl.BlockSpec((n - L, D_NA), lambda b: (ctx_blk, 0))
    return pl.pallas_call(
        functools.partial(_na_kernel, n_blocks=nb),
        out_shape=jax.ShapeDtypeStruct((steps * nq, D_NA), BF16),
        grid=(steps,),
        in_specs=[pl.BlockSpec((nq, D_NA), lambda b: (b, 0)), loc, loc, ctx, ctx,
                  pl.BlockSpec((1, NA_HEADS, nq, KR * GRID_W), lambda b: (case_of(b), 0, 0, 0))],
        out_specs=pl.BlockSpec((nq, D_NA), lambda b: (b, 0)),
        compiler_params=_cparams("arbitrary"),
        name="neighborhood_attn",
    )(qn, kn, vn, kn, vn, bias)


def _out_kernel(xl_ref, xc_ref, hf_ref, hb_ref, ga_ref, yb_ref, yc_ref, w_ref, mod_ref, g_ref,
                xo_ref, ht_ref, *, n_lat):
    d = xl_ref.shape[1]
    i = pl.program_id(0)
    row = (i == n_lat).astype(jnp.int32)
    is_ctx = jnp.full((TOK_TILE, 1), row) == 1
    x = jnp.where(is_ctx, xc_ref[...], xl_ref[...])
    ya = ((hf_ref[...] + hb_ref[...]) * jax.nn.gelu(ga_ref[...])).astype(BF16)
    o1 = D_LRU
    o2 = D_LRU + D_WIN_Q
    mix = (_dot(ya, w_ref[0:o1]) + _dot(yb_ref[...], w_ref[o1:o2]) + _dot(yc_ref[...], w_ref[o2:o2 + D_NA]))
    xn = x + mod_ref[pl.ds(row, 1), 2 * d:3 * d] * mix
    xo_ref[...] = xn
    ms = jnp.mean(xn * xn, axis=-1, keepdims=True)
    y = xn * lax.rsqrt(ms + EPS) * g_ref[...]
    h2 = y * (1.0 + mod_ref[pl.ds(row, 1), 4 * d:5 * d]) + mod_ref[pl.ds(row, 1), 3 * d:4 * d]
    ht_ref[...] = h2.T.astype(BF16)


def _out_proj(xl, xc, hf, hb, ga, yb, yc, w_bf, mod, g, with_ctx):
    L, d = xl.shape
    T = TOK_TILE
    n_lat = L // T
    nt = n_lat + 1 if with_ctx else n_lat
    tok = lambda w: pl.BlockSpec((T, w), lambda i: (i, 0))
    full = lambda a: pl.BlockSpec(a.shape, lambda i: (0,) * a.ndim)
    return pl.pallas_call(
        functools.partial(_out_kernel, n_lat=n_lat),
        out_shape=[jax.ShapeDtypeStruct((nt * T, d), F32), jax.ShapeDtypeStruct((d, nt * T), BF16)],
        grid=(nt,),
        in_specs=[pl.BlockSpec((T, d), lambda i: (jnp.minimum(i, n_lat - 1), 0)),
                  pl.BlockSpec((T, d), lambda i: (0, 0)),
                  tok(D_LRU), tok(D_LRU), tok(D_LRU), tok(D_WIN_Q), tok(D_NA),
                  full(w_bf), full(mod), full(g)],
        out_specs=[tok(d), pl.BlockSpec((d, T), lambda i: (0, i))],
        compiler_params=_cparams("arbitrary"),
        name="out_proj",
    )(xl, xc, hf, hb, ga, yb, yc, w_bf, mod, g)


SUBLANES = 8


def _sorting_pairs(n):
    pairs, p = [], 1
    while p < n:
        k = p
        while k >= 1:
            for j in range(k % p, n - k, 2 * k):
                for i in range(min(k, n - j - k)):
                    if (i + j) // (2 * p) == (i + j + k) // (2 * p):
                        pairs.append((i + j, i + j + k))
            k //= 2
        p *= 2
    return pairs


def _vmax(a, b):
    if a is None:
        return b
    if b is None:
        return a
    return jnp.maximum(a, b)


def _vmin(a, b):
    if a is None or b is None:
        return None
    return jnp.minimum(a, b)


def _top16_sorted(slabs):
    K = PEER_TOPK
    w = list(slabs)
    for a, b in _sorting_pairs(K):
        w[a], w[b] = _vmax(w[a], w[b]), _vmin(w[a], w[b])
    shift = SUBLANES // 2
    while shift >= 1:
        partner = [None if w[K - 1 - i] is None else pltpu.roll(w[K - 1 - i], shift, 0) for i in range(K)]
        w = [_vmax(w[i], partner[i]) for i in range(K)]
        stride = K // 2
        while stride >= 1:
            for i in range(K):
                if i & stride == 0:
                    w[i], w[i + stride] = _vmax(w[i], w[i + stride]), _vmin(w[i], w[i + stride])
            stride //= 2
        shift //= 2
    return w


def _allsum8(x):
    x = x + pltpu.roll(x, 4, 0)
    x = x + pltpu.roll(x, 2, 0)
    return x + pltpu.roll(x, 1, 0)


def _route_tile(s1, s2):
    K = PEER_TOPK
    S = SUBLANES
    n_slab = s1.shape[0] // S
    a1 = [s1[j * S:(j + 1) * S] for j in range(n_slab)]
    a2 = [s2[j * S:(j + 1) * S] for j in range(n_slab)]
    v1 = _top16_sorted(a1)
    v2 = _top16_sorted(a2)
    sub = lax.broadcasted_iota(jnp.int32, a1[0].shape, 0)

    def as_rows(v, lo):
        out = v[lo]
        for b in range(1, S):
            out = jnp.where(sub == b, v[lo + b], out)
        return out

    v2_lo, v2_hi, v1_hi = as_rows(v2, 0), as_rows(v2, S), as_rows(v1, S)
    lens = [K // (a + 1) for a in range(S)]
    cands = [v1[0] + v2_lo, v1[0] + v2_hi]
    for a in range(1, S):
        ca = v1[a] + v2_lo
        cands.append(ca if lens[a] >= S else jnp.where(sub < lens[a], ca, -jnp.inf))
    cands.append(v1_hi + v2[0])
    thr = _top16_sorted(cands + [None] * (K - len(cands)))[K - 1]
    x2_lo, x2_hi = jnp.exp(v2_lo - v2[0]), jnp.exp(v2_hi - v2[0])
    sel_lo, sel_hi = cands[0] >= thr, cands[1] >= thr
    zsum = jnp.where(sel_lo, x2_lo, 0.0) + jnp.where(sel_hi, x2_hi, 0.0)
    cnt = [_allsum8(jnp.where(sel_lo, 1.0, 0.0) + jnp.where(sel_hi, 1.0, 0.0))]
    for a in range(1, S):
        sel = cands[a + 1] >= thr
        zsum = zsum + jnp.where(sel, jnp.exp(v1[a] - v1[0]) * x2_lo, 0.0)
        cnt.append(_allsum8(jnp.where(sel, 1.0, 0.0)))
    zsum = zsum + jnp.where(cands[S + 1] >= thr, jnp.exp(v1_hi - v1[0]), 0.0)
    inv_z = 1.0 / _allsum8(zsum)
    r2, e2, c1, e1, pair0 = [], [], [], [], []
    for j in range(n_slab):
        r = jnp.full(a2[j].shape, float(K), F32)
        for k in reversed(range(K)):
            r = jnp.where(a2[j] >= v2[k], float(k), r)
        r2.append(r)
        e2.append(jnp.exp(a2[j] - v2[0]) * inv_z)
        c = jnp.where(a1[j] + v2[0] >= thr, 1.0, 0.0)
        pair0.append(c)
        for a in reversed(range(S)):
            c = jnp.where(a1[j] >= v1[a], cnt[a], c)
        c1.append(c)
        e1.append(jnp.exp(a1[j] - v1[0]))
    lor, land = jnp.logical_or, jnp.logical_and
    n_sel = functools.reduce(lambda a, b: a + b, cnt) + _allsum8(jnp.where(cands[S + 1] >= thr, 1.0, 0.0))
    f1 = land(v1[S - 1] == v1[S], cnt[S - 1] >= 2.0)
    for k in range(S - 1):
        f1 = lor(f1, land(v1[k] == v1[k + 1], cnt[k] != cnt[k + 1]))
    f1 = lor(f1, _allsum8(functools.reduce(lambda a, b: a + b, pair0)) > float(K))
    f2 = v2[0] == v2[1]
    for k in range(1, K - 1):
        f2 = lor(f2, land(v2[k] == v2[k + 1], cnt[0] > float(k)))
    r_sum = _allsum8(functools.reduce(lambda a, b: a + b, r2))
    any_tie2 = r_sum != float(K * (K - 1) // 2 + K * (n_slab * S - K))
    f2 = lor(f2, land(cnt[0] == float(K), any_tie2))
    flag = lor(lor(f1, f2), n_sel != float(K))
    cat = lambda xs: jnp.concatenate(xs, axis=0)
    return cat(r2), cat(e2), cat(c1), cat(e1), jnp.where(flag, 1.0, 0.0)


def _route_tile_exact(s1, s2):
    K = PEER_TOPK

    def take16(x):
        idx = lax.broadcasted_iota(jnp.int32, x.shape, 0).astype(F32)
        rank = jnp.full(x.shape, float(K), F32)
        vals, cur = [], x
        for k in range(K):
            m = jnp.max(cur, axis=0, keepdims=True)
            first = jnp.min(jnp.where(cur == m, idx, float(x.shape[0])), axis=0, keepdims=True)
            hit = idx == first
            rank = jnp.where(hit, float(k), rank)
            vals.append(m)
            cur = jnp.where(hit, -jnp.inf, cur)
        return rank, vals

    r1, v1 = take16(s1)
    r2, v2 = take16(s2)
    v2a = jnp.concatenate(v2, axis=0)
    x1 = [jnp.exp(v1[a] - v1[0]) for a in range(K)]
    x2a = jnp.exp(v2a - v2[0])
    cand = jnp.concatenate([v1[a] + v2a for a in range(K)], axis=0)
    crank, _ = take16(cand)
    sel = crank < float(K)
    z = jnp.zeros_like(v1[0])
    c1 = jnp.zeros(s1.shape, F32)
    for a in range(K):
        sel_a = sel[a * K:(a + 1) * K]
        z = z + jnp.sum(jnp.where(sel_a, x1[a] * x2a, 0.0), axis=0, keepdims=True)
        cnt_a = jnp.sum(jnp.where(sel_a, 1.0, 0.0), axis=0, keepdims=True)
        c1 = jnp.where(r1 == float(a), cnt_a, c1)
    return r2, jnp.exp(s2 - v2[0]) / z, c1, jnp.exp(s1 - v1[0])


def _prep_kernel(ht_ref, wq_ref, k1_ref, k2_ref, r2_ref, e2_ref, c1_ref, e1_ref, q_scr, flag_scr):
    nk = PEER_NKEYS
    q_scr[...] = _dot(wq_ref[...], ht_ref[...])

    def scores(h):
        base = pl.multiple_of(h * 2 * nk, 2 * nk)
        s1 = _dot(k1_ref[...], q_scr[pl.ds(base, nk), :].astype(BF16))
        s2 = _dot(k2_ref[...], q_scr[pl.ds(base + nk, nk), :].astype(BF16))
        return s1, s2

    def store(h, ts, r2, e2, c1, e1):
        r2_ref[h, :, ts] = _pack_rows(r2)
        e2_ref[h, :, ts] = _pack_rows(e2)
        c1_ref[h, :, ts] = c1
        e1_ref[h, :, ts] = e1

    n_tiles = ht_ref.shape[1] // LANES

    def head(h, flags):
        s1, s2 = scores(h)
        for t in range(n_tiles):
            ts = slice(t * LANES, (t + 1) * LANES)
            r2, e2, c1, e1, flag = _route_tile(s1[:, ts], s2[:, ts])
            store(h, ts, r2, e2, c1, e1)
            flag_scr[h, t * SUBLANES:(t + 1) * SUBLANES, :] = flag
            flags = jnp.maximum(flags, flag)
        return flags

    flags = lax.fori_loop(0, PEER_HEADS, head, jnp.zeros((SUBLANES, LANES), F32))

    @pl.when(jnp.max(flags) > 0.0)
    def _():
        def head_exact(h, carry):
            s1, s2 = scores(h)
            for t in range(n_tiles):
                ts = slice(t * LANES, (t + 1) * LANES)

                @pl.when(jnp.max(flag_scr[h, t * SUBLANES:(t + 1) * SUBLANES, :]) > 0.0)
                def _():
                    store(h, ts, *_route_tile_exact(s1[:, ts], s2[:, ts]))
            return carry

        lax.fori_loop(0, PEER_HEADS, head_exact, 0)


def _peer_prep(ht, wq_t, k1, k2, tok_tile, tok_off, n):
    d = ht.shape[0]
    T = tok_tile
    nk = PEER_NKEYS
    full = lambda a: pl.BlockSpec(a.shape, lambda i: (0,) * a.ndim)
    spec = lambda rows: pl.BlockSpec((PEER_HEADS, rows, T), lambda i: (0, 0, i))
    return pl.pallas_call(
        _prep_kernel,
        out_shape=[jax.ShapeDtypeStruct((PEER_HEADS, nk // 2, n), U32)] * 2
        + [jax.ShapeDtypeStruct((PEER_HEADS, nk, n), F32)] * 2,
        grid=(n // T,),
        in_specs=[pl.BlockSpec((d, T), lambda i: (0, i + tok_off)), full(wq_t), full(k1), full(k2)],
        out_specs=[spec(nk // 2), spec(nk // 2), spec(nk), spec(nk)],
        scratch_shapes=[pltpu.VMEM((wq_t.shape[0], T), F32),
                        pltpu.VMEM((PEER_HEADS, T // LANES * SUBLANES, LANES), F32)],
        compiler_params=_cparams("arbitrary"),
        name="peer_prep",
    )(ht, wq_t, k1, k2)


def _peer_kernel(ht_ref, u_ref, un_ref, vt_ref, vp_ref, r2_ref, e2_ref, c1_ref, e1_ref, x_ref, g_ref,
                 o_ref, acc_ref, *ap_refs):
    nk = PEER_NKEYS
    c = pl.program_id(1)
    n_sub = len(ap_refs) // 2
    a_refs, p_refs = ap_refs[:n_sub], ap_refs[n_sub:]
    sub = 2 * a_refs[0].shape[0]
    T = a_refs[0].shape[1]
    rows_per_sub = sub // nk
    PK = 16
    WK = PK // 2

    tok_blocks = [slice(n, min(n + PEER_MXU_TOKENS, T)) for n in range(0, T, PEER_MXU_TOKENS)]

    def activations_from(rows_ref, dst_ref):
        for tb in tok_blocks:
            dst_ref[:, tb] = _pack_rows(_dot(rows_ref, ht_ref[:, tb]))

    def activations(s):
        activations_from(u_ref[s * sub:(s + 1) * sub, :], a_refs[s])

    def gates(s):
        for t in range(T // LANES):
            ts = slice(t * LANES, (t + 1) * LANES)
            for j in range(rows_per_sub):
                row = s * rows_per_sub + j
                gate = [None] * (nk // PK)
                for h in range(PEER_HEADS):
                    c1 = c1_ref[h, :, ts][row:row + 1]
                    e1 = e1_ref[h, :, ts][row:row + 1]
                    c1 = jnp.broadcast_to(c1, (PK, LANES)).astype(BF16)
                    e1 = jnp.broadcast_to(e1, (PK, LANES)).astype(BF16)
                    for v in range(nk // PK):
                        ws = slice(v * WK, (v + 1) * WK)
                        r2 = _unpack_rows(r2_ref[h, ws, ts])
                        e2 = _unpack_rows(e2_ref[h, ws, ts])
                        g = jnp.where(r2 < c1, e2 * e1, 0.0)
                        gate[v] = g if h == 0 else gate[v] + g
                for v in range(nk // PK):
                    ws = slice((j * nk + v * PK) // 2, (j * nk + (v + 1) * PK) // 2)
                    a = _unpack_rows(a_refs[s][ws, ts])
                    th = jnp.tanh(a * (GELU_K0 + GELU_K1 * (a * a)))
                    p_refs[s][ws, ts] = pltpu.bitcast(gate[v] * (a * (0.5 + 0.5 * th)), U32)

    def project(terms):
        for tb in tok_blocks:
            parts = [_dot(cols, _unpack_rows(src[:, tb])) for cols, src in terms]
            acc_ref[:, tb] += functools.reduce(lambda a, b: a + b, parts)

    def term(s):
        return vt_ref[:, s * sub:(s + 1) * sub], p_refs[s]

    last = n_sub - 1
    assert n_sub % 2 == 0

    @pl.when(c == 0)
    def _():
        acc_ref[...] = jnp.zeros(acc_ref.shape, F32)
        p_refs[last][...] = jnp.zeros(p_refs[last].shape, U32)
        activations(0)

    pending = [(vp_ref[...], p_refs[last])]
    for s in range(n_sub):
        gates(s)
        if s < last:
            pending.append(term(s))
            activations(s + 1)
        if len(pending) == 2:
            project(pending)
            pending = []
    activations_from(un_ref[...], a_refs[0])

    @pl.when(c == pl.num_programs(1) - 1)
    def _():
        project([term(last)])
        o_ref[...] = x_ref[...] + g_ref[...] * acc_ref[...].T


def _peer_dense(ht, u_bf, vt_bf, routing, x, g2, tok_tile, tok_off, n_tok):
    r2, e2, c1, e1 = routing
    d = ht.shape[0]
    n_exp = u_bf.shape[0]
    T = tok_tile
    NC = PEER_EXPERT_CHUNK
    nk = PEER_NKEYS
    SUB = PEER_SUB_CHUNK
    n_sub = NC // SUB
    once = dict(pipeline_mode=pl.Buffered(1))
    rt = lambda rows: pl.BlockSpec((PEER_HEADS, rows, T), lambda i, c: (0, 0, i), **once)
    per_chunk = pl.BlockSpec((PEER_HEADS, NC // nk, T), lambda i, c: (0, c, i))
    return pl.pallas_call(
        _peer_kernel,
        out_shape=jax.ShapeDtypeStruct((n_tok, d), F32),
        grid=(n_tok // T, n_exp // NC),
        in_specs=[pl.BlockSpec((d, T), lambda i, c: (0, i + tok_off), **once),
                  pl.BlockSpec((NC, d), lambda i, c: (c, 0)),
                  pl.BlockSpec((SUB, d), lambda i, c: (jnp.minimum((c + 1) * n_sub, n_exp // SUB - 1), 0)),
                  pl.BlockSpec((d, NC), lambda i, c: (0, c)),
                  pl.BlockSpec((d, SUB), lambda i, c: (0, jnp.maximum(c * n_sub - 1, 0))),
                  rt(nk // 2), rt(nk // 2), per_chunk, per_chunk,
                  pl.BlockSpec((T, d), lambda i, c: (i + tok_off, 0)),
                  pl.BlockSpec((1, d), lambda i, c: (0, 0))],
        out_specs=pl.BlockSpec((T, d), lambda i, c: (i, 0)),
        scratch_shapes=[pltpu.VMEM((d, T), F32)] + [pltpu.VMEM((PEER_SUB_CHUNK // 2, T), U32)] * (2 * n_sub),
        compiler_params=_cparams("arbitrary", "arbitrary"),
        name="peer_dense",
    )(ht, u_bf, u_bf, vt_bf, vt_bf, r2, e2, c1, e1, x, g2)


def _rope_tables(L, n):
    t = jnp.arange(L)
    row = (t // GRID_W).astype(F32)
    col = (t % GRID_W).astype(F32)
    q = HEAD_DIM // 4
    inv = ROPE_THETA ** (-jnp.arange(q, dtype=F32) / q)
    ar = row[:, None] * inv
    ac = col[:, None] * inv
    cos = jnp.concatenate([jnp.cos(ar), jnp.cos(ar), jnp.cos(ac), jnp.cos(ac)], axis=-1)
    sin = jnp.concatenate([-jnp.sin(ar), jnp.sin(ar), -jnp.sin(ac), jnp.sin(ac)], axis=-1)
    cos = jnp.concatenate([cos, jnp.ones((n - L, HEAD_DIM), F32)], axis=0)
    sin = jnp.concatenate([sin, jnp.zeros((n - L, HEAD_DIM), F32)], axis=0)
    return jnp.tile(cos, (1, LANES // HEAD_DIM)), jnp.tile(sin, (1, LANES // HEAD_DIM))


def _block_diag(w):
    nb, di, do = w.shape
    eye = jnp.eye(nb, dtype=w.dtype)
    return (eye[:, None, :, None] * w[:, :, None, :]).reshape(nb * di, nb * do)


def kernel(x, c, ctx, c_ctx, w_mod, b_mod, norm1_g, norm2_g, w_in, w_out, lru_conv_w, lru_conv_b, lru_wa, lru_ba, lru_wx, lru_bx, lru_lam, win_qn_g, win_kn_g, win_sink, na_qn_g, na_kn_g, na_rpb, peer_wq, peer_k1, peer_k2, peer_u, peer_v):
    B, L, D = x.shape
    Lc = ctx.shape[1]
    depth = w_mod.shape[0]
    assert B == 1 and Lc == TOK_TILE and L % PREP_TOK_TILE == 0 and L // GRID_W >= 3 * NA_ROW_BLOCK
    n = L + Lc
    n_lat = L // TOK_TILE
    rows = L // GRID_W

    cvec = jnp.zeros((8, D), F32).at[0].set(c[0]).at[1].set(c_ctx)
    mods = _modulation(cvec, w_mod, b_mod)
    cos_t, sin_t = _rope_tables(L, n)
    tile2 = lambda g: jnp.tile(g, LANES // HEAD_DIM)

    xl, xc = x[0], ctx[0]
    for l in range(depth):
        with_ctx = l < depth - 1
        mod = mods[l]
        hg = jnp.zeros((8, LANES), F32)
        hg = hg.at[0].set(tile2(win_qn_g[l])).at[1].set(tile2(win_kn_g[l]))
        hg = hg.at[2].set(tile2(na_qn_g[l])).at[3].set(tile2(na_kn_g[l]))
        xa, ga, qw, kw, vw, qn, kn, vn = _in_proj(xl, xc, mod, norm1_g[l][None], w_in[l].astype(BF16),
                                                  cos_t, sin_t, hg)

        w_gates = jnp.concatenate([_block_diag(lru_wa[l, 0]), _block_diag(lru_wx[l, 0]),
                                   _block_diag(lru_wa[l, 1]), _block_diag(lru_wx[l, 1])], axis=1).astype(BF16)
        b_gates = jnp.concatenate([lru_ba[l, 0], lru_bx[l, 0], lru_ba[l, 1], lru_bx[l, 1]])[None]
        conv_w = jnp.zeros((8, D_LRU), F32).at[0:lru_conv_w.shape[1]].set(lru_conv_w[l])
        lam = jnp.zeros((8, D_LRU), F32).at[0:2].set(lru_lam[l])
        hf, hb = _lru(xa, conv_w, lru_conv_b[l][None], w_gates, b_gates, lam, n_lat)

        yb = _window_attention(win_sink[l], qw, kw, vw, L, with_ctx)
        yc = _neighborhood_attention(qn, kn, vn, _na_bias_table(na_rpb[l], rows), L, with_ctx)

        xn, ht = _out_proj(xl, xc, hf, hb, ga, yb, yc, w_out[l].astype(BF16), mod, norm2_g[l][None], with_ctx)

        route_w = (peer_wq[l].T.astype(BF16), peer_k1[l].astype(BF16), peer_k2[l].astype(BF16))
        u_bf = peer_u[l].astype(BF16)
        vt_bf = peer_v[l].T.astype(BF16)
        g2 = mod[:, 5 * D:6 * D]
        routing = _peer_prep(ht, *route_w, PREP_TOK_TILE, 0, L)
        xl_new = _peer_dense(ht, u_bf, vt_bf, routing, xn, g2[0:1], PEER_TOK_TILE, 0, L)
        if with_ctx:
            routing = _peer_prep(ht, *route_w, Lc, L // Lc, Lc)
            xc = _peer_dense(ht, u_bf, vt_bf, routing, xn, g2[1:2], Lc, L // Lc, Lc)
        xl = xl_new
    return xl[None]
```
